```python
import jax, jax.numpy as jnp
from jax import lax
import numpy as np

D_MODEL = 1024
BATCH = 8
SEQ = 4096
DEPTH = 1
DEC_BATCH = 1
DEC_SEQ = 16384
PAST_LEN = 128

D_MIX = D_MODEL
D_FOURIER = D_MIX // 2
FOURIER_GROUPS = 4
FOURIER_GROUP_DIM = D_FOURIER // FOURIER_GROUPS
D_CONV = D_MIX - D_FOURIER
CONV_HEADS = 8
CONV_WIDTH = 31
CONV_PAD = (CONV_WIDTH - 1) // 2
PEER_HEADS = 8
N_KEYS = 128
N_EXPERTS = N_KEYS * N_KEYS
PEER_TOPK = 16
D_QUERY = 256
D_HALF = D_QUERY // 2
TOKEN_BLOCK = 128
EPS = 1e-6

kernel_name = 'hybrid_fnet_conformer_peer_encoder'


def rmsnorm(x, g):
    xf = x.astype(jnp.float32)
    y = xf * lax.rsqrt(jnp.mean(xf * xf, axis=-1, keepdims=True) + EPS)
    return (y * g.astype(jnp.float32)).astype(x.dtype)


def layernorm(x, g, b):
    xf = x.astype(jnp.float32)
    mu = jnp.mean(xf, axis=-1, keepdims=True)
    var = jnp.mean(jnp.square(xf - mu), axis=-1, keepdims=True)
    y = (xf - mu) * lax.rsqrt(var + EPS) * g.astype(jnp.float32) + b.astype(jnp.float32)
    return y.astype(x.dtype)


def fourier_mix(u):
    B, S, _ = u.shape
    uf = u.astype(jnp.float32).reshape(B, S, FOURIER_GROUPS, FOURIER_GROUP_DIM)
    y = jnp.fft.fft2(uf, axes=(1, 3), norm='ortho').real
    return y.reshape(B, S, D_FOURIER).astype(u.dtype)


def conv_module(u, conv_w, conv_b, ln_g, ln_b):
    a, gte = jnp.split(u, 2, axis=-1)
    h = a * jax.nn.sigmoid(gte)
    h = lax.conv_general_dilated(
        h, conv_w[:, None, :], window_strides=(1,), padding=[(CONV_PAD, CONV_PAD)],
        dimension_numbers=('NWC', 'WIO', 'NWC'), feature_group_count=D_CONV) + conv_b
    h = layernorm(h, ln_g, ln_b)
    return jax.nn.silu(h)


def mixer_sublayer(x, norm1_g, w_in, conv_w, conv_b, conv_ln_g, conv_ln_b, gn_fourier_g, gn_conv_g, w_out):
    h = rmsnorm(x, norm1_g)
    z = h @ w_in
    yf = rmsnorm(fourier_mix(z[..., :D_FOURIER]), gn_fourier_g)
    yc = rmsnorm(conv_module(z[..., D_FOURIER:], conv_w, conv_b, conv_ln_g, conv_ln_b), gn_conv_g)
    y = jnp.concatenate([yf, yc], axis=-1) @ w_out
    return x + y


def peer_sublayer(x, norm2_g, w_q, keys_a, keys_b, expert_down, expert_up):
    B, S, D = x.shape
    T = B * S
    ht = rmsnorm(x, norm2_g).reshape(T, D)
    q = (ht @ w_q).astype(jnp.float32).reshape(T, PEER_HEADS, 2, D_HALF)
    sa = jnp.einsum('thc,hnc->thn', q[:, :, 0], keys_a.astype(jnp.float32))
    sb = jnp.einsum('thc,hnc->thn', q[:, :, 1], keys_b.astype(jnp.float32))
    va, ia = lax.top_k(sa, PEER_TOPK)
    vb, ib = lax.top_k(sb, PEER_TOPK)
    cand_s = (va[..., :, None] + vb[..., None, :]).reshape(T, PEER_HEADS, PEER_TOPK * PEER_TOPK)
    cand_i = (ia[..., :, None] * N_KEYS + ib[..., None, :]).reshape(T, PEER_HEADS, PEER_TOPK * PEER_TOPK)
    top_s, pos = lax.top_k(cand_s, PEER_TOPK)
    idx = jnp.take_along_axis(cand_i, pos, axis=-1)
    gate = jax.nn.softmax(top_s, axis=-1).astype(x.dtype)

    nb = T // TOKEN_BLOCK

    def block(args):
        hb, ixb, gb = args
        act = jax.nn.gelu(jnp.einsum('td,thkd->thk', hb, expert_down[ixb]), approximate=False) * gb
        return jnp.einsum('thk,thkd->td', act, expert_up[ixb])

    out = lax.map(block, (ht.reshape(nb, TOKEN_BLOCK, D),
                          idx.reshape(nb, TOKEN_BLOCK, PEER_HEADS, PEER_TOPK),
                          gate.reshape(nb, TOKEN_BLOCK, PEER_HEADS, PEER_TOPK)))
    return x + out.reshape(B, S, D)


def run_trunk(x, norm1_g, w_in, conv_w, conv_b, conv_ln_g, conv_ln_b, gn_fourier_g, gn_conv_g, w_out,
              norm2_g, w_q, keys_a, keys_b, expert_down, expert_up, final_g):
    for l in range(DEPTH):
        x = mixer_sublayer(x, norm1_g[l], w_in[l], conv_w[l], conv_b[l], conv_ln_g[l], conv_ln_b[l],
                           gn_fourier_g[l], gn_conv_g[l], w_out[l])
        x = peer_sublayer(x, norm2_g[l], w_q[l], keys_a[l], keys_b[l], expert_down[l], expert_up[l])
    return rmsnorm(x, final_g)


def setup_inputs(seed: int = 0) -> dict:
    key = jax.random.key(seed)
    ks = jax.random.split(key, 18)
    f32 = jnp.float32

    def nrm(k, shape, scale):
        return jax.random.normal(k, shape, f32) * scale

    return {
        'x_prompt': nrm(ks[0], (BATCH, SEQ, D_MODEL), 1.0),
        'x_sample': nrm(ks[1], (DEC_BATCH, DEC_SEQ, D_MODEL), 1.0),
        'norm1_g': 1.0 + nrm(ks[2], (DEPTH, D_MODEL), 0.02),
        'w_in': nrm(ks[3], (DEPTH, D_MODEL, D_FOURIER + 2 * D_CONV), D_MODEL ** -0.5),
        'conv_w': nrm(ks[4], (DEPTH, CONV_WIDTH, D_CONV), CONV_WIDTH ** -0.5),
        'conv_b': nrm(ks[5], (DEPTH, D_CONV), 0.01),
        'conv_ln_g': 1.0 + nrm(ks[6], (DEPTH, D_CONV), 0.02),
        'conv_ln_b': nrm(ks[7], (DEPTH, D_CONV), 0.01),
        'gn_fourier_g': 1.0 + nrm(ks[8], (DEPTH, D_FOURIER), 0.02),
        'gn_conv_g': 1.0 + nrm(ks[9], (DEPTH, D_CONV), 0.02),
        'w_out': nrm(ks[10], (DEPTH, D_MIX, D_MODEL), D_MIX ** -0.5),
        'norm2_g': 1.0 + nrm(ks[11], (DEPTH, D_MODEL), 0.02),
        'w_q': nrm(ks[12], (DEPTH, D_MODEL, PEER_HEADS * D_QUERY), D_MODEL ** -0.5),
        'keys_a': nrm(ks[13], (DEPTH, PEER_HEADS, N_KEYS, D_HALF), D_HALF ** -0.5),
        'keys_b': nrm(ks[14], (DEPTH, PEER_HEADS, N_KEYS, D_HALF), D_HALF ** -0.5),
        'expert_down': nrm(ks[15], (DEPTH, N_EXPERTS, D_MODEL), D_MODEL ** -0.5),
        'expert_up': nrm(ks[16], (DEPTH, N_EXPERTS, D_MODEL), 0.5),
        'final_g': 1.0 + nrm(ks[17], (D_MODEL,), 0.02),
    }


def reference(x_prompt, x_sample, norm1_g, w_in, conv_w, conv_b, conv_ln_g, conv_ln_b, gn_fourier_g,
              gn_conv_g, w_out, norm2_g, w_q, keys_a, keys_b, expert_down, expert_up, final_g):
    y_prompt = run_trunk(x_prompt, norm1_g, w_in, conv_w, conv_b, conv_ln_g, conv_ln_b, gn_fourier_g,
                         gn_conv_g, w_out, norm2_g, w_q, keys_a, keys_b, expert_down, expert_up, final_g)
    y_sample = run_trunk(x_sample, norm1_g, w_in, conv_w, conv_b, conv_ln_g, conv_ln_b, gn_fourier_g,
                         gn_conv_g, w_out, norm2_g, w_q, keys_a, keys_b, expert_down, expert_up, final_g)
    return (y_prompt, y_sample)
```

```python
import functools
import math

import numpy as np
import jax
import jax.numpy as jnp
from jax import lax
from jax.experimental import pallas as pl
from jax.experimental.pallas import tpu as pltpu
from jax.experimental.pallas import tpu_sc as plsc

F32 = jnp.float32
BF16 = jnp.bfloat16

D_MODEL = 1024
D_FOURIER = 512
GROUPS = 4
GDIM = 128
D_CONV = 512
CONV_W = 31
CONV_PAD = 15
HEADS = 8
N_KEYS = 128
TOPK = 16
D_HALF = 128
D_QUERY = 256
N_SEL = HEADS * TOPK
EPS = 1e-6

SUBLANES = 8
LANES = 128
VMEM_LIMIT = 56 * 1024 * 1024

SC_CORES = 2
SC_SUBCORES = 16
SC_LANES = 16
SC_WORKERS = SC_CORES * SC_SUBCORES


def _cparams(sem):
    return pltpu.CompilerParams(dimension_semantics=sem, vmem_limit_bytes=VMEM_LIMIT)


def _rms(x, g):
    return x * lax.rsqrt(jnp.mean(x * x, axis=-1, keepdims=True) + EPS) * g


def _inproj_kernel(x_ref, g1_ref, win_ref, cdft_ref, zg_ref, hg_ref):
    h = _rms(x_ref[...], g1_ref[...]).astype(BF16)
    z = jnp.dot(h, win_ref[...], preferred_element_type=F32)
    for g in range(GROUPS):
        zg = z[:, g * GDIM:(g + 1) * GDIM].astype(BF16)
        zg_ref[g] = jnp.dot(zg, cdft_ref[...], preferred_element_type=F32)
    a = z[:, D_FOURIER:D_FOURIER + D_CONV]
    gate = z[:, D_FOURIER + D_CONV:]
    hg_ref[...] = a * jax.nn.sigmoid(gate)


def _inproj(x, g1, win_bf, cdft, tm):
    B, S, _ = x.shape
    return pl.pallas_call(
        _inproj_kernel,
        grid=(B, S // tm),
        in_specs=[
            pl.BlockSpec((None, tm, D_MODEL), lambda b, m: (b, m, 0)),
            pl.BlockSpec((1, D_MODEL), lambda b, m: (0, 0)),
            pl.BlockSpec((D_MODEL, D_FOURIER + 2 * D_CONV), lambda b, m: (0, 0)),
            pl.BlockSpec((GDIM, 2 * GDIM), lambda b, m: (0, 0)),
        ],
        out_specs=[
            pl.BlockSpec((None, GROUPS, tm, 2 * GDIM), lambda b, m: (b, 0, m, 0)),
            pl.BlockSpec((None, tm, D_CONV), lambda b, m: (b, m, 0)),
        ],
        out_shape=[
            jax.ShapeDtypeStruct((B, GROUPS, S, 2 * GDIM), F32),
            jax.ShapeDtypeStruct((B, S, D_CONV), F32),
        ],
        compiler_params=_cparams(("parallel", "parallel")),
        name="inproj",
    )(x, g1, win_bf, cdft)


def _fft1_kernel(z_ref, k1_ref, tc_ref, ts_ref, a_ref):
    n1 = z_ref.shape[0]
    rows = n1 * SUBLANES
    x = z_ref[...].reshape(rows, 2 * GDIM).astype(BF16)
    p = jnp.dot(k1_ref[...], x, preferred_element_type=F32)
    pc, ps = p[:rows], p[rows:]
    ar = pc[:, :GDIM] + ps[:, GDIM:]
    ai = pc[:, GDIM:] - ps[:, :GDIM]
    tc, ts = tc_ref[...], ts_ref[...]
    a_ref[:, :, :GDIM] = (ar * tc + ai * ts).reshape(n1, SUBLANES, GDIM)
    a_ref[:, :, GDIM:] = (ai * tc - ar * ts).reshape(n1, SUBLANES, GDIM)


def _fft2_kernel(a_ref, c2_ref, s2_ref, y_ref):
    for r in range(SUBLANES):
        blk = a_ref[r].astype(BF16)
        y = jnp.dot(c2_ref[...], blk[:, :GDIM], preferred_element_type=F32)
        y = y + jnp.dot(s2_ref[...], blk[:, GDIM:], preferred_element_type=F32)
        y_ref[:, r * GDIM:(r + 1) * GDIM] = y


def _split_seq(S):
    n1 = 1 << (int(math.log2(S)) // 2)
    return n1, S // n1


@functools.lru_cache(maxsize=None)
def _fft_tables(S):
    n1, n2 = _split_seq(S)
    k1 = np.arange(n1)
    ang1 = 2.0 * np.pi * np.outer(k1, np.arange(n1)) / n1
    eye = np.eye(SUBLANES)
    kmat = np.concatenate([np.kron(np.cos(ang1), eye), np.kron(np.sin(ang1), eye)], axis=0)
    nn2 = np.arange(n2).reshape(n2 // SUBLANES, 1, SUBLANES)
    angt = 2.0 * np.pi * k1.reshape(1, n1, 1) * nn2 / S
    angt = angt.reshape(n2 // SUBLANES, n1 * SUBLANES, 1)
    tc = np.broadcast_to(np.cos(angt), (n2 // SUBLANES, n1 * SUBLANES, GDIM))
    ts = np.broadcast_to(np.sin(angt), (n2 // SUBLANES, n1 * SUBLANES, GDIM))
    ang2 = 2.0 * np.pi * np.outer(np.arange(n2), np.arange(n2)) / n2
    scale = 1.0 / math.sqrt(S)
    return (kmat.astype(np.float32), np.ascontiguousarray(tc, np.float32),
            np.ascontiguousarray(ts, np.float32),
            (np.cos(ang2) * scale).astype(np.float32), (np.sin(ang2) * scale).astype(np.float32))


def _seq_dft(zg):
    B, G, S, _ = zg.shape
    n1, n2 = _split_seq(S)
    kmat, tc, ts, c2, s2 = _fft_tables(S)
    kmat = jnp.asarray(kmat, BF16)
    c2 = jnp.asarray(c2, BF16)
    s2 = jnp.asarray(s2, BF16)
    nb = n2 // SUBLANES
    z6 = zg.reshape(B, G, n1, nb, SUBLANES, 2 * GDIM)
    blk6 = (None, None, n1, None, SUBLANES, 2 * GDIM)
    a6 = pl.pallas_call(
        _fft1_kernel,
        grid=(B, G, nb),
        in_specs=[
            pl.BlockSpec(blk6, lambda b, g, j: (b, g, 0, j, 0, 0)),
            pl.BlockSpec((2 * n1 * SUBLANES, n1 * SUBLANES), lambda b, g, j: (0, 0)),
            pl.BlockSpec((None, n1 * SUBLANES, GDIM), lambda b, g, j: (j, 0, 0)),
            pl.BlockSpec((None, n1 * SUBLANES, GDIM), lambda b, g, j: (j, 0, 0)),
        ],
        out_specs=pl.BlockSpec(blk6, lambda b, g, j: (b, g, 0, j, 0, 0)),
        out_shape=jax.ShapeDtypeStruct(z6.shape, F32),
        compiler_params=_cparams(("parallel", "parallel", "parallel")),
        name="fft_stage1",
    )(z6, kmat, jnp.asarray(tc), jnp.asarray(ts))
    a5 = a6.reshape(B, G, n1, n2, 2 * GDIM)
    y = pl.pallas_call(
        _fft2_kernel,
        grid=(B, G, n1 // SUBLANES),
        in_specs=[
            pl.BlockSpec((None, None, SUBLANES, n2, 2 * GDIM), lambda b, g, i: (b, g, i, 0, 0)),
            pl.BlockSpec((n2, n2), lambda b, g, i: (0, 0)),
            pl.BlockSpec((n2, n2), lambda b, g, i: (0, 0)),
        ],
        out_specs=pl.BlockSpec((None, None, n2, SUBLANES * GDIM), lambda b, g, i: (b, g, 0, i)),
        out_shape=jax.ShapeDtypeStruct((B, G, n2, n1 * GDIM), F32),
        compiler_params=_cparams(("parallel", "parallel", "parallel")),
        name="fft_stage2",
    )(a5, c2, s2)
    return y.reshape(B, G, S, GDIM)


CONV_HALO = 16
CONV_ROWS = 16


def _conv_kernel(prev_ref, cur_ref, next_ref, w_ref, cb_ref, lg_ref, lb_ref, gc_ref, o_ref, scr):
    ts = cur_ref.shape[0]
    m = pl.program_id(1)
    last = pl.num_programs(1) - 1
    scr[0:CONV_HALO, :] = jnp.where(m > 0, prev_ref[...], 0.0)
    scr[CONV_HALO:CONV_HALO + ts, :] = cur_ref[...]
    scr[CONV_HALO + ts:, :] = jnp.where(m < last, next_ref[...], 0.0)
    off = CONV_HALO - CONV_PAD

    def step(c, carry):
        base = pl.multiple_of(c * CONV_ROWS, CONV_ROWS)
        win = scr[pl.ds(base, CONV_ROWS + 2 * CONV_HALO), :]
        acc = jnp.broadcast_to(cb_ref[...], (CONV_ROWS, D_CONV))
        for k in range(CONV_W):
            acc = acc + win[off + k:off + k + CONV_ROWS, :] * w_ref[k:k + 1, :]
        mu = jnp.mean(acc, axis=-1, keepdims=True)
        d = acc - mu
        var = jnp.mean(d * d, axis=-1, keepdims=True)
        y = d * lax.rsqrt(var + EPS) * lg_ref[...] + lb_ref[...]
        y = y * jax.nn.sigmoid(y)
        o_ref[pl.ds(base, CONV_ROWS), :] = _rms(y, gc_ref[...])
        return carry

    lax.fori_loop(0, ts // CONV_ROWS, step, 0)


def _conv_module(hg, conv_w, conv_b, ln_g, ln_b, gn_c, ts):
    B, S, _ = hg.shape
    hb = ts // CONV_HALO
    nh = S // CONV_HALO
    vec = pl.BlockSpec((1, D_CONV), lambda b, m: (0, 0))
    return pl.pallas_call(
        _conv_kernel,
        grid=(B, S // ts),
        in_specs=[
            pl.BlockSpec((None, CONV_HALO, D_CONV), lambda b, m: (b, jnp.maximum(m * hb - 1, 0), 0)),
            pl.BlockSpec((None, ts, D_CONV), lambda b, m: (b, m, 0)),
            pl.BlockSpec((None, CONV_HALO, D_CONV), lambda b, m: (b, jnp.minimum((m + 1) * hb, nh - 1), 0)),
            pl.BlockSpec((CONV_W, D_CONV), lambda b, m: (0, 0)),
            vec, vec, vec, vec,
        ],
        out_specs=pl.BlockSpec((None, ts, D_CONV), lambda b, m: (b, m, 0)),
        out_shape=jax.ShapeDtypeStruct((B, S, D_CONV), F32),
        scratch_shapes=[pltpu.VMEM((ts + 2 * CONV_HALO, D_CONV), F32)],
        compiler_params=_cparams(("parallel", "parallel")),
        name="conv_module",
    )(hg, hg, hg, conv_w, conv_b, ln_g, ln_b, gn_c)


def _extract_topk(s, order, k, payload=None):
    big = jnp.int32(1 << 30)
    vals, sel = [], []
    for _ in range(k):
        m = jnp.max(s, axis=0, keepdims=True)
        am = jnp.min(jnp.where(s == m, order, big), axis=0, keepdims=True)
        hit = order == am
        vals.append(m)
        if payload is None:
            sel.append(am)
        else:
            sel.append(jnp.max(jnp.where(hit, payload, -1), axis=0, keepdims=True))
        s = jnp.where(hit, -jnp.inf, s)
    return jnp.concatenate(vals, axis=0), jnp.concatenate(sel, axis=0)


_CAND_A = 4
_CAND_B = 3


def _route_kernel(x_ref, yf_ref, yc_ref, gf_ref, wout_ref, g2_ref, wq_ref, ka_ref, kb_ref,
                  x1_ref, ht_ref, idx_ref, gate_ref):
    tm = x_ref.shape[0]
    yf = jnp.concatenate([yf_ref[g] for g in range(GROUPS)], axis=-1)
    yfn = _rms(yf, gf_ref[...])
    ycat = jnp.concatenate([yfn, yc_ref[...]], axis=-1).astype(BF16)
    x1 = x_ref[...] + jnp.dot(ycat, wout_ref[...], preferred_element_type=F32)
    x1_ref[...] = x1
    ht = _rms(x1, g2_ref[...])
    ht_ref[...] = ht
    q = jnp.dot(ht.astype(BF16), wq_ref[...], preferred_element_type=F32)

    row16 = lax.broadcasted_iota(jnp.int32, (TOPK, tm), 0)
    key_iota = lax.broadcasted_iota(jnp.int32, (N_KEYS, tm), 0)
    nt = (((1,), (1,)), ((), ()))
    idx_rows, gate_rows = [], []
    for h in range(HEADS):
        qa = q[:, h * D_QUERY:h * D_QUERY + D_HALF].astype(BF16)
        qb = q[:, h * D_QUERY + D_HALF:(h + 1) * D_QUERY].astype(BF16)
        sa = lax.dot_general(ka_ref[h], qa, nt, preferred_element_type=F32)
        sb = lax.dot_general(kb_ref[h], qb, nt, preferred_element_type=F32)
        va, ia = _extract_topk(sa, key_iota, TOPK)
        vb, ib = _extract_topk(sb, key_iota, TOPK)
        cs, ce, co = [], [], []
        for i in range(_CAND_A):
            ok = (i + 1) * (row16 + 1) <= TOPK
            cs.append(jnp.where(ok, va[i:i + 1] + vb, -jnp.inf))
            ce.append(ia[i:i + 1] * N_KEYS + ib)
            co.append(i * TOPK + row16)
        for j in range(_CAND_B):
            ok = jnp.logical_and(row16 >= _CAND_A, (row16 + 1) * (j + 1) <= TOPK)
            cs.append(jnp.where(ok, va + vb[j:j + 1], -jnp.inf))
            ce.append(ia * N_KEYS + ib[j:j + 1])
            co.append(row16 * TOPK + j)
        top_s, top_e = _extract_topk(jnp.concatenate(cs, axis=0), jnp.concatenate(co, axis=0),
                                     TOPK, payload=jnp.concatenate(ce, axis=0))
        ex = jnp.exp(top_s - top_s[0:1])
        gate_rows.append(ex / jnp.sum(ex, axis=0, keepdims=True))
        idx_rows.append(top_e)
    idx_ref[...] = jnp.concatenate(idx_rows, axis=0).T
    gate_ref[...] = jnp.concatenate(gate_rows, axis=0).T


def _route(x, yf, yc, gf, wout_bf, g2, wq_bf, ka_bf, kb_bf, tm):
    B, S, _ = x.shape
    const2 = lambda b, m: (0, 0)
    const3 = lambda b, m: (0, 0, 0)
    row = lambda w: pl.BlockSpec((None, tm, w), lambda b, m: (b, m, 0))
    return pl.pallas_call(
        _route_kernel,
        grid=(B, S // tm),
        in_specs=[
            row(D_MODEL),
            pl.BlockSpec((None, GROUPS, tm, GDIM), lambda b, m: (b, 0, m, 0)),
            row(D_CONV),
            pl.BlockSpec((1, D_FOURIER), const2),
            pl.BlockSpec((D_MODEL, D_MODEL), const2),
            pl.BlockSpec((1, D_MODEL), const2),
            pl.BlockSpec((D_MODEL, HEADS * D_QUERY), const2),
            pl.BlockSpec((HEADS, N_KEYS, D_HALF), const3),
            pl.BlockSpec((HEADS, N_KEYS, D_HALF), const3),
        ],
        out_specs=[row(D_MODEL), row(D_MODEL), row(N_SEL), row(N_SEL)],
        out_shape=[
            jax.ShapeDtypeStruct((B, S, D_MODEL), F32),
            jax.ShapeDtypeStruct((B, S, D_MODEL), F32),
            jax.ShapeDtypeStruct((B, S, N_SEL), jnp.int32),
            jax.ShapeDtypeStruct((B, S, N_SEL), F32),
        ],
        compiler_params=_cparams(("parallel", "parallel")),
        name="route",
    )(x, yf, yc, gf, wout_bf, g2, wq_bf, ka_bf, kb_bf)


SC_ROWS = 32
SC_CHUNKS = N_SEL // SC_ROWS
SC_TOKB = 8
SC_KG = 16


def _sc_mesh():
    return plsc.VectorSubcoreMesh(core_axis_name="c", subcore_axis_name="s",
                                  num_cores=SC_CORES, num_subcores=SC_SUBCORES)


def _sc_token_loop(table_hbm, idx_hbm, side_hbm, out_hbm, idx_v, side_v, rows_v, out_v, sem,
                   tokens_per_worker, compute):
    wid = lax.axis_index("s") * SC_CORES + lax.axis_index("c")

    def gather(t, c, b):
        return pltpu.make_async_copy(table_hbm.at[idx_v.at[t, c]], rows_v.at[b], sem.at[b])

    @pl.loop(0, tokens_per_worker // SC_TOKB)
    def _(bi):
        tok0 = wid * tokens_per_worker + bi * SC_TOKB
        pltpu.sync_copy(idx_hbm.at[pl.ds(tok0, SC_TOKB)], idx_v)
        pltpu.sync_copy(side_hbm.at[pl.ds(tok0, SC_TOKB)], side_v)
        gather(0, 0, 0).start()

        @pl.loop(0, SC_TOKB)
        def _(t):
            for c in range(SC_CHUNKS):
                b = c % 2
                gather(t, c, b).wait()
                if c + 1 < SC_CHUNKS:
                    gather(t, c + 1, 1 - b).start()
                else:
                    @pl.when(t + 1 < SC_TOKB)
                    def _():
                        gather(t + 1, 0, 1 - b).start()
                compute(t, c, b)

        pltpu.sync_copy(out_v, out_hbm.at[pl.ds(tok0, SC_TOKB)])


def _peer_down(idx, ht, down):
    T = ht.shape[0]
    tpw = T // SC_WORKERS
    L = SC_LANES

    @functools.partial(
        pl.kernel, mesh=_sc_mesh(),
        out_type=jax.ShapeDtypeStruct((T, N_SEL), F32),
        scratch_types=[
            pltpu.VMEM((SC_TOKB, SC_CHUNKS, SC_ROWS), jnp.int32),
            pltpu.VMEM((SC_TOKB, D_MODEL), F32),
            pltpu.VMEM((2, SC_ROWS, D_MODEL), F32),
            pltpu.VMEM((SC_ROWS, L), F32),
            pltpu.VMEM((SC_TOKB, N_SEL), F32),
            pltpu.SemaphoreType.DMA((2,)),
        ],
        compiler_params=pltpu.CompilerParams(needs_layout_passes=False),
        name="peer_down",
    )
    def k(idx_hbm, h_hbm, down_hbm, s_hbm, idx_v, h_v, rows_v, red_v, s_v, sem):
        lane = lax.iota(jnp.int32, L)

        def compute(t, c, b):
            def kbody(kk, accs):
                hk = h_v[t, pl.ds(kk * L, L)]
                return tuple(accs[r] + rows_v[b, r, pl.ds(kk * L, L)] * hk for r in range(SC_ROWS))
            accs = lax.fori_loop(0, D_MODEL // L, kbody,
                                 tuple(jnp.zeros((L,), F32) for _ in range(SC_ROWS)))
            for r in range(SC_ROWS):
                red_v[r, :] = accs[r]
            for q in range(SC_ROWS // L):
                tot = jnp.zeros((L,), F32)
                for j in range(L):
                    tot = tot + plsc.load_gather(red_v, [lane + q * L, jnp.full((L,), j, jnp.int32)])
                s_v[t, pl.ds(c * SC_ROWS + q * L, L)] = tot

        _sc_token_loop(down_hbm, idx_hbm, h_hbm, s_hbm, idx_v, h_v, rows_v, s_v, sem, tpw, compute)

    return k(idx.reshape(T, SC_CHUNKS, SC_ROWS), ht, down)


def _peer_up(idx, act, up):
    T = act.shape[0]
    tpw = T // SC_WORKERS
    L = SC_LANES

    @functools.partial(
        pl.kernel, mesh=_sc_mesh(),
        out_type=jax.ShapeDtypeStruct((T, D_MODEL), F32),
        scratch_types=[
            pltpu.VMEM((SC_TOKB, SC_CHUNKS, SC_ROWS), jnp.int32),
            pltpu.VMEM((SC_TOKB, N_SEL), F32),
            pltpu.VMEM((2, SC_ROWS, D_MODEL), F32),
            pltpu.VMEM((SC_TOKB, D_MODEL), F32),
            pltpu.SemaphoreType.DMA((2,)),
        ],
        compiler_params=pltpu.CompilerParams(needs_layout_passes=False),
        name="peer_up",
    )
    def k(idx_hbm, act_hbm, up_hbm, o_hbm, idx_v, act_v, rows_v, o_v, sem):
        def compute(t, c, b):
            tvec = jnp.full((L,), t, jnp.int32)
            for kg in range(D_MODEL // (L * SC_KG)):
                cols = [pl.ds((kg * SC_KG + kk) * L, L) for kk in range(SC_KG)]
                if c == 0:
                    init = tuple(jnp.zeros((L,), F32) for _ in range(SC_KG))
                else:
                    init = tuple(o_v[t, cols[kk]] for kk in range(SC_KG))

                def rbody(r, accs):
                    a = plsc.load_gather(act_v, [tvec, jnp.full((L,), c * SC_ROWS, jnp.int32) + r])
                    return tuple(accs[kk] + rows_v[b, r, cols[kk]] * a for kk in range(SC_KG))
                accs = lax.fori_loop(0, SC_ROWS, rbody, init)
                for kk in range(SC_KG):
                    o_v[t, cols[kk]] = accs[kk]

        _sc_token_loop(up_hbm, idx_hbm, act_hbm, o_hbm, idx_v, act_v, rows_v, o_v, sem, tpw, compute)

    return k(idx.reshape(T, SC_CHUNKS, SC_ROWS), act, up)


def _act_kernel(s_ref, gate_ref, a_ref):
    s = s_ref[...]
    a_ref[...] = 0.5 * s * (1.0 + lax.erf(s * math.sqrt(0.5))) * gate_ref[...]


def _act(s, gate, tm):
    T = s.shape[0]
    spec = pl.BlockSpec((tm, N_SEL), lambda i: (i, 0))
    return pl.pallas_call(
        _act_kernel, grid=(T // tm,), in_specs=[spec, spec], out_specs=spec,
        out_shape=jax.ShapeDtypeStruct((T, N_SEL), F32),
        compiler_params=_cparams(("parallel",)), name="expert_act",
    )(s, gate)


def _final_kernel(x1_ref, p_ref, g_ref, o_ref):
    o_ref[...] = _rms(x1_ref[...] + p_ref[...], g_ref[...])


def _final(x1, peer, g, tm):
    T = x1.shape[0]
    spec = pl.BlockSpec((tm, D_MODEL), lambda i: (i, 0))
    return pl.pallas_call(
        _final_kernel, grid=(T // tm,),
        in_specs=[spec, spec, pl.BlockSpec((1, D_MODEL), lambda i: (0, 0))], out_specs=spec,
        out_shape=jax.ShapeDtypeStruct((T, D_MODEL), F32),
        compiler_params=_cparams(("parallel",)), name="final_norm",
    )(x1, peer, g)


def _tile(n, want):
    t = min(n, want)
    assert n % t == 0
    return t


def _channel_dft():
    ang = 2.0 * np.pi * np.outer(np.arange(GDIM), np.arange(GDIM)) / GDIM
    m = np.concatenate([np.cos(ang), -np.sin(ang)], axis=1) / math.sqrt(GDIM)
    return jnp.asarray(m.astype(np.float32), BF16)


def _trunk(x, p):
    B, S, _ = x.shape
    T = B * S
    assert T % (SC_WORKERS * SC_TOKB) == 0
    zg, hg = _inproj(x, p["g1"], p["win"], p["cdft"], _tile(S, 512))
    yf = _seq_dft(zg)
    yc = _conv_module(hg, p["conv_w"], p["conv_b"], p["ln_g"], p["ln_b"], p["gn_c"], _tile(S, 512))
    x1, ht, idx, gate = _route(x, yf, yc, p["gn_f"], p["wout"], p["g2"], p["wq"], p["ka"], p["kb"],
                               _tile(S, 256))
    x1 = x1.reshape(T, D_MODEL)
    ht = ht.reshape(T, D_MODEL)
    idx = idx.reshape(T, N_SEL)
    gate = gate.reshape(T, N_SEL)
    s = _peer_down(idx, ht, p["down"])
    act = _act(s, gate, _tile(T, 1024))
    peer = _peer_up(idx, act, p["up"])
    return _final(x1, peer, p["final_g"], _tile(T, 512)).reshape(B, S, D_MODEL)


def kernel(x_prompt, x_sample, norm1_g, w_in, conv_w, conv_b, conv_ln_g, conv_ln_b, gn_fourier_g,
           gn_conv_g, w_out, norm2_g, w_q, keys_a, keys_b, expert_down, expert_up, final_g):
    assert norm1_g.shape[0] == 1
    p = dict(
        g1=norm1_g[0].reshape(1, D_MODEL), win=w_in[0].astype(BF16), cdft=_channel_dft(),
        conv_w=conv_w[0], conv_b=conv_b[0].reshape(1, D_CONV),
        ln_g=conv_ln_g[0].reshape(1, D_CONV), ln_b=conv_ln_b[0].reshape(1, D_CONV),
        gn_f=gn_fourier_g[0].reshape(1, D_FOURIER), gn_c=gn_conv_g[0].reshape(1, D_CONV),
        wout=w_out[0].astype(BF16), g2=norm2_g[0].reshape(1, D_MODEL), wq=w_q[0].astype(BF16),
        ka=keys_a[0].astype(BF16), kb=keys_b[0].astype(BF16),
        down=expert_down[0], up=expert_up[0], final_g=final_g.reshape(1, D_MODEL),
    )
    return _trunk(x_prompt, p), _trunk(x_sample, p)
```

```python
import functools
import math

import numpy as np
import jax
import jax.numpy as jnp
from jax import lax
from jax.experimental import pallas as pl
from jax.experimental.pallas import tpu as pltpu
from jax.experimental.pallas import tpu_sc as plsc

F32 = jnp.float32
BF16 = jnp.bfloat16

D_MODEL = 1024
D_FOURIER = 512
GROUPS = 4
GDIM = 128
D_CONV = 512
CONV_W = 31
CONV_PAD = 15
HEADS = 8
N_KEYS = 128
TOPK = 16
D_HALF = 128
D_QUERY = 256
N_SEL = HEADS * TOPK
EPS = 1e-6

SUBLANES = 8
LANES = 128
VMEM_LIMIT = 56 * 1024 * 1024

SC_CORES = 2
SC_SUBCORES = 16
SC_LANES = 16
SC_WORKERS = SC_CORES * SC_SUBCORES


def _cparams(sem):
    return pltpu.CompilerParams(dimension_semantics=sem, vmem_limit_bytes=VMEM_LIMIT)


def _rms(x, g):
    return x * lax.rsqrt(jnp.mean(x * x, axis=-1, keepdims=True) + EPS) * g


D_WORDS = D_MODEL // 2
HI_MASK = -65536


def _bf16_bits(x):
    return lax.bitcast_convert_type(x.astype(BF16).astype(F32), jnp.int32)


def _pack_halves(x):
    lo = lax.shift_right_logical(_bf16_bits(x[..., :D_WORDS]), 16)
    hi = lax.bitwise_and(_bf16_bits(x[..., D_WORDS:]), HI_MASK)
    return lax.bitwise_or(lo, hi)


def _inproj_kernel(x_ref, g1_ref, win_ref, cdft_ref, zg_ref, hg_ref):
    h = _rms(x_ref[...], g1_ref[...]).astype(BF16)
    z = jnp.dot(h, win_ref[...], preferred_element_type=F32)
    for g in range(GROUPS):
        zg = z[:, g * GDIM:(g + 1) * GDIM].astype(BF16)
        zg_ref[g] = jnp.dot(zg, cdft_ref[...], preferred_element_type=F32)
    a = z[:, D_FOURIER:D_FOURIER + D_CONV]
    gate = z[:, D_FOURIER + D_CONV:]
    hg_ref[...] = a * jax.nn.sigmoid(gate)


def _inproj(x, g1, win_bf, cdft, tm):
    B, S, _ = x.shape
    return pl.pallas_call(
        _inproj_kernel,
        grid=(B, S // tm),
        in_specs=[
            pl.BlockSpec((None, tm, D_MODEL), lambda b, m: (b, m, 0)),
            pl.BlockSpec((1, D_MODEL), lambda b, m: (0, 0)),
            pl.BlockSpec((D_MODEL, D_FOURIER + 2 * D_CONV), lambda b, m: (0, 0)),
            pl.BlockSpec((GDIM, 2 * GDIM), lambda b, m: (0, 0)),
        ],
        out_specs=[
            pl.BlockSpec((None, GROUPS, tm, 2 * GDIM), lambda b, m: (b, 0, m, 0)),
            pl.BlockSpec((None, tm, D_CONV), lambda b, m: (b, m, 0)),
        ],
        out_shape=[
            jax.ShapeDtypeStruct((B, GROUPS, S, 2 * GDIM), F32),
            jax.ShapeDtypeStruct((B, S, D_CONV), F32),
        ],
        compiler_params=_cparams(("parallel", "parallel")),
        name="inproj",
    )(x, g1, win_bf, cdft)


def _fft1_kernel(z_ref, k1_ref, tc_ref, ts_ref, a_ref):
    n1 = z_ref.shape[0]
    rows = n1 * SUBLANES
    x = z_ref[...].reshape(rows, 2 * GDIM).astype(BF16)
    p = jnp.dot(k1_ref[...], x, preferred_element_type=F32)
    pc, ps = p[:rows], p[rows:]
    ar = pc[:, :GDIM] + ps[:, GDIM:]
    ai = pc[:, GDIM:] - ps[:, :GDIM]
    tc, ts = tc_ref[...], ts_ref[...]
    a_ref[:, :, :GDIM] = (ar * tc + ai * ts).reshape(n1, SUBLANES, GDIM)
    a_ref[:, :, GDIM:] = (ai * tc - ar * ts).reshape(n1, SUBLANES, GDIM)


def _fft2_kernel(a_ref, c2_ref, s2_ref, y_ref):
    for r in range(SUBLANES):
        blk = a_ref[r].astype(BF16)
        y = jnp.dot(c2_ref[...], blk[:, :GDIM], preferred_element_type=F32)
        y = y + jnp.dot(s2_ref[...], blk[:, GDIM:], preferred_element_type=F32)
        y_ref[:, r * GDIM:(r + 1) * GDIM] = y


def _split_seq(S):
    n1 = 1 << (int(math.log2(S)) // 2)
    return n1, S // n1


@functools.lru_cache(maxsize=None)
def _fft_tables(S):
    n1, n2 = _split_seq(S)
    k1 = np.arange(n1)
    ang1 = 2.0 * np.pi * np.outer(k1, np.arange(n1)) / n1
    eye = np.eye(SUBLANES)
    kmat = np.concatenate([np.kron(np.cos(ang1), eye), np.kron(np.sin(ang1), eye)], axis=0)
    nn2 = np.arange(n2).reshape(n2 // SUBLANES, 1, SUBLANES)
    angt = 2.0 * np.pi * k1.reshape(1, n1, 1) * nn2 / S
    angt = angt.reshape(n2 // SUBLANES, n1 * SUBLANES, 1)
    tc = np.broadcast_to(np.cos(angt), (n2 // SUBLANES, n1 * SUBLANES, GDIM))
    ts = np.broadcast_to(np.sin(angt), (n2 // SUBLANES, n1 * SUBLANES, GDIM))
    ang2 = 2.0 * np.pi * np.outer(np.arange(n2), np.arange(n2)) / n2
    scale = 1.0 / math.sqrt(S)
    return (kmat.astype(np.float32), np.ascontiguousarray(tc, np.float32),
            np.ascontiguousarray(ts, np.float32),
            (np.cos(ang2) * scale).astype(np.float32), (np.sin(ang2) * scale).astype(np.float32))


def _seq_dft(zg):
    B, G, S, _ = zg.shape
    n1, n2 = _split_seq(S)
    kmat, tc, ts, c2, s2 = _fft_tables(S)
    kmat = jnp.asarray(kmat, BF16)
    c2 = jnp.asarray(c2, BF16)
    s2 = jnp.asarray(s2, BF16)
    nb = n2 // SUBLANES
    z6 = zg.reshape(B, G, n1, nb, SUBLANES, 2 * GDIM)
    blk6 = (None, None, n1, None, SUBLANES, 2 * GDIM)
    a6 = pl.pallas_call(
        _fft1_kernel,
        grid=(B, G, nb),
        in_specs=[
            pl.BlockSpec(blk6, lambda b, g, j: (b, g, 0, j, 0, 0)),
            pl.BlockSpec((2 * n1 * SUBLANES, n1 * SUBLANES), lambda b, g, j: (0, 0)),
            pl.BlockSpec((None, n1 * SUBLANES, GDIM), lambda b, g, j: (j, 0, 0)),
            pl.BlockSpec((None, n1 * SUBLANES, GDIM), lambda b, g, j: (j, 0, 0)),
        ],
        out_specs=pl.BlockSpec(blk6, lambda b, g, j: (b, g, 0, j, 0, 0)),
        out_shape=jax.ShapeDtypeStruct(z6.shape, F32),
        compiler_params=_cparams(("parallel", "parallel", "parallel")),
        name="fft_stage1",
    )(z6, kmat, jnp.asarray(tc), jnp.asarray(ts))
    a5 = a6.reshape(B, G, n1, n2, 2 * GDIM)
    y = pl.pallas_call(
        _fft2_kernel,
        grid=(B, G, n1 // SUBLANES),
        in_specs=[
            pl.BlockSpec((None, None, SUBLANES, n2, 2 * GDIM), lambda b, g, i: (b, g, i, 0, 0)),
            pl.BlockSpec((n2, n2), lambda b, g, i: (0, 0)),
            pl.BlockSpec((n2, n2), lambda b, g, i: (0, 0)),
        ],
        out_specs=pl.BlockSpec((None, None, n2, SUBLANES * GDIM), lambda b, g, i: (b, g, 0, i)),
        out_shape=jax.ShapeDtypeStruct((B, G, n2, n1 * GDIM), F32),
        compiler_params=_cparams(("parallel", "parallel", "parallel")),
        name="fft_stage2",
    )(a5, c2, s2)
    return y.reshape(B, G, S, GDIM)


CONV_HALO = 16
CONV_ROWS = 16


def _conv_kernel(prev_ref, cur_ref, next_ref, w_ref, cb_ref, lg_ref, lb_ref, gc_ref, o_ref, scr):
    ts = cur_ref.shape[0]
    m = pl.program_id(1)
    last = pl.num_programs(1) - 1
    scr[0:CONV_HALO, :] = jnp.where(m > 0, prev_ref[...], 0.0)
    scr[CONV_HALO:CONV_HALO + ts, :] = cur_ref[...]
    scr[CONV_HALO + ts:, :] = jnp.where(m < last, next_ref[...], 0.0)
    off = CONV_HALO - CONV_PAD

    def step(c, carry):
        base = pl.multiple_of(c * CONV_ROWS, CONV_ROWS)
        win = scr[pl.ds(base, CONV_ROWS + 2 * CONV_HALO), :]
        acc = jnp.broadcast_to(cb_ref[...], (CONV_ROWS, D_CONV))
        for k in range(CONV_W):
            acc = acc + win[off + k:off + k + CONV_ROWS, :] * w_ref[k:k + 1, :]
        mu = jnp.mean(acc, axis=-1, keepdims=True)
        d = acc - mu
        var = jnp.mean(d * d, axis=-1, keepdims=True)
        y = d * lax.rsqrt(var + EPS) * lg_ref[...] + lb_ref[...]
        y = y * jax.nn.sigmoid(y)
        o_ref[pl.ds(base, CONV_ROWS), :] = _rms(y, gc_ref[...])
        return carry

    lax.fori_loop(0, ts // CONV_ROWS, step, 0)


def _conv_module(hg, conv_w, conv_b, ln_g, ln_b, gn_c, ts):
    B, S, _ = hg.shape
    hb = ts // CONV_HALO
    nh = S // CONV_HALO
    vec = pl.BlockSpec((1, D_CONV), lambda b, m: (0, 0))
    return pl.pallas_call(
        _conv_kernel,
        grid=(B, S // ts),
        in_specs=[
            pl.BlockSpec((None, CONV_HALO, D_CONV), lambda b, m: (b, jnp.maximum(m * hb - 1, 0), 0)),
            pl.BlockSpec((None, ts, D_CONV), lambda b, m: (b, m, 0)),
            pl.BlockSpec((None, CONV_HALO, D_CONV), lambda b, m: (b, jnp.minimum((m + 1) * hb, nh - 1), 0)),
            pl.BlockSpec((CONV_W, D_CONV), lambda b, m: (0, 0)),
            vec, vec, vec, vec,
        ],
        out_specs=pl.BlockSpec((None, ts, D_CONV), lambda b, m: (b, m, 0)),
        out_shape=jax.ShapeDtypeStruct((B, S, D_CONV), F32),
        scratch_shapes=[pltpu.VMEM((ts + 2 * CONV_HALO, D_CONV), F32)],
        compiler_params=_cparams(("parallel", "parallel")),
        name="conv_module",
    )(hg, hg, hg, conv_w, conv_b, ln_g, ln_b, gn_c)


def _extract_topk(s, order, k, payload=None):
    big = jnp.int32(1 << 30)
    vals, sel = [], []
    for _ in range(k):
        m = jnp.max(s, axis=0, keepdims=True)
        am = jnp.min(jnp.where(s == m, order, big), axis=0, keepdims=True)
        hit = order == am
        vals.append(m)
        if payload is None:
            sel.append(am)
        else:
            sel.append(jnp.max(jnp.where(hit, payload, -1), axis=0, keepdims=True))
        s = jnp.where(hit, -jnp.inf, s)
    return jnp.concatenate(vals, axis=0), jnp.concatenate(sel, axis=0)


_CAND_A = 4
_CAND_B = 3


def _route_kernel(x_ref, yf_ref, yc_ref, gf_ref, wout_ref, g2_ref, wq_ref, ka_ref, kb_ref,
                  x1_ref, htw_ref, idx_ref, gate_ref):
    tm = x_ref.shape[0]
    yf = jnp.concatenate([yf_ref[g] for g in range(GROUPS)], axis=-1)
    yfn = _rms(yf, gf_ref[...])
    ycat = jnp.concatenate([yfn, yc_ref[...]], axis=-1).astype(BF16)
    x1 = x_ref[...] + jnp.dot(ycat, wout_ref[...], preferred_element_type=F32)
    x1_ref[...] = x1
    ht = _rms(x1, g2_ref[...])
    htw_ref[...] = _pack_halves(ht)
    q = jnp.dot(ht.astype(BF16), wq_ref[...], preferred_element_type=F32)

    row16 = lax.broadcasted_iota(jnp.int32, (TOPK, tm), 0)
    key_iota = lax.broadcasted_iota(jnp.int32, (N_KEYS, tm), 0)
    nt = (((1,), (1,)), ((), ()))
    idx_rows, gate_rows = [], []
    for h in range(HEADS):
        qa = q[:, h * D_QUERY:h * D_QUERY + D_HALF].astype(BF16)
        qb = q[:, h * D_QUERY + D_HALF:(h + 1) * D_QUERY].astype(BF16)
        sa = lax.dot_general(ka_ref[h], qa, nt, preferred_element_type=F32)
        sb = lax.dot_general(kb_ref[h], qb, nt, preferred_element_type=F32)
        va, ia = _extract_topk(sa, key_iota, TOPK)
        vb, ib = _extract_topk(sb, key_iota, TOPK)
        cs, ce, co = [], [], []
        for i in range(_CAND_A):
            ok = (i + 1) * (row16 + 1) <= TOPK
            cs.append(jnp.where(ok, va[i:i + 1] + vb, -jnp.inf))
            ce.append(ia[i:i + 1] * N_KEYS + ib)
            co.append(i * TOPK + row16)
        for j in range(_CAND_B):
            ok = jnp.logical_and(row16 >= _CAND_A, (row16 + 1) * (j + 1) <= TOPK)
            cs.append(jnp.where(ok, va + vb[j:j + 1], -jnp.inf))
            ce.append(ia * N_KEYS + ib[j:j + 1])
            co.append(row16 * TOPK + j)
        top_s, top_e = _extract_topk(jnp.concatenate(cs, axis=0), jnp.concatenate(co, axis=0),
                                     TOPK, payload=jnp.concatenate(ce, axis=0))
        ex = jnp.exp(top_s - top_s[0:1])
        gate_rows.append(ex / jnp.sum(ex, axis=0, keepdims=True))
        idx_rows.append(top_e)
    idx_ref[...] = jnp.concatenate(idx_rows, axis=0).T
    gate_ref[...] = jnp.concatenate(gate_rows, axis=0).T


def _route(x, yf, yc, gf, wout_bf, g2, wq_bf, ka_bf, kb_bf, tm):
    B, S, _ = x.shape
    const2 = lambda b, m: (0, 0)
    const3 = lambda b, m: (0, 0, 0)
    row = lambda w: pl.BlockSpec((None, tm, w), lambda b, m: (b, m, 0))
    return pl.pallas_call(
        _route_kernel,
        grid=(B, S // tm),
        in_specs=[
            row(D_MODEL),
            pl.BlockSpec((None, GROUPS, tm, GDIM), lambda b, m: (b, 0, m, 0)),
            row(D_CONV),
            pl.BlockSpec((1, D_FOURIER), const2),
            pl.BlockSpec((D_MODEL, D_MODEL), const2),
            pl.BlockSpec((1, D_MODEL), const2),
            pl.BlockSpec((D_MODEL, HEADS * D_QUERY), const2),
            pl.BlockSpec((HEADS, N_KEYS, D_HALF), const3),
            pl.BlockSpec((HEADS, N_KEYS, D_HALF), const3),
        ],
        out_specs=[row(D_MODEL), row(D_WORDS), row(N_SEL), row(N_SEL)],
        out_shape=[
            jax.ShapeDtypeStruct((B, S, D_MODEL), F32),
            jax.ShapeDtypeStruct((B, S, D_WORDS), jnp.int32),
            jax.ShapeDtypeStruct((B, S, N_SEL), jnp.int32),
            jax.ShapeDtypeStruct((B, S, N_SEL), F32),
        ],
        compiler_params=_cparams(("parallel", "parallel")),
        name="route",
    )(x, yf, yc, gf, wout_bf, g2, wq_bf, ka_bf, kb_bf)


SC_ROWS = 32
SC_CHUNKS = N_SEL // SC_ROWS
SC_TOKB = 16
SC_WCH = D_WORDS // SC_LANES
SC_QUAD = 4
SC_PASS = 16
SC_KG = 8


def _sc_mesh():
    return plsc.VectorSubcoreMesh(core_axis_name="c", subcore_axis_name="s",
                                  num_cores=SC_CORES, num_subcores=SC_SUBCORES)


def _sc_bf16(words):
    return plsc.bitcast(words, BF16)


def _sc_widen(pairs):
    w = plsc.bitcast(pairs, jnp.int32)
    lo = plsc.bitcast(lax.shift_left(w, 16), F32)
    hi = plsc.bitcast(lax.bitwise_and(w, jnp.int32(HI_MASK)), F32)
    return lo, hi


def _sc_token_loop(table_hbm, idx_hbm, side_hbm, out_hbm, idx_v, side_v, rows_v, out_v, sem,
                   tokens_per_worker, compute):
    wid = lax.axis_index("s") * SC_CORES + lax.axis_index("c")

    def gather(t, c):
        return pltpu.make_async_copy(table_hbm.at[idx_v.at[t, c]], rows_v.at[c], sem.at[c])

    @pl.loop(0, tokens_per_worker // SC_TOKB)
    def _(bi):
        tok0 = wid * tokens_per_worker + bi * SC_TOKB
        pltpu.sync_copy(idx_hbm.at[pl.ds(tok0, SC_TOKB)], idx_v)
        pltpu.sync_copy(side_hbm.at[pl.ds(tok0, SC_TOKB)], side_v)
        for c in range(SC_CHUNKS - 1):
            gather(0, c).start()

        @pl.loop(0, SC_TOKB)
        def _(t):
            for c in range(SC_CHUNKS):
                gather(t, c).wait()
                nxt = (c + SC_CHUNKS - 1) % SC_CHUNKS
                if c == 0:
                    gather(t, nxt).start()
                else:
                    @pl.when(t + 1 < SC_TOKB)
                    def _():
                        gather(t + 1, nxt).start()
                compute(t, c)

        pltpu.sync_copy(out_v, out_hbm.at[pl.ds(tok0, SC_TOKB)])


def _peer_down(idx, htw, down_w):
    T = htw.shape[0]
    tpw = T // SC_WORKERS
    L = SC_LANES

    @functools.partial(
        pl.kernel, mesh=_sc_mesh(),
        out_type=jax.ShapeDtypeStruct((T, N_SEL), F32),
        scratch_types=[
            pltpu.VMEM((SC_TOKB, SC_CHUNKS, SC_ROWS), jnp.int32),
            pltpu.VMEM((SC_TOKB, D_WORDS), jnp.int32),
            pltpu.VMEM((SC_CHUNKS, SC_ROWS, D_WORDS), jnp.int32),
            pltpu.VMEM((SC_PASS, L), F32),
            pltpu.VMEM((SC_TOKB, N_SEL), F32),
            pltpu.SemaphoreType.DMA((SC_CHUNKS,)),
        ],
        compiler_params=pltpu.CompilerParams(needs_layout_passes=False),
        name="peer_down",
    )
    def k(idx_hbm, h_hbm, down_hbm, s_hbm, idx_v, h_v, rows_v, red_v, s_v, sem):
        lane = lax.iota(jnp.int32, L)

        def compute(t, c):
            for ps in range(SC_ROWS // SC_PASS):
                def qbody(q, accs):
                    cols = [pl.ds((q * SC_QUAD + u) * L, L) for u in range(SC_QUAD)]
                    hs = [_sc_bf16(h_v[t, cols[u]]) for u in range(SC_QUAD)]
                    out = []
                    for r in range(SC_PASS):
                        part = None
                        for u in range(SC_QUAD):
                            prod = _sc_bf16(rows_v[c, ps * SC_PASS + r, cols[u]]) * hs[u]
                            part = prod if part is None else part + prod
                        lo, hi = _sc_widen(part)
                        out.append(accs[r] + lo + hi)
                    return tuple(out)
                accs = lax.fori_loop(0, SC_WCH // SC_QUAD, qbody,
                                     tuple(jnp.zeros((L,), F32) for _ in range(SC_PASS)))
                for r in range(SC_PASS):
                    red_v[r, :] = accs[r]
                tot = jnp.zeros((L,), F32)
                for j in range(L):
                    tot = tot + plsc.load_gather(red_v, [lane, jnp.full((L,), j, jnp.int32)])
                s_v[t, pl.ds(c * SC_ROWS + ps * SC_PASS, L)] = tot

        _sc_token_loop(down_hbm, idx_hbm, h_hbm, s_hbm, idx_v, h_v, rows_v, s_v, sem, tpw, compute)

    return k(idx.reshape(T, SC_CHUNKS, SC_ROWS), htw, down_w)


def _peer_up(idx, act_w, up_w):
    T = act_w.shape[0]
    tpw = T // SC_WORKERS
    L = SC_LANES
    RG = SC_QUAD

    @functools.partial(
        pl.kernel, mesh=_sc_mesh(),
        out_type=jax.ShapeDtypeStruct((T, D_MODEL), F32),
        scratch_types=[
            pltpu.VMEM((SC_TOKB, SC_CHUNKS, SC_ROWS), jnp.int32),
            pltpu.VMEM((SC_TOKB, N_SEL), jnp.int32),
            pltpu.VMEM((SC_CHUNKS, SC_ROWS, D_WORDS), jnp.int32),
            pltpu.VMEM((SC_TOKB, D_MODEL), F32),
            pltpu.SemaphoreType.DMA((SC_CHUNKS,)),
        ],
        compiler_params=pltpu.CompilerParams(needs_layout_passes=False),
        name="peer_up",
    )
    def k(idx_hbm, act_hbm, up_hbm, o_hbm, idx_v, act_v, rows_v, o_v, sem):
        def compute(t, c):
            tvec = jnp.full((L,), t, jnp.int32)
            for kg in range(SC_WCH // SC_KG):
                lo_cols = [pl.ds((kg * SC_KG + kk) * L, L) for kk in range(SC_KG)]
                hi_cols = [pl.ds(D_WORDS + (kg * SC_KG + kk) * L, L) for kk in range(SC_KG)]
                if c == 0:
                    init = tuple(jnp.zeros((L,), F32) for _ in range(2 * SC_KG))
                else:
                    init = tuple(o_v[t, col] for col in lo_cols + hi_cols)

                def rgbody(rg, accs):
                    first = jnp.full((L,), c * SC_ROWS, jnp.int32) + rg * RG
                    a = [_sc_bf16(plsc.load_gather(act_v, [tvec, first + u])) for u in range(RG)]
                    out_lo, out_hi = [], []
                    for kk in range(SC_KG):
                        part = None
                        for u in range(RG):
                            prod = _sc_bf16(rows_v[c, rg * RG + u, lo_cols[kk]]) * a[u]
                            part = prod if part is None else part + prod
                        lo, hi = _sc_widen(part)
                        out_lo.append(accs[kk] + lo)
                        out_hi.append(accs[SC_KG + kk] + hi)
                    return tuple(out_lo + out_hi)
                accs = lax.fori_loop(0, SC_ROWS // RG, rgbody, init)
                for kk in range(SC_KG):
                    o_v[t, lo_cols[kk]] = accs[kk]
                    o_v[t, hi_cols[kk]] = accs[SC_KG + kk]

        _sc_token_loop(up_hbm, idx_hbm, act_hbm, o_hbm, idx_v, act_v, rows_v, o_v, sem, tpw, compute)

    return k(idx.reshape(T, SC_CHUNKS, SC_ROWS), act_w, up_w)


def _act_kernel(s_ref, gate_ref, a_ref):
    s = s_ref[...]
    act = 0.5 * s * (1.0 + lax.erf(s * math.sqrt(0.5))) * gate_ref[...]
    bits = _bf16_bits(act)
    a_ref[...] = lax.bitwise_or(lax.shift_right_logical(bits, 16), bits)


def _act(s, gate, tm):
    T = s.shape[0]
    spec = pl.BlockSpec((tm, N_SEL), lambda i: (i, 0))
    return pl.pallas_call(
        _act_kernel, grid=(T // tm,), in_specs=[spec, spec], out_specs=spec,
        out_shape=jax.ShapeDtypeStruct((T, N_SEL), jnp.int32),
        compiler_params=_cparams(("parallel",)), name="expert_act",
    )(s, gate)


def _final_kernel(x1_ref, p_ref, g_ref, o_ref):
    o_ref[...] = _rms(x1_ref[...] + p_ref[...], g_ref[...])


def _final(x1, peer, g, tm):
    T = x1.shape[0]
    spec = pl.BlockSpec((tm, D_MODEL), lambda i: (i, 0))
    return pl.pallas_call(
        _final_kernel, grid=(T // tm,),
        in_specs=[spec, spec, pl.BlockSpec((1, D_MODEL), lambda i: (0, 0))], out_specs=spec,
        out_shape=jax.ShapeDtypeStruct((T, D_MODEL), F32),
        compiler_params=_cparams(("parallel",)), name="final_norm",
    )(x1, peer, g)


def _tile(n, want):
    t = min(n, want)
    assert n % t == 0
    return t


def _channel_dft():
    ang = 2.0 * np.pi * np.outer(np.arange(GDIM), np.arange(GDIM)) / GDIM
    m = np.concatenate([np.cos(ang), -np.sin(ang)], axis=1) / math.sqrt(GDIM)
    return jnp.asarray(m.astype(np.float32), BF16)


def _trunk(x, p):
    B, S, _ = x.shape
    T = B * S
    assert T % (SC_WORKERS * SC_TOKB) == 0
    zg, hg = _inproj(x, p["g1"], p["win"], p["cdft"], _tile(S, 512))
    yf = _seq_dft(zg)
    yc = _conv_module(hg, p["conv_w"], p["conv_b"], p["ln_g"], p["ln_b"], p["gn_c"], _tile(S, 512))
    x1, htw, idx, gate = _route(x, yf, yc, p["gn_f"], p["wout"], p["g2"], p["wq"], p["ka"], p["kb"],
                                _tile(S, 256))
    x1 = x1.reshape(T, D_MODEL)
    htw = htw.reshape(T, D_WORDS)
    idx = idx.reshape(T, N_SEL)
    gate = gate.reshape(T, N_SEL)
    s = _peer_down(idx, htw, p["down"])
    act = _act(s, gate, _tile(T, 1024))
    peer = _peer_up(idx, act, p["up"])
    return _final(x1, peer, p["final_g"], _tile(T, 512)).reshape(B, S, D_MODEL)


def kernel(x_prompt, x_sample, norm1_g, w_in, conv_w, conv_b, conv_ln_g, conv_ln_b, gn_fourier_g,
           gn_conv_g, w_out, norm2_g, w_q, keys_a, keys_b, expert_down, expert_up, final_g):
    assert norm1_g.shape[0] == 1
    p = dict(
        g1=norm1_g[0].reshape(1, D_MODEL), win=w_in[0].astype(BF16), cdft=_channel_dft(),
        conv_w=conv_w[0], conv_b=conv_b[0].reshape(1, D_CONV),
        ln_g=conv_ln_g[0].reshape(1, D_CONV), ln_b=conv_ln_b[0].reshape(1, D_CONV),
        gn_f=gn_fourier_g[0].reshape(1, D_FOURIER), gn_c=gn_conv_g[0].reshape(1, D_CONV),
        wout=w_out[0].astype(BF16), g2=norm2_g[0].reshape(1, D_MODEL), wq=w_q[0].astype(BF16),
        ka=keys_a[0].astype(BF16), kb=keys_b[0].astype(BF16),
        down=_pack_halves(expert_down[0]), up=_pack_halves(expert_up[0]),
        final_g=final_g.reshape(1, D_MODEL),
    )
    return _trunk(x_prompt, p), _trunk(x_sample, p)
```

```python
import functools
import math

import numpy as np
import jax
import jax.numpy as jnp
from jax import lax
from jax.experimental import pallas as pl
from jax.experimental.pallas import tpu as pltpu
from jax.experimental.pallas import tpu_sc as plsc

F32 = jnp.float32
BF16 = jnp.bfloat16

D_MODEL = 1024
D_FOURIER = 512
GROUPS = 4
GDIM = 128
D_CONV = 512
CONV_W = 31
CONV_PAD = 15
HEADS = 8
N_KEYS = 128
TOPK = 16
D_HALF = 128
D_QUERY = 256
N_SEL = HEADS * TOPK
EPS = 1e-6

SUBLANES = 8
LANES = 128
VMEM_LIMIT = 56 * 1024 * 1024

SC_CORES = 2
SC_SUBCORES = 16
SC_LANES = 16
SC_WORKERS = SC_CORES * SC_SUBCORES


def _cparams(sem):
    return pltpu.CompilerParams(dimension_semantics=sem, vmem_limit_bytes=VMEM_LIMIT)


def _rms(x, g):
    return x * lax.rsqrt(jnp.mean(x * x, axis=-1, keepdims=True) + EPS) * g


D_WORDS = D_MODEL // 2
HI_MASK = -65536


def _bf16_bits(x):
    return lax.bitcast_convert_type(x.astype(BF16).astype(F32), jnp.int32)


def _pack_halves(x):
    lo = lax.shift_right_logical(_bf16_bits(x[..., :D_WORDS]), 16)
    hi = lax.bitwise_and(_bf16_bits(x[..., D_WORDS:]), HI_MASK)
    return lax.bitwise_or(lo, hi)


def _inproj_kernel(x_ref, g1_ref, win_ref, cdft_ref, zg_ref, hg_ref):
    h = _rms(x_ref[...], g1_ref[...]).astype(BF16)
    z = jnp.dot(h, win_ref[...], preferred_element_type=F32)
    for g in range(GROUPS):
        zg = z[:, g * GDIM:(g + 1) * GDIM].astype(BF16)
        zg_ref[g] = jnp.dot(zg, cdft_ref[...], preferred_element_type=F32)
    a = z[:, D_FOURIER:D_FOURIER + D_CONV]
    gate = z[:, D_FOURIER + D_CONV:]
    hg_ref[...] = a * jax.nn.sigmoid(gate)


def _inproj(x, g1, win_bf, cdft, tm):
    B, S, _ = x.shape
    return pl.pallas_call(
        _inproj_kernel,
        grid=(B, S // tm),
        in_specs=[
            pl.BlockSpec((None, tm, D_MODEL), lambda b, m: (b, m, 0)),
            pl.BlockSpec((1, D_MODEL), lambda b, m: (0, 0)),
            pl.BlockSpec((D_MODEL, D_FOURIER + 2 * D_CONV), lambda b, m: (0, 0)),
            pl.BlockSpec((GDIM, 2 * GDIM), lambda b, m: (0, 0)),
        ],
        out_specs=[
            pl.BlockSpec((None, GROUPS, tm, 2 * GDIM), lambda b, m: (b, 0, m, 0)),
            pl.BlockSpec((None, tm, D_CONV), lambda b, m: (b, m, 0)),
        ],
        out_shape=[
            jax.ShapeDtypeStruct((B, GROUPS, S, 2 * GDIM), F32),
            jax.ShapeDtypeStruct((B, S, D_CONV), F32),
        ],
        compiler_params=_cparams(("parallel", "parallel")),
        name="inproj",
    )(x, g1, win_bf, cdft)


def _fft1_kernel(z_ref, k1_ref, tc_ref, ts_ref, a_ref):
    n1 = z_ref.shape[0]
    rows = n1 * SUBLANES
    x = z_ref[...].reshape(rows, 2 * GDIM).astype(BF16)
    p = jnp.dot(k1_ref[...], x, preferred_element_type=F32)
    pc, ps = p[:rows], p[rows:]
    ar = pc[:, :GDIM] + ps[:, GDIM:]
    ai = pc[:, GDIM:] - ps[:, :GDIM]
    tc, ts = tc_ref[...], ts_ref[...]
    a_ref[:, :, :GDIM] = (ar * tc + ai * ts).reshape(n1, SUBLANES, GDIM)
    a_ref[:, :, GDIM:] = (ai * tc - ar * ts).reshape(n1, SUBLANES, GDIM)


def _fft2_kernel(a_ref, c2_ref, s2_ref, y_ref):
    for r in range(SUBLANES):
        blk = a_ref[r].astype(BF16)
        y = jnp.dot(c2_ref[...], blk[:, :GDIM], preferred_element_type=F32)
        y = y + jnp.dot(s2_ref[...], blk[:, GDIM:], preferred_element_type=F32)
        y_ref[:, r * GDIM:(r + 1) * GDIM] = y


def _split_seq(S):
    n1 = 1 << (int(math.log2(S)) // 2)
    return n1, S // n1


@functools.lru_cache(maxsize=None)
def _fft_tables(S):
    n1, n2 = _split_seq(S)
    k1 = np.arange(n1)
    ang1 = 2.0 * np.pi * np.outer(k1, np.arange(n1)) / n1
    eye = np.eye(SUBLANES)
    kmat = np.concatenate([np.kron(np.cos(ang1), eye), np.kron(np.sin(ang1), eye)], axis=0)
    nn2 = np.arange(n2).reshape(n2 // SUBLANES, 1, SUBLANES)
    angt = 2.0 * np.pi * k1.reshape(1, n1, 1) * nn2 / S
    angt = angt.reshape(n2 // SUBLANES, n1 * SUBLANES, 1)
    tc = np.broadcast_to(np.cos(angt), (n2 // SUBLANES, n1 * SUBLANES, GDIM))
    ts = np.broadcast_to(np.sin(angt), (n2 // SUBLANES, n1 * SUBLANES, GDIM))
    ang2 = 2.0 * np.pi * np.outer(np.arange(n2), np.arange(n2)) / n2
    scale = 1.0 / math.sqrt(S)
    return (kmat.astype(np.float32), np.ascontiguousarray(tc, np.float32),
            np.ascontiguousarray(ts, np.float32),
            (np.cos(ang2) * scale).astype(np.float32), (np.sin(ang2) * scale).astype(np.float32))


def _seq_dft(zg):
    B, G, S, _ = zg.shape
    n1, n2 = _split_seq(S)
    kmat, tc, ts, c2, s2 = _fft_tables(S)
    kmat = jnp.asarray(kmat, BF16)
    c2 = jnp.asarray(c2, BF16)
    s2 = jnp.asarray(s2, BF16)
    nb = n2 // SUBLANES
    z6 = zg.reshape(B, G, n1, nb, SUBLANES, 2 * GDIM)
    blk6 = (None, None, n1, None, SUBLANES, 2 * GDIM)
    a6 = pl.pallas_call(
        _fft1_kernel,
        grid=(B, G, nb),
        in_specs=[
            pl.BlockSpec(blk6, lambda b, g, j: (b, g, 0, j, 0, 0)),
            pl.BlockSpec((2 * n1 * SUBLANES, n1 * SUBLANES), lambda b, g, j: (0, 0)),
            pl.BlockSpec((None, n1 * SUBLANES, GDIM), lambda b, g, j: (j, 0, 0)),
            pl.BlockSpec((None, n1 * SUBLANES, GDIM), lambda b, g, j: (j, 0, 0)),
        ],
        out_specs=pl.BlockSpec(blk6, lambda b, g, j: (b, g, 0, j, 0, 0)),
        out_shape=jax.ShapeDtypeStruct(z6.shape, F32),
        compiler_params=_cparams(("parallel", "parallel", "parallel")),
        name="fft_stage1",
    )(z6, kmat, jnp.asarray(tc), jnp.asarray(ts))
    a5 = a6.reshape(B, G, n1, n2, 2 * GDIM)
    y = pl.pallas_call(
        _fft2_kernel,
        grid=(B, G, n1 // SUBLANES),
        in_specs=[
            pl.BlockSpec((None, None, SUBLANES, n2, 2 * GDIM), lambda b, g, i: (b, g, i, 0, 0)),
            pl.BlockSpec((n2, n2), lambda b, g, i: (0, 0)),
            pl.BlockSpec((n2, n2), lambda b, g, i: (0, 0)),
        ],
        out_specs=pl.BlockSpec((None, None, n2, SUBLANES * GDIM), lambda b, g, i: (b, g, 0, i)),
        out_shape=jax.ShapeDtypeStruct((B, G, n2, n1 * GDIM), F32),
        compiler_params=_cparams(("parallel", "parallel", "parallel")),
        name="fft_stage2",
    )(a5, c2, s2)
    return y.reshape(B, G, S, GDIM)


CONV_HALO = 16
CONV_ROWS = 16


def _conv_kernel(prev_ref, cur_ref, next_ref, w_ref, cb_ref, lg_ref, lb_ref, gc_ref, o_ref, scr):
    ts = cur_ref.shape[0]
    m = pl.program_id(1)
    last = pl.num_programs(1) - 1
    scr[0:CONV_HALO, :] = jnp.where(m > 0, prev_ref[...], 0.0)
    scr[CONV_HALO:CONV_HALO + ts, :] = cur_ref[...]
    scr[CONV_HALO + ts:, :] = jnp.where(m < last, next_ref[...], 0.0)
    off = CONV_HALO - CONV_PAD

    def step(c, carry):
        base = pl.multiple_of(c * CONV_ROWS, CONV_ROWS)
        win = scr[pl.ds(base, CONV_ROWS + 2 * CONV_HALO), :]
        acc = jnp.broadcast_to(cb_ref[...], (CONV_ROWS, D_CONV))
        for k in range(CONV_W):
            acc = acc + win[off + k:off + k + CONV_ROWS, :] * w_ref[k:k + 1, :]
        mu = jnp.mean(acc, axis=-1, keepdims=True)
        d = acc - mu
        var = jnp.mean(d * d, axis=-1, keepdims=True)
        y = d * lax.rsqrt(var + EPS) * lg_ref[...] + lb_ref[...]
        y = y * jax.nn.sigmoid(y)
        o_ref[pl.ds(base, CONV_ROWS), :] = _rms(y, gc_ref[...])
        return carry

    lax.fori_loop(0, ts // CONV_ROWS, step, 0)


def _conv_module(hg, conv_w, conv_b, ln_g, ln_b, gn_c, ts):
    B, S, _ = hg.shape
    hb = ts // CONV_HALO
    nh = S // CONV_HALO
    vec = pl.BlockSpec((1, D_CONV), lambda b, m: (0, 0))
    return pl.pallas_call(
        _conv_kernel,
        grid=(B, S // ts),
        in_specs=[
            pl.BlockSpec((None, CONV_HALO, D_CONV), lambda b, m: (b, jnp.maximum(m * hb - 1, 0), 0)),
            pl.BlockSpec((None, ts, D_CONV), lambda b, m: (b, m, 0)),
            pl.BlockSpec((None, CONV_HALO, D_CONV), lambda b, m: (b, jnp.minimum((m + 1) * hb, nh - 1), 0)),
            pl.BlockSpec((CONV_W, D_CONV), lambda b, m: (0, 0)),
            vec, vec, vec, vec,
        ],
        out_specs=pl.BlockSpec((None, ts, D_CONV), lambda b, m: (b, m, 0)),
        out_shape=jax.ShapeDtypeStruct((B, S, D_CONV), F32),
        scratch_shapes=[pltpu.VMEM((ts + 2 * CONV_HALO, D_CONV), F32)],
        compiler_params=_cparams(("parallel", "parallel")),
        name="conv_module",
    )(hg, hg, hg, conv_w, conv_b, ln_g, ln_b, gn_c)


def _extract_topk(s, order, k, payload=None):
    big = jnp.int32(1 << 30)
    vals, sel = [], []
    for _ in range(k):
        m = jnp.max(s, axis=0, keepdims=True)
        am = jnp.min(jnp.where(s == m, order, big), axis=0, keepdims=True)
        hit = order == am
        vals.append(m)
        if payload is None:
            sel.append(am)
        else:
            sel.append(jnp.max(jnp.where(hit, payload, -1), axis=0, keepdims=True))
        s = jnp.where(hit, -jnp.inf, s)
    return jnp.concatenate(vals, axis=0), jnp.concatenate(sel, axis=0)


_CAND_A = 4
_CAND_B = 3


def _route_kernel(x_ref, yf_ref, yc_ref, gf_ref, wout_ref, g2_ref, wq_ref, ka_ref, kb_ref,
                  x1_ref, htw_ref, idx_ref, gate_ref):
    tm = x_ref.shape[0]
    yf = jnp.concatenate([yf_ref[g] for g in range(GROUPS)], axis=-1)
    yfn = _rms(yf, gf_ref[...])
    ycat = jnp.concatenate([yfn, yc_ref[...]], axis=-1).astype(BF16)
    x1 = x_ref[...] + jnp.dot(ycat, wout_ref[...], preferred_element_type=F32)
    x1_ref[...] = x1
    ht = _rms(x1, g2_ref[...])
    htw_ref[...] = _pack_halves(ht)
    q = jnp.dot(ht.astype(BF16), wq_ref[...], preferred_element_type=F32)

    row16 = lax.broadcasted_iota(jnp.int32, (TOPK, tm), 0)
    key_iota = lax.broadcasted_iota(jnp.int32, (N_KEYS, tm), 0)
    nt = (((1,), (1,)), ((), ()))
    idx_rows, gate_rows = [], []
    for h in range(HEADS):
        qa = q[:, h * D_QUERY:h * D_QUERY + D_HALF].astype(BF16)
        qb = q[:, h * D_QUERY + D_HALF:(h + 1) * D_QUERY].astype(BF16)
        sa = lax.dot_general(ka_ref[h], qa, nt, preferred_element_type=F32)
        sb = lax.dot_general(kb_ref[h], qb, nt, preferred_element_type=F32)
        va, ia = _extract_topk(sa, key_iota, TOPK)
        vb, ib = _extract_topk(sb, key_iota, TOPK)
        cs, ce, co = [], [], []
        for i in range(_CAND_A):
            ok = (i + 1) * (row16 + 1) <= TOPK
            cs.append(jnp.where(ok, va[i:i + 1] + vb, -jnp.inf))
            ce.append(ia[i:i + 1] * N_KEYS + ib)
            co.append(i * TOPK + row16)
        for j in range(_CAND_B):
            ok = jnp.logical_and(row16 >= _CAND_A, (row16 + 1) * (j + 1) <= TOPK)
            cs.append(jnp.where(ok, va + vb[j:j + 1], -jnp.inf))
            ce.append(ia * N_KEYS + ib[j:j + 1])
            co.append(row16 * TOPK + j)
        top_s, top_e = _extract_topk(jnp.concatenate(cs, axis=0), jnp.concatenate(co, axis=0),
                                     TOPK, payload=jnp.concatenate(ce, axis=0))
        ex = jnp.exp(top_s - top_s[0:1])
        gate_rows.append(ex / jnp.sum(ex, axis=0, keepdims=True))
        idx_rows.append(top_e)
    idx_ref[...] = jnp.concatenate(idx_rows, axis=0).T
    gate_ref[...] = jnp.concatenate(gate_rows, axis=0).T


def _route(x, yf, yc, gf, wout_bf, g2, wq_bf, ka_bf, kb_bf, tm):
    B, S, _ = x.shape
    const2 = lambda b, m: (0, 0)
    const3 = lambda b, m: (0, 0, 0)
    row = lambda w: pl.BlockSpec((None, tm, w), lambda b, m: (b, m, 0))
    return pl.pallas_call(
        _route_kernel,
        grid=(B, S // tm),
        in_specs=[
            row(D_MODEL),
            pl.BlockSpec((None, GROUPS, tm, GDIM), lambda b, m: (b, 0, m, 0)),
            row(D_CONV),
            pl.BlockSpec((1, D_FOURIER), const2),
            pl.BlockSpec((D_MODEL, D_MODEL), const2),
            pl.BlockSpec((1, D_MODEL), const2),
            pl.BlockSpec((D_MODEL, HEADS * D_QUERY), const2),
            pl.BlockSpec((HEADS, N_KEYS, D_HALF), const3),
            pl.BlockSpec((HEADS, N_KEYS, D_HALF), const3),
        ],
        out_specs=[row(D_MODEL), row(D_WORDS), row(N_SEL), row(N_SEL)],
        out_shape=[
            jax.ShapeDtypeStruct((B, S, D_MODEL), F32),
            jax.ShapeDtypeStruct((B, S, D_WORDS), jnp.int32),
            jax.ShapeDtypeStruct((B, S, N_SEL), jnp.int32),
            jax.ShapeDtypeStruct((B, S, N_SEL), F32),
        ],
        compiler_params=_cparams(("parallel", "parallel")),
        name="route",
    )(x, yf, yc, gf, wout_bf, g2, wq_bf, ka_bf, kb_bf)


SC_ROWS = 32
SC_CHUNKS = N_SEL // SC_ROWS
SC_TOKB = 16
SC_WCH = D_WORDS // SC_LANES
SC_QUAD = 4
SC_PASS = 16
SC_KG = 8


def _sc_mesh():
    return plsc.VectorSubcoreMesh(core_axis_name="c", subcore_axis_name="s",
                                  num_cores=SC_CORES, num_subcores=SC_SUBCORES)


def _sc_cost(tokens):
    rows = tokens * N_SEL
    return pl.CostEstimate(flops=2 * rows * D_MODEL, transcendentals=0, bytes_accessed=rows * D_WORDS * 4)


def _sc_bf16(words):
    return plsc.bitcast(words, BF16)


def _sc_widen(pairs):
    w = plsc.bitcast(pairs, jnp.int32)
    lo = plsc.bitcast(lax.shift_left(w, 16), F32)
    hi = plsc.bitcast(lax.bitwise_and(w, jnp.int32(HI_MASK)), F32)
    return lo, hi


def _sc_token_loop(table_hbm, idx_hbm, side_hbm, out_hbm, idx_v, side_v, rows_v, out_v, sem,
                   tokens_per_worker, compute):
    wid = lax.axis_index("s") * SC_CORES + lax.axis_index("c")

    def gather(t, c):
        rows = idx_v.at[t, pl.ds(c * SC_ROWS, SC_ROWS)]
        return pltpu.make_async_copy(table_hbm.at[rows], rows_v.at[c], sem.at[c])

    @pl.loop(0, tokens_per_worker // SC_TOKB)
    def _(bi):
        tok0 = wid * tokens_per_worker + bi * SC_TOKB
        pltpu.sync_copy(idx_hbm.at[pl.ds(tok0, SC_TOKB)], idx_v)
        pltpu.sync_copy(side_hbm.at[pl.ds(tok0, SC_TOKB)], side_v)
        for c in range(SC_CHUNKS - 1):
            gather(0, c).start()

        @pl.loop(0, SC_TOKB)
        def _(t):
            for c in range(SC_CHUNKS):
                gather(t, c).wait()
                nxt = (c + SC_CHUNKS - 1) % SC_CHUNKS
                if c == 0:
                    gather(t, nxt).start()
                else:
                    @pl.when(t + 1 < SC_TOKB)
                    def _():
                        gather(t + 1, nxt).start()
                compute(t, c)

        pltpu.sync_copy(out_v, out_hbm.at[pl.ds(tok0, SC_TOKB)])


def _peer_down(idx, htw, down_w):
    T = htw.shape[0]
    tpw = T // SC_WORKERS
    L = SC_LANES

    @functools.partial(
        pl.kernel, mesh=_sc_mesh(),
        out_type=jax.ShapeDtypeStruct((T, N_SEL), F32),
        scratch_types=[
            pltpu.VMEM((SC_TOKB, N_SEL), jnp.int32),
            pltpu.VMEM((SC_TOKB, D_WORDS), jnp.int32),
            pltpu.VMEM((SC_CHUNKS, SC_ROWS, D_WORDS), jnp.int32),
            pltpu.VMEM((SC_PASS, L), F32),
            pltpu.VMEM((SC_TOKB, N_SEL), F32),
            pltpu.SemaphoreType.DMA((SC_CHUNKS,)),
        ],
        compiler_params=pltpu.CompilerParams(needs_layout_passes=False),
        name="peer_down", cost_estimate=_sc_cost(T),
    )
    def k(idx_hbm, h_hbm, down_hbm, s_hbm, idx_v, h_v, rows_v, red_v, s_v, sem):
        lane = lax.iota(jnp.int32, L)

        def compute(t, c):
            for ps in range(SC_ROWS // SC_PASS):
                def qbody(q, accs):
                    cols = [pl.ds((q * SC_QUAD + u) * L, L) for u in range(SC_QUAD)]
                    hs = [_sc_bf16(h_v[t, cols[u]]) for u in range(SC_QUAD)]
                    out = []
                    for r in range(SC_PASS):
                        part = None
                        for u in range(SC_QUAD):
                            prod = _sc_bf16(rows_v[c, ps * SC_PASS + r, cols[u]]) * hs[u]
                            part = prod if part is None else part + prod
                        lo, hi = _sc_widen(part)
                        out.append(accs[r] + lo + hi)
                    return tuple(out)
                accs = lax.fori_loop(0, SC_WCH // SC_QUAD, qbody,
                                     tuple(jnp.zeros((L,), F32) for _ in range(SC_PASS)))
                for r in range(SC_PASS):
                    red_v[r, :] = accs[r]
                tot = jnp.zeros((L,), F32)
                for j in range(L):
                    tot = tot + plsc.load_gather(red_v, [lane, jnp.full((L,), j, jnp.int32)])
                s_v[t, pl.ds(c * SC_ROWS + ps * SC_PASS, L)] = tot

        _sc_token_loop(down_hbm, idx_hbm, h_hbm, s_hbm, idx_v, h_v, rows_v, s_v, sem, tpw, compute)

    return k(idx, htw, down_w)


def _peer_up(idx, act_w, up_w):
    T = act_w.shape[0]
    tpw = T // SC_WORKERS
    L = SC_LANES
    RG = SC_QUAD

    @functools.partial(
        pl.kernel, mesh=_sc_mesh(),
        out_type=jax.ShapeDtypeStruct((T, D_MODEL), F32),
        scratch_types=[
            pltpu.VMEM((SC_TOKB, N_SEL), jnp.int32),
            pltpu.VMEM((SC_TOKB, N_SEL), jnp.int32),
            pltpu.VMEM((SC_CHUNKS, SC_ROWS, D_WORDS), jnp.int32),
            pltpu.VMEM((SC_TOKB, D_MODEL), F32),
            pltpu.SemaphoreType.DMA((SC_CHUNKS,)),
        ],
        compiler_params=pltpu.CompilerParams(needs_layout_passes=False),
        name="peer_up", cost_estimate=_sc_cost(T),
    )
    def k(idx_hbm, act_hbm, up_hbm, o_hbm, idx_v, act_v, rows_v, o_v, sem):
        def compute(t, c):
            tvec = jnp.full((L,), t, jnp.int32)
            for kg in range(SC_WCH // SC_KG):
                lo_cols = [pl.ds((kg * SC_KG + kk) * L, L) for kk in range(SC_KG)]
                hi_cols = [pl.ds(D_WORDS + (kg * SC_KG + kk) * L, L) for kk in range(SC_KG)]
                if c == 0:
                    init = tuple(jnp.zeros((L,), F32) for _ in range(2 * SC_KG))
                else:
                    init = tuple(o_v[t, col] for col in lo_cols + hi_cols)

                def rgbody(rg, accs):
                    first = jnp.full((L,), c * SC_ROWS, jnp.int32) + rg * RG
                    a = [_sc_bf16(plsc.load_gather(act_v, [tvec, first + u])) for u in range(RG)]
                    out_lo, out_hi = [], []
                    for kk in range(SC_KG):
                        part = None
                        for u in range(RG):
                            prod = _sc_bf16(rows_v[c, rg * RG + u, lo_cols[kk]]) * a[u]
                            part = prod if part is None else part + prod
                        lo, hi = _sc_widen(part)
                        out_lo.append(accs[kk] + lo)
                        out_hi.append(accs[SC_KG + kk] + hi)
                    return tuple(out_lo + out_hi)
                accs = lax.fori_loop(0, SC_ROWS // RG, rgbody, init)
                for kk in range(SC_KG):
                    o_v[t, lo_cols[kk]] = accs[kk]
                    o_v[t, hi_cols[kk]] = accs[SC_KG + kk]

        _sc_token_loop(up_hbm, idx_hbm, act_hbm, o_hbm, idx_v, act_v, rows_v, o_v, sem, tpw, compute)

    return k(idx, act_w, up_w)


def _act_kernel(s_ref, gate_ref, a_ref):
    s = s_ref[...]
    act = 0.5 * s * (1.0 + lax.erf(s * math.sqrt(0.5))) * gate_ref[...]
    bits = _bf16_bits(act)
    a_ref[...] = lax.bitwise_or(lax.shift_right_logical(bits, 16), bits)


def _act(s, gate, tm):
    T = s.shape[0]
    spec = pl.BlockSpec((tm, N_SEL), lambda i: (i, 0))
    return pl.pallas_call(
        _act_kernel, grid=(T // tm,), in_specs=[spec, spec], out_specs=spec,
        out_shape=jax.ShapeDtypeStruct((T, N_SEL), jnp.int32),
        compiler_params=_cparams(("parallel",)), name="expert_act",
    )(s, gate)


def _final_kernel(x1_ref, p_ref, g_ref, o_ref):
    o_ref[...] = _rms(x1_ref[...] + p_ref[...], g_ref[...])


def _final(x1, peer, g, tm):
    T = x1.shape[0]
    spec = pl.BlockSpec((tm, D_MODEL), lambda i: (i, 0))
    return pl.pallas_call(
        _final_kernel, grid=(T // tm,),
        in_specs=[spec, spec, pl.BlockSpec((1, D_MODEL), lambda i: (0, 0))], out_specs=spec,
        out_shape=jax.ShapeDtypeStruct((T, D_MODEL), F32),
        compiler_params=_cparams(("parallel",)), name="final_norm",
    )(x1, peer, g)


PROMPT_GROUP = 2


def _tile(n, want):
    t = min(n, want)
    assert n % t == 0
    return t


def _channel_dft():
    ang = 2.0 * np.pi * np.outer(np.arange(GDIM), np.arange(GDIM)) / GDIM
    m = np.concatenate([np.cos(ang), -np.sin(ang)], axis=1) / math.sqrt(GDIM)
    return jnp.asarray(m.astype(np.float32), BF16)


def _trunk(x, p):
    B, S, _ = x.shape
    T = B * S
    assert T % (SC_WORKERS * SC_TOKB) == 0
    zg, hg = _inproj(x, p["g1"], p["win"], p["cdft"], _tile(S, 512))
    yf = _seq_dft(zg)
    yc = _conv_module(hg, p["conv_w"], p["conv_b"], p["ln_g"], p["ln_b"], p["gn_c"], _tile(S, 512))
    x1, htw, idx, gate = _route(x, yf, yc, p["gn_f"], p["wout"], p["g2"], p["wq"], p["ka"], p["kb"],
                                _tile(S, 256))
    x1 = x1.reshape(T, D_MODEL)
    htw = htw.reshape(T, D_WORDS)
    idx = idx.reshape(T, N_SEL)
    gate = gate.reshape(T, N_SEL)
    s = _peer_down(idx, htw, p["down"])
    act = _act(s, gate, _tile(T, 1024))
    peer = _peer_up(idx, act, p["up"])
    return _final(x1, peer, p["final_g"], _tile(T, 512)).reshape(B, S, D_MODEL)


def kernel(x_prompt, x_sample, norm1_g, w_in, conv_w, conv_b, conv_ln_g, conv_ln_b, gn_fourier_g,
           gn_conv_g, w_out, norm2_g, w_q, keys_a, keys_b, expert_down, expert_up, final_g):
    assert norm1_g.shape[0] == 1
    p = dict(
        g1=norm1_g[0].reshape(1, D_MODEL), win=w_in[0].astype(BF16), cdft=_channel_dft(),
        conv_w=conv_w[0], conv_b=conv_b[0].reshape(1, D_CONV),
        ln_g=conv_ln_g[0].reshape(1, D_CONV), ln_b=conv_ln_b[0].reshape(1, D_CONV),
        gn_f=gn_fourier_g[0].reshape(1, D_FOURIER), gn_c=gn_conv_g[0].reshape(1, D_CONV),
        wout=w_out[0].astype(BF16), g2=norm2_g[0].reshape(1, D_MODEL), wq=w_q[0].astype(BF16),
        ka=keys_a[0].astype(BF16), kb=keys_b[0].astype(BF16),
        down=_pack_halves(expert_down[0]), up=_pack_halves(expert_up[0]),
        final_g=final_g.reshape(1, D_MODEL),
    )
    nb = x_prompt.shape[0]
    step = PROMPT_GROUP if nb % PROMPT_GROUP == 0 else nb
    y_prompt = jnp.concatenate([_trunk(x_prompt[i:i + step], p) for i in range(0, nb, step)], axis=0)
    return y_prompt, _trunk(x_sample, p)
```

```python
import functools
import math

import numpy as np
import jax
import jax.numpy as jnp
from jax import lax
from jax.experimental import pallas as pl
from jax.experimental.pallas import tpu as pltpu
from jax.experimental.pallas import tpu_sc as plsc

F32 = jnp.float32
BF16 = jnp.bfloat16

D_MODEL = 1024
D_FOURIER = 512
GROUPS = 4
GDIM = 128
D_CONV = 512
CONV_W = 31
CONV_PAD = 15
HEADS = 8
N_KEYS = 128
TOPK = 16
D_HALF = 128
D_QUERY = 256
N_SEL = HEADS * TOPK
EPS = 1e-6

SUBLANES = 8
LANES = 128
VMEM_LIMIT = 56 * 1024 * 1024

SC_CORES = 2
SC_SUBCORES = 16
SC_LANES = 16
SC_WORKERS = SC_CORES * SC_SUBCORES


def _cparams(sem):
    return pltpu.CompilerParams(dimension_semantics=sem, vmem_limit_bytes=VMEM_LIMIT)


def _rms(x, g):
    return x * lax.rsqrt(jnp.mean(x * x, axis=-1, keepdims=True) + EPS) * g


D_WORDS = D_MODEL // 2
HI_MASK = -65536


def _bf16_bits(x):
    return lax.bitcast_convert_type(x.astype(BF16).astype(F32), jnp.int32)


def _pack_halves(x):
    lo = lax.shift_right_logical(_bf16_bits(x[..., :D_WORDS]), 16)
    hi = lax.bitwise_and(_bf16_bits(x[..., D_WORDS:]), HI_MASK)
    return lax.bitwise_or(lo, hi)


def _inproj_kernel(x_ref, g1_ref, win_ref, cdft_ref, zg_ref, hg_ref):
    h = _rms(x_ref[...], g1_ref[...]).astype(BF16)
    z = jnp.dot(h, win_ref[...], preferred_element_type=F32)
    for g in range(GROUPS):
        zg = z[:, g * GDIM:(g + 1) * GDIM].astype(BF16)
        zg_ref[g] = jnp.dot(zg, cdft_ref[...], preferred_element_type=F32)
    a = z[:, D_FOURIER:D_FOURIER + D_CONV]
    gate = z[:, D_FOURIER + D_CONV:]
    hg_ref[...] = a * jax.nn.sigmoid(gate)


def _inproj(x, g1, win_bf, cdft, tm):
    B, S, _ = x.shape
    return pl.pallas_call(
        _inproj_kernel,
        grid=(B, S // tm),
        in_specs=[
            pl.BlockSpec((None, tm, D_MODEL), lambda b, m: (b, m, 0)),
            pl.BlockSpec((1, D_MODEL), lambda b, m: (0, 0)),
            pl.BlockSpec((D_MODEL, D_FOURIER + 2 * D_CONV), lambda b, m: (0, 0)),
            pl.BlockSpec((GDIM, 2 * GDIM), lambda b, m: (0, 0)),
        ],
        out_specs=[
            pl.BlockSpec((None, GROUPS, tm, 2 * GDIM), lambda b, m: (b, 0, m, 0)),
            pl.BlockSpec((None, tm, D_CONV), lambda b, m: (b, m, 0)),
        ],
        out_shape=[
            jax.ShapeDtypeStruct((B, GROUPS, S, 2 * GDIM), F32),
            jax.ShapeDtypeStruct((B, S, D_CONV), F32),
        ],
        compiler_params=_cparams(("parallel", "parallel")),
        name="inproj",
    )(x, g1, win_bf, cdft)


def _fft1_kernel(z_ref, k1_ref, tc_ref, ts_ref, a_ref):
    n1 = z_ref.shape[0]
    rows = n1 * SUBLANES
    x = z_ref[...].reshape(rows, 2 * GDIM).astype(BF16)
    p = jnp.dot(k1_ref[...], x, preferred_element_type=F32)
    pc, ps = p[:rows], p[rows:]
    ar = pc[:, :GDIM] + ps[:, GDIM:]
    ai = pc[:, GDIM:] - ps[:, :GDIM]
    tc, ts = tc_ref[...], ts_ref[...]
    a_ref[:, :, :GDIM] = (ar * tc + ai * ts).reshape(n1, SUBLANES, GDIM)
    a_ref[:, :, GDIM:] = (ai * tc - ar * ts).reshape(n1, SUBLANES, GDIM)


def _fft2_kernel(a_ref, c2_ref, s2_ref, y_ref):
    for r in range(SUBLANES):
        blk = a_ref[r].astype(BF16)
        y = jnp.dot(c2_ref[...], blk[:, :GDIM], preferred_element_type=F32)
        y = y + jnp.dot(s2_ref[...], blk[:, GDIM:], preferred_element_type=F32)
        y_ref[:, r * GDIM:(r + 1) * GDIM] = y


def _split_seq(S):
    n1 = 1 << (int(math.log2(S)) // 2)
    return n1, S // n1


@functools.lru_cache(maxsize=None)
def _fft_tables(S):
    n1, n2 = _split_seq(S)
    k1 = np.arange(n1)
    ang1 = 2.0 * np.pi * np.outer(k1, np.arange(n1)) / n1
    eye = np.eye(SUBLANES)
    kmat = np.concatenate([np.kron(np.cos(ang1), eye), np.kron(np.sin(ang1), eye)], axis=0)
    nn2 = np.arange(n2).reshape(n2 // SUBLANES, 1, SUBLANES)
    angt = 2.0 * np.pi * k1.reshape(1, n1, 1) * nn2 / S
    angt = angt.reshape(n2 // SUBLANES, n1 * SUBLANES, 1)
    tc = np.broadcast_to(np.cos(angt), (n2 // SUBLANES, n1 * SUBLANES, GDIM))
    ts = np.broadcast_to(np.sin(angt), (n2 // SUBLANES, n1 * SUBLANES, GDIM))
    ang2 = 2.0 * np.pi * np.outer(np.arange(n2), np.arange(n2)) / n2
    scale = 1.0 / math.sqrt(S)
    return (kmat.astype(np.float32), np.ascontiguousarray(tc, np.float32),
            np.ascontiguousarray(ts, np.float32),
            (np.cos(ang2) * scale).astype(np.float32), (np.sin(ang2) * scale).astype(np.float32))


def _seq_dft(zg):
    B, G, S, _ = zg.shape
    n1, n2 = _split_seq(S)
    kmat, tc, ts, c2, s2 = _fft_tables(S)
    kmat = jnp.asarray(kmat, BF16)
    c2 = jnp.asarray(c2, BF16)
    s2 = jnp.asarray(s2, BF16)
    nb = n2 // SUBLANES
    z6 = zg.reshape(B, G, n1, nb, SUBLANES, 2 * GDIM)
    blk6 = (None, None, n1, None, SUBLANES, 2 * GDIM)
    a6 = pl.pallas_call(
        _fft1_kernel,
        grid=(B, G, nb),
        in_specs=[
            pl.BlockSpec(blk6, lambda b, g, j: (b, g, 0, j, 0, 0)),
            pl.BlockSpec((2 * n1 * SUBLANES, n1 * SUBLANES), lambda b, g, j: (0, 0)),
            pl.BlockSpec((None, n1 * SUBLANES, GDIM), lambda b, g, j: (j, 0, 0)),
            pl.BlockSpec((None, n1 * SUBLANES, GDIM), lambda b, g, j: (j, 0, 0)),
        ],
        out_specs=pl.BlockSpec(blk6, lambda b, g, j: (b, g, 0, j, 0, 0)),
        out_shape=jax.ShapeDtypeStruct(z6.shape, F32),
        compiler_params=_cparams(("parallel", "parallel", "parallel")),
        name="fft_stage1",
    )(z6, kmat, jnp.asarray(tc), jnp.asarray(ts))
    a5 = a6.reshape(B, G, n1, n2, 2 * GDIM)
    y = pl.pallas_call(
        _fft2_kernel,
        grid=(B, G, n1 // SUBLANES),
        in_specs=[
            pl.BlockSpec((None, None, SUBLANES, n2, 2 * GDIM), lambda b, g, i: (b, g, i, 0, 0)),
            pl.BlockSpec((n2, n2), lambda b, g, i: (0, 0)),
            pl.BlockSpec((n2, n2), lambda b, g, i: (0, 0)),
        ],
        out_specs=pl.BlockSpec((None, None, n2, SUBLANES * GDIM), lambda b, g, i: (b, g, 0, i)),
        out_shape=jax.ShapeDtypeStruct((B, G, n2, n1 * GDIM), F32),
        compiler_params=_cparams(("parallel", "parallel", "parallel")),
        name="fft_stage2",
    )(a5, c2, s2)
    return y.reshape(B, G, S, GDIM)


CONV_HALO = 16
CONV_ROWS = 16


def _conv_kernel(prev_ref, cur_ref, next_ref, w_ref, cb_ref, lg_ref, lb_ref, gc_ref, o_ref, scr):
    ts = cur_ref.shape[0]
    m = pl.program_id(1)
    last = pl.num_programs(1) - 1
    scr[0:CONV_HALO, :] = jnp.where(m > 0, prev_ref[...], 0.0)
    scr[CONV_HALO:CONV_HALO + ts, :] = cur_ref[...]
    scr[CONV_HALO + ts:, :] = jnp.where(m < last, next_ref[...], 0.0)
    off = CONV_HALO - CONV_PAD

    def step(c, carry):
        base = pl.multiple_of(c * CONV_ROWS, CONV_ROWS)
        win = scr[pl.ds(base, CONV_ROWS + 2 * CONV_HALO), :]
        acc = jnp.broadcast_to(cb_ref[...], (CONV_ROWS, D_CONV))
        for k in range(CONV_W):
            acc = acc + win[off + k:off + k + CONV_ROWS, :] * w_ref[k:k + 1, :]
        mu = jnp.mean(acc, axis=-1, keepdims=True)
        d = acc - mu
        var = jnp.mean(d * d, axis=-1, keepdims=True)
        y = d * lax.rsqrt(var + EPS) * lg_ref[...] + lb_ref[...]
        y = y * jax.nn.sigmoid(y)
        o_ref[pl.ds(base, CONV_ROWS), :] = _rms(y, gc_ref[...])
        return carry

    lax.fori_loop(0, ts // CONV_ROWS, step, 0)


def _conv_module(hg, conv_w, conv_b, ln_g, ln_b, gn_c, ts):
    B, S, _ = hg.shape
    hb = ts // CONV_HALO
    nh = S // CONV_HALO
    vec = pl.BlockSpec((1, D_CONV), lambda b, m: (0, 0))
    return pl.pallas_call(
        _conv_kernel,
        grid=(B, S // ts),
        in_specs=[
            pl.BlockSpec((None, CONV_HALO, D_CONV), lambda b, m: (b, jnp.maximum(m * hb - 1, 0), 0)),
            pl.BlockSpec((None, ts, D_CONV), lambda b, m: (b, m, 0)),
            pl.BlockSpec((None, CONV_HALO, D_CONV), lambda b, m: (b, jnp.minimum((m + 1) * hb, nh - 1), 0)),
            pl.BlockSpec((CONV_W, D_CONV), lambda b, m: (0, 0)),
            vec, vec, vec, vec,
        ],
        out_specs=pl.BlockSpec((None, ts, D_CONV), lambda b, m: (b, m, 0)),
        out_shape=jax.ShapeDtypeStruct((B, S, D_CONV), F32),
        scratch_shapes=[pltpu.VMEM((ts + 2 * CONV_HALO, D_CONV), F32)],
        compiler_params=_cparams(("parallel", "parallel")),
        name="conv_module",
    )(hg, hg, hg, conv_w, conv_b, ln_g, ln_b, gn_c)


def _extract_topk(s, order, k, payload=None):
    big = jnp.int32(1 << 30)
    vals, sel = [], []
    for _ in range(k):
        m = jnp.max(s, axis=0, keepdims=True)
        am = jnp.min(jnp.where(s == m, order, big), axis=0, keepdims=True)
        hit = order == am
        vals.append(m)
        if payload is None:
            sel.append(am)
        else:
            sel.append(jnp.max(jnp.where(hit, payload, -1), axis=0, keepdims=True))
        s = jnp.where(hit, -jnp.inf, s)
    return jnp.concatenate(vals, axis=0), jnp.concatenate(sel, axis=0)


_CAND_A = 4
_CAND_B = 3


def _route_kernel(x_ref, yf_ref, yc_ref, gf_ref, wout_ref, g2_ref, wq_ref, ka_ref, kb_ref,
                  x1_ref, htw_ref, idx_ref, gate_ref):
    tm = x_ref.shape[0]
    yf = jnp.concatenate([yf_ref[g] for g in range(GROUPS)], axis=-1)
    yfn = _rms(yf, gf_ref[...])
    ycat = jnp.concatenate([yfn, yc_ref[...]], axis=-1).astype(BF16)
    x1 = x_ref[...] + jnp.dot(ycat, wout_ref[...], preferred_element_type=F32)
    x1_ref[...] = x1
    ht = _rms(x1, g2_ref[...])
    htw_ref[...] = _pack_halves(ht)
    q = jnp.dot(ht.astype(BF16), wq_ref[...], preferred_element_type=F32)

    row16 = lax.broadcasted_iota(jnp.int32, (TOPK, tm), 0)
    key_iota = lax.broadcasted_iota(jnp.int32, (N_KEYS, tm), 0)
    nt = (((1,), (1,)), ((), ()))
    idx_rows, gate_rows = [], []
    for h in range(HEADS):
        qa = q[:, h * D_QUERY:h * D_QUERY + D_HALF].astype(BF16)
        qb = q[:, h * D_QUERY + D_HALF:(h + 1) * D_QUERY].astype(BF16)
        sa = lax.dot_general(ka_ref[h], qa, nt, preferred_element_type=F32)
        sb = lax.dot_general(kb_ref[h], qb, nt, preferred_element_type=F32)
        va, ia = _extract_topk(sa, key_iota, TOPK)
        vb, ib = _extract_topk(sb, key_iota, TOPK)
        cs, ce, co = [], [], []
        for i in range(_CAND_A):
            ok = (i + 1) * (row16 + 1) <= TOPK
            cs.append(jnp.where(ok, va[i:i + 1] + vb, -jnp.inf))
            ce.append(ia[i:i + 1] * N_KEYS + ib)
            co.append(i * TOPK + row16)
        for j in range(_CAND_B):
            ok = jnp.logical_and(row16 >= _CAND_A, (row16 + 1) * (j + 1) <= TOPK)
            cs.append(jnp.where(ok, va + vb[j:j + 1], -jnp.inf))
            ce.append(ia * N_KEYS + ib[j:j + 1])
            co.append(row16 * TOPK + j)
        top_s, top_e = _extract_topk(jnp.concatenate(cs, axis=0), jnp.concatenate(co, axis=0),
                                     TOPK, payload=jnp.concatenate(ce, axis=0))
        ex = jnp.exp(top_s - top_s[0:1])
        gate_rows.append(ex / jnp.sum(ex, axis=0, keepdims=True))
        idx_rows.append(top_e)
    idx_ref[...] = jnp.concatenate(idx_rows, axis=0).T
    gate_ref[...] = jnp.concatenate(gate_rows, axis=0).T


def _route(x, yf, yc, gf, wout_bf, g2, wq_bf, ka_bf, kb_bf, tm):
    B, S, _ = x.shape
    const2 = lambda b, m: (0, 0)
    const3 = lambda b, m: (0, 0, 0)
    row = lambda w: pl.BlockSpec((None, tm, w), lambda b, m: (b, m, 0))
    return pl.pallas_call(
        _route_kernel,
        grid=(B, S // tm),
        in_specs=[
            row(D_MODEL),
            pl.BlockSpec((None, GROUPS, tm, GDIM), lambda b, m: (b, 0, m, 0)),
            row(D_CONV),
            pl.BlockSpec((1, D_FOURIER), const2),
            pl.BlockSpec((D_MODEL, D_MODEL), const2),
            pl.BlockSpec((1, D_MODEL), const2),
            pl.BlockSpec((D_MODEL, HEADS * D_QUERY), const2),
            pl.BlockSpec((HEADS, N_KEYS, D_HALF), const3),
            pl.BlockSpec((HEADS, N_KEYS, D_HALF), const3),
        ],
        out_specs=[row(D_MODEL), row(D_WORDS), row(N_SEL), row(N_SEL)],
        out_shape=[
            jax.ShapeDtypeStruct((B, S, D_MODEL), F32),
            jax.ShapeDtypeStruct((B, S, D_WORDS), jnp.int32),
            jax.ShapeDtypeStruct((B, S, N_SEL), jnp.int32),
            jax.ShapeDtypeStruct((B, S, N_SEL), F32),
        ],
        compiler_params=_cparams(("parallel", "parallel")),
        name="route",
    )(x, yf, yc, gf, wout_bf, g2, wq_bf, ka_bf, kb_bf)


SC_UROWS = 64
SC_UNITS = N_SEL // SC_UROWS
SC_NBUF = 3
SC_TOKB = 16
SC_WCH = D_WORDS // SC_LANES
SC_QUAD = 4
SC_KG = 8


def _sc_mesh():
    return plsc.VectorSubcoreMesh(core_axis_name="c", subcore_axis_name="s",
                                  num_cores=SC_CORES, num_subcores=SC_SUBCORES)


def _sc_cost(tokens):
    rows = tokens * N_SEL
    return pl.CostEstimate(flops=2 * rows * D_MODEL, transcendentals=0, bytes_accessed=rows * D_WORDS * 4)


def _sc_bf16(words):
    return plsc.bitcast(words, BF16)


def _sc_widen(pairs):
    w = plsc.bitcast(pairs, jnp.int32)
    lo = plsc.bitcast(lax.shift_left(w, 16), F32)
    hi = plsc.bitcast(lax.bitwise_and(w, jnp.int32(HI_MASK)), F32)
    return lo, hi


def _sc_quad_sum(row_chunk, scale):
    part = None
    for u in range(SC_QUAD):
        prod = _sc_bf16(row_chunk(u)) * scale(u)
        part = prod if part is None else part + prod
    return part


def _tree_sum(xs):
    while len(xs) > 1:
        xs = [xs[i] + xs[i + 1] for i in range(0, len(xs), 2)]
    return xs[0]


def _sc_token_loop(table_hbm, idx_hbm, side_hbm, out_hbm, idx_v, side_v, rows_v, out_v, gsem, osem,
                   tokens_per_worker, compute):
    wid = lax.axis_index("s") * SC_CORES + lax.axis_index("c")

    def gather(t, unit, buf):
        rows = idx_v.at[t, pl.ds(unit * SC_UROWS, SC_UROWS)]
        return pltpu.make_async_copy(table_hbm.at[rows], rows_v.at[buf], gsem.at[buf])

    def writeback(tok, slot):
        return pltpu.make_async_copy(out_v.at[slot], out_hbm.at[tok], osem.at[slot])

    @pl.loop(0, tokens_per_worker // SC_TOKB)
    def _(bi):
        tok0 = wid * tokens_per_worker + bi * SC_TOKB
        pltpu.sync_copy(idx_hbm.at[pl.ds(tok0, SC_TOKB)], idx_v)
        pltpu.sync_copy(side_hbm.at[pl.ds(tok0, SC_TOKB)], side_v)
        for unit in range(SC_UNITS):
            gather(0, unit, unit).start()

        @pl.loop(0, SC_TOKB)
        def _(t):
            slot = lax.rem(t, 2)

            @pl.when(t >= 2)
            def _():
                writeback(tok0 + t - 2, slot).wait()

            for unit in range(SC_UNITS):
                u = t * SC_UNITS + unit
                buf = lax.rem(u, SC_NBUF)
                gather(t, unit, buf).wait()

                @pl.when(t + 1 < SC_TOKB)
                def _():
                    gather(t + 1, unit, lax.rem(u + SC_UNITS, SC_NBUF)).start()

                compute(t, unit, buf, slot)
            writeback(tok0 + t, slot).start()

        for t in range(SC_TOKB - 2, SC_TOKB):
            writeback(tok0 + t, t % 2).wait()


def _peer_down(idx, htw, down_w):
    T = htw.shape[0]
    tpw = T // SC_WORKERS
    L = SC_LANES

    @functools.partial(
        pl.kernel, mesh=_sc_mesh(),
        out_type=jax.ShapeDtypeStruct((T, L * N_SEL), F32),
        scratch_types=[
            pltpu.VMEM((SC_TOKB, N_SEL), jnp.int32),
            pltpu.VMEM((SC_TOKB, D_WORDS), jnp.int32),
            pltpu.VMEM((SC_NBUF, SC_UROWS, D_WORDS), jnp.int32),
            pltpu.VMEM((2, L * N_SEL), F32),
            pltpu.SemaphoreType.DMA((SC_NBUF,)),
            pltpu.SemaphoreType.DMA((2,)),
        ],
        compiler_params=pltpu.CompilerParams(needs_layout_passes=False),
        name="peer_down", cost_estimate=_sc_cost(T),
    )
    def k(idx_hbm, h_hbm, down_hbm, s_hbm, idx_v, h_v, rows_v, s_v, gsem, osem):
        lane = lax.iota(jnp.int32, L)

        def compute(t, unit, buf, slot):
            svec = jnp.full((L,), slot, jnp.int32)
            hs = [_sc_bf16(h_v[t, pl.ds(w * L, L)]) for w in range(SC_WCH)]

            @plsc.parallel_loop(0, SC_UROWS, unroll=2)
            def _(r):
                parts = []
                for q in range(SC_WCH // SC_QUAD):
                    part = _sc_quad_sum(lambda u: rows_v[buf, r, pl.ds((q * SC_QUAD + u) * L, L)],
                                        lambda u: hs[q * SC_QUAD + u])
                    parts.extend(_sc_widen(part))
                plsc.store_scatter(s_v, [svec, lane * N_SEL + (unit * SC_UROWS + r)], _tree_sum(parts))

        _sc_token_loop(down_hbm, idx_hbm, h_hbm, s_hbm, idx_v, h_v, rows_v, s_v, gsem, osem, tpw, compute)

    return k(idx, htw, down_w)


def _peer_up(idx, act_w, up_w):
    T = act_w.shape[0]
    tpw = T // SC_WORKERS
    L = SC_LANES

    @functools.partial(
        pl.kernel, mesh=_sc_mesh(),
        out_type=jax.ShapeDtypeStruct((T, D_MODEL), F32),
        scratch_types=[
            pltpu.VMEM((SC_TOKB, N_SEL), jnp.int32),
            pltpu.VMEM((SC_TOKB, N_SEL), jnp.int32),
            pltpu.VMEM((SC_NBUF, SC_UROWS, D_WORDS), jnp.int32),
            pltpu.VMEM((2, D_MODEL), F32),
            pltpu.SemaphoreType.DMA((SC_NBUF,)),
            pltpu.SemaphoreType.DMA((2,)),
        ],
        compiler_params=pltpu.CompilerParams(needs_layout_passes=False),
        name="peer_up", cost_estimate=_sc_cost(T),
    )
    def k(idx_hbm, act_hbm, up_hbm, o_hbm, idx_v, act_v, rows_v, o_v, gsem, osem):
        def compute(t, unit, buf, slot):
            tvec = jnp.full((L,), t, jnp.int32)
            for kg in range(SC_WCH // SC_KG):
                lo_cols = [pl.ds((kg * SC_KG + kk) * L, L) for kk in range(SC_KG)]
                hi_cols = [pl.ds(D_WORDS + (kg * SC_KG + kk) * L, L) for kk in range(SC_KG)]
                if unit == 0:
                    init = tuple(jnp.zeros((L,), F32) for _ in range(2 * SC_KG))
                else:
                    init = tuple(o_v[slot, col] for col in lo_cols + hi_cols)

                def rgbody(rg, accs):
                    first = jnp.full((L,), unit * SC_UROWS, jnp.int32) + rg * SC_QUAD
                    a = [_sc_bf16(plsc.load_gather(act_v, [tvec, first + u])) for u in range(SC_QUAD)]
                    out_lo, out_hi = [], []
                    for kk in range(SC_KG):
                        part = _sc_quad_sum(lambda u: rows_v[buf, rg * SC_QUAD + u, lo_cols[kk]], lambda u: a[u])
                        lo, hi = _sc_widen(part)
                        out_lo.append(accs[kk] + lo)
                        out_hi.append(accs[SC_KG + kk] + hi)
                    return tuple(out_lo + out_hi)
                accs = lax.fori_loop(0, SC_UROWS // SC_QUAD, rgbody, init)
                for kk in range(SC_KG):
                    o_v[slot, lo_cols[kk]] = accs[kk]
                    o_v[slot, hi_cols[kk]] = accs[SC_KG + kk]

        _sc_token_loop(up_hbm, idx_hbm, act_hbm, o_hbm, idx_v, act_v, rows_v, o_v, gsem, osem, tpw, compute)

    return k(idx, act_w, up_w)


def _act_kernel(s_ref, gate_ref, a_ref):
    s = _tree_sum([s_ref[:, l * N_SEL:(l + 1) * N_SEL] for l in range(SC_LANES)])
    act = 0.5 * s * (1.0 + lax.erf(s * math.sqrt(0.5))) * gate_ref[...]
    bits = _bf16_bits(act)
    a_ref[...] = lax.bitwise_or(lax.shift_right_logical(bits, 16), bits)


def _act(s, gate, tm):
    T = s.shape[0]
    spec = pl.BlockSpec((tm, N_SEL), lambda i: (i, 0))
    part_spec = pl.BlockSpec((tm, SC_LANES * N_SEL), lambda i: (i, 0))
    return pl.pallas_call(
        _act_kernel, grid=(T // tm,), in_specs=[part_spec, spec], out_specs=spec,
        out_shape=jax.ShapeDtypeStruct((T, N_SEL), jnp.int32),
        compiler_params=_cparams(("parallel",)), name="expert_act",
    )(s, gate)


def _final_kernel(x1_ref, p_ref, g_ref, o_ref):
    o_ref[...] = _rms(x1_ref[...] + p_ref[...], g_ref[...])


def _final(x1, peer, g, tm):
    T = x1.shape[0]
    spec = pl.BlockSpec((tm, D_MODEL), lambda i: (i, 0))
    return pl.pallas_call(
        _final_kernel, grid=(T // tm,),
        in_specs=[spec, spec, pl.BlockSpec((1, D_MODEL), lambda i: (0, 0))], out_specs=spec,
        out_shape=jax.ShapeDtypeStruct((T, D_MODEL), F32),
        compiler_params=_cparams(("parallel",)), name="final_norm",
    )(x1, peer, g)


PROMPT_GROUP = 2


def _tile(n, want):
    t = min(n, want)
    assert n % t == 0
    return t


def _channel_dft():
    ang = 2.0 * np.pi * np.outer(np.arange(GDIM), np.arange(GDIM)) / GDIM
    m = np.concatenate([np.cos(ang), -np.sin(ang)], axis=1) / math.sqrt(GDIM)
    return jnp.asarray(m.astype(np.float32), BF16)


def _trunk(x, p):
    B, S, _ = x.shape
    T = B * S
    assert T % (SC_WORKERS * SC_TOKB) == 0
    zg, hg = _inproj(x, p["g1"], p["win"], p["cdft"], _tile(S, 512))
    yf = _seq_dft(zg)
    yc = _conv_module(hg, p["conv_w"], p["conv_b"], p["ln_g"], p["ln_b"], p["gn_c"], _tile(S, 512))
    x1, htw, idx, gate = _route(x, yf, yc, p["gn_f"], p["wout"], p["g2"], p["wq"], p["ka"], p["kb"],
                                _tile(S, 256))
    x1 = x1.reshape(T, D_MODEL)
    htw = htw.reshape(T, D_WORDS)
    idx = idx.reshape(T, N_SEL)
    gate = gate.reshape(T, N_SEL)
    s = _peer_down(idx, htw, p["down"])
    act = _act(s, gate, _tile(T, 1024))
    peer = _peer_up(idx, act, p["up"])
    return _final(x1, peer, p["final_g"], _tile(T, 512)).reshape(B, S, D_MODEL)


def kernel(x_prompt, x_sample, norm1_g, w_in, conv_w, conv_b, conv_ln_g, conv_ln_b, gn_fourier_g,
           gn_conv_g, w_out, norm2_g, w_q, keys_a, keys_b, expert_down, expert_up, final_g):
    assert norm1_g.shape[0] == 1
    p = dict(
        g1=norm1_g[0].reshape(1, D_MODEL), win=w_in[0].astype(BF16), cdft=_channel_dft(),
        conv_w=conv_w[0], conv_b=conv_b[0].reshape(1, D_CONV),
        ln_g=conv_ln_g[0].reshape(1, D_CONV), ln_b=conv_ln_b[0].reshape(1, D_CONV),
        gn_f=gn_fourier_g[0].reshape(1, D_FOURIER), gn_c=gn_conv_g[0].reshape(1, D_CONV),
        wout=w_out[0].astype(BF16), g2=norm2_g[0].reshape(1, D_MODEL), wq=w_q[0].astype(BF16),
        ka=keys_a[0].astype(BF16), kb=keys_b[0].astype(BF16),
        down=_pack_halves(expert_down[0]), up=_pack_halves(expert_up[0]),
        final_g=final_g.reshape(1, D_MODEL),
    )
    nb = x_prompt.shape[0]
    step = PROMPT_GROUP if nb % PROMPT_GROUP == 0 else nb
    y_prompt = jnp.concatenate([_trunk(x_prompt[i:i + step], p) for i in range(0, nb, step)], axis=0)
    return y_prompt, _trunk(x_sample, p)
```

```python
import functools
import math

import numpy as np
import jax
import jax.numpy as jnp
from jax import lax
from jax.experimental import pallas as pl
from jax.experimental.pallas import tpu as pltpu
from jax.experimental.pallas import tpu_sc as plsc

F32 = jnp.float32
BF16 = jnp.bfloat16

D_MODEL = 1024
D_FOURIER = 512
GROUPS = 4
GDIM = 128
D_CONV = 512
CONV_W = 31
CONV_PAD = 15
HEADS = 8
N_KEYS = 128
TOPK = 16
D_HALF = 128
D_QUERY = 256
N_SEL = HEADS * TOPK
EPS = 1e-6

SUBLANES = 8
LANES = 128
VMEM_LIMIT = 56 * 1024 * 1024

SC_CORES = 2
SC_SUBCORES = 16
SC_LANES = 16
SC_WORKERS = SC_CORES * SC_SUBCORES


def _cparams(sem):
    return pltpu.CompilerParams(dimension_semantics=sem, vmem_limit_bytes=VMEM_LIMIT)


def _rms(x, g):
    return x * lax.rsqrt(jnp.mean(x * x, axis=-1, keepdims=True) + EPS) * g


D_WORDS = D_MODEL // 2
HI_MASK = -65536


def _bf16_bits(x):
    return lax.bitcast_convert_type(x.astype(BF16).astype(F32), jnp.int32)


def _pack_halves(x):
    lo = lax.shift_right_logical(_bf16_bits(x[..., :D_WORDS]), 16)
    hi = lax.bitwise_and(_bf16_bits(x[..., D_WORDS:]), HI_MASK)
    return lax.bitwise_or(lo, hi)


def _inproj_kernel(x_ref, g1_ref, win_ref, cdft_ref, zg_ref, hg_ref):
    h = _rms(x_ref[...], g1_ref[...]).astype(BF16)
    z = jnp.dot(h, win_ref[...], preferred_element_type=F32)
    for g in range(GROUPS):
        zg = z[:, g * GDIM:(g + 1) * GDIM].astype(BF16)
        zg_ref[g] = jnp.dot(zg, cdft_ref[...], preferred_element_type=F32)
    a = z[:, D_FOURIER:D_FOURIER + D_CONV]
    gate = z[:, D_FOURIER + D_CONV:]
    hg_ref[...] = a * jax.nn.sigmoid(gate)


def _inproj(x, g1, win_bf, cdft, tm):
    B, S, _ = x.shape
    return pl.pallas_call(
        _inproj_kernel,
        grid=(B, S // tm),
        in_specs=[
            pl.BlockSpec((None, tm, D_MODEL), lambda b, m: (b, m, 0)),
            pl.BlockSpec((1, D_MODEL), lambda b, m: (0, 0)),
            pl.BlockSpec((D_MODEL, D_FOURIER + 2 * D_CONV), lambda b, m: (0, 0)),
            pl.BlockSpec((GDIM, 2 * GDIM), lambda b, m: (0, 0)),
        ],
        out_specs=[
            pl.BlockSpec((None, GROUPS, tm, 2 * GDIM), lambda b, m: (b, 0, m, 0)),
            pl.BlockSpec((None, tm, D_CONV), lambda b, m: (b, m, 0)),
        ],
        out_shape=[
            jax.ShapeDtypeStruct((B, GROUPS, S, 2 * GDIM), F32),
            jax.ShapeDtypeStruct((B, S, D_CONV), F32),
        ],
        compiler_params=_cparams(("parallel", "parallel")),
        name="inproj",
    )(x, g1, win_bf, cdft)


def _fft1_kernel(z_ref, k1_ref, tc_ref, ts_ref, a_ref):
    n1 = z_ref.shape[0]
    rows = n1 * SUBLANES
    x = z_ref[...].reshape(rows, 2 * GDIM).astype(BF16)
    p = jnp.dot(k1_ref[...], x, preferred_element_type=F32)
    pc, ps = p[:rows], p[rows:]
    ar = pc[:, :GDIM] + ps[:, GDIM:]
    ai = pc[:, GDIM:] - ps[:, :GDIM]
    tc, ts = tc_ref[...], ts_ref[...]
    a_ref[:, :, :GDIM] = (ar * tc + ai * ts).reshape(n1, SUBLANES, GDIM)
    a_ref[:, :, GDIM:] = (ai * tc - ar * ts).reshape(n1, SUBLANES, GDIM)


def _fft2_kernel(a_ref, c2_ref, s2_ref, y_ref):
    for r in range(SUBLANES):
        blk = a_ref[r].astype(BF16)
        y = jnp.dot(c2_ref[...], blk[:, :GDIM], preferred_element_type=F32)
        y = y + jnp.dot(s2_ref[...], blk[:, GDIM:], preferred_element_type=F32)
        y_ref[:, r * GDIM:(r + 1) * GDIM] = y


def _split_seq(S):
    n1 = 1 << (int(math.log2(S)) // 2)
    return n1, S // n1


@functools.lru_cache(maxsize=None)
def _fft_tables(S):
    n1, n2 = _split_seq(S)
    k1 = np.arange(n1)
    ang1 = 2.0 * np.pi * np.outer(k1, np.arange(n1)) / n1
    eye = np.eye(SUBLANES)
    kmat = np.concatenate([np.kron(np.cos(ang1), eye), np.kron(np.sin(ang1), eye)], axis=0)
    nn2 = np.arange(n2).reshape(n2 // SUBLANES, 1, SUBLANES)
    angt = 2.0 * np.pi * k1.reshape(1, n1, 1) * nn2 / S
    angt = angt.reshape(n2 // SUBLANES, n1 * SUBLANES, 1)
    tc = np.broadcast_to(np.cos(angt), (n2 // SUBLANES, n1 * SUBLANES, GDIM))
    ts = np.broadcast_to(np.sin(angt), (n2 // SUBLANES, n1 * SUBLANES, GDIM))
    ang2 = 2.0 * np.pi * np.outer(np.arange(n2), np.arange(n2)) / n2
    scale = 1.0 / math.sqrt(S)
    return (kmat.astype(np.float32), np.ascontiguousarray(tc, np.float32),
            np.ascontiguousarray(ts, np.float32),
            (np.cos(ang2) * scale).astype(np.float32), (np.sin(ang2) * scale).astype(np.float32))


def _seq_dft(zg):
    B, G, S, _ = zg.shape
    n1, n2 = _split_seq(S)
    kmat, tc, ts, c2, s2 = _fft_tables(S)
    kmat = jnp.asarray(kmat, BF16)
    c2 = jnp.asarray(c2, BF16)
    s2 = jnp.asarray(s2, BF16)
    nb = n2 // SUBLANES
    z6 = zg.reshape(B, G, n1, nb, SUBLANES, 2 * GDIM)
    blk6 = (None, None, n1, None, SUBLANES, 2 * GDIM)
    a6 = pl.pallas_call(
        _fft1_kernel,
        grid=(B, G, nb),
        in_specs=[
            pl.BlockSpec(blk6, lambda b, g, j: (b, g, 0, j, 0, 0)),
            pl.BlockSpec((2 * n1 * SUBLANES, n1 * SUBLANES), lambda b, g, j: (0, 0)),
            pl.BlockSpec((None, n1 * SUBLANES, GDIM), lambda b, g, j: (j, 0, 0)),
            pl.BlockSpec((None, n1 * SUBLANES, GDIM), lambda b, g, j: (j, 0, 0)),
        ],
        out_specs=pl.BlockSpec(blk6, lambda b, g, j: (b, g, 0, j, 0, 0)),
        out_shape=jax.ShapeDtypeStruct(z6.shape, F32),
        compiler_params=_cparams(("parallel", "parallel", "parallel")),
        name="fft_stage1",
    )(z6, kmat, jnp.asarray(tc), jnp.asarray(ts))
    a5 = a6.reshape(B, G, n1, n2, 2 * GDIM)
    y = pl.pallas_call(
        _fft2_kernel,
        grid=(B, G, n1 // SUBLANES),
        in_specs=[
            pl.BlockSpec((None, None, SUBLANES, n2, 2 * GDIM), lambda b, g, i: (b, g, i, 0, 0)),
            pl.BlockSpec((n2, n2), lambda b, g, i: (0, 0)),
            pl.BlockSpec((n2, n2), lambda b, g, i: (0, 0)),
        ],
        out_specs=pl.BlockSpec((None, None, n2, SUBLANES * GDIM), lambda b, g, i: (b, g, 0, i)),
        out_shape=jax.ShapeDtypeStruct((B, G, n2, n1 * GDIM), F32),
        compiler_params=_cparams(("parallel", "parallel", "parallel")),
        name="fft_stage2",
    )(a5, c2, s2)
    return y.reshape(B, G, S, GDIM)


CONV_HALO = 16
CONV_ROWS = 16


def _conv_kernel(prev_ref, cur_ref, next_ref, w_ref, cb_ref, lg_ref, lb_ref, gc_ref, o_ref, scr):
    ts = cur_ref.shape[0]
    m = pl.program_id(1)
    last = pl.num_programs(1) - 1
    scr[0:CONV_HALO, :] = jnp.where(m > 0, prev_ref[...], 0.0)
    scr[CONV_HALO:CONV_HALO + ts, :] = cur_ref[...]
    scr[CONV_HALO + ts:, :] = jnp.where(m < last, next_ref[...], 0.0)
    off = CONV_HALO - CONV_PAD

    def step(c, carry):
        base = pl.multiple_of(c * CONV_ROWS, CONV_ROWS)
        win = scr[pl.ds(base, CONV_ROWS + 2 * CONV_HALO), :]
        acc = jnp.broadcast_to(cb_ref[...], (CONV_ROWS, D_CONV))
        for k in range(CONV_W):
            acc = acc + win[off + k:off + k + CONV_ROWS, :] * w_ref[k:k + 1, :]
        mu = jnp.mean(acc, axis=-1, keepdims=True)
        d = acc - mu
        var = jnp.mean(d * d, axis=-1, keepdims=True)
        y = d * lax.rsqrt(var + EPS) * lg_ref[...] + lb_ref[...]
        y = y * jax.nn.sigmoid(y)
        o_ref[pl.ds(base, CONV_ROWS), :] = _rms(y, gc_ref[...])
        return carry

    lax.fori_loop(0, ts // CONV_ROWS, step, 0)


def _conv_module(hg, conv_w, conv_b, ln_g, ln_b, gn_c, ts):
    B, S, _ = hg.shape
    hb = ts // CONV_HALO
    nh = S // CONV_HALO
    vec = pl.BlockSpec((1, D_CONV), lambda b, m: (0, 0))
    return pl.pallas_call(
        _conv_kernel,
        grid=(B, S // ts),
        in_specs=[
            pl.BlockSpec((None, CONV_HALO, D_CONV), lambda b, m: (b, jnp.maximum(m * hb - 1, 0), 0)),
            pl.BlockSpec((None, ts, D_CONV), lambda b, m: (b, m, 0)),
            pl.BlockSpec((None, CONV_HALO, D_CONV), lambda b, m: (b, jnp.minimum((m + 1) * hb, nh - 1), 0)),
            pl.BlockSpec((CONV_W, D_CONV), lambda b, m: (0, 0)),
            vec, vec, vec, vec,
        ],
        out_specs=pl.BlockSpec((None, ts, D_CONV), lambda b, m: (b, m, 0)),
        out_shape=jax.ShapeDtypeStruct((B, S, D_CONV), F32),
        scratch_shapes=[pltpu.VMEM((ts + 2 * CONV_HALO, D_CONV), F32)],
        compiler_params=_cparams(("parallel", "parallel")),
        name="conv_module",
    )(hg, hg, hg, conv_w, conv_b, ln_g, ln_b, gn_c)


def _extract_topk(s, order, k, payload=None):
    big = jnp.int32(1 << 30)
    vals, sel = [], []
    for _ in range(k):
        m = jnp.max(s, axis=0, keepdims=True)
        am = jnp.min(jnp.where(s == m, order, big), axis=0, keepdims=True)
        hit = order == am
        vals.append(m)
        if payload is None:
            sel.append(am)
        else:
            sel.append(jnp.max(jnp.where(hit, payload, -1), axis=0, keepdims=True))
        s = jnp.where(hit, -jnp.inf, s)
    return jnp.concatenate(vals, axis=0), jnp.concatenate(sel, axis=0)


_CAND_A = 4
_CAND_B = 3


def _route_kernel(x_ref, yf_ref, yc_ref, gf_ref, wout_ref, g2_ref, wq_ref, ka_ref, kb_ref,
                  x1_ref, htw_ref, idx_ref, gate_ref):
    tm = x_ref.shape[0]
    yf = jnp.concatenate([yf_ref[g] for g in range(GROUPS)], axis=-1)
    yfn = _rms(yf, gf_ref[...])
    ycat = jnp.concatenate([yfn, yc_ref[...]], axis=-1).astype(BF16)
    x1 = x_ref[...] + jnp.dot(ycat, wout_ref[...], preferred_element_type=F32)
    x1_ref[...] = x1
    ht = _rms(x1, g2_ref[...])
    htw_ref[...] = _pack_halves(ht)
    q = jnp.dot(ht.astype(BF16), wq_ref[...], preferred_element_type=F32)

    row16 = lax.broadcasted_iota(jnp.int32, (TOPK, tm), 0)
    key_iota = lax.broadcasted_iota(jnp.int32, (N_KEYS, tm), 0)
    nt = (((1,), (1,)), ((), ()))
    idx_rows, gate_rows = [], []
    for h in range(HEADS):
        qa = q[:, h * D_QUERY:h * D_QUERY + D_HALF].astype(BF16)
        qb = q[:, h * D_QUERY + D_HALF:(h + 1) * D_QUERY].astype(BF16)
        sa = lax.dot_general(ka_ref[h], qa, nt, preferred_element_type=F32)
        sb = lax.dot_general(kb_ref[h], qb, nt, preferred_element_type=F32)
        va, ia = _extract_topk(sa, key_iota, TOPK)
        vb, ib = _extract_topk(sb, key_iota, TOPK)
        cs, ce, co = [], [], []
        for i in range(_CAND_A):
            ok = (i + 1) * (row16 + 1) <= TOPK
            cs.append(jnp.where(ok, va[i:i + 1] + vb, -jnp.inf))
            ce.append(ia[i:i + 1] * N_KEYS + ib)
            co.append(i * TOPK + row16)
        for j in range(_CAND_B):
            ok = jnp.logical_and(row16 >= _CAND_A, (row16 + 1) * (j + 1) <= TOPK)
            cs.append(jnp.where(ok, va + vb[j:j + 1], -jnp.inf))
            ce.append(ia * N_KEYS + ib[j:j + 1])
            co.append(row16 * TOPK + j)
        top_s, top_e = _extract_topk(jnp.concatenate(cs, axis=0), jnp.concatenate(co, axis=0),
                                     TOPK, payload=jnp.concatenate(ce, axis=0))
        ex = jnp.exp(top_s - top_s[0:1])
        gate_rows.append(ex / jnp.sum(ex, axis=0, keepdims=True))
        idx_rows.append(top_e)
    idx_ref[...] = jnp.concatenate(idx_rows, axis=0).T
    gate_ref[...] = jnp.concatenate(gate_rows, axis=0).T


def _route(x, yf, yc, gf, wout_bf, g2, wq_bf, ka_bf, kb_bf, tm):
    B, S, _ = x.shape
    const2 = lambda b, m: (0, 0)
    const3 = lambda b, m: (0, 0, 0)
    row = lambda w: pl.BlockSpec((None, tm, w), lambda b, m: (b, m, 0))
    return pl.pallas_call(
        _route_kernel,
        grid=(B, S // tm),
        in_specs=[
            row(D_MODEL),
            pl.BlockSpec((None, GROUPS, tm, GDIM), lambda b, m: (b, 0, m, 0)),
            row(D_CONV),
            pl.BlockSpec((1, D_FOURIER), const2),
            pl.BlockSpec((D_MODEL, D_MODEL), const2),
            pl.BlockSpec((1, D_MODEL), const2),
            pl.BlockSpec((D_MODEL, HEADS * D_QUERY), const2),
            pl.BlockSpec((HEADS, N_KEYS, D_HALF), const3),
            pl.BlockSpec((HEADS, N_KEYS, D_HALF), const3),
        ],
        out_specs=[row(D_MODEL), row(D_WORDS), row(N_SEL), row(N_SEL)],
        out_shape=[
            jax.ShapeDtypeStruct((B, S, D_MODEL), F32),
            jax.ShapeDtypeStruct((B, S, D_WORDS), jnp.int32),
            jax.ShapeDtypeStruct((B, S, N_SEL), jnp.int32),
            jax.ShapeDtypeStruct((B, S, N_SEL), F32),
        ],
        compiler_params=_cparams(("parallel", "parallel")),
        name="route",
    )(x, yf, yc, gf, wout_bf, g2, wq_bf, ka_bf, kb_bf)


SC_UROWS = 64
SC_UNITS = N_SEL // SC_UROWS
SC_NBUF = 3
SC_TOKB = 16
SC_WCH = D_WORDS // SC_LANES
SC_QUAD = 4
SC_KG = 8


def _sc_mesh():
    return plsc.VectorSubcoreMesh(core_axis_name="c", subcore_axis_name="s",
                                  num_cores=SC_CORES, num_subcores=SC_SUBCORES)


def _sc_cost(tokens):
    rows = tokens * N_SEL
    return pl.CostEstimate(flops=2 * rows * D_MODEL, transcendentals=0, bytes_accessed=rows * D_WORDS * 4)


def _sc_bf16(words):
    return plsc.bitcast(words, BF16)


def _sc_widen(pairs):
    w = plsc.bitcast(pairs, jnp.int32)
    lo = plsc.bitcast(lax.shift_left(w, 16), F32)
    hi = plsc.bitcast(lax.bitwise_and(w, jnp.int32(HI_MASK)), F32)
    return lo, hi


def _sc_quad_sum(row_chunk, scale):
    part = None
    for u in range(SC_QUAD):
        prod = _sc_bf16(row_chunk(u)) * scale(u)
        part = prod if part is None else part + prod
    return part


def _tree_sum(xs):
    while len(xs) > 1:
        xs = [xs[i] + xs[i + 1] for i in range(0, len(xs), 2)]
    return xs[0]


def _sc_token_loop(table_hbm, idx_hbm, side_hbm, out_hbm, idx_v, side_v, rows_v, out_v, gsem, osem,
                   tokens_per_worker, compute):
    wid = lax.axis_index("s") * SC_CORES + lax.axis_index("c")

    def gather(t, unit, buf):
        rows = idx_v.at[t, pl.ds(unit * SC_UROWS, SC_UROWS)]
        return pltpu.make_async_copy(table_hbm.at[rows], rows_v.at[buf], gsem.at[buf])

    def writeback(tok, slot):
        return pltpu.make_async_copy(out_v.at[slot], out_hbm.at[tok], osem.at[slot])

    @pl.loop(0, tokens_per_worker // SC_TOKB)
    def _(bi):
        tok0 = wid * tokens_per_worker + bi * SC_TOKB
        pltpu.sync_copy(idx_hbm.at[pl.ds(tok0, SC_TOKB)], idx_v)
        pltpu.sync_copy(side_hbm.at[pl.ds(tok0, SC_TOKB)], side_v)
        for unit in range(SC_UNITS):
            gather(0, unit, unit).start()

        @pl.loop(0, SC_TOKB)
        def _(t):
            slot = lax.rem(t, 2)

            @pl.when(t >= 2)
            def _():
                writeback(tok0 + t - 2, slot).wait()

            for unit in range(SC_UNITS):
                u = t * SC_UNITS + unit
                buf = lax.rem(u, SC_NBUF)
                gather(t, unit, buf).wait()

                @pl.when(t + 1 < SC_TOKB)
                def _():
                    gather(t + 1, unit, lax.rem(u + SC_UNITS, SC_NBUF)).start()

                compute(t, unit, buf, slot)
            writeback(tok0 + t, slot).start()

        for t in range(SC_TOKB - 2, SC_TOKB):
            writeback(tok0 + t, t % 2).wait()


def _peer_down(idx, htw, down_w):
    T = htw.shape[0]
    tpw = T // SC_WORKERS
    L = SC_LANES

    @functools.partial(
        pl.kernel, mesh=_sc_mesh(),
        out_type=jax.ShapeDtypeStruct((T, L * N_SEL), F32),
        scratch_types=[
            pltpu.VMEM((SC_TOKB, N_SEL), jnp.int32),
            pltpu.VMEM((SC_TOKB, D_WORDS), jnp.int32),
            pltpu.VMEM((SC_NBUF, SC_UROWS, D_WORDS), jnp.int32),
            pltpu.VMEM((2, L * N_SEL), F32),
            pltpu.SemaphoreType.DMA((SC_NBUF,)),
            pltpu.SemaphoreType.DMA((2,)),
        ],
        compiler_params=pltpu.CompilerParams(needs_layout_passes=False),
        name="peer_down", cost_estimate=_sc_cost(T),
    )
    def k(idx_hbm, h_hbm, down_hbm, s_hbm, idx_v, h_v, rows_v, s_v, gsem, osem):
        lane = lax.iota(jnp.int32, L)

        def compute(t, unit, buf, slot):
            svec = jnp.full((L,), slot, jnp.int32)
            hs = [_sc_bf16(h_v[t, pl.ds(w * L, L)]) for w in range(SC_WCH)]

            @plsc.parallel_loop(0, SC_UROWS, unroll=2)
            def _(r):
                parts = []
                for q in range(SC_WCH // SC_QUAD):
                    part = _sc_quad_sum(lambda u: rows_v[buf, r, pl.ds((q * SC_QUAD + u) * L, L)],
                                        lambda u: hs[q * SC_QUAD + u])
                    parts.extend(_sc_widen(part))
                col = lax.bitwise_and(lane + (unit * SC_UROWS + r), N_SEL - 1)
                plsc.store_scatter(s_v, [svec, lane * N_SEL + col], _tree_sum(parts))

        _sc_token_loop(down_hbm, idx_hbm, h_hbm, s_hbm, idx_v, h_v, rows_v, s_v, gsem, osem, tpw, compute)

    return k(idx, htw, down_w)


def _peer_up(idx, act_w, up_w):
    T = act_w.shape[0]
    tpw = T // SC_WORKERS
    L = SC_LANES

    @functools.partial(
        pl.kernel, mesh=_sc_mesh(),
        out_type=jax.ShapeDtypeStruct((T, D_MODEL), F32),
        scratch_types=[
            pltpu.VMEM((SC_TOKB, N_SEL), jnp.int32),
            pltpu.VMEM((SC_TOKB, N_SEL), jnp.int32),
            pltpu.VMEM((SC_NBUF, SC_UROWS, D_WORDS), jnp.int32),
            pltpu.VMEM((2, D_MODEL), F32),
            pltpu.SemaphoreType.DMA((SC_NBUF,)),
            pltpu.SemaphoreType.DMA((2,)),
        ],
        compiler_params=pltpu.CompilerParams(needs_layout_passes=False),
        name="peer_up", cost_estimate=_sc_cost(T),
    )
    def k(idx_hbm, act_hbm, up_hbm, o_hbm, idx_v, act_v, rows_v, o_v, gsem, osem):
        def compute(t, unit, buf, slot):
            tvec = jnp.full((L,), t, jnp.int32)
            for kg in range(SC_WCH // SC_KG):
                lo_cols = [pl.ds((kg * SC_KG + kk) * L, L) for kk in range(SC_KG)]
                hi_cols = [pl.ds(D_WORDS + (kg * SC_KG + kk) * L, L) for kk in range(SC_KG)]
                if unit == 0:
                    init = tuple(jnp.zeros((L,), F32) for _ in range(2 * SC_KG))
                else:
                    init = tuple(o_v[slot, col] for col in lo_cols + hi_cols)

                def rgbody(rg, accs):
                    first = jnp.full((L,), unit * SC_UROWS, jnp.int32) + rg * SC_QUAD
                    a = [_sc_bf16(plsc.load_gather(act_v, [tvec, first + u])) for u in range(SC_QUAD)]
                    out_lo, out_hi = [], []
                    for kk in range(SC_KG):
                        part = _sc_quad_sum(lambda u: rows_v[buf, rg * SC_QUAD + u, lo_cols[kk]], lambda u: a[u])
                        lo, hi = _sc_widen(part)
                        out_lo.append(accs[kk] + lo)
                        out_hi.append(accs[SC_KG + kk] + hi)
                    return tuple(out_lo + out_hi)
                accs = lax.fori_loop(0, SC_UROWS // SC_QUAD, rgbody, init)
                for kk in range(SC_KG):
                    o_v[slot, lo_cols[kk]] = accs[kk]
                    o_v[slot, hi_cols[kk]] = accs[SC_KG + kk]

        _sc_token_loop(up_hbm, idx_hbm, act_hbm, o_hbm, idx_v, act_v, rows_v, o_v, gsem, osem, tpw, compute)

    return k(idx, act_w, up_w)


def _act_kernel(s_ref, gate_ref, a_ref):
    s = _tree_sum([pltpu.roll(s_ref[:, l * N_SEL:(l + 1) * N_SEL], (N_SEL - l) % N_SEL, axis=1)
                   for l in range(SC_LANES)])
    act = 0.5 * s * (1.0 + lax.erf(s * math.sqrt(0.5))) * gate_ref[...]
    bits = _bf16_bits(act)
    a_ref[...] = lax.bitwise_or(lax.shift_right_logical(bits, 16), bits)


def _act(s, gate, tm):
    T = s.shape[0]
    spec = pl.BlockSpec((tm, N_SEL), lambda i: (i, 0))
    part_spec = pl.BlockSpec((tm, SC_LANES * N_SEL), lambda i: (i, 0))
    return pl.pallas_call(
        _act_kernel, grid=(T // tm,), in_specs=[part_spec, spec], out_specs=spec,
        out_shape=jax.ShapeDtypeStruct((T, N_SEL), jnp.int32),
        compiler_params=_cparams(("parallel",)), name="expert_act",
    )(s, gate)


def _final_kernel(x1_ref, p_ref, g_ref, o_ref):
    o_ref[...] = _rms(x1_ref[...] + p_ref[...], g_ref[...])


def _final(x1, peer, g, tm):
    T = x1.shape[0]
    spec = pl.BlockSpec((tm, D_MODEL), lambda i: (i, 0))
    return pl.pallas_call(
        _final_kernel, grid=(T // tm,),
        in_specs=[spec, spec, pl.BlockSpec((1, D_MODEL), lambda i: (0, 0))], out_specs=spec,
        out_shape=jax.ShapeDtypeStruct((T, D_MODEL), F32),
        compiler_params=_cparams(("parallel",)), name="final_norm",
    )(x1, peer, g)


PROMPT_GROUP = 2
SKEW = 2


def _tile(n, want):
    t = min(n, want)
    assert n % t == 0
    return t


def _channel_dft():
    ang = 2.0 * np.pi * np.outer(np.arange(GDIM), np.arange(GDIM)) / GDIM
    m = np.concatenate([np.cos(ang), -np.sin(ang)], axis=1) / math.sqrt(GDIM)
    return jnp.asarray(m.astype(np.float32), BF16)


def _mix_and_route(x, p):
    B, S, _ = x.shape
    T = B * S
    assert T % (SC_WORKERS * SC_TOKB) == 0
    zg, hg = _inproj(x, p["g1"], p["win"], p["cdft"], _tile(S, 512))
    yf = _seq_dft(zg)
    yc = _conv_module(hg, p["conv_w"], p["conv_b"], p["ln_g"], p["ln_b"], p["gn_c"], _tile(S, 512))
    x1, htw, idx, gate = _route(x, yf, yc, p["gn_f"], p["wout"], p["g2"], p["wq"], p["ka"], p["kb"],
                                _tile(S, 256))
    return (x1.reshape(T, D_MODEL), htw.reshape(T, D_WORDS), idx.reshape(T, N_SEL), gate.reshape(T, N_SEL))


def _encode(groups, p):
    x1s, idxs, acts, peers = [], [], [], []
    for g, x in enumerate(groups):
        if g >= SKEW + 1:
            x, acts[g - SKEW], peers[g - SKEW - 1] = lax.optimization_barrier(
                (x, acts[g - SKEW], peers[g - SKEW - 1]))
        elif g >= SKEW:
            x, acts[g - SKEW] = lax.optimization_barrier((x, acts[g - SKEW]))
        if g >= SKEW:
            peers.append(_peer_up(idxs[g - SKEW], acts[g - SKEW], p["up"]))
        x1, htw, idx, gate = _mix_and_route(x, p)
        s = _peer_down(idx, htw, p["down"])
        x1s.append(x1)
        idxs.append(idx)
        acts.append(_act(s, gate, _tile(x1.shape[0], 1024)))
    for g in range(len(peers), len(groups)):
        peers.append(_peer_up(idxs[g], acts[g], p["up"]))
    return [_final(x1, peer, p["final_g"], _tile(x1.shape[0], 512)).reshape(x.shape)
            for x, x1, peer in zip(groups, x1s, peers)]


def kernel(x_prompt, x_sample, norm1_g, w_in, conv_w, conv_b, conv_ln_g, conv_ln_b, gn_fourier_g,
           gn_conv_g, w_out, norm2_g, w_q, keys_a, keys_b, expert_down, expert_up, final_g):
    assert norm1_g.shape[0] == 1
    p = dict(
        g1=norm1_g[0].reshape(1, D_MODEL), win=w_in[0].astype(BF16), cdft=_channel_dft(),
        conv_w=conv_w[0], conv_b=conv_b[0].reshape(1, D_CONV),
        ln_g=conv_ln_g[0].reshape(1, D_CONV), ln_b=conv_ln_b[0].reshape(1, D_CONV),
        gn_f=gn_fourier_g[0].reshape(1, D_FOURIER), gn_c=gn_conv_g[0].reshape(1, D_CONV),
        wout=w_out[0].astype(BF16), g2=norm2_g[0].reshape(1, D_MODEL), wq=w_q[0].astype(BF16),
        ka=keys_a[0].astype(BF16), kb=keys_b[0].astype(BF16),
        down=_pack_halves(expert_down[0]), up=_pack_halves(expert_up[0]),
        final_g=final_g.reshape(1, D_MODEL),
    )
    nb = x_prompt.shape[0]
    step = PROMPT_GROUP if nb % PROMPT_GROUP == 0 else nb
    groups = [x_prompt[i:i + step] for i in range(0, nb, step)] + [x_sample]
    outs = _encode(groups, p)
    return jnp.concatenate(outs[:-1], axis=0), outs[-1]
```

```python
import functools
import math

import numpy as np
import jax
import jax.numpy as jnp
from jax import lax
from jax.experimental import pallas as pl
from jax.experimental.pallas import tpu as pltpu
from jax.experimental.pallas import tpu_sc as plsc

F32 = jnp.float32
BF16 = jnp.bfloat16

D_MODEL = 1024
D_FOURIER = 512
GROUPS = 4
GDIM = 128
D_CONV = 512
CONV_W = 31
CONV_PAD = 15
HEADS = 8
N_KEYS = 128
TOPK = 16
D_HALF = 128
D_QUERY = 256
N_SEL = HEADS * TOPK
EPS = 1e-6

SUBLANES = 8
LANES = 128
VMEM_LIMIT = 56 * 1024 * 1024

SC_CORES = 2
SC_SUBCORES = 16
SC_LANES = 16
SC_WORKERS = SC_CORES * SC_SUBCORES


def _cparams(sem):
    return pltpu.CompilerParams(dimension_semantics=sem, vmem_limit_bytes=VMEM_LIMIT)


def _rms(x, g):
    return x * lax.rsqrt(jnp.mean(x * x, axis=-1, keepdims=True) + EPS) * g


D_WORDS = D_MODEL // 2
HI_MASK = -65536


def _bf16_bits(x):
    return lax.bitcast_convert_type(x.astype(BF16).astype(F32), jnp.int32)


def _pack_halves(x):
    lo = lax.shift_right_logical(_bf16_bits(x[..., :D_WORDS]), 16)
    hi = lax.bitwise_and(_bf16_bits(x[..., D_WORDS:]), HI_MASK)
    return lax.bitwise_or(lo, hi)


def _inproj_kernel(x_ref, g1_ref, win_ref, cdft_ref, zg_ref, hg_ref):
    h = _rms(x_ref[...], g1_ref[...]).astype(BF16)
    z = jnp.dot(h, win_ref[...], preferred_element_type=F32)
    for g in range(GROUPS):
        zg = z[:, g * GDIM:(g + 1) * GDIM].astype(BF16)
        zg_ref[g] = jnp.dot(zg, cdft_ref[...], preferred_element_type=F32)
    a = z[:, D_FOURIER:D_FOURIER + D_CONV]
    gate = z[:, D_FOURIER + D_CONV:]
    hg_ref[...] = a * jax.nn.sigmoid(gate)


def _inproj(x, g1, win_bf, cdft, tm):
    B, S, _ = x.shape
    return pl.pallas_call(
        _inproj_kernel,
        grid=(B, S // tm),
        in_specs=[
            pl.BlockSpec((None, tm, D_MODEL), lambda b, m: (b, m, 0)),
            pl.BlockSpec((1, D_MODEL), lambda b, m: (0, 0)),
            pl.BlockSpec((D_MODEL, D_FOURIER + 2 * D_CONV), lambda b, m: (0, 0)),
            pl.BlockSpec((GDIM, 2 * GDIM), lambda b, m: (0, 0)),
        ],
        out_specs=[
            pl.BlockSpec((None, GROUPS, tm, 2 * GDIM), lambda b, m: (b, 0, m, 0)),
            pl.BlockSpec((None, tm, D_CONV), lambda b, m: (b, m, 0)),
        ],
        out_shape=[
            jax.ShapeDtypeStruct((B, GROUPS, S, 2 * GDIM), F32),
            jax.ShapeDtypeStruct((B, S, D_CONV), F32),
        ],
        compiler_params=_cparams(("parallel", "parallel")),
        name="inproj",
    )(x, g1, win_bf, cdft)


def _fft1_kernel(z_ref, k1_ref, tc_ref, ts_ref, a_ref):
    n1 = z_ref.shape[0]
    rows = n1 * SUBLANES
    x = z_ref[...].reshape(rows, 2 * GDIM).astype(BF16)
    p = jnp.dot(k1_ref[...], x, preferred_element_type=F32)
    pc, ps = p[:rows], p[rows:]
    ar = pc[:, :GDIM] + ps[:, GDIM:]
    ai = pc[:, GDIM:] - ps[:, :GDIM]
    tc, ts = tc_ref[...], ts_ref[...]
    a_ref[:, :, :GDIM] = (ar * tc + ai * ts).reshape(n1, SUBLANES, GDIM)
    a_ref[:, :, GDIM:] = (ai * tc - ar * ts).reshape(n1, SUBLANES, GDIM)


def _fft2_kernel(a_ref, c2_ref, s2_ref, y_ref):
    for r in range(SUBLANES):
        blk = a_ref[r].astype(BF16)
        y = jnp.dot(c2_ref[...], blk[:, :GDIM], preferred_element_type=F32)
        y = y + jnp.dot(s2_ref[...], blk[:, GDIM:], preferred_element_type=F32)
        y_ref[:, r * GDIM:(r + 1) * GDIM] = y


def _split_seq(S):
    n1 = 1 << (int(math.log2(S)) // 2)
    return n1, S // n1


@functools.lru_cache(maxsize=None)
def _fft_tables(S):
    n1, n2 = _split_seq(S)
    k1 = np.arange(n1)
    ang1 = 2.0 * np.pi * np.outer(k1, np.arange(n1)) / n1
    eye = np.eye(SUBLANES)
    kmat = np.concatenate([np.kron(np.cos(ang1), eye), np.kron(np.sin(ang1), eye)], axis=0)
    nn2 = np.arange(n2).reshape(n2 // SUBLANES, 1, SUBLANES)
    angt = 2.0 * np.pi * k1.reshape(1, n1, 1) * nn2 / S
    angt = angt.reshape(n2 // SUBLANES, n1 * SUBLANES, 1)
    tc = np.broadcast_to(np.cos(angt), (n2 // SUBLANES, n1 * SUBLANES, GDIM))
    ts = np.broadcast_to(np.sin(angt), (n2 // SUBLANES, n1 * SUBLANES, GDIM))
    ang2 = 2.0 * np.pi * np.outer(np.arange(n2), np.arange(n2)) / n2
    scale = 1.0 / math.sqrt(S)
    return (kmat.astype(np.float32), np.ascontiguousarray(tc, np.float32),
            np.ascontiguousarray(ts, np.float32),
            (np.cos(ang2) * scale).astype(np.float32), (np.sin(ang2) * scale).astype(np.float32))


def _seq_dft(zg):
    B, G, S, _ = zg.shape
    n1, n2 = _split_seq(S)
    kmat, tc, ts, c2, s2 = _fft_tables(S)
    kmat = jnp.asarray(kmat, BF16)
    c2 = jnp.asarray(c2, BF16)
    s2 = jnp.asarray(s2, BF16)
    nb = n2 // SUBLANES
    z6 = zg.reshape(B, G, n1, nb, SUBLANES, 2 * GDIM)
    blk6 = (None, None, n1, None, SUBLANES, 2 * GDIM)
    a6 = pl.pallas_call(
        _fft1_kernel,
        grid=(B, G, nb),
        in_specs=[
            pl.BlockSpec(blk6, lambda b, g, j: (b, g, 0, j, 0, 0)),
            pl.BlockSpec((2 * n1 * SUBLANES, n1 * SUBLANES), lambda b, g, j: (0, 0)),
            pl.BlockSpec((None, n1 * SUBLANES, GDIM), lambda b, g, j: (j, 0, 0)),
            pl.BlockSpec((None, n1 * SUBLANES, GDIM), lambda b, g, j: (j, 0, 0)),
        ],
        out_specs=pl.BlockSpec(blk6, lambda b, g, j: (b, g, 0, j, 0, 0)),
        out_shape=jax.ShapeDtypeStruct(z6.shape, F32),
        compiler_params=_cparams(("parallel", "parallel", "parallel")),
        name="fft_stage1",
    )(z6, kmat, jnp.asarray(tc), jnp.asarray(ts))
    a5 = a6.reshape(B, G, n1, n2, 2 * GDIM)
    y = pl.pallas_call(
        _fft2_kernel,
        grid=(B, G, n1 // SUBLANES),
        in_specs=[
            pl.BlockSpec((None, None, SUBLANES, n2, 2 * GDIM), lambda b, g, i: (b, g, i, 0, 0)),
            pl.BlockSpec((n2, n2), lambda b, g, i: (0, 0)),
            pl.BlockSpec((n2, n2), lambda b, g, i: (0, 0)),
        ],
        out_specs=pl.BlockSpec((None, None, n2, SUBLANES * GDIM), lambda b, g, i: (b, g, 0, i)),
        out_shape=jax.ShapeDtypeStruct((B, G, n2, n1 * GDIM), F32),
        compiler_params=_cparams(("parallel", "parallel", "parallel")),
        name="fft_stage2",
    )(a5, c2, s2)
    return y.reshape(B, G, S, GDIM)


CONV_HALO = 16
CONV_ROWS = 16


def _conv_kernel(prev_ref, cur_ref, next_ref, w_ref, cb_ref, lg_ref, lb_ref, gc_ref, o_ref, scr):
    ts = cur_ref.shape[0]
    m = pl.program_id(1)
    last = pl.num_programs(1) - 1
    scr[0:CONV_HALO, :] = jnp.where(m > 0, prev_ref[...], 0.0)
    scr[CONV_HALO:CONV_HALO + ts, :] = cur_ref[...]
    scr[CONV_HALO + ts:, :] = jnp.where(m < last, next_ref[...], 0.0)
    off = CONV_HALO - CONV_PAD

    def step(c, carry):
        base = pl.multiple_of(c * CONV_ROWS, CONV_ROWS)
        win = scr[pl.ds(base, CONV_ROWS + 2 * CONV_HALO), :]
        acc = jnp.broadcast_to(cb_ref[...], (CONV_ROWS, D_CONV))
        for k in range(CONV_W):
            acc = acc + win[off + k:off + k + CONV_ROWS, :] * w_ref[k:k + 1, :]
        mu = jnp.mean(acc, axis=-1, keepdims=True)
        d = acc - mu
        var = jnp.mean(d * d, axis=-1, keepdims=True)
        y = d * lax.rsqrt(var + EPS) * lg_ref[...] + lb_ref[...]
        y = y * jax.nn.sigmoid(y)
        o_ref[pl.ds(base, CONV_ROWS), :] = _rms(y, gc_ref[...])
        return carry

    lax.fori_loop(0, ts // CONV_ROWS, step, 0)


def _conv_module(hg, conv_w, conv_b, ln_g, ln_b, gn_c, ts):
    B, S, _ = hg.shape
    hb = ts // CONV_HALO
    nh = S // CONV_HALO
    vec = pl.BlockSpec((1, D_CONV), lambda b, m: (0, 0))
    return pl.pallas_call(
        _conv_kernel,
        grid=(B, S // ts),
        in_specs=[
            pl.BlockSpec((None, CONV_HALO, D_CONV), lambda b, m: (b, jnp.maximum(m * hb - 1, 0), 0)),
            pl.BlockSpec((None, ts, D_CONV), lambda b, m: (b, m, 0)),
            pl.BlockSpec((None, CONV_HALO, D_CONV), lambda b, m: (b, jnp.minimum((m + 1) * hb, nh - 1), 0)),
            pl.BlockSpec((CONV_W, D_CONV), lambda b, m: (0, 0)),
            vec, vec, vec, vec,
        ],
        out_specs=pl.BlockSpec((None, ts, D_CONV), lambda b, m: (b, m, 0)),
        out_shape=jax.ShapeDtypeStruct((B, S, D_CONV), F32),
        scratch_shapes=[pltpu.VMEM((ts + 2 * CONV_HALO, D_CONV), F32)],
        compiler_params=_cparams(("parallel", "parallel")),
        name="conv_module",
    )(hg, hg, hg, conv_w, conv_b, ln_g, ln_b, gn_c)


def _extract_topk(s, order, k, payload=None):
    big = jnp.int32(1 << 30)
    vals, sel = [], []
    for _ in range(k):
        m = jnp.max(s, axis=0, keepdims=True)
        am = jnp.min(jnp.where(s == m, order, big), axis=0, keepdims=True)
        hit = order == am
        vals.append(m)
        if payload is None:
            sel.append(am)
        else:
            sel.append(jnp.max(jnp.where(hit, payload, -1), axis=0, keepdims=True))
        s = jnp.where(hit, -jnp.inf, s)
    return jnp.concatenate(vals, axis=0), jnp.concatenate(sel, axis=0)


_CAND_A = 4
_CAND_B = 3


def _route_kernel(x_ref, yf_ref, yc_ref, gf_ref, wout_ref, g2_ref, wq_ref, ka_ref, kb_ref,
                  x1_ref, htw_ref, idx_ref, gate_ref):
    tm = x_ref.shape[0]
    yf = jnp.concatenate([yf_ref[g] for g in range(GROUPS)], axis=-1)
    yfn = _rms(yf, gf_ref[...])
    ycat = jnp.concatenate([yfn, yc_ref[...]], axis=-1).astype(BF16)
    x1 = x_ref[...] + jnp.dot(ycat, wout_ref[...], preferred_element_type=F32)
    x1_ref[...] = x1
    ht = _rms(x1, g2_ref[...])
    htw_ref[...] = _pack_halves(ht)
    q = jnp.dot(ht.astype(BF16), wq_ref[...], preferred_element_type=F32)

    row16 = lax.broadcasted_iota(jnp.int32, (TOPK, tm), 0)
    key_iota = lax.broadcasted_iota(jnp.int32, (N_KEYS, tm), 0)
    nt = (((1,), (1,)), ((), ()))
    idx_rows, gate_rows = [], []
    for h in range(HEADS):
        qa = q[:, h * D_QUERY:h * D_QUERY + D_HALF].astype(BF16)
        qb = q[:, h * D_QUERY + D_HALF:(h + 1) * D_QUERY].astype(BF16)
        sa = lax.dot_general(ka_ref[h], qa, nt, preferred_element_type=F32)
        sb = lax.dot_general(kb_ref[h], qb, nt, preferred_element_type=F32)
        va, ia = _extract_topk(sa, key_iota, TOPK)
        vb, ib = _extract_topk(sb, key_iota, TOPK)
        cs, ce, co = [], [], []
        for i in range(_CAND_A):
            ok = (i + 1) * (row16 + 1) <= TOPK
            cs.append(jnp.where(ok, va[i:i + 1] + vb, -jnp.inf))
            ce.append(ia[i:i + 1] * N_KEYS + ib)
            co.append(i * TOPK + row16)
        for j in range(_CAND_B):
            ok = jnp.logical_and(row16 >= _CAND_A, (row16 + 1) * (j + 1) <= TOPK)
            cs.append(jnp.where(ok, va + vb[j:j + 1], -jnp.inf))
            ce.append(ia * N_KEYS + ib[j:j + 1])
            co.append(row16 * TOPK + j)
        top_s, top_e = _extract_topk(jnp.concatenate(cs, axis=0), jnp.concatenate(co, axis=0),
                                     TOPK, payload=jnp.concatenate(ce, axis=0))
        ex = jnp.exp(top_s - top_s[0:1])
        gate_rows.append(ex / jnp.sum(ex, axis=0, keepdims=True))
        idx_rows.append(top_e)
    idx_ref[...] = jnp.concatenate(idx_rows, axis=0).T
    gate_ref[...] = jnp.concatenate(gate_rows, axis=0).T


def _route(x, yf, yc, gf, wout_bf, g2, wq_bf, ka_bf, kb_bf, tm, s0, rows):
    B = x.shape[0]
    S = rows
    m0 = s0 // tm
    assert s0 % tm == 0 and rows % tm == 0
    const2 = lambda b, m: (0, 0)
    const3 = lambda b, m: (0, 0, 0)
    row = lambda w: pl.BlockSpec((None, tm, w), lambda b, m: (b, m, 0))
    row_in = lambda w: pl.BlockSpec((None, tm, w), lambda b, m: (b, m + m0, 0))
    return pl.pallas_call(
        _route_kernel,
        grid=(B, S // tm),
        in_specs=[
            row_in(D_MODEL),
            pl.BlockSpec((None, GROUPS, tm, GDIM), lambda b, m: (b, 0, m + m0, 0)),
            row_in(D_CONV),
            pl.BlockSpec((1, D_FOURIER), const2),
            pl.BlockSpec((D_MODEL, D_MODEL), const2),
            pl.BlockSpec((1, D_MODEL), const2),
            pl.BlockSpec((D_MODEL, HEADS * D_QUERY), const2),
            pl.BlockSpec((HEADS, N_KEYS, D_HALF), const3),
            pl.BlockSpec((HEADS, N_KEYS, D_HALF), const3),
        ],
        out_specs=[row(D_MODEL), row(D_WORDS), row(N_SEL), row(N_SEL)],
        out_shape=[
            jax.ShapeDtypeStruct((B, S, D_MODEL), F32),
            jax.ShapeDtypeStruct((B, S, D_WORDS), jnp.int32),
            jax.ShapeDtypeStruct((B, S, N_SEL), jnp.int32),
            jax.ShapeDtypeStruct((B, S, N_SEL), F32),
        ],
        compiler_params=_cparams(("parallel", "parallel")),
        name="route",
    )(x, yf, yc, gf, wout_bf, g2, wq_bf, ka_bf, kb_bf)


SC_UROWS = 64
SC_UNITS = N_SEL // SC_UROWS
SC_NBUF = 3
SC_TOKB = 16
SC_WCH = D_WORDS // SC_LANES
SC_QUAD = 4
SC_KG = 8
SC_COST_WEIGHT = 16


def _sc_mesh():
    return plsc.VectorSubcoreMesh(core_axis_name="c", subcore_axis_name="s",
                                  num_cores=SC_CORES, num_subcores=SC_SUBCORES)


def _sc_cost(tokens):
    rows = tokens * N_SEL * SC_COST_WEIGHT
    return pl.CostEstimate(flops=2 * rows * D_MODEL, transcendentals=0, bytes_accessed=rows * D_WORDS * 4)


def _sc_bf16(words):
    return plsc.bitcast(words, BF16)


def _sc_widen(pairs):
    w = plsc.bitcast(pairs, jnp.int32)
    lo = plsc.bitcast(lax.shift_left(w, 16), F32)
    hi = plsc.bitcast(lax.bitwise_and(w, jnp.int32(HI_MASK)), F32)
    return lo, hi


def _sc_quad_sum(row_chunk, scale):
    part = None
    for u in range(SC_QUAD):
        prod = _sc_bf16(row_chunk(u)) * scale(u)
        part = prod if part is None else part + prod
    return part


def _tree_sum(xs):
    while len(xs) > 1:
        xs = [xs[i] + xs[i + 1] for i in range(0, len(xs), 2)]
    return xs[0]


def _sc_token_loop(table_hbm, idx_hbm, side_hbm, out_hbm, idx_v, side_v, rows_v, out_v, gsem, osem,
                   tokens_per_worker, compute):
    wid = lax.axis_index("s") * SC_CORES + lax.axis_index("c")

    def gather(t, unit, buf):
        rows = idx_v.at[t, pl.ds(unit * SC_UROWS, SC_UROWS)]
        return pltpu.make_async_copy(table_hbm.at[rows], rows_v.at[buf], gsem.at[buf])

    def writeback(tok, slot):
        return pltpu.make_async_copy(out_v.at[slot], out_hbm.at[tok], osem.at[slot])

    @pl.loop(0, tokens_per_worker // SC_TOKB)
    def _(bi):
        tok0 = wid * tokens_per_worker + bi * SC_TOKB
        pltpu.sync_copy(idx_hbm.at[pl.ds(tok0, SC_TOKB)], idx_v)
        pltpu.sync_copy(side_hbm.at[pl.ds(tok0, SC_TOKB)], side_v)
        for unit in range(SC_UNITS):
            gather(0, unit, unit).start()

        @pl.loop(0, SC_TOKB)
        def _(t):
            slot = lax.rem(t, 2)

            @pl.when(t >= 2)
            def _():
                writeback(tok0 + t - 2, slot).wait()

            for unit in range(SC_UNITS):
                u = t * SC_UNITS + unit
                buf = lax.rem(u, SC_NBUF)
                gather(t, unit, buf).wait()

                @pl.when(t + 1 < SC_TOKB)
                def _():
                    gather(t + 1, unit, lax.rem(u + SC_UNITS, SC_NBUF)).start()

                compute(t, unit, buf, slot)
            writeback(tok0 + t, slot).start()

        for t in range(SC_TOKB - 2, SC_TOKB):
            writeback(tok0 + t, t % 2).wait()


def _peer_down(idx, htw, down_w):
    T = htw.shape[0]
    tpw = T // SC_WORKERS
    L = SC_LANES

    @functools.partial(
        pl.kernel, mesh=_sc_mesh(),
        out_type=jax.ShapeDtypeStruct((T, L * N_SEL), F32),
        scratch_types=[
            pltpu.VMEM((SC_TOKB, N_SEL), jnp.int32),
            pltpu.VMEM((SC_TOKB, D_WORDS), jnp.int32),
            pltpu.VMEM((SC_NBUF, SC_UROWS, D_WORDS), jnp.int32),
            pltpu.VMEM((2, L * N_SEL), F32),
            pltpu.SemaphoreType.DMA((SC_NBUF,)),
            pltpu.SemaphoreType.DMA((2,)),
        ],
        compiler_params=pltpu.CompilerParams(needs_layout_passes=False),
        name="peer_down", cost_estimate=_sc_cost(T),
    )
    def k(idx_hbm, h_hbm, down_hbm, s_hbm, idx_v, h_v, rows_v, s_v, gsem, osem):
        lane = lax.iota(jnp.int32, L)

        def compute(t, unit, buf, slot):
            svec = jnp.full((L,), slot, jnp.int32)
            hs = [_sc_bf16(h_v[t, pl.ds(w * L, L)]) for w in range(SC_WCH)]

            @plsc.parallel_loop(0, SC_UROWS, unroll=2)
            def _(r):
                parts = []
                for q in range(SC_WCH // SC_QUAD):
                    part = _sc_quad_sum(lambda u: rows_v[buf, r, pl.ds((q * SC_QUAD + u) * L, L)],
                                        lambda u: hs[q * SC_QUAD + u])
                    parts.extend(_sc_widen(part))
                col = lax.bitwise_and(lane + (unit * SC_UROWS + r), N_SEL - 1)
                plsc.store_scatter(s_v, [svec, lane * N_SEL + col], _tree_sum(parts))

        _sc_token_loop(down_hbm, idx_hbm, h_hbm, s_hbm, idx_v, h_v, rows_v, s_v, gsem, osem, tpw, compute)

    return k(idx, htw, down_w)


def _peer_up(idx, act_w, up_w):
    T = act_w.shape[0]
    tpw = T // SC_WORKERS
    L = SC_LANES

    @functools.partial(
        pl.kernel, mesh=_sc_mesh(),
        out_type=jax.ShapeDtypeStruct((T, D_MODEL), F32),
        scratch_types=[
            pltpu.VMEM((SC_TOKB, N_SEL), jnp.int32),
            pltpu.VMEM((SC_TOKB, N_SEL), jnp.int32),
            pltpu.VMEM((SC_NBUF, SC_UROWS, D_WORDS), jnp.int32),
            pltpu.VMEM((2, D_MODEL), F32),
            pltpu.SemaphoreType.DMA((SC_NBUF,)),
            pltpu.SemaphoreType.DMA((2,)),
        ],
        compiler_params=pltpu.CompilerParams(needs_layout_passes=False),
        name="peer_up", cost_estimate=_sc_cost(T),
    )
    def k(idx_hbm, act_hbm, up_hbm, o_hbm, idx_v, act_v, rows_v, o_v, gsem, osem):
        def compute(t, unit, buf, slot):
            tvec = jnp.full((L,), t, jnp.int32)
            for kg in range(SC_WCH // SC_KG):
                lo_cols = [pl.ds((kg * SC_KG + kk) * L, L) for kk in range(SC_KG)]
                hi_cols = [pl.ds(D_WORDS + (kg * SC_KG + kk) * L, L) for kk in range(SC_KG)]
                if unit == 0:
                    init = tuple(jnp.zeros((L,), F32) for _ in range(2 * SC_KG))
                else:
                    init = tuple(o_v[slot, col] for col in lo_cols + hi_cols)

                def rgbody(rg, accs):
                    first = jnp.full((L,), unit * SC_UROWS, jnp.int32) + rg * SC_QUAD
                    a = [_sc_bf16(plsc.load_gather(act_v, [tvec, first + u])) for u in range(SC_QUAD)]
                    out_lo, out_hi = [], []
                    for kk in range(SC_KG):
                        part = _sc_quad_sum(lambda u: rows_v[buf, rg * SC_QUAD + u, lo_cols[kk]], lambda u: a[u])
                        lo, hi = _sc_widen(part)
                        out_lo.append(accs[kk] + lo)
                        out_hi.append(accs[SC_KG + kk] + hi)
                    return tuple(out_lo + out_hi)
                accs = lax.fori_loop(0, SC_UROWS // SC_QUAD, rgbody, init)
                for kk in range(SC_KG):
                    o_v[slot, lo_cols[kk]] = accs[kk]
                    o_v[slot, hi_cols[kk]] = accs[SC_KG + kk]

        _sc_token_loop(up_hbm, idx_hbm, act_hbm, o_hbm, idx_v, act_v, rows_v, o_v, gsem, osem, tpw, compute)

    return k(idx, act_w, up_w)


def _act_kernel(s_ref, gate_ref, a_ref):
    s = _tree_sum([pltpu.roll(s_ref[:, l * N_SEL:(l + 1) * N_SEL], (N_SEL - l) % N_SEL, axis=1)
                   for l in range(SC_LANES)])
    act = 0.5 * s * (1.0 + lax.erf(s * math.sqrt(0.5))) * gate_ref[...]
    bits = _bf16_bits(act)
    a_ref[...] = lax.bitwise_or(lax.shift_right_logical(bits, 16), bits)


def _act(s, gate, tm):
    T = s.shape[0]
    spec = pl.BlockSpec((tm, N_SEL), lambda i: (i, 0))
    part_spec = pl.BlockSpec((tm, SC_LANES * N_SEL), lambda i: (i, 0))
    return pl.pallas_call(
        _act_kernel, grid=(T // tm,), in_specs=[part_spec, spec], out_specs=spec,
        out_shape=jax.ShapeDtypeStruct((T, N_SEL), jnp.int32),
        compiler_params=_cparams(("parallel",)), name="expert_act",
    )(s, gate)


def _final_kernel(x1_ref, p_ref, g_ref, o_ref):
    o_ref[...] = _rms(x1_ref[...] + p_ref[...], g_ref[...])


def _final(x1, peer, g, tm):
    T = x1.shape[0]
    spec = pl.BlockSpec((tm, D_MODEL), lambda i: (i, 0))
    return pl.pallas_call(
        _final_kernel, grid=(T // tm,),
        in_specs=[spec, spec, pl.BlockSpec((1, D_MODEL), lambda i: (0, 0))], out_specs=spec,
        out_shape=jax.ShapeDtypeStruct((T, D_MODEL), F32),
        compiler_params=_cparams(("parallel",)), name="final_norm",
    )(x1, peer, g)


PROMPT_GROUP = 2
SAMPLE_PARTS = 2
SKEW = 2


def _tile(n, want):
    t = min(n, want)
    assert n % t == 0
    return t


def _channel_dft():
    ang = 2.0 * np.pi * np.outer(np.arange(GDIM), np.arange(GDIM)) / GDIM
    m = np.concatenate([np.cos(ang), -np.sin(ang)], axis=1) / math.sqrt(GDIM)
    return jnp.asarray(m.astype(np.float32), BF16)


def _mixer(x, p):
    S = x.shape[1]
    zg, hg = _inproj(x, p["g1"], p["win"], p["cdft"], _tile(S, 512))
    yf = _seq_dft(zg)
    yc = _conv_module(hg, p["conv_w"], p["conv_b"], p["ln_g"], p["ln_b"], p["gn_c"], _tile(S, 512))
    return yf, yc


def _encode(units, p):
    x1s, idxs, acts, peers = [], [], [], []
    mixed, mixed_x = None, None
    for g, (x, s0, rows) in enumerate(units):
        reuse = mixed_x is x
        carrier = mixed if reuse else x
        tied = [acts[g - SKEW]] if g >= SKEW else []
        if g >= SKEW + 1:
            tied.append(peers[g - SKEW - 1])
        if tied:
            carrier, *tied = lax.optimization_barrier((carrier, *tied))
            acts[g - SKEW] = tied[0]
            if len(tied) > 1:
                peers[g - SKEW - 1] = tied[1]
        if g >= SKEW:
            peers.append(_peer_up(idxs[g - SKEW], acts[g - SKEW], p["up"]))
        if reuse:
            mixed = carrier
        else:
            x = carrier
            mixed, mixed_x = _mixer(x, p), units[g][0]
        T = x.shape[0] * rows
        assert T % (SC_WORKERS * SC_TOKB) == 0
        x1, htw, idx, gate = _route(x, mixed[0], mixed[1], p["gn_f"], p["wout"], p["g2"], p["wq"], p["ka"],
                                    p["kb"], _tile(rows, 256), s0, rows)
        idx = idx.reshape(T, N_SEL)
        s = _peer_down(idx, htw.reshape(T, D_WORDS), p["down"])
        x1s.append(x1.reshape(T, D_MODEL))
        idxs.append(idx)
        acts.append(_act(s, gate.reshape(T, N_SEL), _tile(T, 1024)))
    for g in range(len(peers), len(units)):
        peers.append(_peer_up(idxs[g], acts[g], p["up"]))
    return [_final(x1, peer, p["final_g"], _tile(x1.shape[0], 512)).reshape(x.shape[0], rows, D_MODEL)
            for (x, _, rows), x1, peer in zip(units, x1s, peers)]


def kernel(x_prompt, x_sample, norm1_g, w_in, conv_w, conv_b, conv_ln_g, conv_ln_b, gn_fourier_g,
           gn_conv_g, w_out, norm2_g, w_q, keys_a, keys_b, expert_down, expert_up, final_g):
    assert norm1_g.shape[0] == 1
    p = dict(
        g1=norm1_g[0].reshape(1, D_MODEL), win=w_in[0].astype(BF16), cdft=_channel_dft(),
        conv_w=conv_w[0], conv_b=conv_b[0].reshape(1, D_CONV),
        ln_g=conv_ln_g[0].reshape(1, D_CONV), ln_b=conv_ln_b[0].reshape(1, D_CONV),
        gn_f=gn_fourier_g[0].reshape(1, D_FOURIER), gn_c=gn_conv_g[0].reshape(1, D_CONV),
        wout=w_out[0].astype(BF16), g2=norm2_g[0].reshape(1, D_MODEL), wq=w_q[0].astype(BF16),
        ka=keys_a[0].astype(BF16), kb=keys_b[0].astype(BF16),
        down=_pack_halves(expert_down[0]), up=_pack_halves(expert_up[0]),
        final_g=final_g.reshape(1, D_MODEL),
    )
    nb = x_prompt.shape[0]
    step = PROMPT_GROUP if nb % PROMPT_GROUP == 0 else nb
    units = [(x_prompt[i:i + step], 0, x_prompt.shape[1]) for i in range(0, nb, step)]
    n_prompt = len(units)
    part = x_sample.shape[1] // SAMPLE_PARTS
    units += [(x_sample, i * part, part) for i in range(SAMPLE_PARTS)]
    outs = _encode(units, p)
    return jnp.concatenate(outs[:n_prompt], axis=0), jnp.concatenate(outs[n_prompt:], axis=1)
```

```python
import functools
import math

import numpy as np
import jax
import jax.numpy as jnp
from jax import lax
from jax.experimental import pallas as pl
from jax.experimental.pallas import tpu as pltpu
from jax.experimental.pallas import tpu_sc as plsc

F32 = jnp.float32
BF16 = jnp.bfloat16

D_MODEL = 1024
D_FOURIER = 512
GROUPS = 4
GDIM = 128
D_CONV = 512
CONV_W = 31
CONV_PAD = 15
HEADS = 8
N_KEYS = 128
TOPK = 16
D_HALF = 128
D_QUERY = 256
N_SEL = HEADS * TOPK
EPS = 1e-6

SUBLANES = 8
LANES = 128
VMEM_LIMIT = 56 * 1024 * 1024

SC_CORES = 2
SC_SUBCORES = 16
SC_LANES = 16
SC_WORKERS = SC_CORES * SC_SUBCORES


def _cparams(sem):
    return pltpu.CompilerParams(dimension_semantics=sem, vmem_limit_bytes=VMEM_LIMIT)


def _rms(x, g):
    return x * lax.rsqrt(jnp.mean(x * x, axis=-1, keepdims=True) + EPS) * g


D_WORDS = D_MODEL // 2
HI_MASK = -65536


def _bf16_bits(x):
    return lax.bitcast_convert_type(x.astype(BF16).astype(F32), jnp.int32)


def _pack_halves(x):
    lo = lax.shift_right_logical(_bf16_bits(x[..., :D_WORDS]), 16)
    hi = lax.bitwise_and(_bf16_bits(x[..., D_WORDS:]), HI_MASK)
    return lax.bitwise_or(lo, hi)


def _inproj_kernel(x_ref, g1_ref, win_ref, cdft_ref, zg_ref, hg_ref):
    h = _rms(x_ref[...], g1_ref[...]).astype(BF16)
    z = jnp.dot(h, win_ref[...], preferred_element_type=F32)
    for g in range(GROUPS):
        zg = z[:, g * GDIM:(g + 1) * GDIM].astype(BF16)
        zg_ref[g] = jnp.dot(zg, cdft_ref[...], preferred_element_type=F32)
    a = z[:, D_FOURIER:D_FOURIER + D_CONV]
    gate = z[:, D_FOURIER + D_CONV:]
    hg_ref[...] = a * jax.nn.sigmoid(gate)


def _inproj(x, g1, win_bf, cdft, tm):
    B, S, _ = x.shape
    return pl.pallas_call(
        _inproj_kernel,
        grid=(B, S // tm),
        in_specs=[
            pl.BlockSpec((None, tm, D_MODEL), lambda b, m: (b, m, 0)),
            pl.BlockSpec((1, D_MODEL), lambda b, m: (0, 0)),
            pl.BlockSpec((D_MODEL, D_FOURIER + 2 * D_CONV), lambda b, m: (0, 0)),
            pl.BlockSpec((GDIM, 2 * GDIM), lambda b, m: (0, 0)),
        ],
        out_specs=[
            pl.BlockSpec((None, GROUPS, tm, 2 * GDIM), lambda b, m: (b, 0, m, 0)),
            pl.BlockSpec((None, tm, D_CONV), lambda b, m: (b, m, 0)),
        ],
        out_shape=[
            jax.ShapeDtypeStruct((B, GROUPS, S, 2 * GDIM), F32),
            jax.ShapeDtypeStruct((B, S, D_CONV), F32),
        ],
        compiler_params=_cparams(("parallel", "parallel")),
        name="inproj",
    )(x, g1, win_bf, cdft)


def _fft1_kernel(z_ref, k1_ref, tc_ref, ts_ref, a_ref):
    n1 = z_ref.shape[0]
    rows = n1 * SUBLANES
    x = z_ref[...].reshape(rows, 2 * GDIM).astype(BF16)
    p = jnp.dot(k1_ref[...], x, preferred_element_type=F32)
    pc, ps = p[:rows], p[rows:]
    ar = pc[:, :GDIM] + ps[:, GDIM:]
    ai = pc[:, GDIM:] - ps[:, :GDIM]
    tc, ts = tc_ref[...], ts_ref[...]
    a_ref[:, :, :GDIM] = (ar * tc + ai * ts).reshape(n1, SUBLANES, GDIM)
    a_ref[:, :, GDIM:] = (ai * tc - ar * ts).reshape(n1, SUBLANES, GDIM)


def _fft2_kernel(a_ref, c2_ref, s2_ref, y_ref):
    for r in range(SUBLANES):
        blk = a_ref[r].astype(BF16)
        y = jnp.dot(c2_ref[...], blk[:, :GDIM], preferred_element_type=F32)
        y = y + jnp.dot(s2_ref[...], blk[:, GDIM:], preferred_element_type=F32)
        y_ref[:, r * GDIM:(r + 1) * GDIM] = y


def _split_seq(S):
    n1 = 1 << (int(math.log2(S)) // 2)
    return n1, S // n1


@functools.lru_cache(maxsize=None)
def _fft_tables(S):
    n1, n2 = _split_seq(S)
    k1 = np.arange(n1)
    ang1 = 2.0 * np.pi * np.outer(k1, np.arange(n1)) / n1
    eye = np.eye(SUBLANES)
    kmat = np.concatenate([np.kron(np.cos(ang1), eye), np.kron(np.sin(ang1), eye)], axis=0)
    nn2 = np.arange(n2).reshape(n2 // SUBLANES, 1, SUBLANES)
    angt = 2.0 * np.pi * k1.reshape(1, n1, 1) * nn2 / S
    angt = angt.reshape(n2 // SUBLANES, n1 * SUBLANES, 1)
    tc = np.broadcast_to(np.cos(angt), (n2 // SUBLANES, n1 * SUBLANES, GDIM))
    ts = np.broadcast_to(np.sin(angt), (n2 // SUBLANES, n1 * SUBLANES, GDIM))
    ang2 = 2.0 * np.pi * np.outer(np.arange(n2), np.arange(n2)) / n2
    scale = 1.0 / math.sqrt(S)
    return (kmat.astype(np.float32), np.ascontiguousarray(tc, np.float32),
            np.ascontiguousarray(ts, np.float32),
            (np.cos(ang2) * scale).astype(np.float32), (np.sin(ang2) * scale).astype(np.float32))


def _seq_dft(zg):
    B, G, S, _ = zg.shape
    n1, n2 = _split_seq(S)
    kmat, tc, ts, c2, s2 = _fft_tables(S)
    kmat = jnp.asarray(kmat, BF16)
    c2 = jnp.asarray(c2, BF16)
    s2 = jnp.asarray(s2, BF16)
    nb = n2 // SUBLANES
    z6 = zg.reshape(B, G, n1, nb, SUBLANES, 2 * GDIM)
    blk6 = (None, None, n1, None, SUBLANES, 2 * GDIM)
    a6 = pl.pallas_call(
        _fft1_kernel,
        grid=(B, G, nb),
        in_specs=[
            pl.BlockSpec(blk6, lambda b, g, j: (b, g, 0, j, 0, 0)),
            pl.BlockSpec((2 * n1 * SUBLANES, n1 * SUBLANES), lambda b, g, j: (0, 0)),
            pl.BlockSpec((None, n1 * SUBLANES, GDIM), lambda b, g, j: (j, 0, 0)),
            pl.BlockSpec((None, n1 * SUBLANES, GDIM), lambda b, g, j: (j, 0, 0)),
        ],
        out_specs=pl.BlockSpec(blk6, lambda b, g, j: (b, g, 0, j, 0, 0)),
        out_shape=jax.ShapeDtypeStruct(z6.shape, F32),
        compiler_params=_cparams(("parallel", "parallel", "parallel")),
        name="fft_stage1",
    )(z6, kmat, jnp.asarray(tc), jnp.asarray(ts))
    a5 = a6.reshape(B, G, n1, n2, 2 * GDIM)
    y = pl.pallas_call(
        _fft2_kernel,
        grid=(B, G, n1 // SUBLANES),
        in_specs=[
            pl.BlockSpec((None, None, SUBLANES, n2, 2 * GDIM), lambda b, g, i: (b, g, i, 0, 0)),
            pl.BlockSpec((n2, n2), lambda b, g, i: (0, 0)),
            pl.BlockSpec((n2, n2), lambda b, g, i: (0, 0)),
        ],
        out_specs=pl.BlockSpec((None, None, n2, SUBLANES * GDIM), lambda b, g, i: (b, g, 0, i)),
        out_shape=jax.ShapeDtypeStruct((B, G, n2, n1 * GDIM), F32),
        compiler_params=_cparams(("parallel", "parallel", "parallel")),
        name="fft_stage2",
    )(a5, c2, s2)
    return y.reshape(B, G, S, GDIM)


CONV_HALO = 16
CONV_ROWS = 16


def _conv_kernel(prev_ref, cur_ref, next_ref, w_ref, cb_ref, lg_ref, lb_ref, gc_ref, o_ref, scr):
    ts = cur_ref.shape[0]
    m = pl.program_id(1)
    last = pl.num_programs(1) - 1
    scr[0:CONV_HALO, :] = jnp.where(m > 0, prev_ref[...], 0.0)
    scr[CONV_HALO:CONV_HALO + ts, :] = cur_ref[...]
    scr[CONV_HALO + ts:, :] = jnp.where(m < last, next_ref[...], 0.0)
    off = CONV_HALO - CONV_PAD

    def step(c, carry):
        base = pl.multiple_of(c * CONV_ROWS, CONV_ROWS)
        win = scr[pl.ds(base, CONV_ROWS + 2 * CONV_HALO), :]
        acc = jnp.broadcast_to(cb_ref[...], (CONV_ROWS, D_CONV))
        for k in range(CONV_W):
            acc = acc + win[off + k:off + k + CONV_ROWS, :] * w_ref[k:k + 1, :]
        mu = jnp.mean(acc, axis=-1, keepdims=True)
        d = acc - mu
        var = jnp.mean(d * d, axis=-1, keepdims=True)
        y = d * lax.rsqrt(var + EPS) * lg_ref[...] + lb_ref[...]
        y = y * jax.nn.sigmoid(y)
        o_ref[pl.ds(base, CONV_ROWS), :] = _rms(y, gc_ref[...])
        return carry

    lax.fori_loop(0, ts // CONV_ROWS, step, 0)


def _conv_module(hg, conv_w, conv_b, ln_g, ln_b, gn_c, ts):
    B, S, _ = hg.shape
    hb = ts // CONV_HALO
    nh = S // CONV_HALO
    vec = pl.BlockSpec((1, D_CONV), lambda b, m: (0, 0))
    return pl.pallas_call(
        _conv_kernel,
        grid=(B, S // ts),
        in_specs=[
            pl.BlockSpec((None, CONV_HALO, D_CONV), lambda b, m: (b, jnp.maximum(m * hb - 1, 0), 0)),
            pl.BlockSpec((None, ts, D_CONV), lambda b, m: (b, m, 0)),
            pl.BlockSpec((None, CONV_HALO, D_CONV), lambda b, m: (b, jnp.minimum((m + 1) * hb, nh - 1), 0)),
            pl.BlockSpec((CONV_W, D_CONV), lambda b, m: (0, 0)),
            vec, vec, vec, vec,
        ],
        out_specs=pl.BlockSpec((None, ts, D_CONV), lambda b, m: (b, m, 0)),
        out_shape=jax.ShapeDtypeStruct((B, S, D_CONV), F32),
        scratch_shapes=[pltpu.VMEM((ts + 2 * CONV_HALO, D_CONV), F32)],
        compiler_params=_cparams(("parallel", "parallel")),
        name="conv_module",
    )(hg, hg, hg, conv_w, conv_b, ln_g, ln_b, gn_c)


def _extract_topk(s, order, k, payload=None):
    big = jnp.int32(1 << 30)
    vals, sel = [], []
    for _ in range(k):
        m = jnp.max(s, axis=0, keepdims=True)
        am = jnp.min(jnp.where(s == m, order, big), axis=0, keepdims=True)
        hit = order == am
        vals.append(m)
        if payload is None:
            sel.append(am)
        else:
            sel.append(jnp.max(jnp.where(hit, payload, -1), axis=0, keepdims=True))
        s = jnp.where(hit, -jnp.inf, s)
    return jnp.concatenate(vals, axis=0), jnp.concatenate(sel, axis=0)


_CAND_A = 4
_CAND_B = 3


def _route_kernel(x_ref, yf_ref, yc_ref, gf_ref, wout_ref, g2_ref, wq_ref, ka_ref, kb_ref,
                  x1_ref, htw_ref, idx_ref, gate_ref):
    tm = x_ref.shape[0]
    yf = jnp.concatenate([yf_ref[g] for g in range(GROUPS)], axis=-1)
    yfn = _rms(yf, gf_ref[...])
    ycat = jnp.concatenate([yfn, yc_ref[...]], axis=-1).astype(BF16)
    x1 = x_ref[...] + jnp.dot(ycat, wout_ref[...], preferred_element_type=F32)
    x1_ref[...] = x1
    ht = _rms(x1, g2_ref[...])
    htw_ref[...] = _pack_halves(ht)
    q = jnp.dot(ht.astype(BF16), wq_ref[...], preferred_element_type=F32)

    row16 = lax.broadcasted_iota(jnp.int32, (TOPK, tm), 0)
    key_iota = lax.broadcasted_iota(jnp.int32, (N_KEYS, tm), 0)
    nt = (((1,), (1,)), ((), ()))
    idx_rows, gate_rows = [], []
    for h in range(HEADS):
        qa = q[:, h * D_QUERY:h * D_QUERY + D_HALF].astype(BF16)
        qb = q[:, h * D_QUERY + D_HALF:(h + 1) * D_QUERY].astype(BF16)
        sa = lax.dot_general(ka_ref[h], qa, nt, preferred_element_type=F32)
        sb = lax.dot_general(kb_ref[h], qb, nt, preferred_element_type=F32)
        va, ia = _extract_topk(sa, key_iota, TOPK)
        vb, ib = _extract_topk(sb, key_iota, TOPK)
        cs, ce, co = [], [], []
        for i in range(_CAND_A):
            ok = (i + 1) * (row16 + 1) <= TOPK
            cs.append(jnp.where(ok, va[i:i + 1] + vb, -jnp.inf))
            ce.append(ia[i:i + 1] * N_KEYS + ib)
            co.append(i * TOPK + row16)
        for j in range(_CAND_B):
            ok = jnp.logical_and(row16 >= _CAND_A, (row16 + 1) * (j + 1) <= TOPK)
            cs.append(jnp.where(ok, va + vb[j:j + 1], -jnp.inf))
            ce.append(ia * N_KEYS + ib[j:j + 1])
            co.append(row16 * TOPK + j)
        top_s, top_e = _extract_topk(jnp.concatenate(cs, axis=0), jnp.concatenate(co, axis=0),
                                     TOPK, payload=jnp.concatenate(ce, axis=0))
        ex = jnp.exp(top_s - top_s[0:1])
        gate_rows.append(ex / jnp.sum(ex, axis=0, keepdims=True))
        idx_rows.append(top_e)
    idx_ref[...] = jnp.concatenate(idx_rows, axis=0).T
    gate_ref[...] = jnp.concatenate(gate_rows, axis=0).T


def _route(x, yf, yc, gf, wout_bf, g2, wq_bf, ka_bf, kb_bf, tm, s0, rows):
    B = x.shape[0]
    S = rows
    m0 = s0 // tm
    assert s0 % tm == 0 and rows % tm == 0
    const2 = lambda b, m: (0, 0)
    const3 = lambda b, m: (0, 0, 0)
    row = lambda w: pl.BlockSpec((None, tm, w), lambda b, m: (b, m, 0))
    row_in = lambda w: pl.BlockSpec((None, tm, w), lambda b, m: (b, m + m0, 0))
    return pl.pallas_call(
        _route_kernel,
        grid=(B, S // tm),
        in_specs=[
            row_in(D_MODEL),
            pl.BlockSpec((None, GROUPS, tm, GDIM), lambda b, m: (b, 0, m + m0, 0)),
            row_in(D_CONV),
            pl.BlockSpec((1, D_FOURIER), const2),
            pl.BlockSpec((D_MODEL, D_MODEL), const2),
            pl.BlockSpec((1, D_MODEL), const2),
            pl.BlockSpec((D_MODEL, HEADS * D_QUERY), const2),
            pl.BlockSpec((HEADS, N_KEYS, D_HALF), const3),
            pl.BlockSpec((HEADS, N_KEYS, D_HALF), const3),
        ],
        out_specs=[row(D_MODEL), row(D_WORDS), row(N_SEL), row(N_SEL)],
        out_shape=[
            jax.ShapeDtypeStruct((B, S, D_MODEL), F32),
            jax.ShapeDtypeStruct((B, S, D_WORDS), jnp.int32),
            jax.ShapeDtypeStruct((B, S, N_SEL), jnp.int32),
            jax.ShapeDtypeStruct((B, S, N_SEL), F32),
        ],
        compiler_params=_cparams(("parallel", "parallel")),
        name="route",
    )(x, yf, yc, gf, wout_bf, g2, wq_bf, ka_bf, kb_bf)


SC_UROWS = 64
SC_UNITS = N_SEL // SC_UROWS
SC_NBUF = 3
SC_TOKB = 16
SC_WCH = D_WORDS // SC_LANES
SC_QUAD = 4
SC_KG = 8
SC_COST_WEIGHT = 16


def _sc_mesh():
    return plsc.VectorSubcoreMesh(core_axis_name="c", subcore_axis_name="s",
                                  num_cores=SC_CORES, num_subcores=SC_SUBCORES)


def _sc_cost(tokens):
    rows = tokens * N_SEL * SC_COST_WEIGHT
    return pl.CostEstimate(flops=2 * rows * D_MODEL, transcendentals=0, bytes_accessed=rows * D_WORDS * 4)


def _sc_bf16(words):
    return plsc.bitcast(words, BF16)


def _sc_widen(pairs):
    w = plsc.bitcast(pairs, jnp.int32)
    lo = plsc.bitcast(lax.shift_left(w, 16), F32)
    hi = plsc.bitcast(lax.bitwise_and(w, jnp.int32(HI_MASK)), F32)
    return lo, hi


def _sc_quad_sum(row_chunk, scale):
    part = None
    for u in range(SC_QUAD):
        prod = _sc_bf16(row_chunk(u)) * scale(u)
        part = prod if part is None else part + prod
    return part


def _tree_sum(xs):
    while len(xs) > 1:
        xs = [xs[i] + xs[i + 1] for i in range(0, len(xs), 2)]
    return xs[0]


def _sc_token_loop(table_hbm, idx_hbm, side_hbm, out_hbm, idx_v, side_v, rows_v, out_v, gsem, osem,
                   tokens_per_worker, compute):
    wid = lax.axis_index("s") * SC_CORES + lax.axis_index("c")

    def gather(t, unit, buf):
        rows = idx_v.at[t, pl.ds(unit * SC_UROWS, SC_UROWS)]
        return pltpu.make_async_copy(table_hbm.at[rows], rows_v.at[buf], gsem.at[buf])

    def writeback(tok, slot):
        return pltpu.make_async_copy(out_v.at[slot], out_hbm.at[tok], osem.at[slot])

    @pl.loop(0, tokens_per_worker // SC_TOKB)
    def _(bi):
        tok0 = wid * tokens_per_worker + bi * SC_TOKB
        pltpu.sync_copy(idx_hbm.at[pl.ds(tok0, SC_TOKB)], idx_v)
        pltpu.sync_copy(side_hbm.at[pl.ds(tok0, SC_TOKB)], side_v)
        for unit in range(SC_UNITS):
            gather(0, unit, unit).start()

        @pl.loop(0, SC_TOKB)
        def _(t):
            slot = lax.rem(t, 2)

            @pl.when(t >= 2)
            def _():
                writeback(tok0 + t - 2, slot).wait()

            for unit in range(SC_UNITS):
                u = t * SC_UNITS + unit
                buf = lax.rem(u, SC_NBUF)
                gather(t, unit, buf).wait()

                @pl.when(t + 1 < SC_TOKB)
                def _():
                    gather(t + 1, unit, lax.rem(u + SC_UNITS, SC_NBUF)).start()

                compute(t, unit, buf, slot)
            writeback(tok0 + t, slot).start()

        for t in range(SC_TOKB - 2, SC_TOKB):
            writeback(tok0 + t, t % 2).wait()


def _peer_down(idx, htw, down_w):
    T = htw.shape[0]
    tpw = T // SC_WORKERS
    L = SC_LANES

    @functools.partial(
        pl.kernel, mesh=_sc_mesh(),
        out_type=jax.ShapeDtypeStruct((T, L * N_SEL), F32),
        scratch_types=[
            pltpu.VMEM((SC_TOKB, N_SEL), jnp.int32),
            pltpu.VMEM((SC_TOKB, D_WORDS), jnp.int32),
            pltpu.VMEM((SC_NBUF, SC_UROWS, D_WORDS), jnp.int32),
            pltpu.VMEM((2, L * N_SEL), F32),
            pltpu.SemaphoreType.DMA((SC_NBUF,)),
            pltpu.SemaphoreType.DMA((2,)),
        ],
        compiler_params=pltpu.CompilerParams(needs_layout_passes=False),
        name="peer_down", cost_estimate=_sc_cost(T),
    )
    def k(idx_hbm, h_hbm, down_hbm, s_hbm, idx_v, h_v, rows_v, s_v, gsem, osem):
        lane = lax.iota(jnp.int32, L)

        def compute(t, unit, buf, slot):
            svec = jnp.full((L,), slot, jnp.int32)
            hs = [_sc_bf16(h_v[t, pl.ds(w * L, L)]) for w in range(SC_WCH)]

            @plsc.parallel_loop(0, SC_UROWS, unroll=2)
            def _(r):
                parts = []
                for q in range(SC_WCH // SC_QUAD):
                    part = _sc_quad_sum(lambda u: rows_v[buf, r, pl.ds((q * SC_QUAD + u) * L, L)],
                                        lambda u: hs[q * SC_QUAD + u])
                    parts.extend(_sc_widen(part))
                col = lax.bitwise_and(lane + (unit * SC_UROWS + r), N_SEL - 1)
                plsc.store_scatter(s_v, [svec, lane * N_SEL + col], _tree_sum(parts))

        _sc_token_loop(down_hbm, idx_hbm, h_hbm, s_hbm, idx_v, h_v, rows_v, s_v, gsem, osem, tpw, compute)

    return k(idx, htw, down_w)


def _peer_up(idx, act_w, up_w):
    T = act_w.shape[0]
    tpw = T // SC_WORKERS
    L = SC_LANES

    @functools.partial(
        pl.kernel, mesh=_sc_mesh(),
        out_type=jax.ShapeDtypeStruct((T, D_MODEL), F32),
        scratch_types=[
            pltpu.VMEM((SC_TOKB, N_SEL), jnp.int32),
            pltpu.VMEM((SC_TOKB, N_SEL), jnp.int32),
            pltpu.VMEM((SC_NBUF, SC_UROWS, D_WORDS), jnp.int32),
            pltpu.VMEM((2, D_MODEL), F32),
            pltpu.SemaphoreType.DMA((SC_NBUF,)),
            pltpu.SemaphoreType.DMA((2,)),
        ],
        compiler_params=pltpu.CompilerParams(needs_layout_passes=False),
        name="peer_up", cost_estimate=_sc_cost(T),
    )
    def k(idx_hbm, act_hbm, up_hbm, o_hbm, idx_v, act_v, rows_v, o_v, gsem, osem):
        def compute(t, unit, buf, slot):
            tvec = jnp.full((L,), t, jnp.int32)
            for kg in range(SC_WCH // SC_KG):
                lo_cols = [pl.ds((kg * SC_KG + kk) * L, L) for kk in range(SC_KG)]
                hi_cols = [pl.ds(D_WORDS + (kg * SC_KG + kk) * L, L) for kk in range(SC_KG)]
                if unit == 0:
                    init = tuple(jnp.zeros((L,), F32) for _ in range(2 * SC_KG))
                else:
                    init = tuple(o_v[slot, col] for col in lo_cols + hi_cols)

                def rgbody(rg, accs):
                    first = jnp.full((L,), unit * SC_UROWS, jnp.int32) + rg * SC_QUAD
                    a = [_sc_bf16(plsc.load_gather(act_v, [tvec, first + u])) for u in range(SC_QUAD)]
                    out_lo, out_hi = [], []
                    for kk in range(SC_KG):
                        part = _sc_quad_sum(lambda u: rows_v[buf, rg * SC_QUAD + u, lo_cols[kk]], lambda u: a[u])
                        lo, hi = _sc_widen(part)
                        out_lo.append(accs[kk] + lo)
                        out_hi.append(accs[SC_KG + kk] + hi)
                    return tuple(out_lo + out_hi)
                accs = lax.fori_loop(0, SC_UROWS // SC_QUAD, rgbody, init)
                for kk in range(SC_KG):
                    o_v[slot, lo_cols[kk]] = accs[kk]
                    o_v[slot, hi_cols[kk]] = accs[SC_KG + kk]

        _sc_token_loop(up_hbm, idx_hbm, act_hbm, o_hbm, idx_v, act_v, rows_v, o_v, gsem, osem, tpw, compute)

    return k(idx, act_w, up_w)


def _act_kernel(s_ref, gate_ref, a_ref):
    s = _tree_sum([pltpu.roll(s_ref[:, l * N_SEL:(l + 1) * N_SEL], (N_SEL - l) % N_SEL, axis=1)
                   for l in range(SC_LANES)])
    act = 0.5 * s * (1.0 + lax.erf(s * math.sqrt(0.5))) * gate_ref[...]
    bits = _bf16_bits(act)
    a_ref[...] = lax.bitwise_or(lax.shift_right_logical(bits, 16), bits)


def _act(s, gate, tm):
    T = s.shape[0]
    spec = pl.BlockSpec((tm, N_SEL), lambda i: (i, 0))
    part_spec = pl.BlockSpec((tm, SC_LANES * N_SEL), lambda i: (i, 0))
    return pl.pallas_call(
        _act_kernel, grid=(T // tm,), in_specs=[part_spec, spec], out_specs=spec,
        out_shape=jax.ShapeDtypeStruct((T, N_SEL), jnp.int32),
        compiler_params=_cparams(("parallel",)), name="expert_act",
    )(s, gate)


def _final_kernel(x1_ref, p_ref, g_ref, o_ref):
    o_ref[...] = _rms(x1_ref[...] + p_ref[...], g_ref[...])


def _final(x1, peer, g, tm):
    T = x1.shape[0]
    spec = pl.BlockSpec((tm, D_MODEL), lambda i: (i, 0))
    return pl.pallas_call(
        _final_kernel, grid=(T // tm,),
        in_specs=[spec, spec, pl.BlockSpec((1, D_MODEL), lambda i: (0, 0))], out_specs=spec,
        out_shape=jax.ShapeDtypeStruct((T, D_MODEL), F32),
        compiler_params=_cparams(("parallel",)), name="final_norm",
    )(x1, peer, g)


PROMPT_PLAN = ((1, 2), (1, 1), (2, 1), (2, 1), (2, 1))
SAMPLE_PLAN = ((1, 2),)
SKEW = 2


def _tile(n, want):
    t = min(n, want)
    assert n % t == 0
    return t


def _channel_dft():
    ang = 2.0 * np.pi * np.outer(np.arange(GDIM), np.arange(GDIM)) / GDIM
    m = np.concatenate([np.cos(ang), -np.sin(ang)], axis=1) / math.sqrt(GDIM)
    return jnp.asarray(m.astype(np.float32), BF16)


def _plan_units(x, plan):
    units, parts_per_pass = [], []
    b = 0
    for nb, parts in plan:
        xb = x[b:b + nb]
        rows = x.shape[1] // parts
        units += [(xb, i * rows, rows) for i in range(parts)]
        parts_per_pass.append(parts)
        b += nb
    assert b == x.shape[0]
    return units, parts_per_pass


def _assemble(outs, parts_per_pass):
    passes, k = [], 0
    for parts in parts_per_pass:
        passes.append(outs[k] if parts == 1 else jnp.concatenate(outs[k:k + parts], axis=1))
        k += parts
    return passes[0] if len(passes) == 1 else jnp.concatenate(passes, axis=0)


def _mixer(x, p):
    S = x.shape[1]
    zg, hg = _inproj(x, p["g1"], p["win"], p["cdft"], _tile(S, 512))
    yf = _seq_dft(zg)
    yc = _conv_module(hg, p["conv_w"], p["conv_b"], p["ln_g"], p["ln_b"], p["gn_c"], _tile(S, 512))
    return yf, yc


def _encode(units, p):
    x1s, idxs, acts, peers = [], [], [], []
    mixed, mixed_x = None, None
    for g, (x, s0, rows) in enumerate(units):
        reuse = mixed_x is x
        carrier = mixed if reuse else x
        tied = [acts[g - SKEW]] if g >= SKEW else []
        if g >= SKEW + 1:
            tied.append(peers[g - SKEW - 1])
        if tied:
            carrier, *tied = lax.optimization_barrier((carrier, *tied))
            acts[g - SKEW] = tied[0]
            if len(tied) > 1:
                peers[g - SKEW - 1] = tied[1]
        if g >= SKEW:
            peers.append(_peer_up(idxs[g - SKEW], acts[g - SKEW], p["up"]))
        if reuse:
            mixed = carrier
        else:
            x = carrier
            mixed, mixed_x = _mixer(x, p), units[g][0]
        T = x.shape[0] * rows
        assert T % (SC_WORKERS * SC_TOKB) == 0
        x1, htw, idx, gate = _route(x, mixed[0], mixed[1], p["gn_f"], p["wout"], p["g2"], p["wq"], p["ka"],
                                    p["kb"], _tile(rows, 256), s0, rows)
        idx = idx.reshape(T, N_SEL)
        s = _peer_down(idx, htw.reshape(T, D_WORDS), p["down"])
        x1s.append(x1.reshape(T, D_MODEL))
        idxs.append(idx)
        acts.append(_act(s, gate.reshape(T, N_SEL), _tile(T, 1024)))
    for g in range(len(peers), len(units)):
        peers.append(_peer_up(idxs[g], acts[g], p["up"]))
    return [_final(x1, peer, p["final_g"], _tile(x1.shape[0], 512)).reshape(x.shape[0], rows, D_MODEL)
            for (x, _, rows), x1, peer in zip(units, x1s, peers)]


def kernel(x_prompt, x_sample, norm1_g, w_in, conv_w, conv_b, conv_ln_g, conv_ln_b, gn_fourier_g,
           gn_conv_g, w_out, norm2_g, w_q, keys_a, keys_b, expert_down, expert_up, final_g):
    assert norm1_g.shape[0] == 1
    p = dict(
        g1=norm1_g[0].reshape(1, D_MODEL), win=w_in[0].astype(BF16), cdft=_channel_dft(),
        conv_w=conv_w[0], conv_b=conv_b[0].reshape(1, D_CONV),
        ln_g=conv_ln_g[0].reshape(1, D_CONV), ln_b=conv_ln_b[0].reshape(1, D_CONV),
        gn_f=gn_fourier_g[0].reshape(1, D_FOURIER), gn_c=gn_conv_g[0].reshape(1, D_CONV),
        wout=w_out[0].astype(BF16), g2=norm2_g[0].reshape(1, D_MODEL), wq=w_q[0].astype(BF16),
        ka=keys_a[0].astype(BF16), kb=keys_b[0].astype(BF16),
        down=_pack_halves(expert_down[0]), up=_pack_halves(expert_up[0]),
        final_g=final_g.reshape(1, D_MODEL),
    )
    prompt_units, prompt_parts = _plan_units(x_prompt, PROMPT_PLAN)
    sample_units, sample_parts = _plan_units(x_sample, SAMPLE_PLAN)
    outs = _encode(prompt_units + sample_units, p)
    n_prompt = len(prompt_units)
    return _assemble(outs[:n_prompt], prompt_parts), _assemble(outs[n_prompt:], sample_parts)
```

```python
import functools
import math

import numpy as np
import jax
import jax.numpy as jnp
from jax import lax
from jax.experimental import pallas as pl
from jax.experimental.pallas import tpu as pltpu
from jax.experimental.pallas import tpu_sc as plsc

F32 = jnp.float32
BF16 = jnp.bfloat16

D_MODEL = 1024
D_FOURIER = 512
GROUPS = 4
GDIM = 128
D_CONV = 512
CONV_W = 31
CONV_PAD = 15
HEADS = 8
N_KEYS = 128
TOPK = 16
D_HALF = 128
D_QUERY = 256
N_SEL = HEADS * TOPK
EPS = 1e-6

SUBLANES = 8
LANES = 128
VMEM_LIMIT = 56 * 1024 * 1024

SC_CORES = 2
SC_SUBCORES = 16
SC_LANES = 16
SC_WORKERS = SC_CORES * SC_SUBCORES


def _cparams(sem):
    return pltpu.CompilerParams(dimension_semantics=sem, vmem_limit_bytes=VMEM_LIMIT)


def _rms(x, g):
    return x * lax.rsqrt(jnp.mean(x * x, axis=-1, keepdims=True) + EPS) * g


D_WORDS = D_MODEL // 2
HI_MASK = -65536


def _bf16_bits(x):
    return lax.bitcast_convert_type(x.astype(BF16).astype(F32), jnp.int32)


def _pack_halves(x):
    lo = lax.shift_right_logical(_bf16_bits(x[..., :D_WORDS]), 16)
    hi = lax.bitwise_and(_bf16_bits(x[..., D_WORDS:]), HI_MASK)
    return lax.bitwise_or(lo, hi)


def _inproj_kernel(x_ref, g1_ref, win_ref, cdft_ref, zg_ref, hg_ref):
    h = _rms(x_ref[...], g1_ref[...]).astype(BF16)
    z = jnp.dot(h, win_ref[...], preferred_element_type=F32)
    for g in range(GROUPS):
        zg = z[:, g * GDIM:(g + 1) * GDIM].astype(BF16)
        zg_ref[g] = jnp.dot(zg, cdft_ref[...], preferred_element_type=F32)
    a = z[:, D_FOURIER:D_FOURIER + D_CONV]
    gate = z[:, D_FOURIER + D_CONV:]
    hg_ref[...] = a * jax.nn.sigmoid(gate)


def _inproj(x, g1, win_bf, cdft, tm):
    B, S, _ = x.shape
    return pl.pallas_call(
        _inproj_kernel,
        grid=(B, S // tm),
        in_specs=[
            pl.BlockSpec((None, tm, D_MODEL), lambda b, m: (b, m, 0)),
            pl.BlockSpec((1, D_MODEL), lambda b, m: (0, 0)),
            pl.BlockSpec((D_MODEL, D_FOURIER + 2 * D_CONV), lambda b, m: (0, 0)),
            pl.BlockSpec((GDIM, 2 * GDIM), lambda b, m: (0, 0)),
        ],
        out_specs=[
            pl.BlockSpec((None, GROUPS, tm, 2 * GDIM), lambda b, m: (b, 0, m, 0)),
            pl.BlockSpec((None, tm, D_CONV), lambda b, m: (b, m, 0)),
        ],
        out_shape=[
            jax.ShapeDtypeStruct((B, GROUPS, S, 2 * GDIM), F32),
            jax.ShapeDtypeStruct((B, S, D_CONV), F32),
        ],
        compiler_params=_cparams(("parallel", "parallel")),
        name="inproj",
    )(x, g1, win_bf, cdft)


def _fft1_kernel(z_ref, k1_ref, tc_ref, ts_ref, a_ref):
    n1 = z_ref.shape[0]
    rows = n1 * SUBLANES
    x = z_ref[...].reshape(rows, 2 * GDIM).astype(BF16)
    p = jnp.dot(k1_ref[...], x, preferred_element_type=F32)
    pc, ps = p[:rows], p[rows:]
    ar = pc[:, :GDIM] + ps[:, GDIM:]
    ai = pc[:, GDIM:] - ps[:, :GDIM]
    tc, ts = tc_ref[...], ts_ref[...]
    a_ref[:, :, :GDIM] = (ar * tc + ai * ts).reshape(n1, SUBLANES, GDIM)
    a_ref[:, :, GDIM:] = (ai * tc - ar * ts).reshape(n1, SUBLANES, GDIM)


def _fft2_kernel(a_ref, c2_ref, s2_ref, y_ref):
    for r in range(SUBLANES):
        blk = a_ref[r].astype(BF16)
        y = jnp.dot(c2_ref[...], blk[:, :GDIM], preferred_element_type=F32)
        y = y + jnp.dot(s2_ref[...], blk[:, GDIM:], preferred_element_type=F32)
        y_ref[:, r * GDIM:(r + 1) * GDIM] = y


def _split_seq(S):
    n1 = 1 << (int(math.log2(S)) // 2)
    return n1, S // n1


@functools.lru_cache(maxsize=None)
def _fft_tables(S):
    n1, n2 = _split_seq(S)
    k1 = np.arange(n1)
    ang1 = 2.0 * np.pi * np.outer(k1, np.arange(n1)) / n1
    eye = np.eye(SUBLANES)
    kmat = np.concatenate([np.kron(np.cos(ang1), eye), np.kron(np.sin(ang1), eye)], axis=0)
    nn2 = np.arange(n2).reshape(n2 // SUBLANES, 1, SUBLANES)
    angt = 2.0 * np.pi * k1.reshape(1, n1, 1) * nn2 / S
    angt = angt.reshape(n2 // SUBLANES, n1 * SUBLANES, 1)
    tc = np.broadcast_to(np.cos(angt), (n2 // SUBLANES, n1 * SUBLANES, GDIM))
    ts = np.broadcast_to(np.sin(angt), (n2 // SUBLANES, n1 * SUBLANES, GDIM))
    ang2 = 2.0 * np.pi * np.outer(np.arange(n2), np.arange(n2)) / n2
    scale = 1.0 / math.sqrt(S)
    return (kmat.astype(np.float32), np.ascontiguousarray(tc, np.float32),
            np.ascontiguousarray(ts, np.float32),
            (np.cos(ang2) * scale).astype(np.float32), (np.sin(ang2) * scale).astype(np.float32))


def _seq_dft(zg):
    B, G, S, _ = zg.shape
    n1, n2 = _split_seq(S)
    kmat, tc, ts, c2, s2 = _fft_tables(S)
    kmat = jnp.asarray(kmat, BF16)
    c2 = jnp.asarray(c2, BF16)
    s2 = jnp.asarray(s2, BF16)
    nb = n2 // SUBLANES
    z6 = zg.reshape(B, G, n1, nb, SUBLANES, 2 * GDIM)
    blk6 = (None, None, n1, None, SUBLANES, 2 * GDIM)
    a6 = pl.pallas_call(
        _fft1_kernel,
        grid=(B, G, nb),
        in_specs=[
            pl.BlockSpec(blk6, lambda b, g, j: (b, g, 0, j, 0, 0)),
            pl.BlockSpec((2 * n1 * SUBLANES, n1 * SUBLANES), lambda b, g, j: (0, 0)),
            pl.BlockSpec((None, n1 * SUBLANES, GDIM), lambda b, g, j: (j, 0, 0)),
            pl.BlockSpec((None, n1 * SUBLANES, GDIM), lambda b, g, j: (j, 0, 0)),
        ],
        out_specs=pl.BlockSpec(blk6, lambda b, g, j: (b, g, 0, j, 0, 0)),
        out_shape=jax.ShapeDtypeStruct(z6.shape, F32),
        compiler_params=_cparams(("parallel", "parallel", "parallel")),
        name="fft_stage1",
    )(z6, kmat, jnp.asarray(tc), jnp.asarray(ts))
    a5 = a6.reshape(B, G, n1, n2, 2 * GDIM)
    y = pl.pallas_call(
        _fft2_kernel,
        grid=(B, G, n1 // SUBLANES),
        in_specs=[
            pl.BlockSpec((None, None, SUBLANES, n2, 2 * GDIM), lambda b, g, i: (b, g, i, 0, 0)),
            pl.BlockSpec((n2, n2), lambda b, g, i: (0, 0)),
            pl.BlockSpec((n2, n2), lambda b, g, i: (0, 0)),
        ],
        out_specs=pl.BlockSpec((None, None, n2, SUBLANES * GDIM), lambda b, g, i: (b, g, 0, i)),
        out_shape=jax.ShapeDtypeStruct((B, G, n2, n1 * GDIM), F32),
        compiler_params=_cparams(("parallel", "parallel", "parallel")),
        name="fft_stage2",
    )(a5, c2, s2)
    return y.reshape(B, G, S, GDIM)


CONV_HALO = 16
CONV_ROWS = 16


def _conv_kernel(prev_ref, cur_ref, next_ref, w_ref, cb_ref, lg_ref, lb_ref, gc_ref, o_ref, scr):
    ts = cur_ref.shape[0]
    m = pl.program_id(1)
    last = pl.num_programs(1) - 1
    scr[0:CONV_HALO, :] = jnp.where(m > 0, prev_ref[...], 0.0)
    scr[CONV_HALO:CONV_HALO + ts, :] = cur_ref[...]
    scr[CONV_HALO + ts:, :] = jnp.where(m < last, next_ref[...], 0.0)
    off = CONV_HALO - CONV_PAD

    def step(c, carry):
        base = pl.multiple_of(c * CONV_ROWS, CONV_ROWS)
        win = scr[pl.ds(base, CONV_ROWS + 2 * CONV_HALO), :]
        acc = jnp.broadcast_to(cb_ref[...], (CONV_ROWS, D_CONV))
        for k in range(CONV_W):
            acc = acc + win[off + k:off + k + CONV_ROWS, :] * w_ref[k:k + 1, :]
        mu = jnp.mean(acc, axis=-1, keepdims=True)
        d = acc - mu
        var = jnp.mean(d * d, axis=-1, keepdims=True)
        y = d * lax.rsqrt(var + EPS) * lg_ref[...] + lb_ref[...]
        y = y * jax.nn.sigmoid(y)
        o_ref[pl.ds(base, CONV_ROWS), :] = _rms(y, gc_ref[...])
        return carry

    lax.fori_loop(0, ts // CONV_ROWS, step, 0)


def _conv_module(hg, conv_w, conv_b, ln_g, ln_b, gn_c, ts):
    B, S, _ = hg.shape
    hb = ts // CONV_HALO
    nh = S // CONV_HALO
    vec = pl.BlockSpec((1, D_CONV), lambda b, m: (0, 0))
    return pl.pallas_call(
        _conv_kernel,
        grid=(B, S // ts),
        in_specs=[
            pl.BlockSpec((None, CONV_HALO, D_CONV), lambda b, m: (b, jnp.maximum(m * hb - 1, 0), 0)),
            pl.BlockSpec((None, ts, D_CONV), lambda b, m: (b, m, 0)),
            pl.BlockSpec((None, CONV_HALO, D_CONV), lambda b, m: (b, jnp.minimum((m + 1) * hb, nh - 1), 0)),
            pl.BlockSpec((CONV_W, D_CONV), lambda b, m: (0, 0)),
            vec, vec, vec, vec,
        ],
        out_specs=pl.BlockSpec((None, ts, D_CONV), lambda b, m: (b, m, 0)),
        out_shape=jax.ShapeDtypeStruct((B, S, D_CONV), F32),
        scratch_shapes=[pltpu.VMEM((ts + 2 * CONV_HALO, D_CONV), F32)],
        compiler_params=_cparams(("parallel", "parallel")),
        name="conv_module",
    )(hg, hg, hg, conv_w, conv_b, ln_g, ln_b, gn_c)


def _extract_topk(s, order, k, payload=None):
    big = jnp.int32(1 << 30)
    vals, sel = [], []
    for _ in range(k):
        m = jnp.max(s, axis=0, keepdims=True)
        am = jnp.min(jnp.where(s == m, order, big), axis=0, keepdims=True)
        hit = order == am
        vals.append(m)
        if payload is None:
            sel.append(am)
        else:
            sel.append(jnp.max(jnp.where(hit, payload, -1), axis=0, keepdims=True))
        s = jnp.where(hit, -jnp.inf, s)
    return jnp.concatenate(vals, axis=0), jnp.concatenate(sel, axis=0)


_CAND_A = 4
_CAND_B = 3


def _route_kernel(x_ref, yf_ref, yc_ref, gf_ref, wout_ref, g2_ref, wq_ref, ka_ref, kb_ref,
                  x1_ref, htw_ref, idx_ref, gate_ref):
    tm = x_ref.shape[0]
    yf = jnp.concatenate([yf_ref[g] for g in range(GROUPS)], axis=-1)
    yfn = _rms(yf, gf_ref[...])
    ycat = jnp.concatenate([yfn, yc_ref[...]], axis=-1).astype(BF16)
    x1 = x_ref[...] + jnp.dot(ycat, wout_ref[...], preferred_element_type=F32)
    x1_ref[...] = x1
    ht = _rms(x1, g2_ref[...])
    htw_ref[...] = _pack_halves(ht)
    q = jnp.dot(ht.astype(BF16), wq_ref[...], preferred_element_type=F32)

    row16 = lax.broadcasted_iota(jnp.int32, (TOPK, tm), 0)
    key_iota = lax.broadcasted_iota(jnp.int32, (N_KEYS, tm), 0)
    nt = (((1,), (1,)), ((), ()))
    idx_rows, gate_rows = [], []
    for h in range(HEADS):
        qa = q[:, h * D_QUERY:h * D_QUERY + D_HALF].astype(BF16)
        qb = q[:, h * D_QUERY + D_HALF:(h + 1) * D_QUERY].astype(BF16)
        sa = lax.dot_general(ka_ref[h], qa, nt, preferred_element_type=F32)
        sb = lax.dot_general(kb_ref[h], qb, nt, preferred_element_type=F32)
        va, ia = _extract_topk(sa, key_iota, TOPK)
        vb, ib = _extract_topk(sb, key_iota, TOPK)
        cs, ce, co = [], [], []
        for i in range(_CAND_A):
            ok = (i + 1) * (row16 + 1) <= TOPK
            cs.append(jnp.where(ok, va[i:i + 1] + vb, -jnp.inf))
            ce.append(ia[i:i + 1] * N_KEYS + ib)
            co.append(i * TOPK + row16)
        for j in range(_CAND_B):
            ok = jnp.logical_and(row16 >= _CAND_A, (row16 + 1) * (j + 1) <= TOPK)
            cs.append(jnp.where(ok, va + vb[j:j + 1], -jnp.inf))
            ce.append(ia * N_KEYS + ib[j:j + 1])
            co.append(row16 * TOPK + j)
        top_s, top_e = _extract_topk(jnp.concatenate(cs, axis=0), jnp.concatenate(co, axis=0),
                                     TOPK, payload=jnp.concatenate(ce, axis=0))
        ex = jnp.exp(top_s - top_s[0:1])
        gate_rows.append(ex / jnp.sum(ex, axis=0, keepdims=True))
        idx_rows.append(top_e)
    idx_ref[...] = jnp.concatenate(idx_rows, axis=0).T
    gate_ref[...] = jnp.concatenate(gate_rows, axis=0).T


def _route(x, yf, yc, gf, wout_bf, g2, wq_bf, ka_bf, kb_bf, tm, s0, rows):
    B = x.shape[0]
    S = rows
    m0 = s0 // tm
    assert s0 % tm == 0 and rows % tm == 0
    const2 = lambda b, m: (0, 0)
    const3 = lambda b, m: (0, 0, 0)
    row = lambda w: pl.BlockSpec((None, tm, w), lambda b, m: (b, m, 0))
    row_in = lambda w: pl.BlockSpec((None, tm, w), lambda b, m: (b, m + m0, 0))
    return pl.pallas_call(
        _route_kernel,
        grid=(B, S // tm),
        in_specs=[
            row_in(D_MODEL),
            pl.BlockSpec((None, GROUPS, tm, GDIM), lambda b, m: (b, 0, m + m0, 0)),
            row_in(D_CONV),
            pl.BlockSpec((1, D_FOURIER), const2),
            pl.BlockSpec((D_MODEL, D_MODEL), const2),
            pl.BlockSpec((1, D_MODEL), const2),
            pl.BlockSpec((D_MODEL, HEADS * D_QUERY), const2),
            pl.BlockSpec((HEADS, N_KEYS, D_HALF), const3),
            pl.BlockSpec((HEADS, N_KEYS, D_HALF), const3),
        ],
        out_specs=[row(D_MODEL), row(D_WORDS), row(N_SEL), row(N_SEL)],
        out_shape=[
            jax.ShapeDtypeStruct((B, S, D_MODEL), F32),
            jax.ShapeDtypeStruct((B, S, D_WORDS), jnp.int32),
            jax.ShapeDtypeStruct((B, S, N_SEL), jnp.int32),
            jax.ShapeDtypeStruct((B, S, N_SEL), F32),
        ],
        compiler_params=_cparams(("parallel", "parallel")),
        name="route",
    )(x, yf, yc, gf, wout_bf, g2, wq_bf, ka_bf, kb_bf)


SC_UROWS = 64
SC_UNITS = N_SEL // SC_UROWS
SC_NBUF = 3
SC_TOKB = 16
SC_WCH = D_WORDS // SC_LANES
SC_QUAD = 4
SC_KG = 8
SC_COST_WEIGHT = 16


def _sc_mesh():
    return plsc.VectorSubcoreMesh(core_axis_name="c", subcore_axis_name="s",
                                  num_cores=SC_CORES, num_subcores=SC_SUBCORES)


def _sc_cost(tokens):
    rows = tokens * N_SEL * SC_COST_WEIGHT
    return pl.CostEstimate(flops=2 * rows * D_MODEL, transcendentals=0, bytes_accessed=rows * D_WORDS * 4)


def _sc_bf16(words):
    return plsc.bitcast(words, BF16)


def _sc_widen(pairs):
    w = plsc.bitcast(pairs, jnp.int32)
    lo = plsc.bitcast(lax.shift_left(w, 16), F32)
    hi = plsc.bitcast(lax.bitwise_and(w, jnp.int32(HI_MASK)), F32)
    return lo, hi


def _sc_quad_sum(row_chunk, scale):
    part = None
    for u in range(SC_QUAD):
        prod = _sc_bf16(row_chunk(u)) * scale(u)
        part = prod if part is None else part + prod
    return part


def _tree_sum(xs):
    while len(xs) > 1:
        xs = [xs[i] + xs[i + 1] for i in range(0, len(xs), 2)]
    return xs[0]


def _sc_token_loop(table_hbm, idx_hbm, side_hbm, out_hbm, idx_v, side_v, rows_v, out_v, gsem, osem,
                   tokens_per_worker, compute):
    wid = lax.axis_index("s") * SC_CORES + lax.axis_index("c")

    def gather(t, unit, buf):
        rows = idx_v.at[t, pl.ds(unit * SC_UROWS, SC_UROWS)]
        return pltpu.make_async_copy(table_hbm.at[rows], rows_v.at[buf], gsem.at[buf])

    def writeback(tok, slot):
        return pltpu.make_async_copy(out_v.at[slot], out_hbm.at[tok], osem.at[slot])

    @pl.loop(0, tokens_per_worker // SC_TOKB)
    def _(bi):
        tok0 = wid * tokens_per_worker + bi * SC_TOKB
        pltpu.sync_copy(idx_hbm.at[pl.ds(tok0, SC_TOKB)], idx_v)
        pltpu.sync_copy(side_hbm.at[pl.ds(tok0, SC_TOKB)], side_v)
        for unit in range(SC_UNITS):
            gather(0, unit, unit).start()

        @pl.loop(0, SC_TOKB)
        def _(t):
            slot = lax.rem(t, 2)

            @pl.when(t >= 2)
            def _():
                writeback(tok0 + t - 2, slot).wait()

            for unit in range(SC_UNITS):
                u = t * SC_UNITS + unit
                buf = lax.rem(u, SC_NBUF)
                gather(t, unit, buf).wait()

                @pl.when(t + 1 < SC_TOKB)
                def _():
                    gather(t + 1, unit, lax.rem(u + SC_UNITS, SC_NBUF)).start()

                compute(t, unit, buf, slot)
            writeback(tok0 + t, slot).start()

        for t in range(SC_TOKB - 2, SC_TOKB):
            writeback(tok0 + t, t % 2).wait()


def _peer_down(idx, htw, down_w):
    T = htw.shape[0]
    tpw = T // SC_WORKERS
    L = SC_LANES

    @functools.partial(
        pl.kernel, mesh=_sc_mesh(),
        out_type=jax.ShapeDtypeStruct((T, L * N_SEL), F32),
        scratch_types=[
            pltpu.VMEM((SC_TOKB, N_SEL), jnp.int32),
            pltpu.VMEM((SC_TOKB, D_WORDS), jnp.int32),
            pltpu.VMEM((SC_NBUF, SC_UROWS, D_WORDS), jnp.int32),
            pltpu.VMEM((2, L * N_SEL), F32),
            pltpu.SemaphoreType.DMA((SC_NBUF,)),
            pltpu.SemaphoreType.DMA((2,)),
        ],
        compiler_params=pltpu.CompilerParams(needs_layout_passes=False),
        name="peer_down", cost_estimate=_sc_cost(T),
    )
    def k(idx_hbm, h_hbm, down_hbm, s_hbm, idx_v, h_v, rows_v, s_v, gsem, osem):
        lane = lax.iota(jnp.int32, L)

        def compute(t, unit, buf, slot):
            svec = jnp.full((L,), slot, jnp.int32)
            hs = [_sc_bf16(h_v[t, pl.ds(w * L, L)]) for w in range(SC_WCH)]

            @plsc.parallel_loop(0, SC_UROWS, unroll=2)
            def _(r):
                parts = []
                for q in range(SC_WCH // SC_QUAD):
                    part = _sc_quad_sum(lambda u: rows_v[buf, r, pl.ds((q * SC_QUAD + u) * L, L)],
                                        lambda u: hs[q * SC_QUAD + u])
                    parts.extend(_sc_widen(part))
                col = lax.bitwise_and(lane + (unit * SC_UROWS + r), N_SEL - 1)
                plsc.store_scatter(s_v, [svec, lane * N_SEL + col], _tree_sum(parts))

        _sc_token_loop(down_hbm, idx_hbm, h_hbm, s_hbm, idx_v, h_v, rows_v, s_v, gsem, osem, tpw, compute)

    return k(idx, htw, down_w)


def _peer_up(idx, act_w, up_w):
    T = act_w.shape[0]
    tpw = T // SC_WORKERS
    L = SC_LANES

    @functools.partial(
        pl.kernel, mesh=_sc_mesh(),
        out_type=jax.ShapeDtypeStruct((T, D_MODEL), F32),
        scratch_types=[
            pltpu.VMEM((SC_TOKB, N_SEL), jnp.int32),
            pltpu.VMEM((SC_TOKB, N_SEL), jnp.int32),
            pltpu.VMEM((SC_NBUF, SC_UROWS, D_WORDS), jnp.int32),
            pltpu.VMEM((2, D_MODEL), F32),
            pltpu.SemaphoreType.DMA((SC_NBUF,)),
            pltpu.SemaphoreType.DMA((2,)),
        ],
        compiler_params=pltpu.CompilerParams(needs_layout_passes=False),
        name="peer_up", cost_estimate=_sc_cost(T),
    )
    def k(idx_hbm, act_hbm, up_hbm, o_hbm, idx_v, act_v, rows_v, o_v, gsem, osem):
        def compute(t, unit, buf, slot):
            tvec = jnp.full((L,), t, jnp.int32)
            for kg in range(SC_WCH // SC_KG):
                lo_cols = [pl.ds((kg * SC_KG + kk) * L, L) for kk in range(SC_KG)]
                hi_cols = [pl.ds(D_WORDS + (kg * SC_KG + kk) * L, L) for kk in range(SC_KG)]
                if unit == 0:
                    init = tuple(jnp.zeros((L,), F32) for _ in range(2 * SC_KG))
                else:
                    init = tuple(o_v[slot, col] for col in lo_cols + hi_cols)

                def rgbody(rg, accs):
                    first = jnp.full((L,), unit * SC_UROWS, jnp.int32) + rg * SC_QUAD
                    a = [_sc_bf16(plsc.load_gather(act_v, [tvec, first + u])) for u in range(SC_QUAD)]
                    out_lo, out_hi = [], []
                    for kk in range(SC_KG):
                        part = _sc_quad_sum(lambda u: rows_v[buf, rg * SC_QUAD + u, lo_cols[kk]], lambda u: a[u])
                        lo, hi = _sc_widen(part)
                        out_lo.append(accs[kk] + lo)
                        out_hi.append(accs[SC_KG + kk] + hi)
                    return tuple(out_lo + out_hi)
                accs = lax.fori_loop(0, SC_UROWS // SC_QUAD, rgbody, init)
                for kk in range(SC_KG):
                    o_v[slot, lo_cols[kk]] = accs[kk]
                    o_v[slot, hi_cols[kk]] = accs[SC_KG + kk]

        _sc_token_loop(up_hbm, idx_hbm, act_hbm, o_hbm, idx_v, act_v, rows_v, o_v, gsem, osem, tpw, compute)

    return k(idx, act_w, up_w)


def _act_kernel(s_ref, gate_ref, a_ref):
    s = _tree_sum([pltpu.roll(s_ref[:, l * N_SEL:(l + 1) * N_SEL], (N_SEL - l) % N_SEL, axis=1)
                   for l in range(SC_LANES)])
    act = 0.5 * s * (1.0 + lax.erf(s * math.sqrt(0.5))) * gate_ref[...]
    bits = _bf16_bits(act)
    a_ref[...] = lax.bitwise_or(lax.shift_right_logical(bits, 16), bits)


def _act(s, gate, tm):
    T = s.shape[0]
    spec = pl.BlockSpec((tm, N_SEL), lambda i: (i, 0))
    part_spec = pl.BlockSpec((tm, SC_LANES * N_SEL), lambda i: (i, 0))
    return pl.pallas_call(
        _act_kernel, grid=(T // tm,), in_specs=[part_spec, spec], out_specs=spec,
        out_shape=jax.ShapeDtypeStruct((T, N_SEL), jnp.int32),
        compiler_params=_cparams(("parallel",)), name="expert_act",
    )(s, gate)


def _final_kernel(x1_ref, p_ref, g_ref, o_ref):
    o_ref[...] = _rms(x1_ref[...] + p_ref[...], g_ref[...])


def _final(x1, peer, g, tm):
    T = x1.shape[0]
    spec = pl.BlockSpec((tm, D_MODEL), lambda i: (i, 0))
    return pl.pallas_call(
        _final_kernel, grid=(T // tm,),
        in_specs=[spec, spec, pl.BlockSpec((1, D_MODEL), lambda i: (0, 0))], out_specs=spec,
        out_shape=jax.ShapeDtypeStruct((T, D_MODEL), F32),
        compiler_params=_cparams(("parallel",)), name="final_norm",
    )(x1, peer, g)


TC_TOKB = 32


def _unpack_bf16(words):
    lo = lax.bitcast_convert_type(lax.shift_left(words, 16), F32)
    hi = lax.bitcast_convert_type(lax.bitwise_and(words, HI_MASK), F32)
    return jnp.concatenate([lo, hi], axis=-1).astype(BF16)


def _peer_tc_kernel(idx_ref, htw_ref, gate_ref, x1_ref, g_ref, tbl_ref, o_ref, rows_v, sem):
    def issue(t, b):
        for j in range(N_SEL):
            e = idx_ref[t, j]
            pltpu.make_async_copy(tbl_ref.at[pl.ds(e, 1)], rows_v.at[b, pl.ds(j, 1)], sem.at[b]).start()

    def wait(b):
        pltpu.make_async_copy(tbl_ref.at[pl.ds(0, N_SEL)], rows_v.at[b], sem.at[b]).wait()

    issue(0, 0)

    def step(t, carry):
        b = lax.rem(t, 2)

        @pl.when(t + 1 < TC_TOKB)
        def _():
            issue(t + 1, 1 - b)

        wait(b)
        words = rows_v[b]
        h = _unpack_bf16(htw_ref[pl.ds(t, 1), :])
        s = lax.dot_general(h, _unpack_bf16(words[:, :D_WORDS]), (((1,), (1,)), ((), ())),
                            preferred_element_type=F32)
        act = 0.5 * s * (1.0 + lax.erf(s * math.sqrt(0.5))) * gate_ref[pl.ds(t, 1), :]
        peer = jnp.dot(act.astype(BF16), _unpack_bf16(words[:, D_WORDS:]), preferred_element_type=F32)
        o_ref[pl.ds(t, 1), :] = _rms(x1_ref[pl.ds(t, 1), :] + peer, g_ref[...])
        return carry

    lax.fori_loop(0, TC_TOKB, step, 0)


def _peer_tc(idx, htw, gate, x1, g, table_w):
    T = x1.shape[0]
    assert T % TC_TOKB == 0
    row = lambda w: pl.BlockSpec((TC_TOKB, w), lambda i: (i, 0))
    return pl.pallas_call(
        _peer_tc_kernel, grid=(T // TC_TOKB,),
        in_specs=[
            pl.BlockSpec((TC_TOKB, N_SEL), lambda i: (i, 0), memory_space=pltpu.SMEM),
            row(D_WORDS), row(N_SEL), row(D_MODEL),
            pl.BlockSpec((1, D_MODEL), lambda i: (0, 0)),
            pl.BlockSpec(memory_space=pl.ANY),
        ],
        out_specs=row(D_MODEL),
        out_shape=jax.ShapeDtypeStruct((T, D_MODEL), F32),
        scratch_shapes=[pltpu.VMEM((2, N_SEL, 2 * D_WORDS), jnp.int32), pltpu.SemaphoreType.DMA((2,))],
        compiler_params=_cparams(("arbitrary",)), name="peer_tc",
    )(idx, htw, gate, x1, g, table_w)


PROMPT_PLAN = ((1, 2), (1, 1), (2, 1), (2, 1), (1, 1))
SAMPLE_PLAN = ((1, 2),)
TC_BATCHES = 1
TC_CHUNK = 512
TC_FIRST_SLOT = 3
TC_SPREAD = 5
TC_TAIL = 2
SKEW = 2


def _tile(n, want):
    t = min(n, want)
    assert n % t == 0
    return t


def _channel_dft():
    ang = 2.0 * np.pi * np.outer(np.arange(GDIM), np.arange(GDIM)) / GDIM
    m = np.concatenate([np.cos(ang), -np.sin(ang)], axis=1) / math.sqrt(GDIM)
    return jnp.asarray(m.astype(np.float32), BF16)


def _plan_units(x, plan):
    units, parts_per_pass = [], []
    b = 0
    for nb, parts in plan:
        xb = x[b:b + nb]
        rows = x.shape[1] // parts
        units += [(xb, i * rows, rows) for i in range(parts)]
        parts_per_pass.append(parts)
        b += nb
    assert b == x.shape[0]
    return units, parts_per_pass


def _assemble(outs, parts_per_pass):
    passes, k = [], 0
    for parts in parts_per_pass:
        passes.append(outs[k] if parts == 1 else jnp.concatenate(outs[k:k + parts], axis=1))
        k += parts
    return passes[0] if len(passes) == 1 else jnp.concatenate(passes, axis=0)


def _mixer(x, p):
    S = x.shape[1]
    zg, hg = _inproj(x, p["g1"], p["win"], p["cdft"], _tile(S, 512))
    yf = _seq_dft(zg)
    yc = _conv_module(hg, p["conv_w"], p["conv_b"], p["ln_g"], p["ln_b"], p["gn_c"], _tile(S, 512))
    return yf, yc


def _route_unit(x, mixed, s0, rows, p):
    T = x.shape[0] * rows
    x1, htw, idx, gate = _route(x, mixed[0], mixed[1], p["gn_f"], p["wout"], p["g2"], p["wq"], p["ka"],
                                p["kb"], _tile(rows, 256), s0, rows)
    return x1.reshape(T, D_MODEL), htw.reshape(T, D_WORDS), idx.reshape(T, N_SEL), gate.reshape(T, N_SEL)


def _tc_share(x, p):
    x1, htw, idx, gate = _route_unit(x, _mixer(x, p), 0, x.shape[1], p)
    cut = lambda a, c: a[c:c + TC_CHUNK]
    return [_peer_tc(cut(idx, c), cut(htw, c), cut(gate, c), cut(x1, c), p["final_g"], p["table"])
            for c in range(0, x1.shape[0], TC_CHUNK)]


def _encode(units, tc_x, p):
    tc_outs = _tc_share(tc_x, p)
    x1s, idxs, acts, peers = [], [], [], []
    mixed, mixed_x = None, None
    for g, (x, s0, rows) in enumerate(units):
        reuse = mixed_x is x
        carrier = mixed if reuse else x
        slots = [(acts, g - SKEW)] if g >= SKEW else []
        if g >= SKEW + 1:
            slots.append((peers, g - SKEW - 1))
        if 0 <= g - TC_FIRST_SLOT < TC_SPREAD:
            slots.append((tc_outs, g - TC_FIRST_SLOT))
        if slots:
            carrier, *tied = lax.optimization_barrier((carrier, *[lst[i] for lst, i in slots]))
            for (lst, i), v in zip(slots, tied):
                lst[i] = v
        if g >= SKEW:
            peers.append(_peer_up(idxs[g - SKEW], acts[g - SKEW], p["up"]))
        if reuse:
            mixed = carrier
        else:
            x = carrier
            mixed, mixed_x = _mixer(x, p), units[g][0]
        assert (x.shape[0] * rows) % (SC_WORKERS * SC_TOKB) == 0
        x1, htw, idx, gate = _route_unit(x, mixed, s0, rows, p)
        s = _peer_down(idx, htw, p["down"])
        if g == len(units) - 2:
            s, *early = lax.optimization_barrier((s, *tc_outs[TC_SPREAD:TC_SPREAD + TC_TAIL]))
            tc_outs[TC_SPREAD:TC_SPREAD + TC_TAIL] = early
        x1s.append(x1)
        idxs.append(idx)
        acts.append(_act(s, gate, _tile(x1.shape[0], 1024)))
    for g in range(len(peers), len(units)):
        peers.append(_peer_up(idxs[g], acts[g], p["up"]))
    outs = [_final(x1, peer, p["final_g"], _tile(x1.shape[0], 512)).reshape(x.shape[0], rows, D_MODEL)
            for (x, _, rows), x1, peer in zip(units, x1s, peers)]
    return outs, jnp.concatenate(tc_outs, axis=0).reshape(tc_x.shape)


def kernel(x_prompt, x_sample, norm1_g, w_in, conv_w, conv_b, conv_ln_g, conv_ln_b, gn_fourier_g,
           gn_conv_g, w_out, norm2_g, w_q, keys_a, keys_b, expert_down, expert_up, final_g):
    assert norm1_g.shape[0] == 1
    p = dict(
        g1=norm1_g[0].reshape(1, D_MODEL), win=w_in[0].astype(BF16), cdft=_channel_dft(),
        conv_w=conv_w[0], conv_b=conv_b[0].reshape(1, D_CONV),
        ln_g=conv_ln_g[0].reshape(1, D_CONV), ln_b=conv_ln_b[0].reshape(1, D_CONV),
        gn_f=gn_fourier_g[0].reshape(1, D_FOURIER), gn_c=gn_conv_g[0].reshape(1, D_CONV),
        wout=w_out[0].astype(BF16), g2=norm2_g[0].reshape(1, D_MODEL), wq=w_q[0].astype(BF16),
        ka=keys_a[0].astype(BF16), kb=keys_b[0].astype(BF16),
        down=_pack_halves(expert_down[0]), up=_pack_halves(expert_up[0]),
        final_g=final_g.reshape(1, D_MODEL),
    )
    p["table"] = jnp.concatenate([p["down"], p["up"]], axis=1)
    n_sc = x_prompt.shape[0] - TC_BATCHES
    prompt_units, prompt_parts = _plan_units(x_prompt[:n_sc], PROMPT_PLAN)
    sample_units, sample_parts = _plan_units(x_sample, SAMPLE_PLAN)
    outs, tc_out = _encode(prompt_units + sample_units, x_prompt[n_sc:], p)
    n_prompt = len(prompt_units)
    y_prompt = jnp.concatenate([_assemble(outs[:n_prompt], prompt_parts), tc_out], axis=0)
    return y_prompt, _assemble(outs[n_prompt:], sample_parts)
```

```python
import functools
import math

import numpy as np
import jax
import jax.numpy as jnp
from jax import lax
from jax.experimental import pallas as pl
from jax.experimental.pallas import tpu as pltpu
from jax.experimental.pallas import tpu_sc as plsc

F32 = jnp.float32
BF16 = jnp.bfloat16

D_MODEL = 1024
D_FOURIER = 512
GROUPS = 4
GDIM = 128
D_CONV = 512
CONV_W = 31
CONV_PAD = 15
HEADS = 8
N_KEYS = 128
TOPK = 16
D_HALF = 128
D_QUERY = 256
N_SEL = HEADS * TOPK
EPS = 1e-6

SUBLANES = 8
LANES = 128
VMEM_LIMIT = 56 * 1024 * 1024

SC_CORES = 2
SC_SUBCORES = 16
SC_LANES = 16
SC_WORKERS = SC_CORES * SC_SUBCORES


def _cparams(sem):
    return pltpu.CompilerParams(dimension_semantics=sem, vmem_limit_bytes=VMEM_LIMIT)


def _rms(x, g):
    return x * lax.rsqrt(jnp.mean(x * x, axis=-1, keepdims=True) + EPS) * g


D_WORDS = D_MODEL // 2
HI_MASK = -65536


def _bf16_bits(x):
    return lax.bitcast_convert_type(x.astype(BF16).astype(F32), jnp.int32)


def _pack_halves(x):
    lo = lax.shift_right_logical(_bf16_bits(x[..., :D_WORDS]), 16)
    hi = lax.bitwise_and(_bf16_bits(x[..., D_WORDS:]), HI_MASK)
    return lax.bitwise_or(lo, hi)


def _inproj_kernel(x_ref, g1_ref, win_ref, cdft_ref, zg_ref, hg_ref):
    h = _rms(x_ref[...], g1_ref[...]).astype(BF16)
    z = jnp.dot(h, win_ref[...], preferred_element_type=F32)
    for g in range(GROUPS):
        zg = z[:, g * GDIM:(g + 1) * GDIM].astype(BF16)
        zg_ref[g] = jnp.dot(zg, cdft_ref[...], preferred_element_type=F32)
    a = z[:, D_FOURIER:D_FOURIER + D_CONV]
    gate = z[:, D_FOURIER + D_CONV:]
    hg_ref[...] = a * jax.nn.sigmoid(gate)


def _inproj(x, g1, win_bf, cdft, tm):
    B, S, _ = x.shape
    return pl.pallas_call(
        _inproj_kernel,
        grid=(B, S // tm),
        in_specs=[
            pl.BlockSpec((None, tm, D_MODEL), lambda b, m: (b, m, 0)),
            pl.BlockSpec((1, D_MODEL), lambda b, m: (0, 0)),
            pl.BlockSpec((D_MODEL, D_FOURIER + 2 * D_CONV), lambda b, m: (0, 0)),
            pl.BlockSpec((GDIM, 2 * GDIM), lambda b, m: (0, 0)),
        ],
        out_specs=[
            pl.BlockSpec((None, GROUPS, tm, 2 * GDIM), lambda b, m: (b, 0, m, 0)),
            pl.BlockSpec((None, tm, D_CONV), lambda b, m: (b, m, 0)),
        ],
        out_shape=[
            jax.ShapeDtypeStruct((B, GROUPS, S, 2 * GDIM), F32),
            jax.ShapeDtypeStruct((B, S, D_CONV), F32),
        ],
        compiler_params=_cparams(("parallel", "parallel")),
        name="inproj",
    )(x, g1, win_bf, cdft)


def _fft1_kernel(z_ref, k1_ref, tc_ref, ts_ref, a_ref):
    n1 = z_ref.shape[0]
    rows = n1 * SUBLANES
    x = z_ref[...].reshape(rows, 2 * GDIM).astype(BF16)
    p = jnp.dot(k1_ref[...], x, preferred_element_type=F32)
    pc, ps = p[:rows], p[rows:]
    ar = pc[:, :GDIM] + ps[:, GDIM:]
    ai = pc[:, GDIM:] - ps[:, :GDIM]
    tc, ts = tc_ref[...], ts_ref[...]
    a_ref[:, :, :GDIM] = (ar * tc + ai * ts).reshape(n1, SUBLANES, GDIM)
    a_ref[:, :, GDIM:] = (ai * tc - ar * ts).reshape(n1, SUBLANES, GDIM)


def _fft2_kernel(a_ref, c2_ref, s2_ref, y_ref):
    for r in range(SUBLANES):
        blk = a_ref[r].astype(BF16)
        y = jnp.dot(c2_ref[...], blk[:, :GDIM], preferred_element_type=F32)
        y = y + jnp.dot(s2_ref[...], blk[:, GDIM:], preferred_element_type=F32)
        y_ref[:, r * GDIM:(r + 1) * GDIM] = y


def _split_seq(S):
    n1 = 1 << (int(math.log2(S)) // 2)
    return n1, S // n1


@functools.lru_cache(maxsize=None)
def _fft_tables(S):
    n1, n2 = _split_seq(S)
    k1 = np.arange(n1)
    ang1 = 2.0 * np.pi * np.outer(k1, np.arange(n1)) / n1
    eye = np.eye(SUBLANES)
    kmat = np.concatenate([np.kron(np.cos(ang1), eye), np.kron(np.sin(ang1), eye)], axis=0)
    nn2 = np.arange(n2).reshape(n2 // SUBLANES, 1, SUBLANES)
    angt = 2.0 * np.pi * k1.reshape(1, n1, 1) * nn2 / S
    angt = angt.reshape(n2 // SUBLANES, n1 * SUBLANES, 1)
    tc = np.broadcast_to(np.cos(angt), (n2 // SUBLANES, n1 * SUBLANES, GDIM))
    ts = np.broadcast_to(np.sin(angt), (n2 // SUBLANES, n1 * SUBLANES, GDIM))
    ang2 = 2.0 * np.pi * np.outer(np.arange(n2), np.arange(n2)) / n2
    scale = 1.0 / math.sqrt(S)
    return (kmat.astype(np.float32), np.ascontiguousarray(tc, np.float32),
            np.ascontiguousarray(ts, np.float32),
            (np.cos(ang2) * scale).astype(np.float32), (np.sin(ang2) * scale).astype(np.float32))


def _seq_dft(zg):
    B, G, S, _ = zg.shape
    n1, n2 = _split_seq(S)
    kmat, tc, ts, c2, s2 = _fft_tables(S)
    kmat = jnp.asarray(kmat, BF16)
    c2 = jnp.asarray(c2, BF16)
    s2 = jnp.asarray(s2, BF16)
    nb = n2 // SUBLANES
    z6 = zg.reshape(B, G, n1, nb, SUBLANES, 2 * GDIM)
    blk6 = (None, None, n1, None, SUBLANES, 2 * GDIM)
    a6 = pl.pallas_call(
        _fft1_kernel,
        grid=(nb, B, G),
        in_specs=[
            pl.BlockSpec(blk6, lambda j, b, g: (b, g, 0, j, 0, 0)),
            pl.BlockSpec((2 * n1 * SUBLANES, n1 * SUBLANES), lambda j, b, g: (0, 0)),
            pl.BlockSpec((None, n1 * SUBLANES, GDIM), lambda j, b, g: (j, 0, 0)),
            pl.BlockSpec((None, n1 * SUBLANES, GDIM), lambda j, b, g: (j, 0, 0)),
        ],
        out_specs=pl.BlockSpec(blk6, lambda j, b, g: (b, g, 0, j, 0, 0)),
        out_shape=jax.ShapeDtypeStruct(z6.shape, F32),
        compiler_params=_cparams(("parallel", "parallel", "parallel")),
        name="fft_stage1",
    )(z6, kmat, jnp.asarray(tc), jnp.asarray(ts))
    a5 = a6.reshape(B, G, n1, n2, 2 * GDIM)
    y = pl.pallas_call(
        _fft2_kernel,
        grid=(B, G, n1 // SUBLANES),
        in_specs=[
            pl.BlockSpec((None, None, SUBLANES, n2, 2 * GDIM), lambda b, g, i: (b, g, i, 0, 0)),
            pl.BlockSpec((n2, n2), lambda b, g, i: (0, 0)),
            pl.BlockSpec((n2, n2), lambda b, g, i: (0, 0)),
        ],
        out_specs=pl.BlockSpec((None, None, n2, SUBLANES * GDIM), lambda b, g, i: (b, g, 0, i)),
        out_shape=jax.ShapeDtypeStruct((B, G, n2, n1 * GDIM), F32),
        compiler_params=_cparams(("parallel", "parallel", "parallel")),
        name="fft_stage2",
    )(a5, c2, s2)
    return y.reshape(B, G, S, GDIM)


CONV_HALO = 16
CONV_ROWS = 16


def _conv_kernel(prev_ref, cur_ref, next_ref, w_ref, cb_ref, lg_ref, lb_ref, gc_ref, o_ref, scr):
    ts = cur_ref.shape[0]
    m = pl.program_id(1)
    last = pl.num_programs(1) - 1
    scr[0:CONV_HALO, :] = jnp.where(m > 0, prev_ref[...], 0.0)
    scr[CONV_HALO:CONV_HALO + ts, :] = cur_ref[...]
    scr[CONV_HALO + ts:, :] = jnp.where(m < last, next_ref[...], 0.0)
    off = CONV_HALO - CONV_PAD

    def step(c, carry):
        base = pl.multiple_of(c * CONV_ROWS, CONV_ROWS)
        win = scr[pl.ds(base, CONV_ROWS + 2 * CONV_HALO), :]
        acc = jnp.broadcast_to(cb_ref[...], (CONV_ROWS, D_CONV))
        for k in range(CONV_W):
            acc = acc + win[off + k:off + k + CONV_ROWS, :] * w_ref[k:k + 1, :]
        mu = jnp.mean(acc, axis=-1, keepdims=True)
        d = acc - mu
        var = jnp.mean(d * d, axis=-1, keepdims=True)
        y = d * lax.rsqrt(var + EPS) * lg_ref[...] + lb_ref[...]
        y = y * jax.nn.sigmoid(y)
        o_ref[pl.ds(base, CONV_ROWS), :] = _rms(y, gc_ref[...])
        return carry

    lax.fori_loop(0, ts // CONV_ROWS, step, 0)


def _conv_module(hg, conv_w, conv_b, ln_g, ln_b, gn_c, ts):
    B, S, _ = hg.shape
    hb = ts // CONV_HALO
    nh = S // CONV_HALO
    vec = pl.BlockSpec((1, D_CONV), lambda b, m: (0, 0))
    return pl.pallas_call(
        _conv_kernel,
        grid=(B, S // ts),
        in_specs=[
            pl.BlockSpec((None, CONV_HALO, D_CONV), lambda b, m: (b, jnp.maximum(m * hb - 1, 0), 0)),
            pl.BlockSpec((None, ts, D_CONV), lambda b, m: (b, m, 0)),
            pl.BlockSpec((None, CONV_HALO, D_CONV), lambda b, m: (b, jnp.minimum((m + 1) * hb, nh - 1), 0)),
            pl.BlockSpec((CONV_W, D_CONV), lambda b, m: (0, 0)),
            vec, vec, vec, vec,
        ],
        out_specs=pl.BlockSpec((None, ts, D_CONV), lambda b, m: (b, m, 0)),
        out_shape=jax.ShapeDtypeStruct((B, S, D_CONV), F32),
        scratch_shapes=[pltpu.VMEM((ts + 2 * CONV_HALO, D_CONV), F32)],
        compiler_params=_cparams(("parallel", "parallel")),
        name="conv_module",
    )(hg, hg, hg, conv_w, conv_b, ln_g, ln_b, gn_c)


def _extract_topk(s, order, k, payload=None):
    big = jnp.int32(1 << 30)
    vals, sel = [], []
    for _ in range(k):
        m = jnp.max(s, axis=0, keepdims=True)
        am = jnp.min(jnp.where(s == m, order, big), axis=0, keepdims=True)
        hit = order == am
        vals.append(m)
        if payload is None:
            sel.append(am)
        else:
            sel.append(jnp.max(jnp.where(hit, payload, -1), axis=0, keepdims=True))
        s = jnp.where(hit, -jnp.inf, s)
    return jnp.concatenate(vals, axis=0), jnp.concatenate(sel, axis=0)


_CAND_A = 4
_CAND_B = 3


def _route_kernel(x_ref, yf_ref, yc_ref, gf_ref, wout_ref, g2_ref, wq_ref, ka_ref, kb_ref,
                  x1_ref, htw_ref, idx_ref, gate_ref):
    tm = x_ref.shape[0]
    yf = jnp.concatenate([yf_ref[g] for g in range(GROUPS)], axis=-1)
    yfn = _rms(yf, gf_ref[...])
    ycat = jnp.concatenate([yfn, yc_ref[...]], axis=-1).astype(BF16)
    x1 = x_ref[...] + jnp.dot(ycat, wout_ref[...], preferred_element_type=F32)
    x1_ref[...] = x1
    ht = _rms(x1, g2_ref[...])
    htw_ref[...] = _pack_halves(ht)
    q = jnp.dot(ht.astype(BF16), wq_ref[...], preferred_element_type=F32)

    row16 = lax.broadcasted_iota(jnp.int32, (TOPK, tm), 0)
    key_iota = lax.broadcasted_iota(jnp.int32, (N_KEYS, tm), 0)
    nt = (((1,), (1,)), ((), ()))
    idx_rows, gate_rows = [], []
    for h in range(HEADS):
        qa = q[:, h * D_QUERY:h * D_QUERY + D_HALF].astype(BF16)
        qb = q[:, h * D_QUERY + D_HALF:(h + 1) * D_QUERY].astype(BF16)
        sa = lax.dot_general(ka_ref[h], qa, nt, preferred_element_type=F32)
        sb = lax.dot_general(kb_ref[h], qb, nt, preferred_element_type=F32)
        va, ia = _extract_topk(sa, key_iota, TOPK)
        vb, ib = _extract_topk(sb, key_iota, TOPK)
        cs, ce, co = [], [], []
        for i in range(_CAND_A):
            ok = (i + 1) * (row16 + 1) <= TOPK
            cs.append(jnp.where(ok, va[i:i + 1] + vb, -jnp.inf))
            ce.append(ia[i:i + 1] * N_KEYS + ib)
            co.append(i * TOPK + row16)
        for j in range(_CAND_B):
            ok = jnp.logical_and(row16 >= _CAND_A, (row16 + 1) * (j + 1) <= TOPK)
            cs.append(jnp.where(ok, va + vb[j:j + 1], -jnp.inf))
            ce.append(ia * N_KEYS + ib[j:j + 1])
            co.append(row16 * TOPK + j)
        top_s, top_e = _extract_topk(jnp.concatenate(cs, axis=0), jnp.concatenate(co, axis=0),
                                     TOPK, payload=jnp.concatenate(ce, axis=0))
        ex = jnp.exp(top_s - top_s[0:1])
        gate_rows.append(ex / jnp.sum(ex, axis=0, keepdims=True))
        idx_rows.append(top_e)
    idx_ref[...] = jnp.concatenate(idx_rows, axis=0).T
    gate_ref[...] = jnp.concatenate(gate_rows, axis=0).T


def _route(x, yf, yc, gf, wout_bf, g2, wq_bf, ka_bf, kb_bf, tm, s0, rows):
    B = x.shape[0]
    S = rows
    m0 = s0 // tm
    assert s0 % tm == 0 and rows % tm == 0
    const2 = lambda b, m: (0, 0)
    const3 = lambda b, m: (0, 0, 0)
    row = lambda w: pl.BlockSpec((None, tm, w), lambda b, m: (b, m, 0))
    row_in = lambda w: pl.BlockSpec((None, tm, w), lambda b, m: (b, m + m0, 0))
    return pl.pallas_call(
        _route_kernel,
        grid=(B, S // tm),
        in_specs=[
            row_in(D_MODEL),
            pl.BlockSpec((None, GROUPS, tm, GDIM), lambda b, m: (b, 0, m + m0, 0)),
            row_in(D_CONV),
            pl.BlockSpec((1, D_FOURIER), const2),
            pl.BlockSpec((D_MODEL, D_MODEL), const2),
            pl.BlockSpec((1, D_MODEL), const2),
            pl.BlockSpec((D_MODEL, HEADS * D_QUERY), const2),
            pl.BlockSpec((HEADS, N_KEYS, D_HALF), const3),
            pl.BlockSpec((HEADS, N_KEYS, D_HALF), const3),
        ],
        out_specs=[row(D_MODEL), row(D_WORDS), row(N_SEL), row(N_SEL)],
        out_shape=[
            jax.ShapeDtypeStruct((B, S, D_MODEL), F32),
            jax.ShapeDtypeStruct((B, S, D_WORDS), jnp.int32),
            jax.ShapeDtypeStruct((B, S, N_SEL), jnp.int32),
            jax.ShapeDtypeStruct((B, S, N_SEL), F32),
        ],
        compiler_params=_cparams(("parallel", "parallel")),
        name="route",
    )(x, yf, yc, gf, wout_bf, g2, wq_bf, ka_bf, kb_bf)


SC_UROWS = 64
SC_UNITS = N_SEL // SC_UROWS
SC_NBUF = 3
SC_TOKB = 32
SC_WCH = D_WORDS // SC_LANES
SC_QUAD = 4
SC_KG = 8
SC_COST_WEIGHT = 16


def _sc_mesh():
    return plsc.VectorSubcoreMesh(core_axis_name="c", subcore_axis_name="s",
                                  num_cores=SC_CORES, num_subcores=SC_SUBCORES)


def _sc_cost(tokens):
    rows = tokens * N_SEL * SC_COST_WEIGHT
    return pl.CostEstimate(flops=2 * rows * D_MODEL, transcendentals=0, bytes_accessed=rows * D_WORDS * 4)


def _sc_bf16(words):
    return plsc.bitcast(words, BF16)


def _sc_widen(pairs):
    w = plsc.bitcast(pairs, jnp.int32)
    lo = plsc.bitcast(lax.shift_left(w, 16), F32)
    hi = plsc.bitcast(lax.bitwise_and(w, jnp.int32(HI_MASK)), F32)
    return lo, hi


def _sc_quad_sum(row_chunk, scale):
    part = None
    for u in range(SC_QUAD):
        prod = _sc_bf16(row_chunk(u)) * scale(u)
        part = prod if part is None else part + prod
    return part


def _tree_sum(xs):
    while len(xs) > 1:
        xs = [xs[i] + xs[i + 1] for i in range(0, len(xs), 2)]
    return xs[0]


def _sc_token_loop(table_hbm, idx_hbm, side_hbm, out_hbm, idx_v, side_v, rows_v, out_v, gsem, osem,
                   tokens_per_worker, compute):
    wid = lax.axis_index("s") * SC_CORES + lax.axis_index("c")

    def gather(t, unit, buf):
        rows = idx_v.at[t, pl.ds(unit * SC_UROWS, SC_UROWS)]
        return pltpu.make_async_copy(table_hbm.at[rows], rows_v.at[buf], gsem.at[buf])

    def writeback(tok, slot):
        return pltpu.make_async_copy(out_v.at[slot], out_hbm.at[tok], osem.at[slot])

    @pl.loop(0, tokens_per_worker // SC_TOKB)
    def _(bi):
        tok0 = wid * tokens_per_worker + bi * SC_TOKB
        pltpu.sync_copy(idx_hbm.at[pl.ds(tok0, SC_TOKB)], idx_v)
        pltpu.sync_copy(side_hbm.at[pl.ds(tok0, SC_TOKB)], side_v)
        for unit in range(SC_UNITS):
            gather(0, unit, unit).start()

        @pl.loop(0, SC_TOKB)
        def _(t):
            slot = lax.rem(t, 2)

            @pl.when(t >= 2)
            def _():
                writeback(tok0 + t - 2, slot).wait()

            for unit in range(SC_UNITS):
                u = t * SC_UNITS + unit
                buf = lax.rem(u, SC_NBUF)
                gather(t, unit, buf).wait()

                @pl.when(t + 1 < SC_TOKB)
                def _():
                    gather(t + 1, unit, lax.rem(u + SC_UNITS, SC_NBUF)).start()

                compute(t, unit, buf, slot)
            writeback(tok0 + t, slot).start()

        for t in range(SC_TOKB - 2, SC_TOKB):
            writeback(tok0 + t, t % 2).wait()


def _peer_down(idx, htw, down_w):
    T = htw.shape[0]
    tpw = T // SC_WORKERS
    L = SC_LANES

    @functools.partial(
        pl.kernel, mesh=_sc_mesh(),
        out_type=jax.ShapeDtypeStruct((T, L * N_SEL), F32),
        scratch_types=[
            pltpu.VMEM((SC_TOKB, N_SEL), jnp.int32),
            pltpu.VMEM((SC_TOKB, D_WORDS), jnp.int32),
            pltpu.VMEM((SC_NBUF, SC_UROWS, D_WORDS), jnp.int32),
            pltpu.VMEM((2, L * N_SEL), F32),
            pltpu.SemaphoreType.DMA((SC_NBUF,)),
            pltpu.SemaphoreType.DMA((2,)),
        ],
        compiler_params=pltpu.CompilerParams(needs_layout_passes=False),
        name="peer_down", cost_estimate=_sc_cost(T),
    )
    def k(idx_hbm, h_hbm, down_hbm, s_hbm, idx_v, h_v, rows_v, s_v, gsem, osem):
        lane = lax.iota(jnp.int32, L)

        def compute(t, unit, buf, slot):
            svec = jnp.full((L,), slot, jnp.int32)
            hs = [_sc_bf16(h_v[t, pl.ds(w * L, L)]) for w in range(SC_WCH)]

            @plsc.parallel_loop(0, SC_UROWS, unroll=2)
            def _(r):
                parts = []
                for q in range(SC_WCH // SC_QUAD):
                    part = _sc_quad_sum(lambda u: rows_v[buf, r, pl.ds((q * SC_QUAD + u) * L, L)],
                                        lambda u: hs[q * SC_QUAD + u])
                    parts.extend(_sc_widen(part))
                col = lax.bitwise_and(lane + (unit * SC_UROWS + r), N_SEL - 1)
                plsc.store_scatter(s_v, [svec, lane * N_SEL + col], _tree_sum(parts))

        _sc_token_loop(down_hbm, idx_hbm, h_hbm, s_hbm, idx_v, h_v, rows_v, s_v, gsem, osem, tpw, compute)

    return k(idx, htw, down_w)


def _peer_up(idx, act_w, up_w):
    T = act_w.shape[0]
    tpw = T // SC_WORKERS
    L = SC_LANES

    @functools.partial(
        pl.kernel, mesh=_sc_mesh(),
        out_type=jax.ShapeDtypeStruct((T, D_MODEL), F32),
        scratch_types=[
            pltpu.VMEM((SC_TOKB, N_SEL), jnp.int32),
            pltpu.VMEM((SC_TOKB, N_SEL), jnp.int32),
            pltpu.VMEM((SC_NBUF, SC_UROWS, D_WORDS), jnp.int32),
            pltpu.VMEM((2, D_MODEL), F32),
            pltpu.SemaphoreType.DMA((SC_NBUF,)),
            pltpu.SemaphoreType.DMA((2,)),
        ],
        compiler_params=pltpu.CompilerParams(needs_layout_passes=False),
        name="peer_up", cost_estimate=_sc_cost(T),
    )
    def k(idx_hbm, act_hbm, up_hbm, o_hbm, idx_v, act_v, rows_v, o_v, gsem, osem):
        def compute(t, unit, buf, slot):
            tvec = jnp.full((L,), t, jnp.int32)
            for kg in range(SC_WCH // SC_KG):
                lo_cols = [pl.ds((kg * SC_KG + kk) * L, L) for kk in range(SC_KG)]
                hi_cols = [pl.ds(D_WORDS + (kg * SC_KG + kk) * L, L) for kk in range(SC_KG)]
                if unit == 0:
                    init = tuple(jnp.zeros((L,), F32) for _ in range(2 * SC_KG))
                else:
                    init = tuple(o_v[slot, col] for col in lo_cols + hi_cols)

                def rgbody(rg, accs):
                    first = jnp.full((L,), unit * SC_UROWS, jnp.int32) + rg * SC_QUAD
                    a = [_sc_bf16(plsc.load_gather(act_v, [tvec, first + u])) for u in range(SC_QUAD)]
                    out_lo, out_hi = [], []
                    for kk in range(SC_KG):
                        part = _sc_quad_sum(lambda u: rows_v[buf, rg * SC_QUAD + u, lo_cols[kk]], lambda u: a[u])
                        lo, hi = _sc_widen(part)
                        out_lo.append(accs[kk] + lo)
                        out_hi.append(accs[SC_KG + kk] + hi)
                    return tuple(out_lo + out_hi)
                accs = lax.fori_loop(0, SC_UROWS // SC_QUAD, rgbody, init)
                for kk in range(SC_KG):
                    o_v[slot, lo_cols[kk]] = accs[kk]
                    o_v[slot, hi_cols[kk]] = accs[SC_KG + kk]

        _sc_token_loop(up_hbm, idx_hbm, act_hbm, o_hbm, idx_v, act_v, rows_v, o_v, gsem, osem, tpw, compute)

    return k(idx, act_w, up_w)


def _act_kernel(s_ref, gate_ref, a_ref):
    s = _tree_sum([pltpu.roll(s_ref[:, l * N_SEL:(l + 1) * N_SEL], (N_SEL - l) % N_SEL, axis=1)
                   for l in range(SC_LANES)])
    act = 0.5 * s * (1.0 + lax.erf(s * math.sqrt(0.5))) * gate_ref[...]
    bits = _bf16_bits(act)
    a_ref[...] = lax.bitwise_or(lax.shift_right_logical(bits, 16), bits)


def _act(s, gate, tm):
    T = s.shape[0]
    spec = pl.BlockSpec((tm, N_SEL), lambda i: (i, 0))
    part_spec = pl.BlockSpec((tm, SC_LANES * N_SEL), lambda i: (i, 0))
    return pl.pallas_call(
        _act_kernel, grid=(T // tm,), in_specs=[part_spec, spec], out_specs=spec,
        out_shape=jax.ShapeDtypeStruct((T, N_SEL), jnp.int32),
        compiler_params=_cparams(("parallel",)), name="expert_act",
    )(s, gate)


def _final_kernel(x1_ref, p_ref, g_ref, o_ref):
    o_ref[...] = _rms(x1_ref[...] + p_ref[...], g_ref[...])


def _final(x1, peer, g, tm):
    T = x1.shape[0]
    spec = pl.BlockSpec((tm, D_MODEL), lambda i: (i, 0))
    return pl.pallas_call(
        _final_kernel, grid=(T // tm,),
        in_specs=[spec, spec, pl.BlockSpec((1, D_MODEL), lambda i: (0, 0))], out_specs=spec,
        out_shape=jax.ShapeDtypeStruct((T, D_MODEL), F32),
        compiler_params=_cparams(("parallel",)), name="final_norm",
    )(x1, peer, g)


PROMPT_PLAN = ((1, 2), (1, 1), (1, 1), (1, 1), (2, 1), (2, 1))
SAMPLE_PLAN = ((1, 2),)
SKEW = 2


def _tile(n, want):
    t = min(n, want)
    assert n % t == 0
    return t


def _channel_dft():
    ang = 2.0 * np.pi * np.outer(np.arange(GDIM), np.arange(GDIM)) / GDIM
    m = np.concatenate([np.cos(ang), -np.sin(ang)], axis=1) / math.sqrt(GDIM)
    return jnp.asarray(m.astype(np.float32), BF16)


def _plan_units(x, plan):
    units, parts_per_pass = [], []
    b = 0
    for nb, parts in plan:
        xb = x[b:b + nb]
        rows = x.shape[1] // parts
        units += [(xb, i * rows, rows) for i in range(parts)]
        parts_per_pass.append(parts)
        b += nb
    assert b == x.shape[0]
    return units, parts_per_pass


def _assemble(outs, parts_per_pass):
    passes, k = [], 0
    for parts in parts_per_pass:
        passes.append(outs[k] if parts == 1 else jnp.concatenate(outs[k:k + parts], axis=1))
        k += parts
    return passes[0] if len(passes) == 1 else jnp.concatenate(passes, axis=0)


def _mixer(x, p):
    S = x.shape[1]
    zg, hg = _inproj(x, p["g1"], p["win"], p["cdft"], _tile(S, 512))
    yf = _seq_dft(zg)
    yc = _conv_module(hg, p["conv_w"], p["conv_b"], p["ln_g"], p["ln_b"], p["gn_c"], _tile(S, 512))
    return yf, yc


def _route_unit(x, mixed, s0, rows, p):
    T = x.shape[0] * rows
    x1, htw, idx, gate = _route(x, mixed[0], mixed[1], p["gn_f"], p["wout"], p["g2"], p["wq"], p["ka"],
                                p["kb"], _tile(rows, 256), s0, rows)
    return x1.reshape(T, D_MODEL), htw.reshape(T, D_WORDS), idx.reshape(T, N_SEL), gate.reshape(T, N_SEL)


def _encode(units, p):
    x1s, idxs, acts, peers = [], [], [], []
    mixed, mixed_x = None, None
    for g, (x, s0, rows) in enumerate(units):
        reuse = mixed_x is x
        carrier = mixed if reuse else x
        slots = [(acts, g - SKEW)] if g >= SKEW else []
        if g >= SKEW + 1:
            slots.append((peers, g - SKEW - 1))
        if slots:
            carrier, *tied = lax.optimization_barrier((carrier, *[lst[i] for lst, i in slots]))
            for (lst, i), v in zip(slots, tied):
                lst[i] = v
        if g >= SKEW:
            peers.append(_peer_up(idxs[g - SKEW], acts[g - SKEW], p["up"]))
        if reuse:
            mixed = carrier
        else:
            x = carrier
            mixed, mixed_x = _mixer(x, p), units[g][0]
        assert (x.shape[0] * rows) % (SC_WORKERS * SC_TOKB) == 0
        x1, htw, idx, gate = _route_unit(x, mixed, s0, rows, p)
        s = _peer_down(idx, htw, p["down"])
        x1s.append(x1)
        idxs.append(idx)
        acts.append(_act(s, gate, _tile(x1.shape[0], 1024)))
    for g in range(len(peers), len(units)):
        peers.append(_peer_up(idxs[g], acts[g], p["up"]))
    return [_final(x1, peer, p["final_g"], _tile(x1.shape[0], 512)).reshape(x.shape[0], rows, D_MODEL)
            for (x, _, rows), x1, peer in zip(units, x1s, peers)]


def kernel(x_prompt, x_sample, norm1_g, w_in, conv_w, conv_b, conv_ln_g, conv_ln_b, gn_fourier_g,
           gn_conv_g, w_out, norm2_g, w_q, keys_a, keys_b, expert_down, expert_up, final_g):
    assert norm1_g.shape[0] == 1
    p = dict(
        g1=norm1_g[0].reshape(1, D_MODEL), win=w_in[0].astype(BF16), cdft=_channel_dft(),
        conv_w=conv_w[0], conv_b=conv_b[0].reshape(1, D_CONV),
        ln_g=conv_ln_g[0].reshape(1, D_CONV), ln_b=conv_ln_b[0].reshape(1, D_CONV),
        gn_f=gn_fourier_g[0].reshape(1, D_FOURIER), gn_c=gn_conv_g[0].reshape(1, D_CONV),
        wout=w_out[0].astype(BF16), g2=norm2_g[0].reshape(1, D_MODEL), wq=w_q[0].astype(BF16),
        ka=keys_a[0].astype(BF16), kb=keys_b[0].astype(BF16),
        down=_pack_halves(expert_down[0]), up=_pack_halves(expert_up[0]),
        final_g=final_g.reshape(1, D_MODEL),
    )
    prompt_units, prompt_parts = _plan_units(x_prompt, PROMPT_PLAN)
    sample_units, sample_parts = _plan_units(x_sample, SAMPLE_PLAN)
    outs = _encode(prompt_units + sample_units, p)
    n_prompt = len(prompt_units)
    return _assemble(outs[:n_prompt], prompt_parts), _assemble(outs[n_prompt:], sample_parts)
```

```python
import functools
import math

import numpy as np
import jax
import jax.numpy as jnp
from jax import lax
from jax.experimental import pallas as pl
from jax.experimental.pallas import tpu as pltpu
from jax.experimental.pallas import tpu_sc as plsc

F32 = jnp.float32
BF16 = jnp.bfloat16

D_MODEL = 1024
D_FOURIER = 512
GROUPS = 4
GDIM = 128
D_CONV = 512
CONV_W = 31
CONV_PAD = 15
HEADS = 8
N_KEYS = 128
TOPK = 16
D_HALF = 128
D_QUERY = 256
N_SEL = HEADS * TOPK
EPS = 1e-6

SUBLANES = 8
LANES = 128
VMEM_LIMIT = 56 * 1024 * 1024

SC_CORES = 2
SC_SUBCORES = 16
SC_LANES = 16
SC_WORKERS = SC_CORES * SC_SUBCORES


def _cparams(sem):
    return pltpu.CompilerParams(dimension_semantics=sem, vmem_limit_bytes=VMEM_LIMIT)


def _rms(x, g):
    return x * lax.rsqrt(jnp.mean(x * x, axis=-1, keepdims=True) + EPS) * g


D_WORDS = D_MODEL // 2
HI_MASK = -65536


def _bf16_bits(x):
    return lax.bitcast_convert_type(x.astype(BF16).astype(F32), jnp.int32)


def _pack_halves(x):
    lo = lax.shift_right_logical(_bf16_bits(x[..., :D_WORDS]), 16)
    hi = lax.bitwise_and(_bf16_bits(x[..., D_WORDS:]), HI_MASK)
    return lax.bitwise_or(lo, hi)


def _inproj_kernel(x_ref, g1_ref, win_ref, cdft_ref, zg_ref, hg_ref):
    h = _rms(x_ref[...], g1_ref[...]).astype(BF16)
    z = jnp.dot(h, win_ref[...], preferred_element_type=F32)
    for g in range(GROUPS):
        zg = z[:, g * GDIM:(g + 1) * GDIM].astype(BF16)
        zg_ref[g] = jnp.dot(zg, cdft_ref[...], preferred_element_type=F32)
    a = z[:, D_FOURIER:D_FOURIER + D_CONV]
    gate = z[:, D_FOURIER + D_CONV:]
    hg_ref[...] = a * jax.nn.sigmoid(gate)


def _inproj(x, g1, win_bf, cdft, tm):
    B, S, _ = x.shape
    return pl.pallas_call(
        _inproj_kernel,
        grid=(B, S // tm),
        in_specs=[
            pl.BlockSpec((None, tm, D_MODEL), lambda b, m: (b, m, 0)),
            pl.BlockSpec((1, D_MODEL), lambda b, m: (0, 0)),
            pl.BlockSpec((D_MODEL, D_FOURIER + 2 * D_CONV), lambda b, m: (0, 0)),
            pl.BlockSpec((GDIM, 2 * GDIM), lambda b, m: (0, 0)),
        ],
        out_specs=[
            pl.BlockSpec((None, GROUPS, tm, 2 * GDIM), lambda b, m: (b, 0, m, 0)),
            pl.BlockSpec((None, tm, D_CONV), lambda b, m: (b, m, 0)),
        ],
        out_shape=[
            jax.ShapeDtypeStruct((B, GROUPS, S, 2 * GDIM), F32),
            jax.ShapeDtypeStruct((B, S, D_CONV), F32),
        ],
        compiler_params=_cparams(("parallel", "parallel")),
        name="inproj",
    )(x, g1, win_bf, cdft)


def _fft1_kernel(z_ref, k1_ref, tc_ref, ts_ref, a_ref):
    n1 = z_ref.shape[0]
    rows = n1 * SUBLANES
    x = z_ref[...].reshape(rows, 2 * GDIM).astype(BF16)
    p = jnp.dot(k1_ref[...], x, preferred_element_type=F32)
    pc, ps = p[:rows], p[rows:]
    ar = pc[:, :GDIM] + ps[:, GDIM:]
    ai = pc[:, GDIM:] - ps[:, :GDIM]
    tc, ts = tc_ref[...], ts_ref[...]
    a_ref[:, :, :GDIM] = (ar * tc + ai * ts).reshape(n1, SUBLANES, GDIM)
    a_ref[:, :, GDIM:] = (ai * tc - ar * ts).reshape(n1, SUBLANES, GDIM)


def _fft2_kernel(a_ref, c2_ref, s2_ref, y_ref):
    for r in range(SUBLANES):
        blk = a_ref[r].astype(BF16)
        y = jnp.dot(c2_ref[...], blk[:, :GDIM], preferred_element_type=F32)
        y = y + jnp.dot(s2_ref[...], blk[:, GDIM:], preferred_element_type=F32)
        y_ref[:, r * GDIM:(r + 1) * GDIM] = y


def _split_seq(S):
    n1 = 1 << (int(math.log2(S)) // 2)
    return n1, S // n1


@functools.lru_cache(maxsize=None)
def _fft_tables(S):
    n1, n2 = _split_seq(S)
    k1 = np.arange(n1)
    ang1 = 2.0 * np.pi * np.outer(k1, np.arange(n1)) / n1
    eye = np.eye(SUBLANES)
    kmat = np.concatenate([np.kron(np.cos(ang1), eye), np.kron(np.sin(ang1), eye)], axis=0)
    nn2 = np.arange(n2).reshape(n2 // SUBLANES, 1, SUBLANES)
    angt = 2.0 * np.pi * k1.reshape(1, n1, 1) * nn2 / S
    angt = angt.reshape(n2 // SUBLANES, n1 * SUBLANES, 1)
    tc = np.broadcast_to(np.cos(angt), (n2 // SUBLANES, n1 * SUBLANES, GDIM))
    ts = np.broadcast_to(np.sin(angt), (n2 // SUBLANES, n1 * SUBLANES, GDIM))
    ang2 = 2.0 * np.pi * np.outer(np.arange(n2), np.arange(n2)) / n2
    scale = 1.0 / math.sqrt(S)
    return (kmat.astype(np.float32), np.ascontiguousarray(tc, np.float32),
            np.ascontiguousarray(ts, np.float32),
            (np.cos(ang2) * scale).astype(np.float32), (np.sin(ang2) * scale).astype(np.float32))


def _seq_dft(zg):
    B, G, S, _ = zg.shape
    n1, n2 = _split_seq(S)
    kmat, tc, ts, c2, s2 = _fft_tables(S)
    kmat = jnp.asarray(kmat, BF16)
    c2 = jnp.asarray(c2, BF16)
    s2 = jnp.asarray(s2, BF16)
    nb = n2 // SUBLANES
    z6 = zg.reshape(B, G, n1, nb, SUBLANES, 2 * GDIM)
    blk6 = (None, None, n1, None, SUBLANES, 2 * GDIM)
    a6 = pl.pallas_call(
        _fft1_kernel,
        grid=(nb, B, G),
        in_specs=[
            pl.BlockSpec(blk6, lambda j, b, g: (b, g, 0, j, 0, 0)),
            pl.BlockSpec((2 * n1 * SUBLANES, n1 * SUBLANES), lambda j, b, g: (0, 0)),
            pl.BlockSpec((None, n1 * SUBLANES, GDIM), lambda j, b, g: (j, 0, 0)),
            pl.BlockSpec((None, n1 * SUBLANES, GDIM), lambda j, b, g: (j, 0, 0)),
        ],
        out_specs=pl.BlockSpec(blk6, lambda j, b, g: (b, g, 0, j, 0, 0)),
        out_shape=jax.ShapeDtypeStruct(z6.shape, F32),
        compiler_params=_cparams(("parallel", "parallel", "parallel")),
        name="fft_stage1",
    )(z6, kmat, jnp.asarray(tc), jnp.asarray(ts))
    a5 = a6.reshape(B, G, n1, n2, 2 * GDIM)
    y = pl.pallas_call(
        _fft2_kernel,
        grid=(B, G, n1 // SUBLANES),
        in_specs=[
            pl.BlockSpec((None, None, SUBLANES, n2, 2 * GDIM), lambda b, g, i: (b, g, i, 0, 0)),
            pl.BlockSpec((n2, n2), lambda b, g, i: (0, 0)),
            pl.BlockSpec((n2, n2), lambda b, g, i: (0, 0)),
        ],
        out_specs=pl.BlockSpec((None, None, n2, SUBLANES * GDIM), lambda b, g, i: (b, g, 0, i)),
        out_shape=jax.ShapeDtypeStruct((B, G, n2, n1 * GDIM), F32),
        compiler_params=_cparams(("parallel", "parallel", "parallel")),
        name="fft_stage2",
    )(a5, c2, s2)
    return y.reshape(B, G, S, GDIM)


CONV_HALO = 16
CONV_ROWS = 16


def _conv_kernel(prev_ref, cur_ref, next_ref, w_ref, cb_ref, lg_ref, lb_ref, gc_ref, o_ref, scr):
    ts = cur_ref.shape[0]
    m = pl.program_id(1)
    last = pl.num_programs(1) - 1
    scr[0:CONV_HALO, :] = jnp.where(m > 0, prev_ref[...], 0.0)
    scr[CONV_HALO:CONV_HALO + ts, :] = cur_ref[...]
    scr[CONV_HALO + ts:, :] = jnp.where(m < last, next_ref[...], 0.0)
    off = CONV_HALO - CONV_PAD

    def step(c, carry):
        base = pl.multiple_of(c * CONV_ROWS, CONV_ROWS)
        win = scr[pl.ds(base, CONV_ROWS + 2 * CONV_HALO), :]
        acc = jnp.broadcast_to(cb_ref[...], (CONV_ROWS, D_CONV))
        for k in range(CONV_W):
            acc = acc + win[off + k:off + k + CONV_ROWS, :] * w_ref[k:k + 1, :]
        mu = jnp.mean(acc, axis=-1, keepdims=True)
        d = acc - mu
        var = jnp.mean(d * d, axis=-1, keepdims=True)
        y = d * lax.rsqrt(var + EPS) * lg_ref[...] + lb_ref[...]
        y = y * jax.nn.sigmoid(y)
        o_ref[pl.ds(base, CONV_ROWS), :] = _rms(y, gc_ref[...])
        return carry

    lax.fori_loop(0, ts // CONV_ROWS, step, 0)


def _conv_module(hg, conv_w, conv_b, ln_g, ln_b, gn_c, ts):
    B, S, _ = hg.shape
    hb = ts // CONV_HALO
    nh = S // CONV_HALO
    vec = pl.BlockSpec((1, D_CONV), lambda b, m: (0, 0))
    return pl.pallas_call(
        _conv_kernel,
        grid=(B, S // ts),
        in_specs=[
            pl.BlockSpec((None, CONV_HALO, D_CONV), lambda b, m: (b, jnp.maximum(m * hb - 1, 0), 0)),
            pl.BlockSpec((None, ts, D_CONV), lambda b, m: (b, m, 0)),
            pl.BlockSpec((None, CONV_HALO, D_CONV), lambda b, m: (b, jnp.minimum((m + 1) * hb, nh - 1), 0)),
            pl.BlockSpec((CONV_W, D_CONV), lambda b, m: (0, 0)),
            vec, vec, vec, vec,
        ],
        out_specs=pl.BlockSpec((None, ts, D_CONV), lambda b, m: (b, m, 0)),
        out_shape=jax.ShapeDtypeStruct((B, S, D_CONV), F32),
        scratch_shapes=[pltpu.VMEM((ts + 2 * CONV_HALO, D_CONV), F32)],
        compiler_params=_cparams(("parallel", "parallel")),
        name="conv_module",
    )(hg, hg, hg, conv_w, conv_b, ln_g, ln_b, gn_c)


def _extract_topk(s, order, k, payload=None):
    big = jnp.int32(1 << 30)
    vals, sel = [], []
    for _ in range(k):
        m = jnp.max(s, axis=0, keepdims=True)
        am = jnp.min(jnp.where(s == m, order, big), axis=0, keepdims=True)
        hit = order == am
        vals.append(m)
        if payload is None:
            sel.append(am)
        else:
            sel.append(jnp.max(jnp.where(hit, payload, -1), axis=0, keepdims=True))
        s = jnp.where(hit, -jnp.inf, s)
    return jnp.concatenate(vals, axis=0), jnp.concatenate(sel, axis=0)


_CAND_A = 4
_CAND_B = 3


def _route_kernel(x_ref, yf_ref, yc_ref, gf_ref, wout_ref, g2_ref, wq_ref, ka_ref, kb_ref,
                  x1_ref, htw_ref, idx_ref, gate_ref):
    tm = x_ref.shape[0]
    yf = jnp.concatenate([yf_ref[g] for g in range(GROUPS)], axis=-1)
    yfn = _rms(yf, gf_ref[...])
    ycat = jnp.concatenate([yfn, yc_ref[...]], axis=-1).astype(BF16)
    x1 = x_ref[...] + jnp.dot(ycat, wout_ref[...], preferred_element_type=F32)
    x1_ref[...] = x1
    ht = _rms(x1, g2_ref[...])
    htw_ref[...] = _pack_halves(ht)
    q = jnp.dot(ht.astype(BF16), wq_ref[...], preferred_element_type=F32)

    row16 = lax.broadcasted_iota(jnp.int32, (TOPK, tm), 0)
    key_iota = lax.broadcasted_iota(jnp.int32, (N_KEYS, tm), 0)
    nt = (((1,), (1,)), ((), ()))
    idx_rows, gate_rows = [], []
    for h in range(HEADS):
        qa = q[:, h * D_QUERY:h * D_QUERY + D_HALF].astype(BF16)
        qb = q[:, h * D_QUERY + D_HALF:(h + 1) * D_QUERY].astype(BF16)
        sa = lax.dot_general(ka_ref[h], qa, nt, preferred_element_type=F32)
        sb = lax.dot_general(kb_ref[h], qb, nt, preferred_element_type=F32)
        va, ia = _extract_topk(sa, key_iota, TOPK)
        vb, ib = _extract_topk(sb, key_iota, TOPK)
        cs, ce, co = [], [], []
        for i in range(_CAND_A):
            ok = (i + 1) * (row16 + 1) <= TOPK
            cs.append(jnp.where(ok, va[i:i + 1] + vb, -jnp.inf))
            ce.append(ia[i:i + 1] * N_KEYS + ib)
            co.append(i * TOPK + row16)
        for j in range(_CAND_B):
            ok = jnp.logical_and(row16 >= _CAND_A, (row16 + 1) * (j + 1) <= TOPK)
            cs.append(jnp.where(ok, va + vb[j:j + 1], -jnp.inf))
            ce.append(ia * N_KEYS + ib[j:j + 1])
            co.append(row16 * TOPK + j)
        top_s, top_e = _extract_topk(jnp.concatenate(cs, axis=0), jnp.concatenate(co, axis=0),
                                     TOPK, payload=jnp.concatenate(ce, axis=0))
        ex = jnp.exp(top_s - top_s[0:1])
        gate_rows.append(ex / jnp.sum(ex, axis=0, keepdims=True))
        idx_rows.append(top_e)
    idx_ref[...] = jnp.concatenate(idx_rows, axis=0).T
    gate_ref[...] = jnp.concatenate(gate_rows, axis=0).T


def _route(x, yf, yc, gf, wout_bf, g2, wq_bf, ka_bf, kb_bf, tm, s0, rows):
    B = x.shape[0]
    S = rows
    m0 = s0 // tm
    assert s0 % tm == 0 and rows % tm == 0
    const2 = lambda b, m: (0, 0)
    const3 = lambda b, m: (0, 0, 0)
    row = lambda w: pl.BlockSpec((None, tm, w), lambda b, m: (b, m, 0))
    row_in = lambda w: pl.BlockSpec((None, tm, w), lambda b, m: (b, m + m0, 0))
    return pl.pallas_call(
        _route_kernel,
        grid=(B, S // tm),
        in_specs=[
            row_in(D_MODEL),
            pl.BlockSpec((None, GROUPS, tm, GDIM), lambda b, m: (b, 0, m + m0, 0)),
            row_in(D_CONV),
            pl.BlockSpec((1, D_FOURIER), const2),
            pl.BlockSpec((D_MODEL, D_MODEL), const2),
            pl.BlockSpec((1, D_MODEL), const2),
            pl.BlockSpec((D_MODEL, HEADS * D_QUERY), const2),
            pl.BlockSpec((HEADS, N_KEYS, D_HALF), const3),
            pl.BlockSpec((HEADS, N_KEYS, D_HALF), const3),
        ],
        out_specs=[row(D_MODEL), row(D_WORDS), row(N_SEL), row(N_SEL)],
        out_shape=[
            jax.ShapeDtypeStruct((B, S, D_MODEL), F32),
            jax.ShapeDtypeStruct((B, S, D_WORDS), jnp.int32),
            jax.ShapeDtypeStruct((B, S, N_SEL), jnp.int32),
            jax.ShapeDtypeStruct((B, S, N_SEL), F32),
        ],
        compiler_params=_cparams(("parallel", "parallel")),
        name="route",
    )(x, yf, yc, gf, wout_bf, g2, wq_bf, ka_bf, kb_bf)


SC_UROWS = 64
SC_UNITS = N_SEL // SC_UROWS
SC_NBUF = 3
SC_TOKB = 32
SC_WCH = D_WORDS // SC_LANES
SC_QUAD = 4
SC_KG = 8
SC_COST_WEIGHT = 16


def _sc_mesh():
    return plsc.VectorSubcoreMesh(core_axis_name="c", subcore_axis_name="s",
                                  num_cores=SC_CORES, num_subcores=SC_SUBCORES)


def _sc_cost(tokens):
    rows = tokens * N_SEL * SC_COST_WEIGHT
    return pl.CostEstimate(flops=2 * rows * D_MODEL, transcendentals=0, bytes_accessed=rows * D_WORDS * 4)


def _sc_bf16(words):
    return plsc.bitcast(words, BF16)


def _sc_widen(pairs):
    w = plsc.bitcast(pairs, jnp.int32)
    lo = plsc.bitcast(lax.shift_left(w, 16), F32)
    hi = plsc.bitcast(lax.bitwise_and(w, jnp.int32(HI_MASK)), F32)
    return lo, hi


def _sc_quad_sum(row_chunk, scale):
    part = None
    for u in range(SC_QUAD):
        prod = _sc_bf16(row_chunk(u)) * scale(u)
        part = prod if part is None else part + prod
    return part


def _tree_sum(xs):
    while len(xs) > 1:
        xs = [xs[i] + xs[i + 1] for i in range(0, len(xs), 2)]
    return xs[0]


def _sc_token_loop(table_hbm, idx_hbm, side_hbm, out_hbm, idx_v, side_v, rows_v, out_v, gsem, osem,
                   tokens_per_worker, compute):
    wid = lax.axis_index("s") * SC_CORES + lax.axis_index("c")

    def gather(t, unit, buf):
        rows = idx_v.at[t, pl.ds(unit * SC_UROWS, SC_UROWS)]
        return pltpu.make_async_copy(table_hbm.at[rows], rows_v.at[buf], gsem.at[buf])

    def writeback(tok, slot):
        return pltpu.make_async_copy(out_v.at[slot], out_hbm.at[tok], osem.at[slot])

    @pl.loop(0, tokens_per_worker // SC_TOKB)
    def _(bi):
        tok0 = wid * tokens_per_worker + bi * SC_TOKB
        pltpu.sync_copy(idx_hbm.at[pl.ds(tok0, SC_TOKB)], idx_v)
        pltpu.sync_copy(side_hbm.at[pl.ds(tok0, SC_TOKB)], side_v)
        for unit in range(SC_UNITS):
            gather(0, unit, unit).start()

        @pl.loop(0, SC_TOKB)
        def _(t):
            slot = lax.rem(t, 2)

            @pl.when(t >= 2)
            def _():
                writeback(tok0 + t - 2, slot).wait()

            for unit in range(SC_UNITS):
                u = t * SC_UNITS + unit
                buf = lax.rem(u, SC_NBUF)
                gather(t, unit, buf).wait()

                @pl.when(t + 1 < SC_TOKB)
                def _():
                    gather(t + 1, unit, lax.rem(u + SC_UNITS, SC_NBUF)).start()

                compute(t, unit, buf, slot)
            writeback(tok0 + t, slot).start()

        for t in range(SC_TOKB - 2, SC_TOKB):
            writeback(tok0 + t, t % 2).wait()


def _peer_down(idx, htw, down_w):
    T = htw.shape[0]
    tpw = T // SC_WORKERS
    L = SC_LANES

    @functools.partial(
        pl.kernel, mesh=_sc_mesh(),
        out_type=jax.ShapeDtypeStruct((T, L * N_SEL), F32),
        scratch_types=[
            pltpu.VMEM((SC_TOKB, N_SEL), jnp.int32),
            pltpu.VMEM((SC_TOKB, D_WORDS), jnp.int32),
            pltpu.VMEM((SC_NBUF, SC_UROWS, D_WORDS), jnp.int32),
            pltpu.VMEM((2, L * N_SEL), F32),
            pltpu.SemaphoreType.DMA((SC_NBUF,)),
            pltpu.SemaphoreType.DMA((2,)),
        ],
        compiler_params=pltpu.CompilerParams(needs_layout_passes=False),
        name="peer_down", cost_estimate=_sc_cost(T),
    )
    def k(idx_hbm, h_hbm, down_hbm, s_hbm, idx_v, h_v, rows_v, s_v, gsem, osem):
        lane = lax.iota(jnp.int32, L)

        def compute(t, unit, buf, slot):
            svec = jnp.full((L,), slot, jnp.int32)
            hs = [_sc_bf16(h_v[t, pl.ds(w * L, L)]) for w in range(SC_WCH)]

            @plsc.parallel_loop(0, SC_UROWS, unroll=2)
            def _(r):
                parts = []
                for q in range(SC_WCH // SC_QUAD):
                    part = _sc_quad_sum(lambda u: rows_v[buf, r, pl.ds((q * SC_QUAD + u) * L, L)],
                                        lambda u: hs[q * SC_QUAD + u])
                    parts.extend(_sc_widen(part))
                col = lax.bitwise_and(lane + (unit * SC_UROWS + r), N_SEL - 1)
                plsc.store_scatter(s_v, [svec, lane * N_SEL + col], _tree_sum(parts))

        _sc_token_loop(down_hbm, idx_hbm, h_hbm, s_hbm, idx_v, h_v, rows_v, s_v, gsem, osem, tpw, compute)

    return k(idx, htw, down_w)


def _peer_up(idx, act_w, up_w):
    T = act_w.shape[0]
    tpw = T // SC_WORKERS
    L = SC_LANES

    @functools.partial(
        pl.kernel, mesh=_sc_mesh(),
        out_type=jax.ShapeDtypeStruct((T, D_MODEL), F32),
        scratch_types=[
            pltpu.VMEM((SC_TOKB, N_SEL), jnp.int32),
            pltpu.VMEM((SC_TOKB, N_SEL), jnp.int32),
            pltpu.VMEM((SC_NBUF, SC_UROWS, D_WORDS), jnp.int32),
            pltpu.VMEM((2, D_MODEL), F32),
            pltpu.SemaphoreType.DMA((SC_NBUF,)),
            pltpu.SemaphoreType.DMA((2,)),
        ],
        compiler_params=pltpu.CompilerParams(needs_layout_passes=False),
        name="peer_up", cost_estimate=_sc_cost(T),
    )
    def k(idx_hbm, act_hbm, up_hbm, o_hbm, idx_v, act_v, rows_v, o_v, gsem, osem):
        def compute(t, unit, buf, slot):
            for kg in range(SC_WCH // SC_KG):
                lo_cols = [pl.ds((kg * SC_KG + kk) * L, L) for kk in range(SC_KG)]
                hi_cols = [pl.ds(D_WORDS + (kg * SC_KG + kk) * L, L) for kk in range(SC_KG)]
                if unit == 0:
                    init = tuple(jnp.zeros((L,), F32) for _ in range(2 * SC_KG))
                else:
                    init = tuple(o_v[slot, col] for col in lo_cols + hi_cols)

                def rgbody(rg, accs):
                    per_vec = L // SC_QUAD
                    a16 = act_v[t, pl.ds(unit * SC_UROWS + (rg // per_vec) * L, L)]
                    lane0 = jnp.full((L,), 0, jnp.int32) + lax.rem(rg, per_vec) * SC_QUAD
                    a = [_sc_bf16(a16.at[lane0 + u].get(mode="promise_in_bounds")) for u in range(SC_QUAD)]
                    out_lo, out_hi = [], []
                    for kk in range(SC_KG):
                        part = _sc_quad_sum(lambda u: rows_v[buf, rg * SC_QUAD + u, lo_cols[kk]], lambda u: a[u])
                        lo, hi = _sc_widen(part)
                        out_lo.append(accs[kk] + lo)
                        out_hi.append(accs[SC_KG + kk] + hi)
                    return tuple(out_lo + out_hi)
                accs = lax.fori_loop(0, SC_UROWS // SC_QUAD, rgbody, init)
                for kk in range(SC_KG):
                    o_v[slot, lo_cols[kk]] = accs[kk]
                    o_v[slot, hi_cols[kk]] = accs[SC_KG + kk]

        _sc_token_loop(up_hbm, idx_hbm, act_hbm, o_hbm, idx_v, act_v, rows_v, o_v, gsem, osem, tpw, compute)

    return k(idx, act_w, up_w)


def _act_kernel(s_ref, gate_ref, a_ref):
    s = _tree_sum([pltpu.roll(s_ref[:, l * N_SEL:(l + 1) * N_SEL], (N_SEL - l) % N_SEL, axis=1)
                   for l in range(SC_LANES)])
    act = 0.5 * s * (1.0 + lax.erf(s * math.sqrt(0.5))) * gate_ref[...]
    bits = _bf16_bits(act)
    a_ref[...] = lax.bitwise_or(lax.shift_right_logical(bits, 16), bits)


def _act(s, gate, tm):
    T = s.shape[0]
    spec = pl.BlockSpec((tm, N_SEL), lambda i: (i, 0))
    part_spec = pl.BlockSpec((tm, SC_LANES * N_SEL), lambda i: (i, 0))
    return pl.pallas_call(
        _act_kernel, grid=(T // tm,), in_specs=[part_spec, spec], out_specs=spec,
        out_shape=jax.ShapeDtypeStruct((T, N_SEL), jnp.int32),
        compiler_params=_cparams(("parallel",)), name="expert_act",
    )(s, gate)


def _final_kernel(x1_ref, p_ref, g_ref, o_ref):
    o_ref[...] = _rms(x1_ref[...] + p_ref[...], g_ref[...])


def _final(x1, peer, g, tm):
    T = x1.shape[0]
    spec = pl.BlockSpec((tm, D_MODEL), lambda i: (i, 0))
    return pl.pallas_call(
        _final_kernel, grid=(T // tm,),
        in_specs=[spec, spec, pl.BlockSpec((1, D_MODEL), lambda i: (0, 0))], out_specs=spec,
        out_shape=jax.ShapeDtypeStruct((T, D_MODEL), F32),
        compiler_params=_cparams(("parallel",)), name="final_norm",
    )(x1, peer, g)


PROMPT_PLAN = ((1, 2),) + ((1, 1),) * 7
SAMPLE_PLAN = ((1, 2),)
SKEW = 2


def _tile(n, want):
    t = min(n, want)
    assert n % t == 0
    return t


def _channel_dft():
    ang = 2.0 * np.pi * np.outer(np.arange(GDIM), np.arange(GDIM)) / GDIM
    m = np.concatenate([np.cos(ang), -np.sin(ang)], axis=1) / math.sqrt(GDIM)
    return jnp.asarray(m.astype(np.float32), BF16)


def _plan_units(x, plan):
    units, parts_per_pass = [], []
    b = 0
    for nb, parts in plan:
        xb = x[b:b + nb]
        rows = x.shape[1] // parts
        units += [(xb, i * rows, rows) for i in range(parts)]
        parts_per_pass.append(parts)
        b += nb
    assert b == x.shape[0]
    return units, parts_per_pass


def _assemble(outs, parts_per_pass):
    passes, k = [], 0
    for parts in parts_per_pass:
        passes.append(outs[k] if parts == 1 else jnp.concatenate(outs[k:k + parts], axis=1))
        k += parts
    return passes[0] if len(passes) == 1 else jnp.concatenate(passes, axis=0)


def _mixer(x, p):
    S = x.shape[1]
    zg, hg = _inproj(x, p["g1"], p["win"], p["cdft"], _tile(S, 512))
    yf = _seq_dft(zg)
    yc = _conv_module(hg, p["conv_w"], p["conv_b"], p["ln_g"], p["ln_b"], p["gn_c"], _tile(S, 512))
    return yf, yc


def _route_unit(x, mixed, s0, rows, p):
    T = x.shape[0] * rows
    x1, htw, idx, gate = _route(x, mixed[0], mixed[1], p["gn_f"], p["wout"], p["g2"], p["wq"], p["ka"],
                                p["kb"], _tile(rows, 256), s0, rows)
    return x1.reshape(T, D_MODEL), htw.reshape(T, D_WORDS), idx.reshape(T, N_SEL), gate.reshape(T, N_SEL)


def _encode(units, p):
    x1s, idxs, acts, peers = [], [], [], []
    mixed, mixed_x = None, None
    for g, (x, s0, rows) in enumerate(units):
        reuse = mixed_x is x
        carrier = mixed if reuse else x
        slots = [(acts, g - SKEW)] if g >= SKEW else []
        if g >= SKEW + 1:
            slots.append((peers, g - SKEW - 1))
        if slots:
            carrier, *tied = lax.optimization_barrier((carrier, *[lst[i] for lst, i in slots]))
            for (lst, i), v in zip(slots, tied):
                lst[i] = v
        if g >= SKEW:
            peers.append(_peer_up(idxs[g - SKEW], acts[g - SKEW], p["up"]))
        if reuse:
            mixed = carrier
        else:
            x = carrier
            mixed, mixed_x = _mixer(x, p), units[g][0]
        assert (x.shape[0] * rows) % (SC_WORKERS * SC_TOKB) == 0
        x1, htw, idx, gate = _route_unit(x, mixed, s0, rows, p)
        s = _peer_down(idx, htw, p["down"])
        x1s.append(x1)
        idxs.append(idx)
        acts.append(_act(s, gate, _tile(x1.shape[0], 1024)))
    for g in range(len(peers), len(units)):
        peers.append(_peer_up(idxs[g], acts[g], p["up"]))
    return [_final(x1, peer, p["final_g"], _tile(x1.shape[0], 512)).reshape(x.shape[0], rows, D_MODEL)
            for (x, _, rows), x1, peer in zip(units, x1s, peers)]


def kernel(x_prompt, x_sample, norm1_g, w_in, conv_w, conv_b, conv_ln_g, conv_ln_b, gn_fourier_g,
           gn_conv_g, w_out, norm2_g, w_q, keys_a, keys_b, expert_down, expert_up, final_g):
    assert norm1_g.shape[0] == 1
    p = dict(
        g1=norm1_g[0].reshape(1, D_MODEL), win=w_in[0].astype(BF16), cdft=_channel_dft(),
        conv_w=conv_w[0], conv_b=conv_b[0].reshape(1, D_CONV),
        ln_g=conv_ln_g[0].reshape(1, D_CONV), ln_b=conv_ln_b[0].reshape(1, D_CONV),
        gn_f=gn_fourier_g[0].reshape(1, D_FOURIER), gn_c=gn_conv_g[0].reshape(1, D_CONV),
        wout=w_out[0].astype(BF16), g2=norm2_g[0].reshape(1, D_MODEL), wq=w_q[0].astype(BF16),
        ka=keys_a[0].astype(BF16), kb=keys_b[0].astype(BF16),
        down=_pack_halves(expert_down[0]), up=_pack_halves(expert_up[0]),
        final_g=final_g.reshape(1, D_MODEL),
    )
    prompt_units, prompt_parts = _plan_units(x_prompt, PROMPT_PLAN)
    sample_units, sample_parts = _plan_units(x_sample, SAMPLE_PLAN)
    outs = _encode(prompt_units + sample_units, p)
    n_prompt = len(prompt_units)
    return _assemble(outs[:n_prompt], prompt_parts), _assemble(outs[n_prompt:], sample_parts)
```

```python
import functools
import math

import numpy as np
import jax
import jax.numpy as jnp
from jax import lax
from jax.experimental import pallas as pl
from jax.experimental.pallas import tpu as pltpu
from jax.experimental.pallas import tpu_sc as plsc

F32 = jnp.float32
BF16 = jnp.bfloat16

D_MODEL = 1024
D_FOURIER = 512
GROUPS = 4
GDIM = 128
D_CONV = 512
CONV_W = 31
CONV_PAD = 15
HEADS = 8
N_KEYS = 128
TOPK = 16
D_HALF = 128
D_QUERY = 256
N_SEL = HEADS * TOPK
EPS = 1e-6

SUBLANES = 8
LANES = 128
VMEM_LIMIT = 56 * 1024 * 1024

SC_CORES = 2
SC_SUBCORES = 16
SC_LANES = 16
SC_WORKERS = SC_CORES * SC_SUBCORES


def _cparams(sem):
    return pltpu.CompilerParams(dimension_semantics=sem, vmem_limit_bytes=VMEM_LIMIT)


def _rms(x, g):
    return x * lax.rsqrt(jnp.mean(x * x, axis=-1, keepdims=True) + EPS) * g


D_WORDS = D_MODEL // 2
HI_MASK = -65536


def _bf16_bits(x):
    return lax.bitcast_convert_type(x.astype(BF16).astype(F32), jnp.int32)


def _pack_halves(x):
    lo = lax.shift_right_logical(_bf16_bits(x[..., :D_WORDS]), 16)
    hi = lax.bitwise_and(_bf16_bits(x[..., D_WORDS:]), HI_MASK)
    return lax.bitwise_or(lo, hi)


def _inproj_kernel(x_ref, g1_ref, win_ref, cdft_ref, zg_ref, hg_ref):
    h = _rms(x_ref[...], g1_ref[...]).astype(BF16)
    z = jnp.dot(h, win_ref[...], preferred_element_type=F32)
    for g in range(GROUPS):
        zg = z[:, g * GDIM:(g + 1) * GDIM].astype(BF16)
        zg_ref[g] = jnp.dot(zg, cdft_ref[...], preferred_element_type=F32)
    a = z[:, D_FOURIER:D_FOURIER + D_CONV]
    gate = z[:, D_FOURIER + D_CONV:]
    hg_ref[...] = a * jax.nn.sigmoid(gate)


def _inproj(x, g1, win_bf, cdft, tm):
    B, S, _ = x.shape
    return pl.pallas_call(
        _inproj_kernel,
        grid=(B, S // tm),
        in_specs=[
            pl.BlockSpec((None, tm, D_MODEL), lambda b, m: (b, m, 0)),
            pl.BlockSpec((1, D_MODEL), lambda b, m: (0, 0)),
            pl.BlockSpec((D_MODEL, D_FOURIER + 2 * D_CONV), lambda b, m: (0, 0)),
            pl.BlockSpec((GDIM, 2 * GDIM), lambda b, m: (0, 0)),
        ],
        out_specs=[
            pl.BlockSpec((None, GROUPS, tm, 2 * GDIM), lambda b, m: (b, 0, m, 0)),
            pl.BlockSpec((None, tm, D_CONV), lambda b, m: (b, m, 0)),
        ],
        out_shape=[
            jax.ShapeDtypeStruct((B, GROUPS, S, 2 * GDIM), F32),
            jax.ShapeDtypeStruct((B, S, D_CONV), F32),
        ],
        compiler_params=_cparams(("parallel", "parallel")),
        name="inproj",
    )(x, g1, win_bf, cdft)


def _fft1_kernel(z_ref, k1_ref, tc_ref, ts_ref, a_ref):
    n1 = z_ref.shape[0]
    rows = n1 * SUBLANES
    x = z_ref[...].reshape(rows, 2 * GDIM).astype(BF16)
    p = jnp.dot(k1_ref[...], x, preferred_element_type=F32)
    pc, ps = p[:rows], p[rows:]
    ar = pc[:, :GDIM] + ps[:, GDIM:]
    ai = pc[:, GDIM:] - ps[:, :GDIM]
    tc, ts = tc_ref[...], ts_ref[...]
    a_ref[:, :, :GDIM] = (ar * tc + ai * ts).reshape(n1, SUBLANES, GDIM)
    a_ref[:, :, GDIM:] = (ai * tc - ar * ts).reshape(n1, SUBLANES, GDIM)


def _fft2_kernel(a_ref, c2_ref, s2_ref, y_ref):
    for r in range(SUBLANES):
        blk = a_ref[r].astype(BF16)
        y = jnp.dot(c2_ref[...], blk[:, :GDIM], preferred_element_type=F32)
        y = y + jnp.dot(s2_ref[...], blk[:, GDIM:], preferred_element_type=F32)
        y_ref[:, r * GDIM:(r + 1) * GDIM] = y


def _split_seq(S):
    n1 = 1 << (int(math.log2(S)) // 2)
    return n1, S // n1


@functools.lru_cache(maxsize=None)
def _fft_tables(S):
    n1, n2 = _split_seq(S)
    k1 = np.arange(n1)
    ang1 = 2.0 * np.pi * np.outer(k1, np.arange(n1)) / n1
    eye = np.eye(SUBLANES)
    kmat = np.concatenate([np.kron(np.cos(ang1), eye), np.kron(np.sin(ang1), eye)], axis=0)
    nn2 = np.arange(n2).reshape(n2 // SUBLANES, 1, SUBLANES)
    angt = 2.0 * np.pi * k1.reshape(1, n1, 1) * nn2 / S
    angt = angt.reshape(n2 // SUBLANES, n1 * SUBLANES, 1)
    tc = np.broadcast_to(np.cos(angt), (n2 // SUBLANES, n1 * SUBLANES, GDIM))
    ts = np.broadcast_to(np.sin(angt), (n2 // SUBLANES, n1 * SUBLANES, GDIM))
    ang2 = 2.0 * np.pi * np.outer(np.arange(n2), np.arange(n2)) / n2
    scale = 1.0 / math.sqrt(S)
    return (kmat.astype(np.float32), np.ascontiguousarray(tc, np.float32),
            np.ascontiguousarray(ts, np.float32),
            (np.cos(ang2) * scale).astype(np.float32), (np.sin(ang2) * scale).astype(np.float32))


def _seq_dft(zg):
    B, G, S, _ = zg.shape
    n1, n2 = _split_seq(S)
    kmat, tc, ts, c2, s2 = _fft_tables(S)
    kmat = jnp.asarray(kmat, BF16)
    c2 = jnp.asarray(c2, BF16)
    s2 = jnp.asarray(s2, BF16)
    nb = n2 // SUBLANES
    z6 = zg.reshape(B, G, n1, nb, SUBLANES, 2 * GDIM)
    blk6 = (None, None, n1, None, SUBLANES, 2 * GDIM)
    a6 = pl.pallas_call(
        _fft1_kernel,
        grid=(nb, B, G),
        in_specs=[
            pl.BlockSpec(blk6, lambda j, b, g: (b, g, 0, j, 0, 0)),
            pl.BlockSpec((2 * n1 * SUBLANES, n1 * SUBLANES), lambda j, b, g: (0, 0)),
            pl.BlockSpec((None, n1 * SUBLANES, GDIM), lambda j, b, g: (j, 0, 0)),
            pl.BlockSpec((None, n1 * SUBLANES, GDIM), lambda j, b, g: (j, 0, 0)),
        ],
        out_specs=pl.BlockSpec(blk6, lambda j, b, g: (b, g, 0, j, 0, 0)),
        out_shape=jax.ShapeDtypeStruct(z6.shape, F32),
        compiler_params=_cparams(("parallel", "parallel", "parallel")),
        name="fft_stage1",
    )(z6, kmat, jnp.asarray(tc), jnp.asarray(ts))
    a5 = a6.reshape(B, G, n1, n2, 2 * GDIM)
    y = pl.pallas_call(
        _fft2_kernel,
        grid=(B, G, n1 // SUBLANES),
        in_specs=[
            pl.BlockSpec((None, None, SUBLANES, n2, 2 * GDIM), lambda b, g, i: (b, g, i, 0, 0)),
            pl.BlockSpec((n2, n2), lambda b, g, i: (0, 0)),
            pl.BlockSpec((n2, n2), lambda b, g, i: (0, 0)),
        ],
        out_specs=pl.BlockSpec((None, None, n2, SUBLANES * GDIM), lambda b, g, i: (b, g, 0, i)),
        out_shape=jax.ShapeDtypeStruct((B, G, n2, n1 * GDIM), F32),
        compiler_params=_cparams(("parallel", "parallel", "parallel")),
        name="fft_stage2",
    )(a5, c2, s2)
    return y.reshape(B, G, S, GDIM)


CONV_HALO = 16
CONV_ROWS = 16


def _conv_kernel(prev_ref, cur_ref, next_ref, w_ref, cb_ref, lg_ref, lb_ref, gc_ref, o_ref, scr):
    ts = cur_ref.shape[0]
    m = pl.program_id(1)
    last = pl.num_programs(1) - 1
    scr[0:CONV_HALO, :] = jnp.where(m > 0, prev_ref[...], 0.0)
    scr[CONV_HALO:CONV_HALO + ts, :] = cur_ref[...]
    scr[CONV_HALO + ts:, :] = jnp.where(m < last, next_ref[...], 0.0)
    off = CONV_HALO - CONV_PAD

    def step(c, carry):
        base = pl.multiple_of(c * CONV_ROWS, CONV_ROWS)
        win = scr[pl.ds(base, CONV_ROWS + 2 * CONV_HALO), :]
        acc = jnp.broadcast_to(cb_ref[...], (CONV_ROWS, D_CONV))
        for k in range(CONV_W):
            acc = acc + win[off + k:off + k + CONV_ROWS, :] * w_ref[k:k + 1, :]
        mu = jnp.mean(acc, axis=-1, keepdims=True)
        d = acc - mu
        var = jnp.mean(d * d, axis=-1, keepdims=True)
        y = d * lax.rsqrt(var + EPS) * lg_ref[...] + lb_ref[...]
        y = y * jax.nn.sigmoid(y)
        o_ref[pl.ds(base, CONV_ROWS), :] = _rms(y, gc_ref[...])
        return carry

    lax.fori_loop(0, ts // CONV_ROWS, step, 0)


def _conv_module(hg, conv_w, conv_b, ln_g, ln_b, gn_c, ts):
    B, S, _ = hg.shape
    hb = ts // CONV_HALO
    nh = S // CONV_HALO
    vec = pl.BlockSpec((1, D_CONV), lambda b, m: (0, 0))
    return pl.pallas_call(
        _conv_kernel,
        grid=(B, S // ts),
        in_specs=[
            pl.BlockSpec((None, CONV_HALO, D_CONV), lambda b, m: (b, jnp.maximum(m * hb - 1, 0), 0)),
            pl.BlockSpec((None, ts, D_CONV), lambda b, m: (b, m, 0)),
            pl.BlockSpec((None, CONV_HALO, D_CONV), lambda b, m: (b, jnp.minimum((m + 1) * hb, nh - 1), 0)),
            pl.BlockSpec((CONV_W, D_CONV), lambda b, m: (0, 0)),
            vec, vec, vec, vec,
        ],
        out_specs=pl.BlockSpec((None, ts, D_CONV), lambda b, m: (b, m, 0)),
        out_shape=jax.ShapeDtypeStruct((B, S, D_CONV), F32),
        scratch_shapes=[pltpu.VMEM((ts + 2 * CONV_HALO, D_CONV), F32)],
        compiler_params=_cparams(("parallel", "parallel")),
        name="conv_module",
    )(hg, hg, hg, conv_w, conv_b, ln_g, ln_b, gn_c)


def _extract_topk(s, order, k, payload=None):
    big = jnp.int32(1 << 30)
    vals, sel = [], []
    for _ in range(k):
        m = jnp.max(s, axis=0, keepdims=True)
        am = jnp.min(jnp.where(s == m, order, big), axis=0, keepdims=True)
        hit = order == am
        vals.append(m)
        if payload is None:
            sel.append(am)
        else:
            sel.append(jnp.max(jnp.where(hit, payload, -1), axis=0, keepdims=True))
        s = jnp.where(hit, -jnp.inf, s)
    return jnp.concatenate(vals, axis=0), jnp.concatenate(sel, axis=0)


_CAND_A = 4
_CAND_B = 3


def _route_kernel(x_ref, yf_ref, yc_ref, gf_ref, wout_ref, g2_ref, wq_ref, ka_ref, kb_ref,
                  x1_ref, htw_ref, idx_ref, gate_ref):
    tm = x_ref.shape[0]
    yf = jnp.concatenate([yf_ref[g] for g in range(GROUPS)], axis=-1)
    yfn = _rms(yf, gf_ref[...])
    ycat = jnp.concatenate([yfn, yc_ref[...]], axis=-1).astype(BF16)
    x1 = x_ref[...] + jnp.dot(ycat, wout_ref[...], preferred_element_type=F32)
    x1_ref[...] = x1
    ht = _rms(x1, g2_ref[...])
    htw_ref[...] = _pack_halves(ht)
    q = jnp.dot(ht.astype(BF16), wq_ref[...], preferred_element_type=F32)

    row16 = lax.broadcasted_iota(jnp.int32, (TOPK, tm), 0)
    key_iota = lax.broadcasted_iota(jnp.int32, (N_KEYS, tm), 0)
    nt = (((1,), (1,)), ((), ()))
    idx_rows, gate_rows = [], []
    for h in range(HEADS):
        qa = q[:, h * D_QUERY:h * D_QUERY + D_HALF].astype(BF16)
        qb = q[:, h * D_QUERY + D_HALF:(h + 1) * D_QUERY].astype(BF16)
        sa = lax.dot_general(ka_ref[h], qa, nt, preferred_element_type=F32)
        sb = lax.dot_general(kb_ref[h], qb, nt, preferred_element_type=F32)
        va, ia = _extract_topk(sa, key_iota, TOPK)
        vb, ib = _extract_topk(sb, key_iota, TOPK)
        cs, ce, co = [], [], []
        for i in range(_CAND_A):
            ok = (i + 1) * (row16 + 1) <= TOPK
            cs.append(jnp.where(ok, va[i:i + 1] + vb, -jnp.inf))
            ce.append(ia[i:i + 1] * N_KEYS + ib)
            co.append(i * TOPK + row16)
        for j in range(_CAND_B):
            ok = jnp.logical_and(row16 >= _CAND_A, (row16 + 1) * (j + 1) <= TOPK)
            cs.append(jnp.where(ok, va + vb[j:j + 1], -jnp.inf))
            ce.append(ia * N_KEYS + ib[j:j + 1])
            co.append(row16 * TOPK + j)
        top_s, top_e = _extract_topk(jnp.concatenate(cs, axis=0), jnp.concatenate(co, axis=0),
                                     TOPK, payload=jnp.concatenate(ce, axis=0))
        ex = jnp.exp(top_s - top_s[0:1])
        gate_rows.append(ex / jnp.sum(ex, axis=0, keepdims=True))
        idx_rows.append(top_e)
    idx_ref[...] = jnp.concatenate(idx_rows, axis=0).T
    gate_ref[...] = jnp.concatenate(gate_rows, axis=0).T


def _route(x, yf, yc, gf, wout_bf, g2, wq_bf, ka_bf, kb_bf, tm, s0, rows):
    B = x.shape[0]
    S = rows
    m0 = s0 // tm
    assert s0 % tm == 0 and rows % tm == 0
    const2 = lambda b, m: (0, 0)
    const3 = lambda b, m: (0, 0, 0)
    row = lambda w: pl.BlockSpec((None, tm, w), lambda b, m: (b, m, 0))
    row_in = lambda w: pl.BlockSpec((None, tm, w), lambda b, m: (b, m + m0, 0))
    return pl.pallas_call(
        _route_kernel,
        grid=(B, S // tm),
        in_specs=[
            row_in(D_MODEL),
            pl.BlockSpec((None, GROUPS, tm, GDIM), lambda b, m: (b, 0, m + m0, 0)),
            row_in(D_CONV),
            pl.BlockSpec((1, D_FOURIER), const2),
            pl.BlockSpec((D_MODEL, D_MODEL), const2),
            pl.BlockSpec((1, D_MODEL), const2),
            pl.BlockSpec((D_MODEL, HEADS * D_QUERY), const2),
            pl.BlockSpec((HEADS, N_KEYS, D_HALF), const3),
            pl.BlockSpec((HEADS, N_KEYS, D_HALF), const3),
        ],
        out_specs=[row(D_MODEL), row(D_WORDS), row(N_SEL), row(N_SEL)],
        out_shape=[
            jax.ShapeDtypeStruct((B, S, D_MODEL), F32),
            jax.ShapeDtypeStruct((B, S, D_WORDS), jnp.int32),
            jax.ShapeDtypeStruct((B, S, N_SEL), jnp.int32),
            jax.ShapeDtypeStruct((B, S, N_SEL), F32),
        ],
        compiler_params=_cparams(("parallel", "parallel")),
        name="route",
    )(x, yf, yc, gf, wout_bf, g2, wq_bf, ka_bf, kb_bf)


SC_UROWS = 64
SC_UNITS = N_SEL // SC_UROWS
SC_NBUF = 3
SC_TOKB = 32
SC_WCH = D_WORDS // SC_LANES
SC_QUAD = 4
SC_KG = 8
SC_COST_WEIGHT = 16


def _sc_mesh():
    return plsc.VectorSubcoreMesh(core_axis_name="c", subcore_axis_name="s",
                                  num_cores=SC_CORES, num_subcores=SC_SUBCORES)


def _sc_cost(tokens):
    rows = tokens * N_SEL * SC_COST_WEIGHT
    return pl.CostEstimate(flops=2 * rows * D_MODEL, transcendentals=0, bytes_accessed=rows * D_WORDS * 4)


def _sc_bf16(words):
    return plsc.bitcast(words, BF16)


def _sc_widen(pairs):
    w = plsc.bitcast(pairs, jnp.int32)
    lo = plsc.bitcast(lax.shift_left(w, 16), F32)
    hi = plsc.bitcast(lax.bitwise_and(w, jnp.int32(HI_MASK)), F32)
    return lo, hi


def _sc_quad_sum(row_chunk, scale):
    part = None
    for u in range(SC_QUAD):
        prod = _sc_bf16(row_chunk(u)) * scale(u)
        part = prod if part is None else part + prod
    return part


def _tree_sum(xs):
    while len(xs) > 1:
        xs = [xs[i] + xs[i + 1] for i in range(0, len(xs), 2)]
    return xs[0]


def _sc_token_loop(table_hbm, idx_hbm, side_hbm, out_hbm, idx_v, side_v, rows_v, out_v, gsem, osem,
                   tokens_per_worker, compute):
    wid = lax.axis_index("s") * SC_CORES + lax.axis_index("c")

    def gather(t, unit, buf):
        rows = idx_v.at[t, pl.ds(unit * SC_UROWS, SC_UROWS)]
        return pltpu.make_async_copy(table_hbm.at[rows], rows_v.at[buf], gsem.at[buf])

    def writeback(tok, slot):
        return pltpu.make_async_copy(out_v.at[slot], out_hbm.at[tok], osem.at[slot])

    @pl.loop(0, tokens_per_worker // SC_TOKB)
    def _(bi):
        tok0 = wid * tokens_per_worker + bi * SC_TOKB
        pltpu.sync_copy(idx_hbm.at[pl.ds(tok0, SC_TOKB)], idx_v)
        pltpu.sync_copy(side_hbm.at[pl.ds(tok0, SC_TOKB)], side_v)
        for unit in range(SC_UNITS):
            gather(0, unit, unit).start()

        @pl.loop(0, SC_TOKB)
        def _(t):
            slot = lax.rem(t, 2)

            @pl.when(t >= 2)
            def _():
                writeback(tok0 + t - 2, slot).wait()

            for unit in range(SC_UNITS):
                u = t * SC_UNITS + unit
                buf = lax.rem(u, SC_NBUF)
                gather(t, unit, buf).wait()

                @pl.when(t + 1 < SC_TOKB)
                def _():
                    gather(t + 1, unit, lax.rem(u + SC_UNITS, SC_NBUF)).start()

                compute(t, unit, buf, slot)
            writeback(tok0 + t, slot).start()

        for t in range(SC_TOKB - 2, SC_TOKB):
            writeback(tok0 + t, t % 2).wait()


def _peer_down(idx, htw, down_w):
    T = htw.shape[0]
    tpw = T // SC_WORKERS
    L = SC_LANES

    @functools.partial(
        pl.kernel, mesh=_sc_mesh(),
        out_type=jax.ShapeDtypeStruct((T, L * N_SEL), F32),
        scratch_types=[
            pltpu.VMEM((SC_TOKB, N_SEL), jnp.int32),
            pltpu.VMEM((SC_TOKB, D_WORDS), jnp.int32),
            pltpu.VMEM((SC_NBUF, SC_UROWS, D_WORDS), jnp.int32),
            pltpu.VMEM((2, L * N_SEL), F32),
            pltpu.SemaphoreType.DMA((SC_NBUF,)),
            pltpu.SemaphoreType.DMA((2,)),
        ],
        compiler_params=pltpu.CompilerParams(needs_layout_passes=False),
        name="peer_down", cost_estimate=_sc_cost(T),
    )
    def k(idx_hbm, h_hbm, down_hbm, s_hbm, idx_v, h_v, rows_v, s_v, gsem, osem):
        lane = lax.iota(jnp.int32, L)

        def compute(t, unit, buf, slot):
            svec = jnp.full((L,), slot, jnp.int32)
            hs = [_sc_bf16(h_v[t, pl.ds(w * L, L)]) for w in range(SC_WCH)]

            @plsc.parallel_loop(0, SC_UROWS, unroll=2)
            def _(r):
                parts = []
                for q in range(SC_WCH // SC_QUAD):
                    part = _sc_quad_sum(lambda u: rows_v[buf, r, pl.ds((q * SC_QUAD + u) * L, L)],
                                        lambda u: hs[q * SC_QUAD + u])
                    parts.extend(_sc_widen(part))
                col = lax.bitwise_and(lane + (unit * SC_UROWS + r), N_SEL - 1)
                plsc.store_scatter(s_v, [svec, lane * N_SEL + col], _tree_sum(parts))

        _sc_token_loop(down_hbm, idx_hbm, h_hbm, s_hbm, idx_v, h_v, rows_v, s_v, gsem, osem, tpw, compute)

    return k(idx, htw, down_w)


def _peer_up(idx, act_w, up_w):
    T = act_w.shape[0]
    tpw = T // SC_WORKERS
    L = SC_LANES

    @functools.partial(
        pl.kernel, mesh=_sc_mesh(),
        out_type=jax.ShapeDtypeStruct((T, D_MODEL), F32),
        scratch_types=[
            pltpu.VMEM((SC_TOKB, N_SEL), jnp.int32),
            pltpu.VMEM((SC_TOKB, N_SEL), jnp.int32),
            pltpu.VMEM((SC_NBUF, SC_UROWS, D_WORDS), jnp.int32),
            pltpu.VMEM((2, D_MODEL), F32),
            pltpu.SemaphoreType.DMA((SC_NBUF,)),
            pltpu.SemaphoreType.DMA((2,)),
        ],
        compiler_params=pltpu.CompilerParams(needs_layout_passes=False),
        name="peer_up", cost_estimate=_sc_cost(T),
    )
    def k(idx_hbm, act_hbm, up_hbm, o_hbm, idx_v, act_v, rows_v, o_v, gsem, osem):
        def compute(t, unit, buf, slot):
            for kg in range(SC_WCH // SC_KG):
                lo_cols = [pl.ds((kg * SC_KG + kk) * L, L) for kk in range(SC_KG)]
                hi_cols = [pl.ds(D_WORDS + (kg * SC_KG + kk) * L, L) for kk in range(SC_KG)]
                if unit == 0:
                    init = tuple(jnp.zeros((L,), F32) for _ in range(2 * SC_KG))
                else:
                    init = tuple(o_v[slot, col] for col in lo_cols + hi_cols)

                def rgbody(rg, accs):
                    per_vec = L // SC_QUAD
                    a16 = act_v[t, pl.ds(unit * SC_UROWS + (rg // per_vec) * L, L)]
                    lane0 = jnp.full((L,), 0, jnp.int32) + lax.rem(rg, per_vec) * SC_QUAD
                    a = [_sc_bf16(a16.at[lane0 + u].get(mode="promise_in_bounds")) for u in range(SC_QUAD)]
                    out_lo, out_hi = [], []
                    for kk in range(SC_KG):
                        part = _sc_quad_sum(lambda u: rows_v[buf, rg * SC_QUAD + u, lo_cols[kk]], lambda u: a[u])
                        lo, hi = _sc_widen(part)
                        out_lo.append(accs[kk] + lo)
                        out_hi.append(accs[SC_KG + kk] + hi)
                    return tuple(out_lo + out_hi)
                accs = lax.fori_loop(0, SC_UROWS // SC_QUAD, rgbody, init)
                for kk in range(SC_KG):
                    o_v[slot, lo_cols[kk]] = accs[kk]
                    o_v[slot, hi_cols[kk]] = accs[SC_KG + kk]

        _sc_token_loop(up_hbm, idx_hbm, act_hbm, o_hbm, idx_v, act_v, rows_v, o_v, gsem, osem, tpw, compute)

    return k(idx, act_w, up_w)


def _act_kernel(s_ref, gate_ref, a_ref):
    s = _tree_sum([pltpu.roll(s_ref[:, l * N_SEL:(l + 1) * N_SEL], (N_SEL - l) % N_SEL, axis=1)
                   for l in range(SC_LANES)])
    act = 0.5 * s * (1.0 + lax.erf(s * math.sqrt(0.5))) * gate_ref[...]
    bits = _bf16_bits(act)
    a_ref[...] = lax.bitwise_or(lax.shift_right_logical(bits, 16), bits)


def _act(s, gate, tm):
    T = s.shape[0]
    spec = pl.BlockSpec((tm, N_SEL), lambda i: (i, 0))
    part_spec = pl.BlockSpec((tm, SC_LANES * N_SEL), lambda i: (i, 0))
    return pl.pallas_call(
        _act_kernel, grid=(T // tm,), in_specs=[part_spec, spec], out_specs=spec,
        out_shape=jax.ShapeDtypeStruct((T, N_SEL), jnp.int32),
        compiler_params=_cparams(("parallel",)), name="expert_act",
    )(s, gate)


def _final_kernel(x1_ref, p_ref, g_ref, o_ref):
    o_ref[...] = _rms(x1_ref[...] + p_ref[...], g_ref[...])


def _final(x1, peer, g, tm):
    T = x1.shape[0]
    spec = pl.BlockSpec((tm, D_MODEL), lambda i: (i, 0))
    return pl.pallas_call(
        _final_kernel, grid=(T // tm,),
        in_specs=[spec, spec, pl.BlockSpec((1, D_MODEL), lambda i: (0, 0))], out_specs=spec,
        out_shape=jax.ShapeDtypeStruct((T, D_MODEL), F32),
        compiler_params=_cparams(("parallel",)), name="final_norm",
    )(x1, peer, g)


PROMPT_PLAN = ((1, 2),) + ((1, 1),) * 7
SAMPLE_PLAN = ((1, 2),)
SKEW = 2
EARLY_MIX_SLOT = 5


def _tile(n, want):
    t = min(n, want)
    assert n % t == 0
    return t


def _channel_dft():
    ang = 2.0 * np.pi * np.outer(np.arange(GDIM), np.arange(GDIM)) / GDIM
    m = np.concatenate([np.cos(ang), -np.sin(ang)], axis=1) / math.sqrt(GDIM)
    return jnp.asarray(m.astype(np.float32), BF16)


def _plan_units(x, plan):
    units, parts_per_pass = [], []
    b = 0
    for nb, parts in plan:
        xb = x[b:b + nb]
        rows = x.shape[1] // parts
        units += [(xb, i * rows, rows) for i in range(parts)]
        parts_per_pass.append(parts)
        b += nb
    assert b == x.shape[0]
    return units, parts_per_pass


def _assemble(outs, parts_per_pass):
    passes, k = [], 0
    for parts in parts_per_pass:
        passes.append(outs[k] if parts == 1 else jnp.concatenate(outs[k:k + parts], axis=1))
        k += parts
    return passes[0] if len(passes) == 1 else jnp.concatenate(passes, axis=0)


def _mixer(x, p):
    S = x.shape[1]
    zg, hg = _inproj(x, p["g1"], p["win"], p["cdft"], _tile(S, 512))
    yf = _seq_dft(zg)
    yc = _conv_module(hg, p["conv_w"], p["conv_b"], p["ln_g"], p["ln_b"], p["gn_c"], _tile(S, 512))
    return yf, yc


def _route_unit(x, mixed, s0, rows, p):
    T = x.shape[0] * rows
    x1, htw, idx, gate = _route(x, mixed[0], mixed[1], p["gn_f"], p["wout"], p["g2"], p["wq"], p["ka"],
                                p["kb"], _tile(rows, 256), s0, rows)
    return x1.reshape(T, D_MODEL), htw.reshape(T, D_WORDS), idx.reshape(T, N_SEL), gate.reshape(T, N_SEL)


def _encode(units, early_x, p):
    x1s, idxs, acts, peers = [], [], [], []
    early = [_mixer(early_x, p)]
    mixed, mixed_x = None, None
    for g, (x, s0, rows) in enumerate(units):
        if x is early_x and mixed_x is not x:
            mixed, mixed_x = early[0], x
        reuse = mixed_x is x
        carrier = mixed if reuse else x
        slots = [(acts, g - SKEW)] if g >= SKEW else []
        if g >= SKEW + 1:
            slots.append((peers, g - SKEW - 1))
        if g == EARLY_MIX_SLOT:
            slots.append((early, 0))
        if slots:
            carrier, *tied = lax.optimization_barrier((carrier, *[lst[i] for lst, i in slots]))
            for (lst, i), v in zip(slots, tied):
                lst[i] = v
        if g >= SKEW:
            peers.append(_peer_up(idxs[g - SKEW], acts[g - SKEW], p["up"]))
        if reuse:
            mixed = carrier
        else:
            x = carrier
            mixed, mixed_x = _mixer(x, p), units[g][0]
        assert (x.shape[0] * rows) % (SC_WORKERS * SC_TOKB) == 0
        x1, htw, idx, gate = _route_unit(x, mixed, s0, rows, p)
        s = _peer_down(idx, htw, p["down"])
        x1s.append(x1)
        idxs.append(idx)
        acts.append(_act(s, gate, _tile(x1.shape[0], 1024)))
    for g in range(len(peers), len(units)):
        peers.append(_peer_up(idxs[g], acts[g], p["up"]))
    return [_final(x1, peer, p["final_g"], _tile(x1.shape[0], 512)).reshape(x.shape[0], rows, D_MODEL)
            for (x, _, rows), x1, peer in zip(units, x1s, peers)]


def kernel(x_prompt, x_sample, norm1_g, w_in, conv_w, conv_b, conv_ln_g, conv_ln_b, gn_fourier_g,
           gn_conv_g, w_out, norm2_g, w_q, keys_a, keys_b, expert_down, expert_up, final_g):
    assert norm1_g.shape[0] == 1
    p = dict(
        g1=norm1_g[0].reshape(1, D_MODEL), win=w_in[0].astype(BF16), cdft=_channel_dft(),
        conv_w=conv_w[0], conv_b=conv_b[0].reshape(1, D_CONV),
        ln_g=conv_ln_g[0].reshape(1, D_CONV), ln_b=conv_ln_b[0].reshape(1, D_CONV),
        gn_f=gn_fourier_g[0].reshape(1, D_FOURIER), gn_c=gn_conv_g[0].reshape(1, D_CONV),
        wout=w_out[0].astype(BF16), g2=norm2_g[0].reshape(1, D_MODEL), wq=w_q[0].astype(BF16),
        ka=keys_a[0].astype(BF16), kb=keys_b[0].astype(BF16),
        down=_pack_halves(expert_down[0]), up=_pack_halves(expert_up[0]),
        final_g=final_g.reshape(1, D_MODEL),
    )
    prompt_units, prompt_parts = _plan_units(x_prompt, PROMPT_PLAN)
    sample_units, sample_parts = _plan_units(x_sample, SAMPLE_PLAN)
    outs = _encode(prompt_units + sample_units, x_sample, p)
    n_prompt = len(prompt_units)
    return _assemble(outs[:n_prompt], prompt_parts), _assemble(outs[n_prompt:], sample_parts)
```

```python
import functools
import math

import numpy as np
import jax
import jax.numpy as jnp
from jax import lax
from jax.experimental import pallas as pl
from jax.experimental.pallas import tpu as pltpu
from jax.experimental.pallas import tpu_sc as plsc

F32 = jnp.float32
BF16 = jnp.bfloat16

D_MODEL = 1024
D_FOURIER = 512
GROUPS = 4
GDIM = 128
D_CONV = 512
CONV_W = 31
CONV_PAD = 15
HEADS = 8
N_KEYS = 128
TOPK = 16
D_HALF = 128
D_QUERY = 256
N_SEL = HEADS * TOPK
EPS = 1e-6

SUBLANES = 8
LANES = 128
VMEM_LIMIT = 56 * 1024 * 1024

SC_CORES = 2
SC_SUBCORES = 16
SC_LANES = 16
SC_WORKERS = SC_CORES * SC_SUBCORES


def _cparams(sem):
    return pltpu.CompilerParams(dimension_semantics=sem, vmem_limit_bytes=VMEM_LIMIT)


def _rms(x, g):
    return x * lax.rsqrt(jnp.mean(x * x, axis=-1, keepdims=True) + EPS) * g


D_WORDS = D_MODEL // 2
HI_MASK = -65536


def _bf16_bits(x):
    return lax.bitcast_convert_type(x.astype(BF16).astype(F32), jnp.int32)


def _pack_halves(x):
    lo = lax.shift_right_logical(_bf16_bits(x[..., :D_WORDS]), 16)
    hi = lax.bitwise_and(_bf16_bits(x[..., D_WORDS:]), HI_MASK)
    return lax.bitwise_or(lo, hi)


def _inproj_kernel(x_ref, g1_ref, win_ref, cdft_ref, zg_ref, hg_ref):
    h = _rms(x_ref[...], g1_ref[...]).astype(BF16)
    z = jnp.dot(h, win_ref[...], preferred_element_type=F32)
    for g in range(GROUPS):
        zg = z[:, g * GDIM:(g + 1) * GDIM].astype(BF16)
        zg_ref[g] = jnp.dot(zg, cdft_ref[...], preferred_element_type=F32)
    a = z[:, D_FOURIER:D_FOURIER + D_CONV]
    gate = z[:, D_FOURIER + D_CONV:]
    hg_ref[...] = a * jax.nn.sigmoid(gate)


def _inproj(x, g1, win_bf, cdft, tm):
    B, S, _ = x.shape
    return pl.pallas_call(
        _inproj_kernel,
        grid=(B, S // tm),
        in_specs=[
            pl.BlockSpec((None, tm, D_MODEL), lambda b, m: (b, m, 0)),
            pl.BlockSpec((1, D_MODEL), lambda b, m: (0, 0)),
            pl.BlockSpec((D_MODEL, D_FOURIER + 2 * D_CONV), lambda b, m: (0, 0)),
            pl.BlockSpec((GDIM, 2 * GDIM), lambda b, m: (0, 0)),
        ],
        out_specs=[
            pl.BlockSpec((None, GROUPS, tm, 2 * GDIM), lambda b, m: (b, 0, m, 0)),
            pl.BlockSpec((None, tm, D_CONV), lambda b, m: (b, m, 0)),
        ],
        out_shape=[
            jax.ShapeDtypeStruct((B, GROUPS, S, 2 * GDIM), F32),
            jax.ShapeDtypeStruct((B, S, D_CONV), F32),
        ],
        compiler_params=_cparams(("parallel", "parallel")),
        name="inproj",
    )(x, g1, win_bf, cdft)


def _fft1_kernel(z_ref, k1_ref, tc_ref, ts_ref, a_ref):
    n1 = z_ref.shape[0]
    rows = n1 * SUBLANES
    x = z_ref[...].reshape(rows, 2 * GDIM).astype(BF16)
    p = jnp.dot(k1_ref[...], x, preferred_element_type=F32)
    pc, ps = p[:rows], p[rows:]
    ar = pc[:, :GDIM] + ps[:, GDIM:]
    ai = pc[:, GDIM:] - ps[:, :GDIM]
    tc, ts = tc_ref[...], ts_ref[...]
    a_ref[:, :, :GDIM] = (ar * tc + ai * ts).reshape(n1, SUBLANES, GDIM)
    a_ref[:, :, GDIM:] = (ai * tc - ar * ts).reshape(n1, SUBLANES, GDIM)


def _fft2_kernel(a_ref, c2_ref, s2_ref, y_ref):
    for r in range(SUBLANES):
        blk = a_ref[r].astype(BF16)
        y = jnp.dot(c2_ref[...], blk[:, :GDIM], preferred_element_type=F32)
        y = y + jnp.dot(s2_ref[...], blk[:, GDIM:], preferred_element_type=F32)
        y_ref[:, r * GDIM:(r + 1) * GDIM] = y


def _split_seq(S):
    n1 = 1 << (int(math.log2(S)) // 2)
    return n1, S // n1


@functools.lru_cache(maxsize=None)
def _fft_tables(S):
    n1, n2 = _split_seq(S)
    k1 = np.arange(n1)
    ang1 = 2.0 * np.pi * np.outer(k1, np.arange(n1)) / n1
    eye = np.eye(SUBLANES)
    kmat = np.concatenate([np.kron(np.cos(ang1), eye), np.kron(np.sin(ang1), eye)], axis=0)
    nn2 = np.arange(n2).reshape(n2 // SUBLANES, 1, SUBLANES)
    angt = 2.0 * np.pi * k1.reshape(1, n1, 1) * nn2 / S
    angt = angt.reshape(n2 // SUBLANES, n1 * SUBLANES, 1)
    tc = np.broadcast_to(np.cos(angt), (n2 // SUBLANES, n1 * SUBLANES, GDIM))
    ts = np.broadcast_to(np.sin(angt), (n2 // SUBLANES, n1 * SUBLANES, GDIM))
    ang2 = 2.0 * np.pi * np.outer(np.arange(n2), np.arange(n2)) / n2
    scale = 1.0 / math.sqrt(S)
    return (kmat.astype(np.float32), np.ascontiguousarray(tc, np.float32),
            np.ascontiguousarray(ts, np.float32),
            (np.cos(ang2) * scale).astype(np.float32), (np.sin(ang2) * scale).astype(np.float32))


def _seq_dft(zg):
    B, G, S, _ = zg.shape
    n1, n2 = _split_seq(S)
    kmat, tc, ts, c2, s2 = _fft_tables(S)
    kmat = jnp.asarray(kmat, BF16)
    c2 = jnp.asarray(c2, BF16)
    s2 = jnp.asarray(s2, BF16)
    nb = n2 // SUBLANES
    z6 = zg.reshape(B, G, n1, nb, SUBLANES, 2 * GDIM)
    blk6 = (None, None, n1, None, SUBLANES, 2 * GDIM)
    a6 = pl.pallas_call(
        _fft1_kernel,
        grid=(nb, B, G),
        in_specs=[
            pl.BlockSpec(blk6, lambda j, b, g: (b, g, 0, j, 0, 0)),
            pl.BlockSpec((2 * n1 * SUBLANES, n1 * SUBLANES), lambda j, b, g: (0, 0)),
            pl.BlockSpec((None, n1 * SUBLANES, GDIM), lambda j, b, g: (j, 0, 0)),
            pl.BlockSpec((None, n1 * SUBLANES, GDIM), lambda j, b, g: (j, 0, 0)),
        ],
        out_specs=pl.BlockSpec(blk6, lambda j, b, g: (b, g, 0, j, 0, 0)),
        out_shape=jax.ShapeDtypeStruct(z6.shape, F32),
        compiler_params=_cparams(("parallel", "parallel", "parallel")),
        name="fft_stage1",
    )(z6, kmat, jnp.asarray(tc), jnp.asarray(ts))
    a5 = a6.reshape(B, G, n1, n2, 2 * GDIM)
    y = pl.pallas_call(
        _fft2_kernel,
        grid=(B, G, n1 // SUBLANES),
        in_specs=[
            pl.BlockSpec((None, None, SUBLANES, n2, 2 * GDIM), lambda b, g, i: (b, g, i, 0, 0)),
            pl.BlockSpec((n2, n2), lambda b, g, i: (0, 0)),
            pl.BlockSpec((n2, n2), lambda b, g, i: (0, 0)),
        ],
        out_specs=pl.BlockSpec((None, None, n2, SUBLANES * GDIM), lambda b, g, i: (b, g, 0, i)),
        out_shape=jax.ShapeDtypeStruct((B, G, n2, n1 * GDIM), F32),
        compiler_params=_cparams(("parallel", "parallel", "parallel")),
        name="fft_stage2",
    )(a5, c2, s2)
    return y.reshape(B, G, S, GDIM)


CONV_HALO = 16
CONV_ROWS = 16


def _conv_kernel(prev_ref, cur_ref, next_ref, w_ref, cb_ref, lg_ref, lb_ref, gc_ref, o_ref, scr):
    ts = cur_ref.shape[0]
    m = pl.program_id(1)
    last = pl.num_programs(1) - 1
    scr[0:CONV_HALO, :] = jnp.where(m > 0, prev_ref[...], 0.0)
    scr[CONV_HALO:CONV_HALO + ts, :] = cur_ref[...]
    scr[CONV_HALO + ts:, :] = jnp.where(m < last, next_ref[...], 0.0)
    off = CONV_HALO - CONV_PAD

    def step(c, carry):
        base = pl.multiple_of(c * CONV_ROWS, CONV_ROWS)
        win = scr[pl.ds(base, CONV_ROWS + 2 * CONV_HALO), :]
        acc = jnp.broadcast_to(cb_ref[...], (CONV_ROWS, D_CONV))
        for k in range(CONV_W):
            acc = acc + win[off + k:off + k + CONV_ROWS, :] * w_ref[k:k + 1, :]
        mu = jnp.mean(acc, axis=-1, keepdims=True)
        d = acc - mu
        var = jnp.mean(d * d, axis=-1, keepdims=True)
        y = d * lax.rsqrt(var + EPS) * lg_ref[...] + lb_ref[...]
        y = y * jax.nn.sigmoid(y)
        o_ref[pl.ds(base, CONV_ROWS), :] = _rms(y, gc_ref[...])
        return carry

    lax.fori_loop(0, ts // CONV_ROWS, step, 0)


def _conv_module(hg, conv_w, conv_b, ln_g, ln_b, gn_c, ts):
    B, S, _ = hg.shape
    hb = ts // CONV_HALO
    nh = S // CONV_HALO
    vec = pl.BlockSpec((1, D_CONV), lambda b, m: (0, 0))
    return pl.pallas_call(
        _conv_kernel,
        grid=(B, S // ts),
        in_specs=[
            pl.BlockSpec((None, CONV_HALO, D_CONV), lambda b, m: (b, jnp.maximum(m * hb - 1, 0), 0)),
            pl.BlockSpec((None, ts, D_CONV), lambda b, m: (b, m, 0)),
            pl.BlockSpec((None, CONV_HALO, D_CONV), lambda b, m: (b, jnp.minimum((m + 1) * hb, nh - 1), 0)),
            pl.BlockSpec((CONV_W, D_CONV), lambda b, m: (0, 0)),
            vec, vec, vec, vec,
        ],
        out_specs=pl.BlockSpec((None, ts, D_CONV), lambda b, m: (b, m, 0)),
        out_shape=jax.ShapeDtypeStruct((B, S, D_CONV), F32),
        scratch_shapes=[pltpu.VMEM((ts + 2 * CONV_HALO, D_CONV), F32)],
        compiler_params=_cparams(("parallel", "parallel")),
        name="conv_module",
    )(hg, hg, hg, conv_w, conv_b, ln_g, ln_b, gn_c)


def _extract_topk(s, order, k, payload=None):
    big = jnp.int32(1 << 30)
    vals, sel = [], []
    for _ in range(k):
        m = jnp.max(s, axis=0, keepdims=True)
        am = jnp.min(jnp.where(s == m, order, big), axis=0, keepdims=True)
        hit = order == am
        vals.append(m)
        if payload is None:
            sel.append(am)
        else:
            sel.append(jnp.max(jnp.where(hit, payload, -1), axis=0, keepdims=True))
        s = jnp.where(hit, -jnp.inf, s)
    return jnp.concatenate(vals, axis=0), jnp.concatenate(sel, axis=0)


_CAND_A = 4
_CAND_B = 3


def _route_kernel(x_ref, yf_ref, yc_ref, gf_ref, wout_ref, g2_ref, wq_ref, ka_ref, kb_ref,
                  x1_ref, htw_ref, idx_ref, gate_ref):
    tm = x_ref.shape[0]
    yf = jnp.concatenate([yf_ref[g] for g in range(GROUPS)], axis=-1)
    yfn = _rms(yf, gf_ref[...])
    ycat = jnp.concatenate([yfn, yc_ref[...]], axis=-1).astype(BF16)
    x1 = x_ref[...] + jnp.dot(ycat, wout_ref[...], preferred_element_type=F32)
    x1_ref[...] = x1
    ht = _rms(x1, g2_ref[...])
    htw_ref[...] = _pack_halves(ht)
    q = jnp.dot(ht.astype(BF16), wq_ref[...], preferred_element_type=F32)

    row16 = lax.broadcasted_iota(jnp.int32, (TOPK, tm), 0)
    key_iota = lax.broadcasted_iota(jnp.int32, (N_KEYS, tm), 0)
    nt = (((1,), (1,)), ((), ()))
    idx_rows, gate_rows = [], []
    for h in range(HEADS):
        qa = q[:, h * D_QUERY:h * D_QUERY + D_HALF].astype(BF16)
        qb = q[:, h * D_QUERY + D_HALF:(h + 1) * D_QUERY].astype(BF16)
        sa = lax.dot_general(ka_ref[h], qa, nt, preferred_element_type=F32)
        sb = lax.dot_general(kb_ref[h], qb, nt, preferred_element_type=F32)
        va, ia = _extract_topk(sa, key_iota, TOPK)
        vb, ib = _extract_topk(sb, key_iota, TOPK)
        cs, ce, co = [], [], []
        for i in range(_CAND_A):
            ok = (i + 1) * (row16 + 1) <= TOPK
            cs.append(jnp.where(ok, va[i:i + 1] + vb, -jnp.inf))
            ce.append(ia[i:i + 1] * N_KEYS + ib)
            co.append(i * TOPK + row16)
        for j in range(_CAND_B):
            ok = jnp.logical_and(row16 >= _CAND_A, (row16 + 1) * (j + 1) <= TOPK)
            cs.append(jnp.where(ok, va + vb[j:j + 1], -jnp.inf))
            ce.append(ia * N_KEYS + ib[j:j + 1])
            co.append(row16 * TOPK + j)
        top_s, top_e = _extract_topk(jnp.concatenate(cs, axis=0), jnp.concatenate(co, axis=0),
                                     TOPK, payload=jnp.concatenate(ce, axis=0))
        ex = jnp.exp(top_s - top_s[0:1])
        gate_rows.append(ex / jnp.sum(ex, axis=0, keepdims=True))
        idx_rows.append(top_e)
    idx_ref[...] = jnp.concatenate(idx_rows, axis=0).T
    gate_ref[...] = jnp.concatenate(gate_rows, axis=0).T


def _route(x, yf, yc, gf, wout_bf, g2, wq_bf, ka_bf, kb_bf, tm, s0, rows):
    B = x.shape[0]
    S = rows
    m0 = s0 // tm
    assert s0 % tm == 0 and rows % tm == 0
    const2 = lambda b, m: (0, 0)
    const3 = lambda b, m: (0, 0, 0)
    row = lambda w: pl.BlockSpec((None, tm, w), lambda b, m: (b, m, 0))
    row_in = lambda w: pl.BlockSpec((None, tm, w), lambda b, m: (b, m + m0, 0))
    return pl.pallas_call(
        _route_kernel,
        grid=(B, S // tm),
        in_specs=[
            row_in(D_MODEL),
            pl.BlockSpec((None, GROUPS, tm, GDIM), lambda b, m: (b, 0, m + m0, 0)),
            row_in(D_CONV),
            pl.BlockSpec((1, D_FOURIER), const2),
            pl.BlockSpec((D_MODEL, D_MODEL), const2),
            pl.BlockSpec((1, D_MODEL), const2),
            pl.BlockSpec((D_MODEL, HEADS * D_QUERY), const2),
            pl.BlockSpec((HEADS, N_KEYS, D_HALF), const3),
            pl.BlockSpec((HEADS, N_KEYS, D_HALF), const3),
        ],
        out_specs=[row(D_MODEL), row(D_WORDS), row(N_SEL), row(N_SEL)],
        out_shape=[
            jax.ShapeDtypeStruct((B, S, D_MODEL), F32),
            jax.ShapeDtypeStruct((B, S, D_WORDS), jnp.int32),
            jax.ShapeDtypeStruct((B, S, N_SEL), jnp.int32),
            jax.ShapeDtypeStruct((B, S, N_SEL), F32),
        ],
        compiler_params=_cparams(("parallel", "parallel")),
        name="route",
    )(x, yf, yc, gf, wout_bf, g2, wq_bf, ka_bf, kb_bf)


SC_UROWS = 64
SC_UNITS = N_SEL // SC_UROWS
SC_NBUF = 3
SC_TOKB = 16
SC_WCH = D_WORDS // SC_LANES
SC_QUAD = 4
SC_KG = 8
SC_COST_WEIGHT = 16


def _sc_mesh():
    return plsc.VectorSubcoreMesh(core_axis_name="c", subcore_axis_name="s",
                                  num_cores=SC_CORES, num_subcores=SC_SUBCORES)


def _sc_cost(tokens):
    rows = tokens * N_SEL * SC_COST_WEIGHT
    return pl.CostEstimate(flops=2 * rows * D_MODEL, transcendentals=0, bytes_accessed=rows * D_WORDS * 4)


def _sc_bf16(words):
    return plsc.bitcast(words, BF16)


def _sc_widen(pairs):
    w = plsc.bitcast(pairs, jnp.int32)
    lo = plsc.bitcast(lax.shift_left(w, 16), F32)
    hi = plsc.bitcast(lax.bitwise_and(w, jnp.int32(HI_MASK)), F32)
    return lo, hi


def _sc_quad_sum(row_chunk, scale):
    part = None
    for u in range(SC_QUAD):
        prod = _sc_bf16(row_chunk(u)) * scale(u)
        part = prod if part is None else part + prod
    return part


def _tree_sum(xs):
    while len(xs) > 1:
        xs = [xs[i] + xs[i + 1] for i in range(0, len(xs), 2)]
    return xs[0]


def _sc_token_loop(table_hbm, idx_hbm, side_hbm, out_hbm, idx_v, side_v, rows_v, out_v, gsem, osem, bsem,
                   tokens_per_worker, compute):
    wid = lax.axis_index("s") * SC_CORES + lax.axis_index("c")
    base = wid * tokens_per_worker
    n_batches = tokens_per_worker // SC_TOKB

    def batch_copies(bi, bslot):
        src = pl.ds(base + bi * SC_TOKB, SC_TOKB)
        return (pltpu.make_async_copy(idx_hbm.at[src], idx_v.at[bslot], bsem.at[2 * bslot]),
                pltpu.make_async_copy(side_hbm.at[src], side_v.at[bslot], bsem.at[2 * bslot + 1]))

    def gather(t, unit, buf):
        rows = idx_v.at[lax.rem(t // SC_TOKB, 2), lax.rem(t, SC_TOKB), pl.ds(unit * SC_UROWS, SC_UROWS)]
        return pltpu.make_async_copy(table_hbm.at[rows], rows_v.at[buf], gsem.at[buf])

    def writeback(tok, slot):
        return pltpu.make_async_copy(out_v.at[slot], out_hbm.at[tok], osem.at[slot])

    for c in batch_copies(0, 0):
        c.start()
    for c in batch_copies(0, 0):
        c.wait()
    if n_batches > 1:
        for c in batch_copies(1, 1):
            c.start()
    for unit in range(SC_UNITS):
        gather(0, unit, unit).start()

    @pl.loop(0, tokens_per_worker)
    def _(t):
        slot = lax.rem(t, 2)
        bi = t // SC_TOKB
        bslot = lax.rem(bi, 2)
        tl = lax.rem(t, SC_TOKB)
        nxt = t + 1

        @pl.when(t >= 2)
        def _():
            writeback(base + t - 2, slot).wait()

        for unit in range(SC_UNITS):
            u = t * SC_UNITS + unit
            buf = lax.rem(u, SC_NBUF)
            gather(t, unit, buf).wait()

            if unit == 0:
                @pl.when(jnp.logical_and(nxt < tokens_per_worker, lax.rem(nxt, SC_TOKB) == 0))
                def _():
                    for c in batch_copies(nxt // SC_TOKB, lax.rem(nxt // SC_TOKB, 2)):
                        c.wait()

            @pl.when(nxt < tokens_per_worker)
            def _():
                gather(nxt, unit, lax.rem(u + SC_UNITS, SC_NBUF)).start()

            compute(bslot, tl, unit, buf, slot)
        writeback(base + t, slot).start()

        @pl.when(jnp.logical_and(tl == SC_TOKB - 1, bi + 2 < n_batches))
        def _():
            for c in batch_copies(bi + 2, bslot):
                c.start()

    for t in range(tokens_per_worker - 2, tokens_per_worker):
        writeback(base + t, t % 2).wait()


def _peer_down(idx, htw, down_w):
    T = htw.shape[0]
    tpw = T // SC_WORKERS
    L = SC_LANES

    @functools.partial(
        pl.kernel, mesh=_sc_mesh(),
        out_type=jax.ShapeDtypeStruct((T, L * N_SEL), F32),
        scratch_types=[
            pltpu.VMEM((2, SC_TOKB, N_SEL), jnp.int32),
            pltpu.VMEM((2, SC_TOKB, D_WORDS), jnp.int32),
            pltpu.VMEM((SC_NBUF, SC_UROWS, D_WORDS), jnp.int32),
            pltpu.VMEM((2, L * N_SEL), F32),
            pltpu.SemaphoreType.DMA((SC_NBUF,)),
            pltpu.SemaphoreType.DMA((2,)),
            pltpu.SemaphoreType.DMA((4,)),
        ],
        compiler_params=pltpu.CompilerParams(needs_layout_passes=False),
        name="peer_down", cost_estimate=_sc_cost(T),
    )
    def k(idx_hbm, h_hbm, down_hbm, s_hbm, idx_v, h_v, rows_v, s_v, gsem, osem, bsem):
        lane = lax.iota(jnp.int32, L)

        def compute(bslot, tl, unit, buf, slot):
            svec = jnp.full((L,), slot, jnp.int32)
            hs = [_sc_bf16(h_v[bslot, tl, pl.ds(w * L, L)]) for w in range(SC_WCH)]

            @plsc.parallel_loop(0, SC_UROWS, unroll=2)
            def _(r):
                parts = []
                for q in range(SC_WCH // SC_QUAD):
                    part = _sc_quad_sum(lambda u: rows_v[buf, r, pl.ds((q * SC_QUAD + u) * L, L)],
                                        lambda u: hs[q * SC_QUAD + u])
                    parts.extend(_sc_widen(part))
                col = lax.bitwise_and(lane + (unit * SC_UROWS + r), N_SEL - 1)
                plsc.store_scatter(s_v, [svec, lane * N_SEL + col], _tree_sum(parts))

        _sc_token_loop(down_hbm, idx_hbm, h_hbm, s_hbm, idx_v, h_v, rows_v, s_v, gsem, osem, bsem, tpw, compute)

    return k(idx, htw, down_w)


def _peer_up(idx, act_w, up_w):
    T = act_w.shape[0]
    tpw = T // SC_WORKERS
    L = SC_LANES

    @functools.partial(
        pl.kernel, mesh=_sc_mesh(),
        out_type=jax.ShapeDtypeStruct((T, D_MODEL), F32),
        scratch_types=[
            pltpu.VMEM((2, SC_TOKB, N_SEL), jnp.int32),
            pltpu.VMEM((2, SC_TOKB, N_SEL), jnp.int32),
            pltpu.VMEM((SC_NBUF, SC_UROWS, D_WORDS), jnp.int32),
            pltpu.VMEM((2, D_MODEL), F32),
            pltpu.SemaphoreType.DMA((SC_NBUF,)),
            pltpu.SemaphoreType.DMA((2,)),
            pltpu.SemaphoreType.DMA((4,)),
        ],
        compiler_params=pltpu.CompilerParams(needs_layout_passes=False),
        name="peer_up", cost_estimate=_sc_cost(T),
    )
    def k(idx_hbm, act_hbm, up_hbm, o_hbm, idx_v, act_v, rows_v, o_v, gsem, osem, bsem):
        def compute(bslot, tl, unit, buf, slot):
            for kg in range(SC_WCH // SC_KG):
                lo_cols = [pl.ds((kg * SC_KG + kk) * L, L) for kk in range(SC_KG)]
                hi_cols = [pl.ds(D_WORDS + (kg * SC_KG + kk) * L, L) for kk in range(SC_KG)]
                if unit == 0:
                    init = tuple(jnp.zeros((L,), F32) for _ in range(2 * SC_KG))
                else:
                    init = tuple(o_v[slot, col] for col in lo_cols + hi_cols)

                def rgbody(rg, accs):
                    per_vec = L // SC_QUAD
                    a16 = act_v[bslot, tl, pl.ds(unit * SC_UROWS + (rg // per_vec) * L, L)]
                    lane0 = jnp.full((L,), 0, jnp.int32) + lax.rem(rg, per_vec) * SC_QUAD
                    a = [_sc_bf16(a16.at[lane0 + u].get(mode="promise_in_bounds")) for u in range(SC_QUAD)]
                    out_lo, out_hi = [], []
                    for kk in range(SC_KG):
                        part = _sc_quad_sum(lambda u: rows_v[buf, rg * SC_QUAD + u, lo_cols[kk]], lambda u: a[u])
                        lo, hi = _sc_widen(part)
                        out_lo.append(accs[kk] + lo)
                        out_hi.append(accs[SC_KG + kk] + hi)
                    return tuple(out_lo + out_hi)
                accs = lax.fori_loop(0, SC_UROWS // SC_QUAD, rgbody, init)
                for kk in range(SC_KG):
                    o_v[slot, lo_cols[kk]] = accs[kk]
                    o_v[slot, hi_cols[kk]] = accs[SC_KG + kk]

        _sc_token_loop(up_hbm, idx_hbm, act_hbm, o_hbm, idx_v, act_v, rows_v, o_v, gsem, osem, bsem, tpw, compute)

    return k(idx, act_w, up_w)


def _act_kernel(s_ref, gate_ref, a_ref):
    s = _tree_sum([pltpu.roll(s_ref[:, l * N_SEL:(l + 1) * N_SEL], (N_SEL - l) % N_SEL, axis=1)
                   for l in range(SC_LANES)])
    act = 0.5 * s * (1.0 + lax.erf(s * math.sqrt(0.5))) * gate_ref[...]
    bits = _bf16_bits(act)
    a_ref[...] = lax.bitwise_or(lax.shift_right_logical(bits, 16), bits)


def _act(s, gate, tm):
    T = s.shape[0]
    spec = pl.BlockSpec((tm, N_SEL), lambda i: (i, 0))
    part_spec = pl.BlockSpec((tm, SC_LANES * N_SEL), lambda i: (i, 0))
    return pl.pallas_call(
        _act_kernel, grid=(T // tm,), in_specs=[part_spec, spec], out_specs=spec,
        out_shape=jax.ShapeDtypeStruct((T, N_SEL), jnp.int32),
        compiler_params=_cparams(("parallel",)), name="expert_act",
    )(s, gate)


def _final_kernel(x1_ref, p_ref, g_ref, *rest):
    rest[-1][...] = _rms(x1_ref[...] + p_ref[...], g_ref[...])


def _final_into(out, out_shape, x1, peer, g, b0, nb, s0, rows):
    tm = _tile(rows, 512)
    assert s0 % tm == 0
    m0 = s0 // tm
    tok = pl.BlockSpec((None, tm, D_MODEL), lambda b, m: (b, m, 0))
    in_specs = [tok, tok, pl.BlockSpec((1, D_MODEL), lambda b, m: (0, 0))]
    args = [x1.reshape(nb, rows, D_MODEL), peer.reshape(nb, rows, D_MODEL), g]
    aliases = {}
    if out is not None:
        in_specs.append(pl.BlockSpec(memory_space=pl.ANY))
        args.append(out)
        aliases = {3: 0}
    return pl.pallas_call(
        _final_kernel, grid=(nb, rows // tm), in_specs=in_specs,
        out_specs=pl.BlockSpec((None, tm, D_MODEL), lambda b, m: (b + b0, m + m0, 0)),
        out_shape=jax.ShapeDtypeStruct(out_shape, F32), input_output_aliases=aliases,
        compiler_params=_cparams(("parallel", "parallel")), name="final_norm",
    )(*args)


PROMPT_PLAN = ((1, 2), (1, 1), (1, 1), (1, 1), (2, 1), (2, 1))
SAMPLE_PLAN = ((1, 2),)
SKEW = 2
EARLY_MIX_SLOT = 4


def _tile(n, want):
    t = min(n, want)
    assert n % t == 0
    return t


def _channel_dft():
    ang = 2.0 * np.pi * np.outer(np.arange(GDIM), np.arange(GDIM)) / GDIM
    m = np.concatenate([np.cos(ang), -np.sin(ang)], axis=1) / math.sqrt(GDIM)
    return jnp.asarray(m.astype(np.float32), BF16)


def _plan_units(x, plan, out_id):
    units = []
    b = 0
    for nb, parts in plan:
        xb = x[b:b + nb]
        rows = x.shape[1] // parts
        units += [(xb, i * rows, rows, out_id, b) for i in range(parts)]
        b += nb
    assert b == x.shape[0]
    return units


def _mixer(x, p):
    S = x.shape[1]
    zg, hg = _inproj(x, p["g1"], p["win"], p["cdft"], _tile(S, 512))
    yf = _seq_dft(zg)
    yc = _conv_module(hg, p["conv_w"], p["conv_b"], p["ln_g"], p["ln_b"], p["gn_c"], _tile(S, 512))
    return yf, yc


def _route_unit(x, mixed, s0, rows, p):
    T = x.shape[0] * rows
    x1, htw, idx, gate = _route(x, mixed[0], mixed[1], p["gn_f"], p["wout"], p["g2"], p["wq"], p["ka"],
                                p["kb"], _tile(rows, 256), s0, rows)
    return x1.reshape(T, D_MODEL), htw.reshape(T, D_WORDS), idx.reshape(T, N_SEL), gate.reshape(T, N_SEL)


def _encode(units, out_shapes, early_x, p):
    x1s, idxs, acts, peers = [], [], [], []
    early = [_mixer(early_x, p)]
    mixed, mixed_x = None, None
    for g, (x, s0, rows, _, _) in enumerate(units):
        if x is early_x and mixed_x is not x:
            mixed, mixed_x = early[0], x
        reuse = mixed_x is x
        carrier = mixed if reuse else x
        slots = [(acts, g - SKEW)] if g >= SKEW else []
        if g >= SKEW + 1:
            slots.append((peers, g - SKEW - 1))
        if g == EARLY_MIX_SLOT:
            slots.append((early, 0))
        if slots:
            carrier, *tied = lax.optimization_barrier((carrier, *[lst[i] for lst, i in slots]))
            for (lst, i), v in zip(slots, tied):
                lst[i] = v
        if g >= SKEW:
            peers.append(_peer_up(idxs[g - SKEW], acts[g - SKEW], p["up"]))
        if reuse:
            mixed = carrier
        else:
            x = carrier
            mixed, mixed_x = _mixer(x, p), units[g][0]
        assert (x.shape[0] * rows) % (SC_WORKERS * SC_TOKB) == 0
        x1, htw, idx, gate = _route_unit(x, mixed, s0, rows, p)
        s = _peer_down(idx, htw, p["down"])
        x1s.append(x1)
        idxs.append(idx)
        acts.append(_act(s, gate, _tile(x1.shape[0], 1024)))
    for g in range(len(peers), len(units)):
        peers.append(_peer_up(idxs[g], acts[g], p["up"]))
    outs = {}
    for (x, s0, rows, out_id, b0), x1, peer in zip(units, x1s, peers):
        outs[out_id] = _final_into(outs.get(out_id), out_shapes[out_id], x1, peer, p["final_g"],
                                   b0, x.shape[0], s0, rows)
    return outs


def kernel(x_prompt, x_sample, norm1_g, w_in, conv_w, conv_b, conv_ln_g, conv_ln_b, gn_fourier_g,
           gn_conv_g, w_out, norm2_g, w_q, keys_a, keys_b, expert_down, expert_up, final_g):
    assert norm1_g.shape[0] == 1
    p = dict(
        g1=norm1_g[0].reshape(1, D_MODEL), win=w_in[0].astype(BF16), cdft=_channel_dft(),
        conv_w=conv_w[0], conv_b=conv_b[0].reshape(1, D_CONV),
        ln_g=conv_ln_g[0].reshape(1, D_CONV), ln_b=conv_ln_b[0].reshape(1, D_CONV),
        gn_f=gn_fourier_g[0].reshape(1, D_FOURIER), gn_c=gn_conv_g[0].reshape(1, D_CONV),
        wout=w_out[0].astype(BF16), g2=norm2_g[0].reshape(1, D_MODEL), wq=w_q[0].astype(BF16),
        ka=keys_a[0].astype(BF16), kb=keys_b[0].astype(BF16),
        down=_pack_halves(expert_down[0]), up=_pack_halves(expert_up[0]),
        final_g=final_g.reshape(1, D_MODEL),
    )
    units = _plan_units(x_prompt, PROMPT_PLAN, 0) + _plan_units(x_sample, SAMPLE_PLAN, 1)
    outs = _encode(units, (x_prompt.shape, x_sample.shape), x_sample, p)
    return outs[0], outs[1]
```

```python
import functools
import math

import numpy as np
import jax
import jax.numpy as jnp
from jax import lax
from jax.experimental import pallas as pl
from jax.experimental.pallas import tpu as pltpu
from jax.experimental.pallas import tpu_sc as plsc

F32 = jnp.float32
BF16 = jnp.bfloat16

D_MODEL = 1024
D_FOURIER = 512
GROUPS = 4
GDIM = 128
D_CONV = 512
CONV_W = 31
CONV_PAD = 15
HEADS = 8
N_KEYS = 128
TOPK = 16
D_HALF = 128
D_QUERY = 256
N_SEL = HEADS * TOPK
EPS = 1e-6

SUBLANES = 8
LANES = 128
VMEM_LIMIT = 56 * 1024 * 1024

SC_CORES = 2
SC_SUBCORES = 16
SC_LANES = 16
SC_WORKERS = SC_CORES * SC_SUBCORES


def _cparams(sem):
    return pltpu.CompilerParams(dimension_semantics=sem, vmem_limit_bytes=VMEM_LIMIT)


def _rms(x, g):
    return x * lax.rsqrt(jnp.mean(x * x, axis=-1, keepdims=True) + EPS) * g


D_WORDS = D_MODEL // 2
HI_MASK = -65536


def _bf16_bits(x):
    return lax.bitcast_convert_type(x.astype(BF16).astype(F32), jnp.int32)


def _pack_halves(x):
    lo = lax.shift_right_logical(_bf16_bits(x[..., :D_WORDS]), 16)
    hi = lax.bitwise_and(_bf16_bits(x[..., D_WORDS:]), HI_MASK)
    return lax.bitwise_or(lo, hi)


def _inproj_kernel(x_ref, g1_ref, win_ref, cdft_ref, zg_ref, hg_ref):
    h = _rms(x_ref[...], g1_ref[...]).astype(BF16)
    z = jnp.dot(h, win_ref[...], preferred_element_type=F32)
    for g in range(GROUPS):
        zg = z[:, g * GDIM:(g + 1) * GDIM].astype(BF16)
        zg_ref[g] = jnp.dot(zg, cdft_ref[...], preferred_element_type=F32)
    a = z[:, D_FOURIER:D_FOURIER + D_CONV]
    gate = z[:, D_FOURIER + D_CONV:]
    hg_ref[...] = a * jax.nn.sigmoid(gate)


def _inproj(x, g1, win_bf, cdft, tm):
    B, S, _ = x.shape
    return pl.pallas_call(
        _inproj_kernel,
        grid=(B, S // tm),
        in_specs=[
            pl.BlockSpec((None, tm, D_MODEL), lambda b, m: (b, m, 0)),
            pl.BlockSpec((1, D_MODEL), lambda b, m: (0, 0)),
            pl.BlockSpec((D_MODEL, D_FOURIER + 2 * D_CONV), lambda b, m: (0, 0)),
            pl.BlockSpec((GDIM, 2 * GDIM), lambda b, m: (0, 0)),
        ],
        out_specs=[
            pl.BlockSpec((None, GROUPS, tm, 2 * GDIM), lambda b, m: (b, 0, m, 0)),
            pl.BlockSpec((None, tm, D_CONV), lambda b, m: (b, m, 0)),
        ],
        out_shape=[
            jax.ShapeDtypeStruct((B, GROUPS, S, 2 * GDIM), F32),
            jax.ShapeDtypeStruct((B, S, D_CONV), F32),
        ],
        compiler_params=_cparams(("parallel", "parallel")),
        name="inproj",
    )(x, g1, win_bf, cdft)


def _fft1_kernel(z_ref, k1_ref, tc_ref, ts_ref, a_ref):
    n1 = z_ref.shape[0]
    rows = n1 * SUBLANES
    x = z_ref[...].reshape(rows, 2 * GDIM).astype(BF16)
    p = jnp.dot(k1_ref[...], x, preferred_element_type=F32)
    pc, ps = p[:rows], p[rows:]
    ar = pc[:, :GDIM] + ps[:, GDIM:]
    ai = pc[:, GDIM:] - ps[:, :GDIM]
    tc, ts = tc_ref[...], ts_ref[...]
    a_ref[:, :, :GDIM] = (ar * tc + ai * ts).reshape(n1, SUBLANES, GDIM)
    a_ref[:, :, GDIM:] = (ai * tc - ar * ts).reshape(n1, SUBLANES, GDIM)


def _fft2_kernel(a_ref, c2_ref, s2_ref, y_ref):
    for r in range(SUBLANES):
        blk = a_ref[r].astype(BF16)
        y = jnp.dot(c2_ref[...], blk[:, :GDIM], preferred_element_type=F32)
        y = y + jnp.dot(s2_ref[...], blk[:, GDIM:], preferred_element_type=F32)
        y_ref[:, r * GDIM:(r + 1) * GDIM] = y


def _split_seq(S):
    n1 = 1 << (int(math.log2(S)) // 2)
    return n1, S // n1


@functools.lru_cache(maxsize=None)
def _fft_tables(S):
    n1, n2 = _split_seq(S)
    k1 = np.arange(n1)
    ang1 = 2.0 * np.pi * np.outer(k1, np.arange(n1)) / n1
    eye = np.eye(SUBLANES)
    kmat = np.concatenate([np.kron(np.cos(ang1), eye), np.kron(np.sin(ang1), eye)], axis=0)
    nn2 = np.arange(n2).reshape(n2 // SUBLANES, 1, SUBLANES)
    angt = 2.0 * np.pi * k1.reshape(1, n1, 1) * nn2 / S
    angt = angt.reshape(n2 // SUBLANES, n1 * SUBLANES, 1)
    tc = np.broadcast_to(np.cos(angt), (n2 // SUBLANES, n1 * SUBLANES, GDIM))
    ts = np.broadcast_to(np.sin(angt), (n2 // SUBLANES, n1 * SUBLANES, GDIM))
    ang2 = 2.0 * np.pi * np.outer(np.arange(n2), np.arange(n2)) / n2
    scale = 1.0 / math.sqrt(S)
    return (kmat.astype(np.float32), np.ascontiguousarray(tc, np.float32),
            np.ascontiguousarray(ts, np.float32),
            (np.cos(ang2) * scale).astype(np.float32), (np.sin(ang2) * scale).astype(np.float32))


def _seq_dft(zg):
    B, G, S, _ = zg.shape
    n1, n2 = _split_seq(S)
    kmat, tc, ts, c2, s2 = _fft_tables(S)
    kmat = jnp.asarray(kmat, BF16)
    c2 = jnp.asarray(c2, BF16)
    s2 = jnp.asarray(s2, BF16)
    nb = n2 // SUBLANES
    z6 = zg.reshape(B, G, n1, nb, SUBLANES, 2 * GDIM)
    blk6 = (None, None, n1, None, SUBLANES, 2 * GDIM)
    a6 = pl.pallas_call(
        _fft1_kernel,
        grid=(nb, B, G),
        in_specs=[
            pl.BlockSpec(blk6, lambda j, b, g: (b, g, 0, j, 0, 0)),
            pl.BlockSpec((2 * n1 * SUBLANES, n1 * SUBLANES), lambda j, b, g: (0, 0)),
            pl.BlockSpec((None, n1 * SUBLANES, GDIM), lambda j, b, g: (j, 0, 0)),
            pl.BlockSpec((None, n1 * SUBLANES, GDIM), lambda j, b, g: (j, 0, 0)),
        ],
        out_specs=pl.BlockSpec(blk6, lambda j, b, g: (b, g, 0, j, 0, 0)),
        out_shape=jax.ShapeDtypeStruct(z6.shape, F32),
        compiler_params=_cparams(("parallel", "parallel", "parallel")),
        name="fft_stage1",
    )(z6, kmat, jnp.asarray(tc), jnp.asarray(ts))
    a5 = a6.reshape(B, G, n1, n2, 2 * GDIM)
    y = pl.pallas_call(
        _fft2_kernel,
        grid=(B, G, n1 // SUBLANES),
        in_specs=[
            pl.BlockSpec((None, None, SUBLANES, n2, 2 * GDIM), lambda b, g, i: (b, g, i, 0, 0)),
            pl.BlockSpec((n2, n2), lambda b, g, i: (0, 0)),
            pl.BlockSpec((n2, n2), lambda b, g, i: (0, 0)),
        ],
        out_specs=pl.BlockSpec((None, None, n2, SUBLANES * GDIM), lambda b, g, i: (b, g, 0, i)),
        out_shape=jax.ShapeDtypeStruct((B, G, n2, n1 * GDIM), F32),
        compiler_params=_cparams(("parallel", "parallel", "parallel")),
        name="fft_stage2",
    )(a5, c2, s2)
    return y.reshape(B, G, S, GDIM)


CONV_HALO = 16
CONV_ROWS = 16


def _conv_kernel(prev_ref, cur_ref, next_ref, w_ref, cb_ref, lg_ref, lb_ref, gc_ref, o_ref, scr):
    ts = cur_ref.shape[0]
    m = pl.program_id(1)
    last = pl.num_programs(1) - 1
    scr[0:CONV_HALO, :] = jnp.where(m > 0, prev_ref[...], 0.0)
    scr[CONV_HALO:CONV_HALO + ts, :] = cur_ref[...]
    scr[CONV_HALO + ts:, :] = jnp.where(m < last, next_ref[...], 0.0)
    off = CONV_HALO - CONV_PAD

    def step(c, carry):
        base = pl.multiple_of(c * CONV_ROWS, CONV_ROWS)
        win = scr[pl.ds(base, CONV_ROWS + 2 * CONV_HALO), :]
        acc = jnp.broadcast_to(cb_ref[...], (CONV_ROWS, D_CONV))
        for k in range(CONV_W):
            acc = acc + win[off + k:off + k + CONV_ROWS, :] * w_ref[k:k + 1, :]
        mu = jnp.mean(acc, axis=-1, keepdims=True)
        d = acc - mu
        var = jnp.mean(d * d, axis=-1, keepdims=True)
        y = d * lax.rsqrt(var + EPS) * lg_ref[...] + lb_ref[...]
        y = y * jax.nn.sigmoid(y)
        o_ref[pl.ds(base, CONV_ROWS), :] = _rms(y, gc_ref[...])
        return carry

    lax.fori_loop(0, ts // CONV_ROWS, step, 0)


def _conv_module(hg, conv_w, conv_b, ln_g, ln_b, gn_c, ts):
    B, S, _ = hg.shape
    hb = ts // CONV_HALO
    nh = S // CONV_HALO
    vec = pl.BlockSpec((1, D_CONV), lambda b, m: (0, 0))
    return pl.pallas_call(
        _conv_kernel,
        grid=(B, S // ts),
        in_specs=[
            pl.BlockSpec((None, CONV_HALO, D_CONV), lambda b, m: (b, jnp.maximum(m * hb - 1, 0), 0)),
            pl.BlockSpec((None, ts, D_CONV), lambda b, m: (b, m, 0)),
            pl.BlockSpec((None, CONV_HALO, D_CONV), lambda b, m: (b, jnp.minimum((m + 1) * hb, nh - 1), 0)),
            pl.BlockSpec((CONV_W, D_CONV), lambda b, m: (0, 0)),
            vec, vec, vec, vec,
        ],
        out_specs=pl.BlockSpec((None, ts, D_CONV), lambda b, m: (b, m, 0)),
        out_shape=jax.ShapeDtypeStruct((B, S, D_CONV), F32),
        scratch_shapes=[pltpu.VMEM((ts + 2 * CONV_HALO, D_CONV), F32)],
        compiler_params=_cparams(("parallel", "parallel")),
        name="conv_module",
    )(hg, hg, hg, conv_w, conv_b, ln_g, ln_b, gn_c)


def _extract_topk(s, order, k, payload=None):
    big = jnp.int32(1 << 30)
    vals, sel = [], []
    for _ in range(k):
        m = jnp.max(s, axis=0, keepdims=True)
        am = jnp.min(jnp.where(s == m, order, big), axis=0, keepdims=True)
        hit = order == am
        vals.append(m)
        if payload is None:
            sel.append(am)
        else:
            sel.append(jnp.max(jnp.where(hit, payload, -1), axis=0, keepdims=True))
        s = jnp.where(hit, -jnp.inf, s)
    return jnp.concatenate(vals, axis=0), jnp.concatenate(sel, axis=0)


_CAND_A = 4
_CAND_B = 3


def _route_kernel(x_ref, yf_ref, yc_ref, gf_ref, wout_ref, g2_ref, wq_ref, ka_ref, kb_ref,
                  x1_ref, htw_ref, idx_ref, gate_ref):
    tm = x_ref.shape[0]
    yf = jnp.concatenate([yf_ref[g] for g in range(GROUPS)], axis=-1)
    yfn = _rms(yf, gf_ref[...])
    ycat = jnp.concatenate([yfn, yc_ref[...]], axis=-1).astype(BF16)
    x1 = x_ref[...] + jnp.dot(ycat, wout_ref[...], preferred_element_type=F32)
    x1_ref[...] = x1
    ht = _rms(x1, g2_ref[...])
    htw_ref[...] = _pack_halves(ht)
    q = jnp.dot(ht.astype(BF16), wq_ref[...], preferred_element_type=F32)

    row16 = lax.broadcasted_iota(jnp.int32, (TOPK, tm), 0)
    key_iota = lax.broadcasted_iota(jnp.int32, (N_KEYS, tm), 0)
    nt = (((1,), (1,)), ((), ()))
    idx_rows, gate_rows = [], []
    for h in range(HEADS):
        qa = q[:, h * D_QUERY:h * D_QUERY + D_HALF].astype(BF16)
        qb = q[:, h * D_QUERY + D_HALF:(h + 1) * D_QUERY].astype(BF16)
        sa = lax.dot_general(ka_ref[h], qa, nt, preferred_element_type=F32)
        sb = lax.dot_general(kb_ref[h], qb, nt, preferred_element_type=F32)
        va, ia = _extract_topk(sa, key_iota, TOPK)
        vb, ib = _extract_topk(sb, key_iota, TOPK)
        cs, ce, co = [], [], []
        for i in range(_CAND_A):
            ok = (i + 1) * (row16 + 1) <= TOPK
            cs.append(jnp.where(ok, va[i:i + 1] + vb, -jnp.inf))
            ce.append(ia[i:i + 1] * N_KEYS + ib)
            co.append(i * TOPK + row16)
        for j in range(_CAND_B):
            ok = jnp.logical_and(row16 >= _CAND_A, (row16 + 1) * (j + 1) <= TOPK)
            cs.append(jnp.where(ok, va + vb[j:j + 1], -jnp.inf))
            ce.append(ia * N_KEYS + ib[j:j + 1])
            co.append(row16 * TOPK + j)
        top_s, top_e = _extract_topk(jnp.concatenate(cs, axis=0), jnp.concatenate(co, axis=0),
                                     TOPK, payload=jnp.concatenate(ce, axis=0))
        ex = jnp.exp(top_s - top_s[0:1])
        gate_rows.append(ex / jnp.sum(ex, axis=0, keepdims=True))
        idx_rows.append(top_e)
    idx_ref[...] = jnp.concatenate(idx_rows, axis=0).T
    gate_ref[...] = jnp.concatenate(gate_rows, axis=0).T


def _route(x, yf, yc, gf, wout_bf, g2, wq_bf, ka_bf, kb_bf, tm, s0, rows):
    B = x.shape[0]
    S = rows
    m0 = s0 // tm
    assert s0 % tm == 0 and rows % tm == 0
    const2 = lambda b, m: (0, 0)
    const3 = lambda b, m: (0, 0, 0)
    row = lambda w: pl.BlockSpec((None, tm, w), lambda b, m: (b, m, 0))
    row_in = lambda w: pl.BlockSpec((None, tm, w), lambda b, m: (b, m + m0, 0))
    return pl.pallas_call(
        _route_kernel,
        grid=(B, S // tm),
        in_specs=[
            row_in(D_MODEL),
            pl.BlockSpec((None, GROUPS, tm, GDIM), lambda b, m: (b, 0, m + m0, 0)),
            row_in(D_CONV),
            pl.BlockSpec((1, D_FOURIER), const2),
            pl.BlockSpec((D_MODEL, D_MODEL), const2),
            pl.BlockSpec((1, D_MODEL), const2),
            pl.BlockSpec((D_MODEL, HEADS * D_QUERY), const2),
            pl.BlockSpec((HEADS, N_KEYS, D_HALF), const3),
            pl.BlockSpec((HEADS, N_KEYS, D_HALF), const3),
        ],
        out_specs=[row(D_MODEL), row(D_WORDS), row(N_SEL), row(N_SEL)],
        out_shape=[
            jax.ShapeDtypeStruct((B, S, D_MODEL), F32),
            jax.ShapeDtypeStruct((B, S, D_WORDS), jnp.int32),
            jax.ShapeDtypeStruct((B, S, N_SEL), jnp.int32),
            jax.ShapeDtypeStruct((B, S, N_SEL), F32),
        ],
        compiler_params=_cparams(("parallel", "parallel")),
        name="route",
    )(x, yf, yc, gf, wout_bf, g2, wq_bf, ka_bf, kb_bf)


SC_UROWS = 64
SC_UNITS = N_SEL // SC_UROWS
SC_NBUF = 3
SC_TOKB = 16
SC_WCH = D_WORDS // SC_LANES
SC_QUAD = 4
SC_KG = 8
SC_COST_WEIGHT = 16


def _sc_mesh():
    return plsc.VectorSubcoreMesh(core_axis_name="c", subcore_axis_name="s",
                                  num_cores=SC_CORES, num_subcores=SC_SUBCORES)


def _sc_cost(tokens):
    rows = tokens * N_SEL * SC_COST_WEIGHT
    return pl.CostEstimate(flops=2 * rows * D_MODEL, transcendentals=0, bytes_accessed=rows * D_WORDS * 4)


def _sc_bf16(words):
    return plsc.bitcast(words, BF16)


def _sc_widen(pairs):
    w = plsc.bitcast(pairs, jnp.int32)
    lo = plsc.bitcast(lax.shift_left(w, 16), F32)
    hi = plsc.bitcast(lax.bitwise_and(w, jnp.int32(HI_MASK)), F32)
    return lo, hi


def _sc_quad_sum(row_chunk, scale):
    part = None
    for u in range(SC_QUAD):
        prod = _sc_bf16(row_chunk(u)) * scale(u)
        part = prod if part is None else part + prod
    return part


def _tree_sum(xs):
    while len(xs) > 1:
        xs = [xs[i] + xs[i + 1] for i in range(0, len(xs), 2)]
    return xs[0]


def _sc_token_loop(table_hbm, idx_hbm, side_hbm, out_hbm, idx_v, side_v, rows_v, out_v, gsem, osem, bsem,
                   tokens_per_worker, compute):
    wid = lax.axis_index("s") * SC_CORES + lax.axis_index("c")
    base = wid * tokens_per_worker
    n_batches = tokens_per_worker // SC_TOKB

    def batch_copies(bi, bslot):
        src = pl.ds(base + bi * SC_TOKB, SC_TOKB)
        return (pltpu.make_async_copy(idx_hbm.at[src], idx_v.at[bslot], bsem.at[2 * bslot]),
                pltpu.make_async_copy(side_hbm.at[src], side_v.at[bslot], bsem.at[2 * bslot + 1]))

    def gather(t, unit, buf):
        rows = idx_v.at[lax.rem(t // SC_TOKB, 2), lax.rem(t, SC_TOKB), pl.ds(unit * SC_UROWS, SC_UROWS)]
        return pltpu.make_async_copy(table_hbm.at[rows], rows_v.at[buf], gsem.at[buf])

    def writeback(tok, slot):
        return pltpu.make_async_copy(out_v.at[slot], out_hbm.at[tok], osem.at[slot])

    for c in batch_copies(0, 0):
        c.start()
    for c in batch_copies(0, 0):
        c.wait()
    if n_batches > 1:
        for c in batch_copies(1, 1):
            c.start()
    for unit in range(SC_UNITS):
        gather(0, unit, unit).start()

    @pl.loop(0, tokens_per_worker)
    def _(t):
        slot = lax.rem(t, 2)
        bi = t // SC_TOKB
        bslot = lax.rem(bi, 2)
        tl = lax.rem(t, SC_TOKB)
        nxt = t + 1

        @pl.when(t >= 2)
        def _():
            writeback(base + t - 2, slot).wait()

        for unit in range(SC_UNITS):
            u = t * SC_UNITS + unit
            buf = lax.rem(u, SC_NBUF)
            gather(t, unit, buf).wait()

            if unit == 0:
                @pl.when(jnp.logical_and(nxt < tokens_per_worker, lax.rem(nxt, SC_TOKB) == 0))
                def _():
                    for c in batch_copies(nxt // SC_TOKB, lax.rem(nxt // SC_TOKB, 2)):
                        c.wait()

            @pl.when(nxt < tokens_per_worker)
            def _():
                gather(nxt, unit, lax.rem(u + SC_UNITS, SC_NBUF)).start()

            compute(bslot, tl, unit, buf, slot)
        writeback(base + t, slot).start()

        @pl.when(jnp.logical_and(tl == SC_TOKB - 1, bi + 2 < n_batches))
        def _():
            for c in batch_copies(bi + 2, bslot):
                c.start()

    for t in range(tokens_per_worker - 2, tokens_per_worker):
        writeback(base + t, t % 2).wait()


def _peer_down(idx, htw, down_w):
    T = htw.shape[0]
    tpw = T // SC_WORKERS
    L = SC_LANES

    @functools.partial(
        pl.kernel, mesh=_sc_mesh(),
        out_type=jax.ShapeDtypeStruct((T, L * N_SEL), F32),
        scratch_types=[
            pltpu.VMEM((2, SC_TOKB, N_SEL), jnp.int32),
            pltpu.VMEM((2, SC_TOKB, D_WORDS), jnp.int32),
            pltpu.VMEM((SC_NBUF, SC_UROWS, D_WORDS), jnp.int32),
            pltpu.VMEM((2, L * N_SEL), F32),
            pltpu.SemaphoreType.DMA((SC_NBUF,)),
            pltpu.SemaphoreType.DMA((2,)),
            pltpu.SemaphoreType.DMA((4,)),
        ],
        compiler_params=pltpu.CompilerParams(needs_layout_passes=False),
        name="peer_down", cost_estimate=_sc_cost(T),
    )
    def k(idx_hbm, h_hbm, down_hbm, s_hbm, idx_v, h_v, rows_v, s_v, gsem, osem, bsem):
        lane = lax.iota(jnp.int32, L)

        def compute(bslot, tl, unit, buf, slot):
            svec = jnp.full((L,), slot, jnp.int32)
            hs = [_sc_bf16(h_v[bslot, tl, pl.ds(w * L, L)]) for w in range(SC_WCH)]

            @plsc.parallel_loop(0, SC_UROWS, unroll=2)
            def _(r):
                parts = []
                for q in range(SC_WCH // SC_QUAD):
                    part = _sc_quad_sum(lambda u: rows_v[buf, r, pl.ds((q * SC_QUAD + u) * L, L)],
                                        lambda u: hs[q * SC_QUAD + u])
                    parts.extend(_sc_widen(part))
                col = lax.bitwise_and(lane + (unit * SC_UROWS + r), N_SEL - 1)
                plsc.store_scatter(s_v, [svec, lane * N_SEL + col], _tree_sum(parts))

        _sc_token_loop(down_hbm, idx_hbm, h_hbm, s_hbm, idx_v, h_v, rows_v, s_v, gsem, osem, bsem, tpw, compute)

    return k(idx, htw, down_w)


def _peer_up(idx, act_w, up_w):
    T = act_w.shape[0]
    tpw = T // SC_WORKERS
    L = SC_LANES

    @functools.partial(
        pl.kernel, mesh=_sc_mesh(),
        out_type=jax.ShapeDtypeStruct((T, D_MODEL), F32),
        scratch_types=[
            pltpu.VMEM((2, SC_TOKB, N_SEL), jnp.int32),
            pltpu.VMEM((2, SC_TOKB, N_SEL), jnp.int32),
            pltpu.VMEM((SC_NBUF, SC_UROWS, D_WORDS), jnp.int32),
            pltpu.VMEM((2, D_MODEL), F32),
            pltpu.SemaphoreType.DMA((SC_NBUF,)),
            pltpu.SemaphoreType.DMA((2,)),
            pltpu.SemaphoreType.DMA((4,)),
        ],
        compiler_params=pltpu.CompilerParams(needs_layout_passes=False),
        name="peer_up", cost_estimate=_sc_cost(T),
    )
    def k(idx_hbm, act_hbm, up_hbm, o_hbm, idx_v, act_v, rows_v, o_v, gsem, osem, bsem):
        def compute(bslot, tl, unit, buf, slot):
            for kg in range(SC_WCH // SC_KG):
                lo_cols = [pl.ds((kg * SC_KG + kk) * L, L) for kk in range(SC_KG)]
                hi_cols = [pl.ds(D_WORDS + (kg * SC_KG + kk) * L, L) for kk in range(SC_KG)]
                if unit == 0:
                    init = tuple(jnp.zeros((L,), F32) for _ in range(2 * SC_KG))
                else:
                    init = tuple(o_v[slot, col] for col in lo_cols + hi_cols)

                def rgbody(rg, accs):
                    per_vec = L // SC_QUAD
                    a16 = act_v[bslot, tl, pl.ds(unit * SC_UROWS + (rg // per_vec) * L, L)]
                    lane0 = jnp.full((L,), 0, jnp.int32) + lax.rem(rg, per_vec) * SC_QUAD
                    a = [_sc_bf16(a16.at[lane0 + u].get(mode="promise_in_bounds")) for u in range(SC_QUAD)]
                    out_lo, out_hi = [], []
                    for kk in range(SC_KG):
                        part = _sc_quad_sum(lambda u: rows_v[buf, rg * SC_QUAD + u, lo_cols[kk]], lambda u: a[u])
                        lo, hi = _sc_widen(part)
                        out_lo.append(accs[kk] + lo)
                        out_hi.append(accs[SC_KG + kk] + hi)
                    return tuple(out_lo + out_hi)
                accs = lax.fori_loop(0, SC_UROWS // SC_QUAD, rgbody, init)
                for kk in range(SC_KG):
                    o_v[slot, lo_cols[kk]] = accs[kk]
                    o_v[slot, hi_cols[kk]] = accs[SC_KG + kk]

        _sc_token_loop(up_hbm, idx_hbm, act_hbm, o_hbm, idx_v, act_v, rows_v, o_v, gsem, osem, bsem, tpw, compute)

    return k(idx, act_w, up_w)


def _act_kernel(s_ref, gate_ref, a_ref):
    s = _tree_sum([pltpu.roll(s_ref[:, l * N_SEL:(l + 1) * N_SEL], (N_SEL - l) % N_SEL, axis=1)
                   for l in range(SC_LANES)])
    act = 0.5 * s * (1.0 + lax.erf(s * math.sqrt(0.5))) * gate_ref[...]
    bits = _bf16_bits(act)
    a_ref[...] = lax.bitwise_or(lax.shift_right_logical(bits, 16), bits)


def _act(s, gate, tm):
    T = s.shape[0]
    spec = pl.BlockSpec((tm, N_SEL), lambda i: (i, 0))
    part_spec = pl.BlockSpec((tm, SC_LANES * N_SEL), lambda i: (i, 0))
    return pl.pallas_call(
        _act_kernel, grid=(T // tm,), in_specs=[part_spec, spec], out_specs=spec,
        out_shape=jax.ShapeDtypeStruct((T, N_SEL), jnp.int32),
        compiler_params=_cparams(("parallel",)), name="expert_act",
    )(s, gate)


def _final_kernel(x1_ref, p_ref, g_ref, *rest):
    rest[-1][...] = _rms(x1_ref[...] + p_ref[...], g_ref[...])


def _final_into(out, out_shape, x1, peer, g, b0, nb, s0, rows):
    tm = _tile(rows, 512)
    assert s0 % tm == 0
    m0 = s0 // tm
    tok = pl.BlockSpec((None, tm, D_MODEL), lambda b, m: (b, m, 0))
    in_specs = [tok, tok, pl.BlockSpec((1, D_MODEL), lambda b, m: (0, 0))]
    args = [x1.reshape(nb, rows, D_MODEL), peer.reshape(nb, rows, D_MODEL), g]
    aliases = {}
    if out is not None:
        in_specs.append(pl.BlockSpec(memory_space=pl.ANY))
        args.append(out)
        aliases = {3: 0}
    return pl.pallas_call(
        _final_kernel, grid=(nb, rows // tm), in_specs=in_specs,
        out_specs=pl.BlockSpec((None, tm, D_MODEL), lambda b, m: (b + b0, m + m0, 0)),
        out_shape=jax.ShapeDtypeStruct(out_shape, F32), input_output_aliases=aliases,
        compiler_params=_cparams(("parallel", "parallel")), name="final_norm",
    )(*args)


PROMPT_PLAN = ((1, 2),) + ((1, 1),) * 7
SAMPLE_PLAN = ((1, 2),)
SKEW = 2
EARLY_FRONT_SLOT = 4
EARLY_CONV_SLOT = 6


def _tile(n, want):
    t = min(n, want)
    assert n % t == 0
    return t


def _channel_dft():
    ang = 2.0 * np.pi * np.outer(np.arange(GDIM), np.arange(GDIM)) / GDIM
    m = np.concatenate([np.cos(ang), -np.sin(ang)], axis=1) / math.sqrt(GDIM)
    return jnp.asarray(m.astype(np.float32), BF16)


def _plan_units(x, plan, out_id):
    units = []
    b = 0
    for nb, parts in plan:
        xb = x[b:b + nb]
        rows = x.shape[1] // parts
        units += [(xb, i * rows, rows, out_id, b) for i in range(parts)]
        b += nb
    assert b == x.shape[0]
    return units


def _mixer_front(x, p):
    zg, hg = _inproj(x, p["g1"], p["win"], p["cdft"], _tile(x.shape[1], 512))
    return _seq_dft(zg), hg


def _mixer_conv(hg, p):
    return _conv_module(hg, p["conv_w"], p["conv_b"], p["ln_g"], p["ln_b"], p["gn_c"], _tile(hg.shape[1], 512))


def _mixer(x, p):
    yf, hg = _mixer_front(x, p)
    return yf, _mixer_conv(hg, p)


def _route_unit(x, mixed, s0, rows, p):
    T = x.shape[0] * rows
    x1, htw, idx, gate = _route(x, mixed[0], mixed[1], p["gn_f"], p["wout"], p["g2"], p["wq"], p["ka"],
                                p["kb"], _tile(rows, 256), s0, rows)
    return x1.reshape(T, D_MODEL), htw.reshape(T, D_WORDS), idx.reshape(T, N_SEL), gate.reshape(T, N_SEL)


def _encode(units, out_shapes, early_x, p):
    x1s, idxs, acts, peers = [], [], [], []
    early_front, early_conv = [_mixer_front(early_x, p)], [None]
    mixed, mixed_x = None, None
    for g, (x, s0, rows, _, _) in enumerate(units):
        if x is early_x and mixed_x is not x:
            mixed, mixed_x = (early_front[0][0], early_conv[0]), x
        reuse = mixed_x is x
        carrier = mixed if reuse else x
        slots = [(acts, g - SKEW)] if g >= SKEW else []
        if g >= SKEW + 1:
            slots.append((peers, g - SKEW - 1))
        if g == EARLY_FRONT_SLOT:
            slots.append((early_front, 0))
        if g == EARLY_CONV_SLOT:
            slots.append((early_conv, 0))
        if slots:
            carrier, *tied = lax.optimization_barrier((carrier, *[lst[i] for lst, i in slots]))
            for (lst, i), v in zip(slots, tied):
                lst[i] = v
        if g == EARLY_FRONT_SLOT:
            early_conv[0] = _mixer_conv(early_front[0][1], p)
        if g >= SKEW:
            peers.append(_peer_up(idxs[g - SKEW], acts[g - SKEW], p["up"]))
        if reuse:
            mixed = carrier
        else:
            x = carrier
            mixed, mixed_x = _mixer(x, p), units[g][0]
        assert (x.shape[0] * rows) % (SC_WORKERS * SC_TOKB) == 0
        x1, htw, idx, gate = _route_unit(x, mixed, s0, rows, p)
        s = _peer_down(idx, htw, p["down"])
        x1s.append(x1)
        idxs.append(idx)
        acts.append(_act(s, gate, _tile(x1.shape[0], 1024)))
    for g in range(len(peers), len(units)):
        peers.append(_peer_up(idxs[g], acts[g], p["up"]))
    outs = {}
    for (x, s0, rows, out_id, b0), x1, peer in zip(units, x1s, peers):
        outs[out_id] = _final_into(outs.get(out_id), out_shapes[out_id], x1, peer, p["final_g"],
                                   b0, x.shape[0], s0, rows)
    return outs


def kernel(x_prompt, x_sample, norm1_g, w_in, conv_w, conv_b, conv_ln_g, conv_ln_b, gn_fourier_g,
           gn_conv_g, w_out, norm2_g, w_q, keys_a, keys_b, expert_down, expert_up, final_g):
    assert norm1_g.shape[0] == 1
    p = dict(
        g1=norm1_g[0].reshape(1, D_MODEL), win=w_in[0].astype(BF16), cdft=_channel_dft(),
        conv_w=conv_w[0], conv_b=conv_b[0].reshape(1, D_CONV),
        ln_g=conv_ln_g[0].reshape(1, D_CONV), ln_b=conv_ln_b[0].reshape(1, D_CONV),
        gn_f=gn_fourier_g[0].reshape(1, D_FOURIER), gn_c=gn_conv_g[0].reshape(1, D_CONV),
        wout=w_out[0].astype(BF16), g2=norm2_g[0].reshape(1, D_MODEL), wq=w_q[0].astype(BF16),
        ka=keys_a[0].astype(BF16), kb=keys_b[0].astype(BF16),
        down=_pack_halves(expert_down[0]), up=_pack_halves(expert_up[0]),
        final_g=final_g.reshape(1, D_MODEL),
    )
    units = _plan_units(x_prompt, PROMPT_PLAN, 0) + _plan_units(x_sample, SAMPLE_PLAN, 1)
    outs = _encode(units, (x_prompt.shape, x_sample.shape), x_sample, p)
    return outs[0], outs[1]
```

```python
import functools
import math

import numpy as np
import jax
import jax.numpy as jnp
from jax import lax
from jax.experimental import pallas as pl
from jax.experimental.pallas import tpu as pltpu
from jax.experimental.pallas import tpu_sc as plsc

F32 = jnp.float32
BF16 = jnp.bfloat16

D_MODEL = 1024
D_FOURIER = 512
GROUPS = 4
GDIM = 128
D_CONV = 512
CONV_W = 31
CONV_PAD = 15
HEADS = 8
N_KEYS = 128
TOPK = 16
D_HALF = 128
D_QUERY = 256
N_SEL = HEADS * TOPK
EPS = 1e-6

SUBLANES = 8
LANES = 128
VMEM_LIMIT = 56 * 1024 * 1024

SC_CORES = 2
SC_SUBCORES = 16
SC_LANES = 16
SC_WORKERS = SC_CORES * SC_SUBCORES


def _cparams(sem):
    return pltpu.CompilerParams(dimension_semantics=sem, vmem_limit_bytes=VMEM_LIMIT)


def _rms(x, g):
    return x * lax.rsqrt(jnp.mean(x * x, axis=-1, keepdims=True) + EPS) * g


D_WORDS = D_MODEL // 2
HI_MASK = -65536


def _bf16_bits(x):
    return lax.bitcast_convert_type(x.astype(BF16).astype(F32), jnp.int32)


def _pack_halves(x):
    lo = lax.shift_right_logical(_bf16_bits(x[..., :D_WORDS]), 16)
    hi = lax.bitwise_and(_bf16_bits(x[..., D_WORDS:]), HI_MASK)
    return lax.bitwise_or(lo, hi)


def _inproj_kernel(x_ref, g1_ref, win_ref, cdft_ref, zg_ref, hg_ref):
    h = _rms(x_ref[...], g1_ref[...]).astype(BF16)
    z = jnp.dot(h, win_ref[...], preferred_element_type=F32)
    for g in range(GROUPS):
        zg = z[:, g * GDIM:(g + 1) * GDIM].astype(BF16)
        zg_ref[g] = jnp.dot(zg, cdft_ref[...], preferred_element_type=F32)
    a = z[:, D_FOURIER:D_FOURIER + D_CONV]
    gate = z[:, D_FOURIER + D_CONV:]
    hg_ref[...] = a * jax.nn.sigmoid(gate)


def _inproj(x, g1, win_bf, cdft, tm):
    B, S, _ = x.shape
    return pl.pallas_call(
        _inproj_kernel,
        grid=(B, S // tm),
        in_specs=[
            pl.BlockSpec((None, tm, D_MODEL), lambda b, m: (b, m, 0)),
            pl.BlockSpec((1, D_MODEL), lambda b, m: (0, 0)),
            pl.BlockSpec((D_MODEL, D_FOURIER + 2 * D_CONV), lambda b, m: (0, 0)),
            pl.BlockSpec((GDIM, 2 * GDIM), lambda b, m: (0, 0)),
        ],
        out_specs=[
            pl.BlockSpec((None, GROUPS, tm, 2 * GDIM), lambda b, m: (b, 0, m, 0)),
            pl.BlockSpec((None, tm, D_CONV), lambda b, m: (b, m, 0)),
        ],
        out_shape=[
            jax.ShapeDtypeStruct((B, GROUPS, S, 2 * GDIM), F32),
            jax.ShapeDtypeStruct((B, S, D_CONV), F32),
        ],
        compiler_params=_cparams(("parallel", "parallel")),
        name="inproj",
    )(x, g1, win_bf, cdft)


def _fft1_kernel(z_ref, k1_ref, tc_ref, ts_ref, a_ref):
    n1 = z_ref.shape[0]
    rows = n1 * SUBLANES
    x = z_ref[...].reshape(rows, 2 * GDIM).astype(BF16)
    p = jnp.dot(k1_ref[...], x, preferred_element_type=F32)
    pc, ps = p[:rows], p[rows:]
    ar = pc[:, :GDIM] + ps[:, GDIM:]
    ai = pc[:, GDIM:] - ps[:, :GDIM]
    tc, ts = tc_ref[...], ts_ref[...]
    a_ref[:, :, :GDIM] = (ar * tc + ai * ts).reshape(n1, SUBLANES, GDIM)
    a_ref[:, :, GDIM:] = (ai * tc - ar * ts).reshape(n1, SUBLANES, GDIM)


def _fft2_kernel(a_ref, c2_ref, s2_ref, y_ref):
    for r in range(SUBLANES):
        blk = a_ref[r].astype(BF16)
        y = jnp.dot(c2_ref[...], blk[:, :GDIM], preferred_element_type=F32)
        y = y + jnp.dot(s2_ref[...], blk[:, GDIM:], preferred_element_type=F32)
        y_ref[:, r * GDIM:(r + 1) * GDIM] = y


def _split_seq(S):
    n1 = 1 << (int(math.log2(S)) // 2)
    return n1, S // n1


@functools.lru_cache(maxsize=None)
def _fft_tables(S):
    n1, n2 = _split_seq(S)
    k1 = np.arange(n1)
    ang1 = 2.0 * np.pi * np.outer(k1, np.arange(n1)) / n1
    eye = np.eye(SUBLANES)
    kmat = np.concatenate([np.kron(np.cos(ang1), eye), np.kron(np.sin(ang1), eye)], axis=0)
    nn2 = np.arange(n2).reshape(n2 // SUBLANES, 1, SUBLANES)
    angt = 2.0 * np.pi * k1.reshape(1, n1, 1) * nn2 / S
    angt = angt.reshape(n2 // SUBLANES, n1 * SUBLANES, 1)
    tc = np.broadcast_to(np.cos(angt), (n2 // SUBLANES, n1 * SUBLANES, GDIM))
    ts = np.broadcast_to(np.sin(angt), (n2 // SUBLANES, n1 * SUBLANES, GDIM))
    ang2 = 2.0 * np.pi * np.outer(np.arange(n2), np.arange(n2)) / n2
    scale = 1.0 / math.sqrt(S)
    return (kmat.astype(np.float32), np.ascontiguousarray(tc, np.float32),
            np.ascontiguousarray(ts, np.float32),
            (np.cos(ang2) * scale).astype(np.float32), (np.sin(ang2) * scale).astype(np.float32))


def _seq_dft(zg):
    B, G, S, _ = zg.shape
    n1, n2 = _split_seq(S)
    kmat, tc, ts, c2, s2 = _fft_tables(S)
    kmat = jnp.asarray(kmat, BF16)
    c2 = jnp.asarray(c2, BF16)
    s2 = jnp.asarray(s2, BF16)
    nb = n2 // SUBLANES
    z6 = zg.reshape(B, G, n1, nb, SUBLANES, 2 * GDIM)
    blk6 = (None, None, n1, None, SUBLANES, 2 * GDIM)
    a6 = pl.pallas_call(
        _fft1_kernel,
        grid=(nb, B, G),
        in_specs=[
            pl.BlockSpec(blk6, lambda j, b, g: (b, g, 0, j, 0, 0)),
            pl.BlockSpec((2 * n1 * SUBLANES, n1 * SUBLANES), lambda j, b, g: (0, 0)),
            pl.BlockSpec((None, n1 * SUBLANES, GDIM), lambda j, b, g: (j, 0, 0)),
            pl.BlockSpec((None, n1 * SUBLANES, GDIM), lambda j, b, g: (j, 0, 0)),
        ],
        out_specs=pl.BlockSpec(blk6, lambda j, b, g: (b, g, 0, j, 0, 0)),
        out_shape=jax.ShapeDtypeStruct(z6.shape, F32),
        compiler_params=_cparams(("parallel", "parallel", "parallel")),
        name="fft_stage1",
    )(z6, kmat, jnp.asarray(tc), jnp.asarray(ts))
    a5 = a6.reshape(B, G, n1, n2, 2 * GDIM)
    y = pl.pallas_call(
        _fft2_kernel,
        grid=(B, G, n1 // SUBLANES),
        in_specs=[
            pl.BlockSpec((None, None, SUBLANES, n2, 2 * GDIM), lambda b, g, i: (b, g, i, 0, 0)),
            pl.BlockSpec((n2, n2), lambda b, g, i: (0, 0)),
            pl.BlockSpec((n2, n2), lambda b, g, i: (0, 0)),
        ],
        out_specs=pl.BlockSpec((None, None, n2, SUBLANES * GDIM), lambda b, g, i: (b, g, 0, i)),
        out_shape=jax.ShapeDtypeStruct((B, G, n2, n1 * GDIM), F32),
        compiler_params=_cparams(("parallel", "parallel", "parallel")),
        name="fft_stage2",
    )(a5, c2, s2)
    return y.reshape(B, G, S, GDIM)


CONV_HALO = 16
CONV_ROWS = 16


def _conv_kernel(prev_ref, cur_ref, next_ref, w_ref, cb_ref, lg_ref, lb_ref, gc_ref, o_ref, scr):
    ts = cur_ref.shape[0]
    m = pl.program_id(1)
    last = pl.num_programs(1) - 1
    scr[0:CONV_HALO, :] = jnp.where(m > 0, prev_ref[...], 0.0)
    scr[CONV_HALO:CONV_HALO + ts, :] = cur_ref[...]
    scr[CONV_HALO + ts:, :] = jnp.where(m < last, next_ref[...], 0.0)
    off = CONV_HALO - CONV_PAD

    def step(c, carry):
        base = pl.multiple_of(c * CONV_ROWS, CONV_ROWS)
        win = scr[pl.ds(base, CONV_ROWS + 2 * CONV_HALO), :]
        acc = jnp.broadcast_to(cb_ref[...], (CONV_ROWS, D_CONV))
        for k in range(CONV_W):
            acc = acc + win[off + k:off + k + CONV_ROWS, :] * w_ref[k:k + 1, :]
        mu = jnp.mean(acc, axis=-1, keepdims=True)
        d = acc - mu
        var = jnp.mean(d * d, axis=-1, keepdims=True)
        y = d * lax.rsqrt(var + EPS) * lg_ref[...] + lb_ref[...]
        y = y * jax.nn.sigmoid(y)
        o_ref[pl.ds(base, CONV_ROWS), :] = _rms(y, gc_ref[...])
        return carry

    lax.fori_loop(0, ts // CONV_ROWS, step, 0)


def _conv_module(hg, conv_w, conv_b, ln_g, ln_b, gn_c, ts):
    B, S, _ = hg.shape
    hb = ts // CONV_HALO
    nh = S // CONV_HALO
    vec = pl.BlockSpec((1, D_CONV), lambda b, m: (0, 0))
    return pl.pallas_call(
        _conv_kernel,
        grid=(B, S // ts),
        in_specs=[
            pl.BlockSpec((None, CONV_HALO, D_CONV), lambda b, m: (b, jnp.maximum(m * hb - 1, 0), 0)),
            pl.BlockSpec((None, ts, D_CONV), lambda b, m: (b, m, 0)),
            pl.BlockSpec((None, CONV_HALO, D_CONV), lambda b, m: (b, jnp.minimum((m + 1) * hb, nh - 1), 0)),
            pl.BlockSpec((CONV_W, D_CONV), lambda b, m: (0, 0)),
            vec, vec, vec, vec,
        ],
        out_specs=pl.BlockSpec((None, ts, D_CONV), lambda b, m: (b, m, 0)),
        out_shape=jax.ShapeDtypeStruct((B, S, D_CONV), F32),
        scratch_shapes=[pltpu.VMEM((ts + 2 * CONV_HALO, D_CONV), F32)],
        compiler_params=_cparams(("parallel", "parallel")),
        name="conv_module",
    )(hg, hg, hg, conv_w, conv_b, ln_g, ln_b, gn_c)


def _extract_topk(s, order, k, payload=None):
    big = jnp.int32(1 << 30)
    vals, sel = [], []
    for _ in range(k):
        m = jnp.max(s, axis=0, keepdims=True)
        am = jnp.min(jnp.where(s == m, order, big), axis=0, keepdims=True)
        hit = order == am
        vals.append(m)
        if payload is None:
            sel.append(am)
        else:
            sel.append(jnp.max(jnp.where(hit, payload, -1), axis=0, keepdims=True))
        s = jnp.where(hit, -jnp.inf, s)
    return jnp.concatenate(vals, axis=0), jnp.concatenate(sel, axis=0)


_CAND_A = 4
_CAND_B = 3


def _route_kernel(x_ref, yf_ref, yc_ref, gf_ref, wout_ref, g2_ref, wq_ref, ka_ref, kb_ref,
                  x1_ref, htw_ref, idx_ref, gate_ref):
    tm = x_ref.shape[0]
    yf = jnp.concatenate([yf_ref[g] for g in range(GROUPS)], axis=-1)
    yfn = _rms(yf, gf_ref[...])
    ycat = jnp.concatenate([yfn, yc_ref[...]], axis=-1).astype(BF16)
    x1 = x_ref[...] + jnp.dot(ycat, wout_ref[...], preferred_element_type=F32)
    x1_ref[...] = x1
    ht = _rms(x1, g2_ref[...])
    htw_ref[...] = _pack_halves(ht)
    q = jnp.dot(ht.astype(BF16), wq_ref[...], preferred_element_type=F32)

    row16 = lax.broadcasted_iota(jnp.int32, (TOPK, tm), 0)
    key_iota = lax.broadcasted_iota(jnp.int32, (N_KEYS, tm), 0)
    nt = (((1,), (1,)), ((), ()))
    idx_rows, gate_rows = [], []
    for h in range(HEADS):
        qa = q[:, h * D_QUERY:h * D_QUERY + D_HALF].astype(BF16)
        qb = q[:, h * D_QUERY + D_HALF:(h + 1) * D_QUERY].astype(BF16)
        sa = lax.dot_general(ka_ref[h], qa, nt, preferred_element_type=F32)
        sb = lax.dot_general(kb_ref[h], qb, nt, preferred_element_type=F32)
        va, ia = _extract_topk(sa, key_iota, TOPK)
        vb, ib = _extract_topk(sb, key_iota, TOPK)
        cs, ce, co = [], [], []
        for i in range(_CAND_A):
            ok = (i + 1) * (row16 + 1) <= TOPK
            cs.append(jnp.where(ok, va[i:i + 1] + vb, -jnp.inf))
            ce.append(ia[i:i + 1] * N_KEYS + ib)
            co.append(i * TOPK + row16)
        for j in range(_CAND_B):
            ok = jnp.logical_and(row16 >= _CAND_A, (row16 + 1) * (j + 1) <= TOPK)
            cs.append(jnp.where(ok, va + vb[j:j + 1], -jnp.inf))
            ce.append(ia * N_KEYS + ib[j:j + 1])
            co.append(row16 * TOPK + j)
        top_s, top_e = _extract_topk(jnp.concatenate(cs, axis=0), jnp.concatenate(co, axis=0),
                                     TOPK, payload=jnp.concatenate(ce, axis=0))
        ex = jnp.exp(top_s - top_s[0:1])
        gate_rows.append(ex / jnp.sum(ex, axis=0, keepdims=True))
        idx_rows.append(top_e)
    idx_ref[...] = jnp.concatenate(idx_rows, axis=0).T
    gate_ref[...] = jnp.concatenate(gate_rows, axis=0).T


def _route(x, yf, yc, gf, wout_bf, g2, wq_bf, ka_bf, kb_bf, tm, s0, rows):
    B = x.shape[0]
    S = rows
    m0 = s0 // tm
    assert s0 % tm == 0 and rows % tm == 0
    const2 = lambda b, m: (0, 0)
    const3 = lambda b, m: (0, 0, 0)
    row = lambda w: pl.BlockSpec((None, tm, w), lambda b, m: (b, m, 0))
    row_in = lambda w: pl.BlockSpec((None, tm, w), lambda b, m: (b, m + m0, 0))
    return pl.pallas_call(
        _route_kernel,
        grid=(B, S // tm),
        in_specs=[
            row_in(D_MODEL),
            pl.BlockSpec((None, GROUPS, tm, GDIM), lambda b, m: (b, 0, m + m0, 0)),
            row_in(D_CONV),
            pl.BlockSpec((1, D_FOURIER), const2),
            pl.BlockSpec((D_MODEL, D_MODEL), const2),
            pl.BlockSpec((1, D_MODEL), const2),
            pl.BlockSpec((D_MODEL, HEADS * D_QUERY), const2),
            pl.BlockSpec((HEADS, N_KEYS, D_HALF), const3),
            pl.BlockSpec((HEADS, N_KEYS, D_HALF), const3),
        ],
        out_specs=[row(D_MODEL), row(D_WORDS), row(N_SEL), row(N_SEL)],
        out_shape=[
            jax.ShapeDtypeStruct((B, S, D_MODEL), F32),
            jax.ShapeDtypeStruct((B, S, D_WORDS), jnp.int32),
            jax.ShapeDtypeStruct((B, S, N_SEL), jnp.int32),
            jax.ShapeDtypeStruct((B, S, N_SEL), F32),
        ],
        compiler_params=_cparams(("parallel", "parallel")),
        name="route",
    )(x, yf, yc, gf, wout_bf, g2, wq_bf, ka_bf, kb_bf)


SC_UROWS = 64
SC_UNITS = N_SEL // SC_UROWS
SC_NBUF = 3
SC_TOKB = 16
SC_WCH = D_WORDS // SC_LANES
SC_QUAD = 4
SC_KG = 8
SC_COST_WEIGHT = 16


def _sc_mesh():
    return plsc.VectorSubcoreMesh(core_axis_name="c", subcore_axis_name="s",
                                  num_cores=SC_CORES, num_subcores=SC_SUBCORES)


def _sc_cost(tokens):
    rows = tokens * N_SEL * SC_COST_WEIGHT
    return pl.CostEstimate(flops=2 * rows * D_MODEL, transcendentals=0, bytes_accessed=rows * D_WORDS * 4)


def _sc_bf16(words):
    return plsc.bitcast(words, BF16)


def _sc_widen(pairs):
    w = plsc.bitcast(pairs, jnp.int32)
    lo = plsc.bitcast(lax.shift_left(w, 16), F32)
    hi = plsc.bitcast(lax.bitwise_and(w, jnp.int32(HI_MASK)), F32)
    return lo, hi


def _sc_quad_sum(row_chunk, scale):
    part = None
    for u in range(SC_QUAD):
        prod = _sc_bf16(row_chunk(u)) * scale(u)
        part = prod if part is None else part + prod
    return part


def _tree_sum(xs):
    while len(xs) > 1:
        xs = [xs[i] + xs[i + 1] for i in range(0, len(xs), 2)]
    return xs[0]


def _sc_token_loop(table_hbm, idx_hbm, side_hbm, out_hbm, idx_v, side_v, rows_v, out_v, gsem, osem, bsem,
                   tokens_per_worker, compute):
    wid = lax.axis_index("s") * SC_CORES + lax.axis_index("c")
    base = wid * tokens_per_worker
    n_batches = tokens_per_worker // SC_TOKB

    def batch_copies(bi, bslot):
        src = pl.ds(base + bi * SC_TOKB, SC_TOKB)
        return (pltpu.make_async_copy(idx_hbm.at[src], idx_v.at[bslot], bsem.at[2 * bslot]),
                pltpu.make_async_copy(side_hbm.at[src], side_v.at[bslot], bsem.at[2 * bslot + 1]))

    def gather(t, unit, buf):
        rows = idx_v.at[lax.rem(t // SC_TOKB, 2), lax.rem(t, SC_TOKB), pl.ds(unit * SC_UROWS, SC_UROWS)]
        return pltpu.make_async_copy(table_hbm.at[rows], rows_v.at[buf], gsem.at[buf])

    def writeback(tok, slot):
        return pltpu.make_async_copy(out_v.at[slot], out_hbm.at[tok], osem.at[slot])

    for c in batch_copies(0, 0):
        c.start()
    for c in batch_copies(0, 0):
        c.wait()
    if n_batches > 1:
        for c in batch_copies(1, 1):
            c.start()
    for unit in range(SC_UNITS):
        gather(0, unit, unit).start()

    @pl.loop(0, tokens_per_worker)
    def _(t):
        slot = lax.rem(t, 2)
        bi = t // SC_TOKB
        bslot = lax.rem(bi, 2)
        tl = lax.rem(t, SC_TOKB)
        nxt = t + 1

        @pl.when(t >= 2)
        def _():
            writeback(base + t - 2, slot).wait()

        for unit in range(SC_UNITS):
            u = t * SC_UNITS + unit
            buf = lax.rem(u, SC_NBUF)
            gather(t, unit, buf).wait()

            if unit == 0:
                @pl.when(jnp.logical_and(nxt < tokens_per_worker, lax.rem(nxt, SC_TOKB) == 0))
                def _():
                    for c in batch_copies(nxt // SC_TOKB, lax.rem(nxt // SC_TOKB, 2)):
                        c.wait()

            @pl.when(nxt < tokens_per_worker)
            def _():
                gather(nxt, unit, lax.rem(u + SC_UNITS, SC_NBUF)).start()

            compute(bslot, tl, unit, buf, slot)
        writeback(base + t, slot).start()

        @pl.when(jnp.logical_and(tl == SC_TOKB - 1, bi + 2 < n_batches))
        def _():
            for c in batch_copies(bi + 2, bslot):
                c.start()

    for t in range(tokens_per_worker - 2, tokens_per_worker):
        writeback(base + t, t % 2).wait()


def _peer_down(idx, htw, down_w):
    T = htw.shape[0]
    tpw = T // SC_WORKERS
    L = SC_LANES

    @functools.partial(
        pl.kernel, mesh=_sc_mesh(),
        out_type=jax.ShapeDtypeStruct((T, L * N_SEL), F32),
        scratch_types=[
            pltpu.VMEM((2, SC_TOKB, N_SEL), jnp.int32),
            pltpu.VMEM((2, SC_TOKB, D_WORDS), jnp.int32),
            pltpu.VMEM((SC_NBUF, SC_UROWS, D_WORDS), jnp.int32),
            pltpu.VMEM((2, L * N_SEL), F32),
            pltpu.SemaphoreType.DMA((SC_NBUF,)),
            pltpu.SemaphoreType.DMA((2,)),
            pltpu.SemaphoreType.DMA((4,)),
        ],
        compiler_params=pltpu.CompilerParams(needs_layout_passes=False),
        name="peer_down", cost_estimate=_sc_cost(T),
    )
    def k(idx_hbm, h_hbm, down_hbm, s_hbm, idx_v, h_v, rows_v, s_v, gsem, osem, bsem):
        lane = lax.iota(jnp.int32, L)

        def compute(bslot, tl, unit, buf, slot):
            svec = jnp.full((L,), slot, jnp.int32)
            hs = [_sc_bf16(h_v[bslot, tl, pl.ds(w * L, L)]) for w in range(SC_WCH)]

            @plsc.parallel_loop(0, SC_UROWS, unroll=2)
            def _(r):
                parts = []
                for q in range(SC_WCH // SC_QUAD):
                    part = _sc_quad_sum(lambda u: rows_v[buf, r, pl.ds((q * SC_QUAD + u) * L, L)],
                                        lambda u: hs[q * SC_QUAD + u])
                    parts.extend(_sc_widen(part))
                col = lax.bitwise_and(lane + (unit * SC_UROWS + r), N_SEL - 1)
                plsc.store_scatter(s_v, [svec, lane * N_SEL + col], _tree_sum(parts))

        _sc_token_loop(down_hbm, idx_hbm, h_hbm, s_hbm, idx_v, h_v, rows_v, s_v, gsem, osem, bsem, tpw, compute)

    return k(idx, htw, down_w)


def _peer_up(idx, act_w, up_w):
    T = act_w.shape[0]
    tpw = T // SC_WORKERS
    L = SC_LANES

    @functools.partial(
        pl.kernel, mesh=_sc_mesh(),
        out_type=jax.ShapeDtypeStruct((T, D_MODEL), F32),
        scratch_types=[
            pltpu.VMEM((2, SC_TOKB, N_SEL), jnp.int32),
            pltpu.VMEM((2, SC_TOKB, N_SEL), jnp.int32),
            pltpu.VMEM((SC_NBUF, SC_UROWS, D_WORDS), jnp.int32),
            pltpu.VMEM((2, D_MODEL), F32),
            pltpu.SemaphoreType.DMA((SC_NBUF,)),
            pltpu.SemaphoreType.DMA((2,)),
            pltpu.SemaphoreType.DMA((4,)),
        ],
        compiler_params=pltpu.CompilerParams(needs_layout_passes=False),
        name="peer_up", cost_estimate=_sc_cost(T),
    )
    def k(idx_hbm, act_hbm, up_hbm, o_hbm, idx_v, act_v, rows_v, o_v, gsem, osem, bsem):
        def compute(bslot, tl, unit, buf, slot):
            for kg in range(SC_WCH // SC_KG):
                lo_cols = [pl.ds((kg * SC_KG + kk) * L, L) for kk in range(SC_KG)]
                hi_cols = [pl.ds(D_WORDS + (kg * SC_KG + kk) * L, L) for kk in range(SC_KG)]
                if unit == 0:
                    init = tuple(jnp.zeros((L,), F32) for _ in range(2 * SC_KG))
                else:
                    init = tuple(o_v[slot, col] for col in lo_cols + hi_cols)

                def rgbody(rg, accs):
                    per_vec = L // SC_QUAD
                    a16 = act_v[bslot, tl, pl.ds(unit * SC_UROWS + (rg // per_vec) * L, L)]
                    lane0 = jnp.full((L,), 0, jnp.int32) + lax.rem(rg, per_vec) * SC_QUAD
                    a = [_sc_bf16(a16.at[lane0 + u].get(mode="promise_in_bounds")) for u in range(SC_QUAD)]
                    out_lo, out_hi = [], []
                    for kk in range(SC_KG):
                        part = _sc_quad_sum(lambda u: rows_v[buf, rg * SC_QUAD + u, lo_cols[kk]], lambda u: a[u])
                        lo, hi = _sc_widen(part)
                        out_lo.append(accs[kk] + lo)
                        out_hi.append(accs[SC_KG + kk] + hi)
                    return tuple(out_lo + out_hi)
                accs = lax.fori_loop(0, SC_UROWS // SC_QUAD, rgbody, init)
                for kk in range(SC_KG):
                    o_v[slot, lo_cols[kk]] = accs[kk]
                    o_v[slot, hi_cols[kk]] = accs[SC_KG + kk]

        _sc_token_loop(up_hbm, idx_hbm, act_hbm, o_hbm, idx_v, act_v, rows_v, o_v, gsem, osem, bsem, tpw, compute)

    return k(idx, act_w, up_w)


def _act_kernel(s_ref, gate_ref, a_ref):
    s = _tree_sum([pltpu.roll(s_ref[:, l * N_SEL:(l + 1) * N_SEL], (N_SEL - l) % N_SEL, axis=1)
                   for l in range(SC_LANES)])
    act = 0.5 * s * (1.0 + lax.erf(s * math.sqrt(0.5))) * gate_ref[...]
    bits = _bf16_bits(act)
    a_ref[...] = lax.bitwise_or(lax.shift_right_logical(bits, 16), bits)


def _act(s, gate, tm):
    T = s.shape[0]
    spec = pl.BlockSpec((tm, N_SEL), lambda i: (i, 0))
    part_spec = pl.BlockSpec((tm, SC_LANES * N_SEL), lambda i: (i, 0))
    return pl.pallas_call(
        _act_kernel, grid=(T // tm,), in_specs=[part_spec, spec], out_specs=spec,
        out_shape=jax.ShapeDtypeStruct((T, N_SEL), jnp.int32),
        compiler_params=_cparams(("parallel",)), name="expert_act",
    )(s, gate)


def _final_kernel(x1_ref, p_ref, g_ref, *rest):
    rest[-1][...] = _rms(x1_ref[...] + p_ref[...], g_ref[...])


def _final_into(out, out_shape, x1, peer, g, b0, nb, s0, rows):
    tm = _tile(rows, 512)
    assert s0 % tm == 0
    m0 = s0 // tm
    tok = pl.BlockSpec((None, tm, D_MODEL), lambda b, m: (b, m, 0))
    in_specs = [tok, tok, pl.BlockSpec((1, D_MODEL), lambda b, m: (0, 0))]
    args = [x1.reshape(nb, rows, D_MODEL), peer.reshape(nb, rows, D_MODEL), g]
    aliases = {}
    if out is not None:
        in_specs.append(pl.BlockSpec(memory_space=pl.ANY))
        args.append(out)
        aliases = {3: 0}
    return pl.pallas_call(
        _final_kernel, grid=(nb, rows // tm), in_specs=in_specs,
        out_specs=pl.BlockSpec((None, tm, D_MODEL), lambda b, m: (b + b0, m + m0, 0)),
        out_shape=jax.ShapeDtypeStruct(out_shape, F32), input_output_aliases=aliases,
        compiler_params=_cparams(("parallel", "parallel")), name="final_norm",
    )(*args)


PROMPT_PLAN = ((1, 2), (1, 2)) + ((1, 1),) * 6
SAMPLE_PLAN = ((1, 2),)
SKEW = 2
EARLY_FRONT_SLOT = 5
EARLY_CONV_SLOT = 7


def _tile(n, want):
    t = min(n, want)
    assert n % t == 0
    return t


def _channel_dft():
    ang = 2.0 * np.pi * np.outer(np.arange(GDIM), np.arange(GDIM)) / GDIM
    m = np.concatenate([np.cos(ang), -np.sin(ang)], axis=1) / math.sqrt(GDIM)
    return jnp.asarray(m.astype(np.float32), BF16)


def _plan_units(x, plan, out_id):
    units = []
    b = 0
    for nb, parts in plan:
        xb = x[b:b + nb]
        rows = x.shape[1] // parts
        units += [(xb, i * rows, rows, out_id, b) for i in range(parts)]
        b += nb
    assert b == x.shape[0]
    return units


def _mixer_front(x, p):
    zg, hg = _inproj(x, p["g1"], p["win"], p["cdft"], _tile(x.shape[1], 512))
    return _seq_dft(zg), hg


def _mixer_conv(hg, p):
    return _conv_module(hg, p["conv_w"], p["conv_b"], p["ln_g"], p["ln_b"], p["gn_c"], _tile(hg.shape[1], 512))


def _mixer(x, p):
    yf, hg = _mixer_front(x, p)
    return yf, _mixer_conv(hg, p)


def _route_unit(x, mixed, s0, rows, p):
    T = x.shape[0] * rows
    x1, htw, idx, gate = _route(x, mixed[0], mixed[1], p["gn_f"], p["wout"], p["g2"], p["wq"], p["ka"],
                                p["kb"], _tile(rows, 256), s0, rows)
    return x1.reshape(T, D_MODEL), htw.reshape(T, D_WORDS), idx.reshape(T, N_SEL), gate.reshape(T, N_SEL)


def _encode(units, out_shapes, early_x, p):
    x1s, idxs, acts, peers = [], [], [], []
    early_front, early_conv = [_mixer_front(early_x, p)], [None]
    mixed, mixed_x = None, None
    for g, (x, s0, rows, _, _) in enumerate(units):
        if x is early_x and mixed_x is not x:
            mixed, mixed_x = (early_front[0][0], early_conv[0]), x
        reuse = mixed_x is x
        carrier = mixed if reuse else x
        slots = [(acts, g - SKEW)] if g >= SKEW else []
        if g >= SKEW + 1:
            slots.append((peers, g - SKEW - 1))
        if g == EARLY_FRONT_SLOT:
            slots.append((early_front, 0))
        if g == EARLY_CONV_SLOT:
            slots.append((early_conv, 0))
        if slots:
            carrier, *tied = lax.optimization_barrier((carrier, *[lst[i] for lst, i in slots]))
            for (lst, i), v in zip(slots, tied):
                lst[i] = v
        if g == EARLY_FRONT_SLOT:
            early_conv[0] = _mixer_conv(early_front[0][1], p)
        if g >= SKEW:
            peers.append(_peer_up(idxs[g - SKEW], acts[g - SKEW], p["up"]))
        if reuse:
            mixed = carrier
        else:
            x = carrier
            mixed, mixed_x = _mixer(x, p), units[g][0]
        assert (x.shape[0] * rows) % (SC_WORKERS * SC_TOKB) == 0
        x1, htw, idx, gate = _route_unit(x, mixed, s0, rows, p)
        s = _peer_down(idx, htw, p["down"])
        x1s.append(x1)
        idxs.append(idx)
        acts.append(_act(s, gate, _tile(x1.shape[0], 1024)))
    for g in range(len(peers), len(units)):
        peers.append(_peer_up(idxs[g], acts[g], p["up"]))
    outs = {}
    for (x, s0, rows, out_id, b0), x1, peer in zip(units, x1s, peers):
        outs[out_id] = _final_into(outs.get(out_id), out_shapes[out_id], x1, peer, p["final_g"],
                                   b0, x.shape[0], s0, rows)
    return outs


def kernel(x_prompt, x_sample, norm1_g, w_in, conv_w, conv_b, conv_ln_g, conv_ln_b, gn_fourier_g,
           gn_conv_g, w_out, norm2_g, w_q, keys_a, keys_b, expert_down, expert_up, final_g):
    assert norm1_g.shape[0] == 1
    p = dict(
        g1=norm1_g[0].reshape(1, D_MODEL), win=w_in[0].astype(BF16), cdft=_channel_dft(),
        conv_w=conv_w[0], conv_b=conv_b[0].reshape(1, D_CONV),
        ln_g=conv_ln_g[0].reshape(1, D_CONV), ln_b=conv_ln_b[0].reshape(1, D_CONV),
        gn_f=gn_fourier_g[0].reshape(1, D_FOURIER), gn_c=gn_conv_g[0].reshape(1, D_CONV),
        wout=w_out[0].astype(BF16), g2=norm2_g[0].reshape(1, D_MODEL), wq=w_q[0].astype(BF16),
        ka=keys_a[0].astype(BF16), kb=keys_b[0].astype(BF16),
        down=_pack_halves(expert_down[0]), up=_pack_halves(expert_up[0]),
        final_g=final_g.reshape(1, D_MODEL),
    )
    units = _plan_units(x_prompt, PROMPT_PLAN, 0) + _plan_units(x_sample, SAMPLE_PLAN, 1)
    outs = _encode(units, (x_prompt.shape, x_sample.shape), x_sample, p)
    return outs[0], outs[1]
```

```python
import functools
import math

import numpy as np
import jax
import jax.numpy as jnp
from jax import lax
from jax.experimental import pallas as pl
from jax.experimental.pallas import tpu as pltpu
from jax.experimental.pallas import tpu_sc as plsc

F32 = jnp.float32
BF16 = jnp.bfloat16

D_MODEL = 1024
D_FOURIER = 512
GROUPS = 4
GDIM = 128
D_CONV = 512
CONV_W = 31
CONV_PAD = 15
HEADS = 8
N_KEYS = 128
TOPK = 16
D_HALF = 128
D_QUERY = 256
N_SEL = HEADS * TOPK
EPS = 1e-6

SUBLANES = 8
LANES = 128
VMEM_LIMIT = 56 * 1024 * 1024

SC_CORES = 2
SC_SUBCORES = 16
SC_LANES = 16
SC_WORKERS = SC_CORES * SC_SUBCORES


def _cparams(sem):
    return pltpu.CompilerParams(dimension_semantics=sem, vmem_limit_bytes=VMEM_LIMIT)


def _rms(x, g):
    return x * lax.rsqrt(jnp.mean(x * x, axis=-1, keepdims=True) + EPS) * g


D_WORDS = D_MODEL // 2
HI_MASK = -65536


def _bf16_bits(x):
    return lax.bitcast_convert_type(x.astype(BF16).astype(F32), jnp.int32)


def _pack_halves(x):
    lo = lax.shift_right_logical(_bf16_bits(x[..., :D_WORDS]), 16)
    hi = lax.bitwise_and(_bf16_bits(x[..., D_WORDS:]), HI_MASK)
    return lax.bitwise_or(lo, hi)


def _inproj_kernel(x_ref, g1_ref, win_ref, cdft_ref, zg_ref, hg_ref):
    h = _rms(x_ref[...], g1_ref[...]).astype(BF16)
    z = jnp.dot(h, win_ref[...], preferred_element_type=F32)
    for g in range(GROUPS):
        zg = z[:, g * GDIM:(g + 1) * GDIM].astype(BF16)
        zg_ref[g] = jnp.dot(zg, cdft_ref[...], preferred_element_type=F32)
    a = z[:, D_FOURIER:D_FOURIER + D_CONV]
    gate = z[:, D_FOURIER + D_CONV:]
    hg_ref[...] = a * jax.nn.sigmoid(gate)


def _inproj(x, g1, win_bf, cdft, tm):
    B, S, _ = x.shape
    return pl.pallas_call(
        _inproj_kernel,
        grid=(B, S // tm),
        in_specs=[
            pl.BlockSpec((None, tm, D_MODEL), lambda b, m: (b, m, 0)),
            pl.BlockSpec((1, D_MODEL), lambda b, m: (0, 0)),
            pl.BlockSpec((D_MODEL, D_FOURIER + 2 * D_CONV), lambda b, m: (0, 0)),
            pl.BlockSpec((GDIM, 2 * GDIM), lambda b, m: (0, 0)),
        ],
        out_specs=[
            pl.BlockSpec((None, GROUPS, tm, 2 * GDIM), lambda b, m: (b, 0, m, 0)),
            pl.BlockSpec((None, tm, D_CONV), lambda b, m: (b, m, 0)),
        ],
        out_shape=[
            jax.ShapeDtypeStruct((B, GROUPS, S, 2 * GDIM), F32),
            jax.ShapeDtypeStruct((B, S, D_CONV), F32),
        ],
        compiler_params=_cparams(("parallel", "parallel")),
        name="inproj",
    )(x, g1, win_bf, cdft)


def _fft1_kernel(z_ref, k1_ref, tc_ref, ts_ref, a_ref):
    n1 = z_ref.shape[0]
    rows = n1 * SUBLANES
    x = z_ref[...].reshape(rows, 2 * GDIM).astype(BF16)
    p = jnp.dot(k1_ref[...], x, preferred_element_type=F32)
    pc, ps = p[:rows], p[rows:]
    ar = pc[:, :GDIM] + ps[:, GDIM:]
    ai = pc[:, GDIM:] - ps[:, :GDIM]
    tc, ts = tc_ref[...], ts_ref[...]
    a_ref[:, :, :GDIM] = (ar * tc + ai * ts).reshape(n1, SUBLANES, GDIM)
    a_ref[:, :, GDIM:] = (ai * tc - ar * ts).reshape(n1, SUBLANES, GDIM)


def _fft2_kernel(a_ref, c2_ref, s2_ref, y_ref):
    for r in range(SUBLANES):
        blk = a_ref[r].astype(BF16)
        y = jnp.dot(c2_ref[...], blk[:, :GDIM], preferred_element_type=F32)
        y = y + jnp.dot(s2_ref[...], blk[:, GDIM:], preferred_element_type=F32)
        y_ref[:, r * GDIM:(r + 1) * GDIM] = y


def _split_seq(S):
    n1 = 1 << (int(math.log2(S)) // 2)
    return n1, S // n1


@functools.lru_cache(maxsize=None)
def _fft_tables(S):
    n1, n2 = _split_seq(S)
    k1 = np.arange(n1)
    ang1 = 2.0 * np.pi * np.outer(k1, np.arange(n1)) / n1
    eye = np.eye(SUBLANES)
    kmat = np.concatenate([np.kron(np.cos(ang1), eye), np.kron(np.sin(ang1), eye)], axis=0)
    nn2 = np.arange(n2).reshape(n2 // SUBLANES, 1, SUBLANES)
    angt = 2.0 * np.pi * k1.reshape(1, n1, 1) * nn2 / S
    angt = angt.reshape(n2 // SUBLANES, n1 * SUBLANES, 1)
    tc = np.broadcast_to(np.cos(angt), (n2 // SUBLANES, n1 * SUBLANES, GDIM))
    ts = np.broadcast_to(np.sin(angt), (n2 // SUBLANES, n1 * SUBLANES, GDIM))
    ang2 = 2.0 * np.pi * np.outer(np.arange(n2), np.arange(n2)) / n2
    scale = 1.0 / math.sqrt(S)
    return (kmat.astype(np.float32), np.ascontiguousarray(tc, np.float32),
            np.ascontiguousarray(ts, np.float32),
            (np.cos(ang2) * scale).astype(np.float32), (np.sin(ang2) * scale).astype(np.float32))


def _seq_dft(zg):
    B, G, S, _ = zg.shape
    n1, n2 = _split_seq(S)
    kmat, tc, ts, c2, s2 = _fft_tables(S)
    kmat = jnp.asarray(kmat, BF16)
    c2 = jnp.asarray(c2, BF16)
    s2 = jnp.asarray(s2, BF16)
    nb = n2 // SUBLANES
    z6 = zg.reshape(B, G, n1, nb, SUBLANES, 2 * GDIM)
    blk6 = (None, None, n1, None, SUBLANES, 2 * GDIM)
    a6 = pl.pallas_call(
        _fft1_kernel,
        grid=(nb, B, G),
        in_specs=[
            pl.BlockSpec(blk6, lambda j, b, g: (b, g, 0, j, 0, 0)),
            pl.BlockSpec((2 * n1 * SUBLANES, n1 * SUBLANES), lambda j, b, g: (0, 0)),
            pl.BlockSpec((None, n1 * SUBLANES, GDIM), lambda j, b, g: (j, 0, 0)),
            pl.BlockSpec((None, n1 * SUBLANES, GDIM), lambda j, b, g: (j, 0, 0)),
        ],
        out_specs=pl.BlockSpec(blk6, lambda j, b, g: (b, g, 0, j, 0, 0)),
        out_shape=jax.ShapeDtypeStruct(z6.shape, F32),
        compiler_params=_cparams(("parallel", "parallel", "parallel")),
        name="fft_stage1",
    )(z6, kmat, jnp.asarray(tc), jnp.asarray(ts))
    a5 = a6.reshape(B, G, n1, n2, 2 * GDIM)
    y = pl.pallas_call(
        _fft2_kernel,
        grid=(B, G, n1 // SUBLANES),
        in_specs=[
            pl.BlockSpec((None, None, SUBLANES, n2, 2 * GDIM), lambda b, g, i: (b, g, i, 0, 0)),
            pl.BlockSpec((n2, n2), lambda b, g, i: (0, 0)),
            pl.BlockSpec((n2, n2), lambda b, g, i: (0, 0)),
        ],
        out_specs=pl.BlockSpec((None, None, n2, SUBLANES * GDIM), lambda b, g, i: (b, g, 0, i)),
        out_shape=jax.ShapeDtypeStruct((B, G, n2, n1 * GDIM), F32),
        compiler_params=_cparams(("parallel", "parallel", "parallel")),
        name="fft_stage2",
    )(a5, c2, s2)
    return y.reshape(B, G, S, GDIM)


CONV_HALO = 16
CONV_ROWS = 16


def _conv_kernel(prev_ref, cur_ref, next_ref, w_ref, cb_ref, lg_ref, lb_ref, gc_ref, o_ref, scr):
    ts = cur_ref.shape[0]
    m = pl.program_id(1)
    last = pl.num_programs(1) - 1
    scr[0:CONV_HALO, :] = jnp.where(m > 0, prev_ref[...], 0.0)
    scr[CONV_HALO:CONV_HALO + ts, :] = cur_ref[...]
    scr[CONV_HALO + ts:, :] = jnp.where(m < last, next_ref[...], 0.0)
    off = CONV_HALO - CONV_PAD

    def step(c, carry):
        base = pl.multiple_of(c * CONV_ROWS, CONV_ROWS)
        win = scr[pl.ds(base, CONV_ROWS + 2 * CONV_HALO), :]
        acc = jnp.broadcast_to(cb_ref[...], (CONV_ROWS, D_CONV))
        for k in range(CONV_W):
            acc = acc + win[off + k:off + k + CONV_ROWS, :] * w_ref[k:k + 1, :]
        mu = jnp.mean(acc, axis=-1, keepdims=True)
        d = acc - mu
        var = jnp.mean(d * d, axis=-1, keepdims=True)
        y = d * lax.rsqrt(var + EPS) * lg_ref[...] + lb_ref[...]
        y = y * jax.nn.sigmoid(y)
        o_ref[pl.ds(base, CONV_ROWS), :] = _rms(y, gc_ref[...])
        return carry

    lax.fori_loop(0, ts // CONV_ROWS, step, 0)


def _conv_module(hg, conv_w, conv_b, ln_g, ln_b, gn_c, ts):
    B, S, _ = hg.shape
    hb = ts // CONV_HALO
    nh = S // CONV_HALO
    vec = pl.BlockSpec((1, D_CONV), lambda b, m: (0, 0))
    return pl.pallas_call(
        _conv_kernel,
        grid=(B, S // ts),
        in_specs=[
            pl.BlockSpec((None, CONV_HALO, D_CONV), lambda b, m: (b, jnp.maximum(m * hb - 1, 0), 0)),
            pl.BlockSpec((None, ts, D_CONV), lambda b, m: (b, m, 0)),
            pl.BlockSpec((None, CONV_HALO, D_CONV), lambda b, m: (b, jnp.minimum((m + 1) * hb, nh - 1), 0)),
            pl.BlockSpec((CONV_W, D_CONV), lambda b, m: (0, 0)),
            vec, vec, vec, vec,
        ],
        out_specs=pl.BlockSpec((None, ts, D_CONV), lambda b, m: (b, m, 0)),
        out_shape=jax.ShapeDtypeStruct((B, S, D_CONV), F32),
        scratch_shapes=[pltpu.VMEM((ts + 2 * CONV_HALO, D_CONV), F32)],
        compiler_params=_cparams(("parallel", "parallel")),
        name="conv_module",
    )(hg, hg, hg, conv_w, conv_b, ln_g, ln_b, gn_c)


def _extract_topk(s, order, k, payload=None):
    big = jnp.int32(1 << 30)
    vals, sel = [], []
    for _ in range(k):
        m = jnp.max(s, axis=0, keepdims=True)
        am = jnp.min(jnp.where(s == m, order, big), axis=0, keepdims=True)
        hit = order == am
        vals.append(m)
        if payload is None:
            sel.append(am)
        else:
            sel.append(jnp.max(jnp.where(hit, payload, -1), axis=0, keepdims=True))
        s = jnp.where(hit, -jnp.inf, s)
    return jnp.concatenate(vals, axis=0), jnp.concatenate(sel, axis=0)


_CAND_A = 4
_CAND_B = 3


def _route_kernel(x_ref, yf_ref, yc_ref, gf_ref, wout_ref, g2_ref, wq_ref, ka_ref, kb_ref,
                  x1_ref, htw_ref, idx_ref, gate_ref):
    tm = x_ref.shape[0]
    yf = jnp.concatenate([yf_ref[g] for g in range(GROUPS)], axis=-1)
    yfn = _rms(yf, gf_ref[...])
    ycat = jnp.concatenate([yfn, yc_ref[...]], axis=-1).astype(BF16)
    x1 = x_ref[...] + jnp.dot(ycat, wout_ref[...], preferred_element_type=F32)
    x1_ref[...] = x1
    ht = _rms(x1, g2_ref[...])
    htw_ref[...] = _pack_halves(ht)
    q = jnp.dot(ht.astype(BF16), wq_ref[...], preferred_element_type=F32)

    row16 = lax.broadcasted_iota(jnp.int32, (TOPK, tm), 0)
    key_iota = lax.broadcasted_iota(jnp.int32, (N_KEYS, tm), 0)
    nt = (((1,), (1,)), ((), ()))
    idx_rows, gate_rows = [], []
    for h in range(HEADS):
        qa = q[:, h * D_QUERY:h * D_QUERY + D_HALF].astype(BF16)
        qb = q[:, h * D_QUERY + D_HALF:(h + 1) * D_QUERY].astype(BF16)
        sa = lax.dot_general(ka_ref[h], qa, nt, preferred_element_type=F32)
        sb = lax.dot_general(kb_ref[h], qb, nt, preferred_element_type=F32)
        va, ia = _extract_topk(sa, key_iota, TOPK)
        vb, ib = _extract_topk(sb, key_iota, TOPK)
        cs, ce, co = [], [], []
        for i in range(_CAND_A):
            ok = (i + 1) * (row16 + 1) <= TOPK
            cs.append(jnp.where(ok, va[i:i + 1] + vb, -jnp.inf))
            ce.append(ia[i:i + 1] * N_KEYS + ib)
            co.append(i * TOPK + row16)
        for j in range(_CAND_B):
            ok = jnp.logical_and(row16 >= _CAND_A, (row16 + 1) * (j + 1) <= TOPK)
            cs.append(jnp.where(ok, va + vb[j:j + 1], -jnp.inf))
            ce.append(ia * N_KEYS + ib[j:j + 1])
            co.append(row16 * TOPK + j)
        top_s, top_e = _extract_topk(jnp.concatenate(cs, axis=0), jnp.concatenate(co, axis=0),
                                     TOPK, payload=jnp.concatenate(ce, axis=0))
        ex = jnp.exp(top_s - top_s[0:1])
        gate_rows.append(ex / jnp.sum(ex, axis=0, keepdims=True))
        idx_rows.append(top_e)
    idx_ref[...] = jnp.concatenate(idx_rows, axis=0).T
    gate_ref[...] = jnp.concatenate(gate_rows, axis=0).T


def _route(x, yf, yc, gf, wout_bf, g2, wq_bf, ka_bf, kb_bf, tm, s0, rows):
    B = x.shape[0]
    S = rows
    m0 = s0 // tm
    assert s0 % tm == 0 and rows % tm == 0
    const2 = lambda b, m: (0, 0)
    const3 = lambda b, m: (0, 0, 0)
    row = lambda w: pl.BlockSpec((None, tm, w), lambda b, m: (b, m, 0))
    row_in = lambda w: pl.BlockSpec((None, tm, w), lambda b, m: (b, m + m0, 0))
    return pl.pallas_call(
        _route_kernel,
        grid=(B, S // tm),
        in_specs=[
            row_in(D_MODEL),
            pl.BlockSpec((None, GROUPS, tm, GDIM), lambda b, m: (b, 0, m + m0, 0)),
            row_in(D_CONV),
            pl.BlockSpec((1, D_FOURIER), const2),
            pl.BlockSpec((D_MODEL, D_MODEL), const2),
            pl.BlockSpec((1, D_MODEL), const2),
            pl.BlockSpec((D_MODEL, HEADS * D_QUERY), const2),
            pl.BlockSpec((HEADS, N_KEYS, D_HALF), const3),
            pl.BlockSpec((HEADS, N_KEYS, D_HALF), const3),
        ],
        out_specs=[row(D_MODEL), row(D_WORDS), row(N_SEL), row(N_SEL)],
        out_shape=[
            jax.ShapeDtypeStruct((B, S, D_MODEL), F32),
            jax.ShapeDtypeStruct((B, S, D_WORDS), jnp.int32),
            jax.ShapeDtypeStruct((B, S, N_SEL), jnp.int32),
            jax.ShapeDtypeStruct((B, S, N_SEL), F32),
        ],
        compiler_params=_cparams(("parallel", "parallel")),
        name="route",
    )(x, yf, yc, gf, wout_bf, g2, wq_bf, ka_bf, kb_bf)


SC_UROWS = 64
SC_UNITS = N_SEL // SC_UROWS
SC_NBUF = 3
SC_TOKB = 16
SC_WCH = D_WORDS // SC_LANES
SC_QUAD = 4
SC_KG = 8
SC_COST_WEIGHT = 16


def _sc_mesh():
    return plsc.VectorSubcoreMesh(core_axis_name="c", subcore_axis_name="s",
                                  num_cores=SC_CORES, num_subcores=SC_SUBCORES)


def _sc_cost(tokens):
    rows = tokens * N_SEL * SC_COST_WEIGHT
    return pl.CostEstimate(flops=2 * rows * D_MODEL, transcendentals=0, bytes_accessed=rows * D_WORDS * 4)


def _sc_bf16(words):
    return plsc.bitcast(words, BF16)


def _sc_widen(pairs):
    w = plsc.bitcast(pairs, jnp.int32)
    lo = plsc.bitcast(lax.shift_left(w, 16), F32)
    hi = plsc.bitcast(lax.bitwise_and(w, jnp.int32(HI_MASK)), F32)
    return lo, hi


def _sc_quad_sum(row_chunk, scale):
    part = None
    for u in range(SC_QUAD):
        prod = _sc_bf16(row_chunk(u)) * scale(u)
        part = prod if part is None else part + prod
    return part


def _tree_sum(xs):
    while len(xs) > 1:
        xs = [xs[i] + xs[i + 1] for i in range(0, len(xs), 2)]
    return xs[0]


def _sc_token_loop(table_hbm, idx_hbm, side_hbm, out_hbm, idx_v, side_v, rows_v, out_v, gsem, osem, bsem,
                   tokens_per_worker, compute):
    wid = lax.axis_index("s") * SC_CORES + lax.axis_index("c")
    base = wid * tokens_per_worker
    n_batches = tokens_per_worker // SC_TOKB

    def batch_copies(bi, bslot):
        src = pl.ds(base + bi * SC_TOKB, SC_TOKB)
        return (pltpu.make_async_copy(idx_hbm.at[src], idx_v.at[bslot], bsem.at[2 * bslot]),
                pltpu.make_async_copy(side_hbm.at[src], side_v.at[bslot], bsem.at[2 * bslot + 1]))

    def gather(t, unit, buf):
        rows = idx_v.at[lax.rem(t // SC_TOKB, 2), lax.rem(t, SC_TOKB), pl.ds(unit * SC_UROWS, SC_UROWS)]
        return pltpu.make_async_copy(table_hbm.at[rows], rows_v.at[buf], gsem.at[buf])

    def writeback(tok, slot):
        return pltpu.make_async_copy(out_v.at[slot], out_hbm.at[tok], osem.at[slot])

    for c in batch_copies(0, 0):
        c.start()
    for c in batch_copies(0, 0):
        c.wait()
    if n_batches > 1:
        for c in batch_copies(1, 1):
            c.start()
    for unit in range(SC_UNITS):
        gather(0, unit, unit).start()

    @pl.loop(0, tokens_per_worker)
    def _(t):
        slot = lax.rem(t, 2)
        bi = t // SC_TOKB
        bslot = lax.rem(bi, 2)
        tl = lax.rem(t, SC_TOKB)
        nxt = t + 1

        @pl.when(t >= 2)
        def _():
            writeback(base + t - 2, slot).wait()

        for unit in range(SC_UNITS):
            u = t * SC_UNITS + unit
            buf = lax.rem(u, SC_NBUF)
            gather(t, unit, buf).wait()

            if unit == 0:
                @pl.when(jnp.logical_and(nxt < tokens_per_worker, lax.rem(nxt, SC_TOKB) == 0))
                def _():
                    for c in batch_copies(nxt // SC_TOKB, lax.rem(nxt // SC_TOKB, 2)):
                        c.wait()

            @pl.when(nxt < tokens_per_worker)
            def _():
                gather(nxt, unit, lax.rem(u + SC_UNITS, SC_NBUF)).start()

            compute(bslot, tl, unit, buf, slot)
        writeback(base + t, slot).start()

        @pl.when(jnp.logical_and(tl == SC_TOKB - 1, bi + 2 < n_batches))
        def _():
            for c in batch_copies(bi + 2, bslot):
                c.start()

    for t in range(tokens_per_worker - 2, tokens_per_worker):
        writeback(base + t, t % 2).wait()


def _peer_down(idx, htw, down_w):
    T = htw.shape[0]
    tpw = T // SC_WORKERS
    L = SC_LANES

    @functools.partial(
        pl.kernel, mesh=_sc_mesh(),
        out_type=jax.ShapeDtypeStruct((T, N_SEL), F32),
        scratch_types=[
            pltpu.VMEM((2, SC_TOKB, N_SEL), jnp.int32),
            pltpu.VMEM((2, SC_TOKB, D_WORDS), jnp.int32),
            pltpu.VMEM((SC_NBUF, SC_UROWS, D_WORDS), jnp.int32),
            pltpu.VMEM((2, N_SEL), F32),
            pltpu.SemaphoreType.DMA((SC_NBUF,)),
            pltpu.SemaphoreType.DMA((2,)),
            pltpu.SemaphoreType.DMA((4,)),
        ],
        compiler_params=pltpu.CompilerParams(needs_layout_passes=False),
        name="peer_down", cost_estimate=_sc_cost(T),
    )
    def k(idx_hbm, h_hbm, down_hbm, s_hbm, idx_v, h_v, rows_v, s_v, gsem, osem, bsem):
        lane = lax.iota(jnp.int32, L)

        def compute(bslot, tl, unit, buf, slot):
            svec = jnp.full((L,), slot, jnp.int32)
            hs = [_sc_bf16(h_v[bslot, tl, pl.ds(w * L, L)]) for w in range(SC_WCH)]

            @plsc.parallel_loop(0, SC_UROWS, unroll=2)
            def _(r):
                parts = []
                for q in range(SC_WCH // SC_QUAD):
                    part = _sc_quad_sum(lambda u: rows_v[buf, r, pl.ds((q * SC_QUAD + u) * L, L)],
                                        lambda u: hs[q * SC_QUAD + u])
                    parts.extend(_sc_widen(part))
                total = plsc.cumsum(_tree_sum(parts))
                col = jnp.full((L,), unit * SC_UROWS, jnp.int32) + r
                plsc.store_scatter(s_v, [svec, col], total, mask=lane == L - 1)

        _sc_token_loop(down_hbm, idx_hbm, h_hbm, s_hbm, idx_v, h_v, rows_v, s_v, gsem, osem, bsem, tpw, compute)

    return k(idx, htw, down_w)


def _peer_up(idx, act_w, up_w):
    T = act_w.shape[0]
    tpw = T // SC_WORKERS
    L = SC_LANES

    @functools.partial(
        pl.kernel, mesh=_sc_mesh(),
        out_type=jax.ShapeDtypeStruct((T, D_MODEL), F32),
        scratch_types=[
            pltpu.VMEM((2, SC_TOKB, N_SEL), jnp.int32),
            pltpu.VMEM((2, SC_TOKB, N_SEL), jnp.int32),
            pltpu.VMEM((SC_NBUF, SC_UROWS, D_WORDS), jnp.int32),
            pltpu.VMEM((2, D_MODEL), F32),
            pltpu.SemaphoreType.DMA((SC_NBUF,)),
            pltpu.SemaphoreType.DMA((2,)),
            pltpu.SemaphoreType.DMA((4,)),
        ],
        compiler_params=pltpu.CompilerParams(needs_layout_passes=False),
        name="peer_up", cost_estimate=_sc_cost(T),
    )
    def k(idx_hbm, act_hbm, up_hbm, o_hbm, idx_v, act_v, rows_v, o_v, gsem, osem, bsem):
        def compute(bslot, tl, unit, buf, slot):
            for kg in range(SC_WCH // SC_KG):
                lo_cols = [pl.ds((kg * SC_KG + kk) * L, L) for kk in range(SC_KG)]
                hi_cols = [pl.ds(D_WORDS + (kg * SC_KG + kk) * L, L) for kk in range(SC_KG)]
                if unit == 0:
                    init = tuple(jnp.zeros((L,), F32) for _ in range(2 * SC_KG))
                else:
                    init = tuple(o_v[slot, col] for col in lo_cols + hi_cols)

                def rgbody(rg, accs):
                    per_vec = L // SC_QUAD
                    a16 = act_v[bslot, tl, pl.ds(unit * SC_UROWS + (rg // per_vec) * L, L)]
                    lane0 = jnp.full((L,), 0, jnp.int32) + lax.rem(rg, per_vec) * SC_QUAD
                    a = [_sc_bf16(a16.at[lane0 + u].get(mode="promise_in_bounds")) for u in range(SC_QUAD)]
                    out_lo, out_hi = [], []
                    for kk in range(SC_KG):
                        part = _sc_quad_sum(lambda u: rows_v[buf, rg * SC_QUAD + u, lo_cols[kk]], lambda u: a[u])
                        lo, hi = _sc_widen(part)
                        out_lo.append(accs[kk] + lo)
                        out_hi.append(accs[SC_KG + kk] + hi)
                    return tuple(out_lo + out_hi)
                accs = lax.fori_loop(0, SC_UROWS // SC_QUAD, rgbody, init)
                for kk in range(SC_KG):
                    o_v[slot, lo_cols[kk]] = accs[kk]
                    o_v[slot, hi_cols[kk]] = accs[SC_KG + kk]

        _sc_token_loop(up_hbm, idx_hbm, act_hbm, o_hbm, idx_v, act_v, rows_v, o_v, gsem, osem, bsem, tpw, compute)

    return k(idx, act_w, up_w)


def _act_kernel(s_ref, gate_ref, a_ref):
    s = s_ref[...]
    act = 0.5 * s * (1.0 + lax.erf(s * math.sqrt(0.5))) * gate_ref[...]
    bits = _bf16_bits(act)
    a_ref[...] = lax.bitwise_or(lax.shift_right_logical(bits, 16), bits)


def _act(s, gate, tm):
    T = s.shape[0]
    spec = pl.BlockSpec((tm, N_SEL), lambda i: (i, 0))
    return pl.pallas_call(
        _act_kernel, grid=(T // tm,), in_specs=[spec, spec], out_specs=spec,
        out_shape=jax.ShapeDtypeStruct((T, N_SEL), jnp.int32),
        compiler_params=_cparams(("parallel",)), name="expert_act",
    )(s, gate)


def _final_kernel(x1_ref, p_ref, g_ref, *rest):
    rest[-1][...] = _rms(x1_ref[...] + p_ref[...], g_ref[...])


def _final_into(out, out_shape, x1, peer, g, b0, nb, s0, rows):
    tm = _tile(rows, 512)
    assert s0 % tm == 0
    m0 = s0 // tm
    tok = pl.BlockSpec((None, tm, D_MODEL), lambda b, m: (b, m, 0))
    in_specs = [tok, tok, pl.BlockSpec((1, D_MODEL), lambda b, m: (0, 0))]
    args = [x1.reshape(nb, rows, D_MODEL), peer.reshape(nb, rows, D_MODEL), g]
    aliases = {}
    if out is not None:
        in_specs.append(pl.BlockSpec(memory_space=pl.ANY))
        args.append(out)
        aliases = {3: 0}
    return pl.pallas_call(
        _final_kernel, grid=(nb, rows // tm), in_specs=in_specs,
        out_specs=pl.BlockSpec((None, tm, D_MODEL), lambda b, m: (b + b0, m + m0, 0)),
        out_shape=jax.ShapeDtypeStruct(out_shape, F32), input_output_aliases=aliases,
        compiler_params=_cparams(("parallel", "parallel")), name="final_norm",
    )(*args)


PROMPT_PLAN = ((1, 2),) + ((1, 1),) * 7
SAMPLE_PLAN = ((1, 2),)
SKEW = 2
EARLY_FRONT_SLOT = 4
EARLY_CONV_SLOT = 6


def _tile(n, want):
    t = min(n, want)
    assert n % t == 0
    return t


def _channel_dft():
    ang = 2.0 * np.pi * np.outer(np.arange(GDIM), np.arange(GDIM)) / GDIM
    m = np.concatenate([np.cos(ang), -np.sin(ang)], axis=1) / math.sqrt(GDIM)
    return jnp.asarray(m.astype(np.float32), BF16)


def _plan_units(x, plan, out_id):
    units = []
    b = 0
    for nb, parts in plan:
        xb = x[b:b + nb]
        rows = x.shape[1] // parts
        units += [(xb, i * rows, rows, out_id, b) for i in range(parts)]
        b += nb
    assert b == x.shape[0]
    return units


def _mixer_front(x, p):
    zg, hg = _inproj(x, p["g1"], p["win"], p["cdft"], _tile(x.shape[1], 512))
    return _seq_dft(zg), hg


def _mixer_conv(hg, p):
    return _conv_module(hg, p["conv_w"], p["conv_b"], p["ln_g"], p["ln_b"], p["gn_c"], _tile(hg.shape[1], 512))


def _mixer(x, p):
    yf, hg = _mixer_front(x, p)
    return yf, _mixer_conv(hg, p)


def _route_unit(x, mixed, s0, rows, p):
    T = x.shape[0] * rows
    x1, htw, idx, gate = _route(x, mixed[0], mixed[1], p["gn_f"], p["wout"], p["g2"], p["wq"], p["ka"],
                                p["kb"], _tile(rows, 256), s0, rows)
    return x1.reshape(T, D_MODEL), htw.reshape(T, D_WORDS), idx.reshape(T, N_SEL), gate.reshape(T, N_SEL)


def _encode(units, out_shapes, early_x, p):
    x1s, idxs, acts, peers = [], [], [], []
    early_front, early_conv = [_mixer_front(early_x, p)], [None]
    mixed, mixed_x = None, None
    for g, (x, s0, rows, _, _) in enumerate(units):
        if x is early_x and mixed_x is not x:
            mixed, mixed_x = (early_front[0][0], early_conv[0]), x
        reuse = mixed_x is x
        carrier = mixed if reuse else x
        slots = [(acts, g - SKEW)] if g >= SKEW else []
        if g >= SKEW + 1:
            slots.append((peers, g - SKEW - 1))
        if g == EARLY_FRONT_SLOT:
            slots.append((early_front, 0))
        if g == EARLY_CONV_SLOT:
            slots.append((early_conv, 0))
        if slots:
            carrier, *tied = lax.optimization_barrier((carrier, *[lst[i] for lst, i in slots]))
            for (lst, i), v in zip(slots, tied):
                lst[i] = v
        if g == EARLY_FRONT_SLOT:
            early_conv[0] = _mixer_conv(early_front[0][1], p)
        if g >= SKEW:
            peers.append(_peer_up(idxs[g - SKEW], acts[g - SKEW], p["up"]))
        if reuse:
            mixed = carrier
        else:
            x = carrier
            mixed, mixed_x = _mixer(x, p), units[g][0]
        assert (x.shape[0] * rows) % (SC_WORKERS * SC_TOKB) == 0
        x1, htw, idx, gate = _route_unit(x, mixed, s0, rows, p)
        s = _peer_down(idx, htw, p["down"])
        x1s.append(x1)
        idxs.append(idx)
        acts.append(_act(s, gate, _tile(x1.shape[0], 1024)))
    for g in range(len(peers), len(units)):
        peers.append(_peer_up(idxs[g], acts[g], p["up"]))
    outs = {}
    for (x, s0, rows, out_id, b0), x1, peer in zip(units, x1s, peers):
        outs[out_id] = _final_into(outs.get(out_id), out_shapes[out_id], x1, peer, p["final_g"],
                                   b0, x.shape[0], s0, rows)
    return outs


def kernel(x_prompt, x_sample, norm1_g, w_in, conv_w, conv_b, conv_ln_g, conv_ln_b, gn_fourier_g,
           gn_conv_g, w_out, norm2_g, w_q, keys_a, keys_b, expert_down, expert_up, final_g):
    assert norm1_g.shape[0] == 1
    p = dict(
        g1=norm1_g[0].reshape(1, D_MODEL), win=w_in[0].astype(BF16), cdft=_channel_dft(),
        conv_w=conv_w[0], conv_b=conv_b[0].reshape(1, D_CONV),
        ln_g=conv_ln_g[0].reshape(1, D_CONV), ln_b=conv_ln_b[0].reshape(1, D_CONV),
        gn_f=gn_fourier_g[0].reshape(1, D_FOURIER), gn_c=gn_conv_g[0].reshape(1, D_CONV),
        wout=w_out[0].astype(BF16), g2=norm2_g[0].reshape(1, D_MODEL), wq=w_q[0].astype(BF16),
        ka=keys_a[0].astype(BF16), kb=keys_b[0].astype(BF16),
        down=_pack_halves(expert_down[0]), up=_pack_halves(expert_up[0]),
        final_g=final_g.reshape(1, D_MODEL),
    )
    units = _plan_units(x_prompt, PROMPT_PLAN, 0) + _plan_units(x_sample, SAMPLE_PLAN, 1)
    outs = _encode(units, (x_prompt.shape, x_sample.shape), x_sample, p)
    return outs[0], outs[1]
```

```python
import functools
import math

import numpy as np
import jax
import jax.numpy as jnp
from jax import lax
from jax.experimental import pallas as pl
from jax.experimental.pallas import tpu as pltpu
from jax.experimental.pallas import tpu_sc as plsc

F32 = jnp.float32
BF16 = jnp.bfloat16

D_MODEL = 1024
D_FOURIER = 512
GROUPS = 4
GDIM = 128
D_CONV = 512
CONV_W = 31
CONV_PAD = 15
HEADS = 8
N_KEYS = 128
TOPK = 16
D_HALF = 128
D_QUERY = 256
N_SEL = HEADS * TOPK
EPS = 1e-6

SUBLANES = 8
LANES = 128
VMEM_LIMIT = 56 * 1024 * 1024

SC_CORES = 2
SC_SUBCORES = 16
SC_LANES = 16
SC_WORKERS = SC_CORES * SC_SUBCORES


def _cparams(sem):
    return pltpu.CompilerParams(dimension_semantics=sem, vmem_limit_bytes=VMEM_LIMIT)


def _rms(x, g):
    return x * lax.rsqrt(jnp.mean(x * x, axis=-1, keepdims=True) + EPS) * g


D_WORDS = D_MODEL // 2
HI_MASK = -65536


def _bf16_bits(x):
    return lax.bitcast_convert_type(x.astype(BF16).astype(F32), jnp.int32)


def _pack_halves(x):
    lo = lax.shift_right_logical(_bf16_bits(x[..., :D_WORDS]), 16)
    hi = lax.bitwise_and(_bf16_bits(x[..., D_WORDS:]), HI_MASK)
    return lax.bitwise_or(lo, hi)


def _inproj_kernel(x_ref, g1_ref, win_ref, cdft_ref, zg_ref, hg_ref):
    h = _rms(x_ref[...], g1_ref[...]).astype(BF16)
    z = jnp.dot(h, win_ref[...], preferred_element_type=F32)
    for g in range(GROUPS):
        zg = z[:, g * GDIM:(g + 1) * GDIM].astype(BF16)
        zg_ref[g] = jnp.dot(zg, cdft_ref[...], preferred_element_type=F32)
    a = z[:, D_FOURIER:D_FOURIER + D_CONV]
    gate = z[:, D_FOURIER + D_CONV:]
    hg_ref[...] = a * jax.nn.sigmoid(gate)


def _inproj(x, g1, win_bf, cdft, tm):
    B, S, _ = x.shape
    return pl.pallas_call(
        _inproj_kernel,
        grid=(B, S // tm),
        in_specs=[
            pl.BlockSpec((None, tm, D_MODEL), lambda b, m: (b, m, 0)),
            pl.BlockSpec((1, D_MODEL), lambda b, m: (0, 0)),
            pl.BlockSpec((D_MODEL, D_FOURIER + 2 * D_CONV), lambda b, m: (0, 0)),
            pl.BlockSpec((GDIM, 2 * GDIM), lambda b, m: (0, 0)),
        ],
        out_specs=[
            pl.BlockSpec((None, GROUPS, tm, 2 * GDIM), lambda b, m: (b, 0, m, 0)),
            pl.BlockSpec((None, tm, D_CONV), lambda b, m: (b, m, 0)),
        ],
        out_shape=[
            jax.ShapeDtypeStruct((B, GROUPS, S, 2 * GDIM), F32),
            jax.ShapeDtypeStruct((B, S, D_CONV), F32),
        ],
        compiler_params=_cparams(("parallel", "parallel")),
        name="inproj",
    )(x, g1, win_bf, cdft)


def _fft1_kernel(z_ref, k1_ref, tc_ref, ts_ref, a_ref):
    n1 = z_ref.shape[0]
    rows = n1 * SUBLANES
    x = z_ref[...].reshape(rows, 2 * GDIM).astype(BF16)
    p = jnp.dot(k1_ref[...], x, preferred_element_type=F32)
    pc, ps = p[:rows], p[rows:]
    ar = pc[:, :GDIM] + ps[:, GDIM:]
    ai = pc[:, GDIM:] - ps[:, :GDIM]
    tc, ts = tc_ref[...], ts_ref[...]
    a_ref[:, :, :GDIM] = (ar * tc + ai * ts).reshape(n1, SUBLANES, GDIM)
    a_ref[:, :, GDIM:] = (ai * tc - ar * ts).reshape(n1, SUBLANES, GDIM)


def _fft2_kernel(a_ref, c2_ref, s2_ref, y_ref):
    for r in range(SUBLANES):
        blk = a_ref[r].astype(BF16)
        y = jnp.dot(c2_ref[...], blk[:, :GDIM], preferred_element_type=F32)
        y = y + jnp.dot(s2_ref[...], blk[:, GDIM:], preferred_element_type=F32)
        y_ref[:, r * GDIM:(r + 1) * GDIM] = y


def _split_seq(S):
    n1 = 1 << (int(math.log2(S)) // 2)
    return n1, S // n1


@functools.lru_cache(maxsize=None)
def _fft_tables(S):
    n1, n2 = _split_seq(S)
    k1 = np.arange(n1)
    ang1 = 2.0 * np.pi * np.outer(k1, np.arange(n1)) / n1
    eye = np.eye(SUBLANES)
    kmat = np.concatenate([np.kron(np.cos(ang1), eye), np.kron(np.sin(ang1), eye)], axis=0)
    nn2 = np.arange(n2).reshape(n2 // SUBLANES, 1, SUBLANES)
    angt = 2.0 * np.pi * k1.reshape(1, n1, 1) * nn2 / S
    angt = angt.reshape(n2 // SUBLANES, n1 * SUBLANES, 1)
    tc = np.broadcast_to(np.cos(angt), (n2 // SUBLANES, n1 * SUBLANES, GDIM))
    ts = np.broadcast_to(np.sin(angt), (n2 // SUBLANES, n1 * SUBLANES, GDIM))
    ang2 = 2.0 * np.pi * np.outer(np.arange(n2), np.arange(n2)) / n2
    scale = 1.0 / math.sqrt(S)
    return (kmat.astype(np.float32), np.ascontiguousarray(tc, np.float32),
            np.ascontiguousarray(ts, np.float32),
            (np.cos(ang2) * scale).astype(np.float32), (np.sin(ang2) * scale).astype(np.float32))


def _seq_dft(zg):
    B, G, S, _ = zg.shape
    n1, n2 = _split_seq(S)
    kmat, tc, ts, c2, s2 = _fft_tables(S)
    kmat = jnp.asarray(kmat, BF16)
    c2 = jnp.asarray(c2, BF16)
    s2 = jnp.asarray(s2, BF16)
    nb = n2 // SUBLANES
    z6 = zg.reshape(B, G, n1, nb, SUBLANES, 2 * GDIM)
    blk6 = (None, None, n1, None, SUBLANES, 2 * GDIM)
    a6 = pl.pallas_call(
        _fft1_kernel,
        grid=(nb, B, G),
        in_specs=[
            pl.BlockSpec(blk6, lambda j, b, g: (b, g, 0, j, 0, 0)),
            pl.BlockSpec((2 * n1 * SUBLANES, n1 * SUBLANES), lambda j, b, g: (0, 0)),
            pl.BlockSpec((None, n1 * SUBLANES, GDIM), lambda j, b, g: (j, 0, 0)),
            pl.BlockSpec((None, n1 * SUBLANES, GDIM), lambda j, b, g: (j, 0, 0)),
        ],
        out_specs=pl.BlockSpec(blk6, lambda j, b, g: (b, g, 0, j, 0, 0)),
        out_shape=jax.ShapeDtypeStruct(z6.shape, F32),
        compiler_params=_cparams(("parallel", "parallel", "parallel")),
        name="fft_stage1",
    )(z6, kmat, jnp.asarray(tc), jnp.asarray(ts))
    a5 = a6.reshape(B, G, n1, n2, 2 * GDIM)
    y = pl.pallas_call(
        _fft2_kernel,
        grid=(B, G, n1 // SUBLANES),
        in_specs=[
            pl.BlockSpec((None, None, SUBLANES, n2, 2 * GDIM), lambda b, g, i: (b, g, i, 0, 0)),
            pl.BlockSpec((n2, n2), lambda b, g, i: (0, 0)),
            pl.BlockSpec((n2, n2), lambda b, g, i: (0, 0)),
        ],
        out_specs=pl.BlockSpec((None, None, n2, SUBLANES * GDIM), lambda b, g, i: (b, g, 0, i)),
        out_shape=jax.ShapeDtypeStruct((B, G, n2, n1 * GDIM), F32),
        compiler_params=_cparams(("parallel", "parallel", "parallel")),
        name="fft_stage2",
    )(a5, c2, s2)
    return y.reshape(B, G, S, GDIM)


CONV_HALO = 16
CONV_ROWS = 16


def _conv_kernel(prev_ref, cur_ref, next_ref, w_ref, cb_ref, lg_ref, lb_ref, gc_ref, o_ref, scr):
    ts = cur_ref.shape[0]
    m = pl.program_id(1)
    last = pl.num_programs(1) - 1
    scr[0:CONV_HALO, :] = jnp.where(m > 0, prev_ref[...], 0.0)
    scr[CONV_HALO:CONV_HALO + ts, :] = cur_ref[...]
    scr[CONV_HALO + ts:, :] = jnp.where(m < last, next_ref[...], 0.0)
    off = CONV_HALO - CONV_PAD

    def step(c, carry):
        base = pl.multiple_of(c * CONV_ROWS, CONV_ROWS)
        win = scr[pl.ds(base, CONV_ROWS + 2 * CONV_HALO), :]
        acc = jnp.broadcast_to(cb_ref[...], (CONV_ROWS, D_CONV))
        for k in range(CONV_W):
            acc = acc + win[off + k:off + k + CONV_ROWS, :] * w_ref[k:k + 1, :]
        mu = jnp.mean(acc, axis=-1, keepdims=True)
        d = acc - mu
        var = jnp.mean(d * d, axis=-1, keepdims=True)
        y = d * lax.rsqrt(var + EPS) * lg_ref[...] + lb_ref[...]
        y = y * jax.nn.sigmoid(y)
        o_ref[pl.ds(base, CONV_ROWS), :] = _rms(y, gc_ref[...])
        return carry

    lax.fori_loop(0, ts // CONV_ROWS, step, 0)


def _conv_module(hg, conv_w, conv_b, ln_g, ln_b, gn_c, ts):
    B, S, _ = hg.shape
    hb = ts // CONV_HALO
    nh = S // CONV_HALO
    vec = pl.BlockSpec((1, D_CONV), lambda b, m: (0, 0))
    return pl.pallas_call(
        _conv_kernel,
        grid=(B, S // ts),
        in_specs=[
            pl.BlockSpec((None, CONV_HALO, D_CONV), lambda b, m: (b, jnp.maximum(m * hb - 1, 0), 0)),
            pl.BlockSpec((None, ts, D_CONV), lambda b, m: (b, m, 0)),
            pl.BlockSpec((None, CONV_HALO, D_CONV), lambda b, m: (b, jnp.minimum((m + 1) * hb, nh - 1), 0)),
            pl.BlockSpec((CONV_W, D_CONV), lambda b, m: (0, 0)),
            vec, vec, vec, vec,
        ],
        out_specs=pl.BlockSpec((None, ts, D_CONV), lambda b, m: (b, m, 0)),
        out_shape=jax.ShapeDtypeStruct((B, S, D_CONV), F32),
        scratch_shapes=[pltpu.VMEM((ts + 2 * CONV_HALO, D_CONV), F32)],
        compiler_params=_cparams(("parallel", "parallel")),
        name="conv_module",
    )(hg, hg, hg, conv_w, conv_b, ln_g, ln_b, gn_c)


def _extract_topk(s, order, k, payload=None):
    big = jnp.int32(1 << 30)
    vals, sel = [], []
    for _ in range(k):
        m = jnp.max(s, axis=0, keepdims=True)
        am = jnp.min(jnp.where(s == m, order, big), axis=0, keepdims=True)
        hit = order == am
        vals.append(m)
        if payload is None:
            sel.append(am)
        else:
            sel.append(jnp.max(jnp.where(hit, payload, -1), axis=0, keepdims=True))
        s = jnp.where(hit, -jnp.inf, s)
    return jnp.concatenate(vals, axis=0), jnp.concatenate(sel, axis=0)


_CAND_A = 4
_CAND_B = 3


def _route_kernel(x_ref, yf_ref, yc_ref, gf_ref, wout_ref, g2_ref, wq_ref, ka_ref, kb_ref,
                  x1_ref, htw_ref, idx_ref, gate_ref):
    tm = x_ref.shape[0]
    yf = jnp.concatenate([yf_ref[g] for g in range(GROUPS)], axis=-1)
    yfn = _rms(yf, gf_ref[...])
    ycat = jnp.concatenate([yfn, yc_ref[...]], axis=-1).astype(BF16)
    x1 = x_ref[...] + jnp.dot(ycat, wout_ref[...], preferred_element_type=F32)
    x1_ref[...] = x1
    ht = _rms(x1, g2_ref[...])
    htw_ref[...] = _pack_halves(ht)
    q = jnp.dot(ht.astype(BF16), wq_ref[...], preferred_element_type=F32)

    row16 = lax.broadcasted_iota(jnp.int32, (TOPK, tm), 0)
    key_iota = lax.broadcasted_iota(jnp.int32, (N_KEYS, tm), 0)
    nt = (((1,), (1,)), ((), ()))
    idx_rows, gate_rows = [], []
    for h in range(HEADS):
        qa = q[:, h * D_QUERY:h * D_QUERY + D_HALF].astype(BF16)
        qb = q[:, h * D_QUERY + D_HALF:(h + 1) * D_QUERY].astype(BF16)
        sa = lax.dot_general(ka_ref[h], qa, nt, preferred_element_type=F32)
        sb = lax.dot_general(kb_ref[h], qb, nt, preferred_element_type=F32)
        va, ia = _extract_topk(sa, key_iota, TOPK)
        vb, ib = _extract_topk(sb, key_iota, TOPK)
        cs, ce, co = [], [], []
        for i in range(_CAND_A):
            ok = (i + 1) * (row16 + 1) <= TOPK
            cs.append(jnp.where(ok, va[i:i + 1] + vb, -jnp.inf))
            ce.append(ia[i:i + 1] * N_KEYS + ib)
            co.append(i * TOPK + row16)
        for j in range(_CAND_B):
            ok = jnp.logical_and(row16 >= _CAND_A, (row16 + 1) * (j + 1) <= TOPK)
            cs.append(jnp.where(ok, va + vb[j:j + 1], -jnp.inf))
            ce.append(ia * N_KEYS + ib[j:j + 1])
            co.append(row16 * TOPK + j)
        top_s, top_e = _extract_topk(jnp.concatenate(cs, axis=0), jnp.concatenate(co, axis=0),
                                     TOPK, payload=jnp.concatenate(ce, axis=0))
        ex = jnp.exp(top_s - top_s[0:1])
        gate_rows.append(ex / jnp.sum(ex, axis=0, keepdims=True))
        idx_rows.append(top_e)
    idx_ref[...] = jnp.concatenate(idx_rows, axis=0).T
    gate_ref[...] = jnp.concatenate(gate_rows, axis=0).T


def _route(x, yf, yc, gf, wout_bf, g2, wq_bf, ka_bf, kb_bf, tm, s0, rows):
    B = x.shape[0]
    S = rows
    m0 = s0 // tm
    assert s0 % tm == 0 and rows % tm == 0
    const2 = lambda b, m: (0, 0)
    const3 = lambda b, m: (0, 0, 0)
    row = lambda w: pl.BlockSpec((None, tm, w), lambda b, m: (b, m, 0))
    row_in = lambda w: pl.BlockSpec((None, tm, w), lambda b, m: (b, m + m0, 0))
    return pl.pallas_call(
        _route_kernel,
        grid=(B, S // tm),
        in_specs=[
            row_in(D_MODEL),
            pl.BlockSpec((None, GROUPS, tm, GDIM), lambda b, m: (b, 0, m + m0, 0)),
            row_in(D_CONV),
            pl.BlockSpec((1, D_FOURIER), const2),
            pl.BlockSpec((D_MODEL, D_MODEL), const2),
            pl.BlockSpec((1, D_MODEL), const2),
            pl.BlockSpec((D_MODEL, HEADS * D_QUERY), const2),
            pl.BlockSpec((HEADS, N_KEYS, D_HALF), const3),
            pl.BlockSpec((HEADS, N_KEYS, D_HALF), const3),
        ],
        out_specs=[row(D_MODEL), row(D_WORDS), row(N_SEL), row(N_SEL)],
        out_shape=[
            jax.ShapeDtypeStruct((B, S, D_MODEL), F32),
            jax.ShapeDtypeStruct((B, S, D_WORDS), jnp.int32),
            jax.ShapeDtypeStruct((B, S, N_SEL), jnp.int32),
            jax.ShapeDtypeStruct((B, S, N_SEL), F32),
        ],
        compiler_params=_cparams(("parallel", "parallel")),
        name="route",
    )(x, yf, yc, gf, wout_bf, g2, wq_bf, ka_bf, kb_bf)


SC_UROWS = 64
SC_UNITS = N_SEL // SC_UROWS
SC_NBUF = 3
SC_TOKB = 16
SC_WCH = D_WORDS // SC_LANES
SC_QUAD = 4
SC_KG = 8
SC_COST_WEIGHT = 16


def _sc_mesh():
    return plsc.VectorSubcoreMesh(core_axis_name="c", subcore_axis_name="s",
                                  num_cores=SC_CORES, num_subcores=SC_SUBCORES)


def _sc_cost(tokens):
    rows = tokens * N_SEL * SC_COST_WEIGHT
    return pl.CostEstimate(flops=2 * rows * D_MODEL, transcendentals=0, bytes_accessed=rows * D_WORDS * 4)


def _sc_bf16(words):
    return plsc.bitcast(words, BF16)


def _sc_widen(pairs):
    w = plsc.bitcast(pairs, jnp.int32)
    lo = plsc.bitcast(lax.shift_left(w, 16), F32)
    hi = plsc.bitcast(lax.bitwise_and(w, jnp.int32(HI_MASK)), F32)
    return lo, hi


def _sc_quad_sum(row_chunk, scale):
    part = None
    for u in range(SC_QUAD):
        prod = _sc_bf16(row_chunk(u)) * scale(u)
        part = prod if part is None else part + prod
    return part


def _tree_sum(xs):
    while len(xs) > 1:
        xs = [xs[i] + xs[i + 1] for i in range(0, len(xs), 2)]
    return xs[0]


def _sc_token_loop(table_hbm, idx_hbm, side_hbm, out_hbm, idx_v, side_v, rows_v, out_v, gsem, osem, bsem,
                   tokens_per_worker, compute):
    wid = lax.axis_index("s") * SC_CORES + lax.axis_index("c")
    base = wid * tokens_per_worker
    n_batches = tokens_per_worker // SC_TOKB

    def batch_copies(bi, bslot):
        src = pl.ds(base + bi * SC_TOKB, SC_TOKB)
        return (pltpu.make_async_copy(idx_hbm.at[src], idx_v.at[bslot], bsem.at[2 * bslot]),
                pltpu.make_async_copy(side_hbm.at[src], side_v.at[bslot], bsem.at[2 * bslot + 1]))

    def gather(t, unit, buf):
        rows = idx_v.at[lax.rem(t // SC_TOKB, 2), lax.rem(t, SC_TOKB), pl.ds(unit * SC_UROWS, SC_UROWS)]
        return pltpu.make_async_copy(table_hbm.at[rows], rows_v.at[buf], gsem.at[buf])

    def writeback(tok, slot):
        return pltpu.make_async_copy(out_v.at[slot], out_hbm.at[tok], osem.at[slot])

    for c in batch_copies(0, 0):
        c.start()
    for c in batch_copies(0, 0):
        c.wait()
    if n_batches > 1:
        for c in batch_copies(1, 1):
            c.start()
    for unit in range(SC_UNITS):
        gather(0, unit, unit).start()

    @pl.loop(0, tokens_per_worker)
    def _(t):
        slot = lax.rem(t, 2)
        bi = t // SC_TOKB
        bslot = lax.rem(bi, 2)
        tl = lax.rem(t, SC_TOKB)
        nxt = t + 1

        @pl.when(t >= 2)
        def _():
            writeback(base + t - 2, slot).wait()

        for unit in range(SC_UNITS):
            u = t * SC_UNITS + unit
            buf = lax.rem(u, SC_NBUF)
            gather(t, unit, buf).wait()

            if unit == 0:
                @pl.when(jnp.logical_and(nxt < tokens_per_worker, lax.rem(nxt, SC_TOKB) == 0))
                def _():
                    for c in batch_copies(nxt // SC_TOKB, lax.rem(nxt // SC_TOKB, 2)):
                        c.wait()

            @pl.when(nxt < tokens_per_worker)
            def _():
                gather(nxt, unit, lax.rem(u + SC_UNITS, SC_NBUF)).start()

            compute(bslot, tl, unit, buf, slot)
        writeback(base + t, slot).start()

        @pl.when(jnp.logical_and(tl == SC_TOKB - 1, bi + 2 < n_batches))
        def _():
            for c in batch_copies(bi + 2, bslot):
                c.start()

    for t in range(tokens_per_worker - 2, tokens_per_worker):
        writeback(base + t, t % 2).wait()


def _peer_down(idx, htw, down_w):
    T = htw.shape[0]
    tpw = T // SC_WORKERS
    L = SC_LANES

    @functools.partial(
        pl.kernel, mesh=_sc_mesh(),
        out_type=jax.ShapeDtypeStruct((T, N_SEL), F32),
        scratch_types=[
            pltpu.VMEM((2, SC_TOKB, N_SEL), jnp.int32),
            pltpu.VMEM((2, SC_TOKB, D_WORDS), jnp.int32),
            pltpu.VMEM((SC_NBUF, SC_UROWS, D_WORDS), jnp.int32),
            pltpu.VMEM((2, N_SEL), F32),
            pltpu.SemaphoreType.DMA((SC_NBUF,)),
            pltpu.SemaphoreType.DMA((2,)),
            pltpu.SemaphoreType.DMA((4,)),
        ],
        compiler_params=pltpu.CompilerParams(needs_layout_passes=False),
        name="peer_down", cost_estimate=_sc_cost(T),
    )
    def k(idx_hbm, h_hbm, down_hbm, s_hbm, idx_v, h_v, rows_v, s_v, gsem, osem, bsem):
        lane = lax.iota(jnp.int32, L)

        def compute(bslot, tl, unit, buf, slot):
            svec = jnp.full((L,), slot, jnp.int32)
            hs = [_sc_bf16(h_v[bslot, tl, pl.ds(w * L, L)]) for w in range(SC_WCH)]

            @plsc.parallel_loop(0, SC_UROWS, unroll=2)
            def _(r):
                parts = []
                for q in range(SC_WCH // SC_QUAD):
                    part = _sc_quad_sum(lambda u: rows_v[buf, r, pl.ds((q * SC_QUAD + u) * L, L)],
                                        lambda u: hs[q * SC_QUAD + u])
                    parts.extend(_sc_widen(part))
                total = plsc.cumsum(_tree_sum(parts))
                col = jnp.full((L,), unit * SC_UROWS, jnp.int32) + r
                plsc.store_scatter(s_v, [svec, col], total, mask=lane == L - 1)

        _sc_token_loop(down_hbm, idx_hbm, h_hbm, s_hbm, idx_v, h_v, rows_v, s_v, gsem, osem, bsem, tpw, compute)

    return k(idx, htw, down_w)


def _peer_up(idx, act_w, up_w):
    T = act_w.shape[0]
    tpw = T // SC_WORKERS
    L = SC_LANES

    @functools.partial(
        pl.kernel, mesh=_sc_mesh(),
        out_type=jax.ShapeDtypeStruct((T, D_MODEL), F32),
        scratch_types=[
            pltpu.VMEM((2, SC_TOKB, N_SEL), jnp.int32),
            pltpu.VMEM((2, SC_TOKB, N_SEL), jnp.int32),
            pltpu.VMEM((SC_NBUF, SC_UROWS, D_WORDS), jnp.int32),
            pltpu.VMEM((2, D_MODEL), F32),
            pltpu.SemaphoreType.DMA((SC_NBUF,)),
            pltpu.SemaphoreType.DMA((2,)),
            pltpu.SemaphoreType.DMA((4,)),
        ],
        compiler_params=pltpu.CompilerParams(needs_layout_passes=False),
        name="peer_up", cost_estimate=_sc_cost(T),
    )
    def k(idx_hbm, act_hbm, up_hbm, o_hbm, idx_v, act_v, rows_v, o_v, gsem, osem, bsem):
        def compute(bslot, tl, unit, buf, slot):
            for kg in range(SC_WCH // SC_KG):
                lo_cols = [pl.ds((kg * SC_KG + kk) * L, L) for kk in range(SC_KG)]
                hi_cols = [pl.ds(D_WORDS + (kg * SC_KG + kk) * L, L) for kk in range(SC_KG)]
                if unit == 0:
                    init = tuple(jnp.zeros((L,), F32) for _ in range(2 * SC_KG))
                else:
                    init = tuple(o_v[slot, col] for col in lo_cols + hi_cols)

                def rgbody(rg, accs):
                    per_vec = L // SC_QUAD
                    a16 = act_v[bslot, tl, pl.ds(unit * SC_UROWS + (rg // per_vec) * L, L)]
                    lane0 = jnp.full((L,), 0, jnp.int32) + lax.rem(rg, per_vec) * SC_QUAD
                    a = [_sc_bf16(a16.at[lane0 + u].get(mode="promise_in_bounds")) for u in range(SC_QUAD)]
                    out_lo, out_hi = [], []
                    for kk in range(SC_KG):
                        part = _sc_quad_sum(lambda u: rows_v[buf, rg * SC_QUAD + u, lo_cols[kk]], lambda u: a[u])
                        lo, hi = _sc_widen(part)
                        out_lo.append(accs[kk] + lo)
                        out_hi.append(accs[SC_KG + kk] + hi)
                    return tuple(out_lo + out_hi)
                accs = lax.fori_loop(0, SC_UROWS // SC_QUAD, rgbody, init)
                for kk in range(SC_KG):
                    o_v[slot, lo_cols[kk]] = accs[kk]
                    o_v[slot, hi_cols[kk]] = accs[SC_KG + kk]

        _sc_token_loop(up_hbm, idx_hbm, act_hbm, o_hbm, idx_v, act_v, rows_v, o_v, gsem, osem, bsem, tpw, compute)

    return k(idx, act_w, up_w)


def _act_kernel(s_ref, gate_ref, a_ref):
    s = s_ref[...]
    act = 0.5 * s * (1.0 + lax.erf(s * math.sqrt(0.5))) * gate_ref[...]
    bits = _bf16_bits(act)
    a_ref[...] = lax.bitwise_or(lax.shift_right_logical(bits, 16), bits)


def _act(s, gate, tm):
    T = s.shape[0]
    spec = pl.BlockSpec((tm, N_SEL), lambda i: (i, 0))
    return pl.pallas_call(
        _act_kernel, grid=(T // tm,), in_specs=[spec, spec], out_specs=spec,
        out_shape=jax.ShapeDtypeStruct((T, N_SEL), jnp.int32),
        compiler_params=_cparams(("parallel",)), name="expert_act",
    )(s, gate)


def _final_kernel(x1_ref, p_ref, g_ref, *rest):
    rest[-1][...] = _rms(x1_ref[...] + p_ref[...], g_ref[...])


def _final_into(out, out_shape, x1, peer, g, b0, nb, s0, rows):
    tm = _tile(rows, 512)
    assert s0 % tm == 0
    m0 = s0 // tm
    tok = pl.BlockSpec((None, tm, D_MODEL), lambda b, m: (b, m, 0))
    in_specs = [tok, tok, pl.BlockSpec((1, D_MODEL), lambda b, m: (0, 0))]
    args = [x1.reshape(nb, rows, D_MODEL), peer.reshape(nb, rows, D_MODEL), g]
    aliases = {}
    if out is not None:
        in_specs.append(pl.BlockSpec(memory_space=pl.ANY))
        args.append(out)
        aliases = {3: 0}
    return pl.pallas_call(
        _final_kernel, grid=(nb, rows // tm), in_specs=in_specs,
        out_specs=pl.BlockSpec((None, tm, D_MODEL), lambda b, m: (b + b0, m + m0, 0)),
        out_shape=jax.ShapeDtypeStruct(out_shape, F32), input_output_aliases=aliases,
        compiler_params=_cparams(("parallel", "parallel")), name="final_norm",
    )(*args)


PROMPT_PLAN = ((1, 2),) + ((1, 1),) * 7
SAMPLE_PLAN = ((1, 2),)
SKEW = 2
EARLY_FRONT_SLOT = 4
EARLY_CONV_SLOT = 6


def _tile(n, want):
    t = min(n, want)
    assert n % t == 0
    return t


def _channel_dft():
    ang = 2.0 * np.pi * np.outer(np.arange(GDIM), np.arange(GDIM)) / GDIM
    m = np.concatenate([np.cos(ang), -np.sin(ang)], axis=1) / math.sqrt(GDIM)
    return jnp.asarray(m.astype(np.float32), BF16)


def _plan_units(x, plan, out_id):
    units = []
    b = 0
    for nb, parts in plan:
        xb = x[b:b + nb]
        rows = x.shape[1] // parts
        units += [(xb, i * rows, rows, out_id, b) for i in range(parts)]
        b += nb
    assert b == x.shape[0]
    return units


def _mixer_front(x, p):
    zg, hg = _inproj(x, p["g1"], p["win"], p["cdft"], _tile(x.shape[1], 512))
    return _seq_dft(zg), hg


def _mixer_conv(hg, p):
    return _conv_module(hg, p["conv_w"], p["conv_b"], p["ln_g"], p["ln_b"], p["gn_c"], _tile(hg.shape[1], 512))


def _mixer(x, p):
    yf, hg = _mixer_front(x, p)
    return yf, _mixer_conv(hg, p)


def _route_unit(x, mixed, s0, rows, p):
    T = x.shape[0] * rows
    x1, htw, idx, gate = _route(x, mixed[0], mixed[1], p["gn_f"], p["wout"], p["g2"], p["wq"], p["ka"],
                                p["kb"], _tile(rows, 256), s0, rows)
    return x1.reshape(T, D_MODEL), htw.reshape(T, D_WORDS), idx.reshape(T, N_SEL), gate.reshape(T, N_SEL)


def _encode(units, out_shapes, early_x, p):
    x1s, idxs, acts, peers = [], [], [], []
    early_front, early_conv = [_mixer_front(early_x, p)], [None]
    mixed, mixed_x = None, None
    outs, pending = {}, None

    def finish(j):
        xj, s0j, rowsj, out_id, b0 = units[j]
        outs[out_id] = _final_into(outs.get(out_id), out_shapes[out_id], x1s[j], peers[j], p["final_g"],
                                   b0, xj.shape[0], s0j, rowsj)
        return out_id

    for g, (x, s0, rows, _, _) in enumerate(units):
        if x is early_x and mixed_x is not x:
            mixed, mixed_x = (early_front[0][0], early_conv[0]), x
        reuse = mixed_x is x
        carrier = mixed if reuse else x
        slots = [(acts, g - SKEW)] if g >= SKEW else []
        if g >= SKEW + 1:
            slots.append((peers, g - SKEW - 1))
        if g == EARLY_FRONT_SLOT:
            slots.append((early_front, 0))
        if g == EARLY_CONV_SLOT:
            slots.append((early_conv, 0))
        if pending is not None:
            slots.append((outs, pending))
            pending = None
        if slots:
            carrier, *tied = lax.optimization_barrier((carrier, *[lst[i] for lst, i in slots]))
            for (lst, i), v in zip(slots, tied):
                lst[i] = v
        if g == EARLY_FRONT_SLOT:
            early_conv[0] = _mixer_conv(early_front[0][1], p)
        if g >= SKEW:
            peers.append(_peer_up(idxs[g - SKEW], acts[g - SKEW], p["up"]))
        if g >= SKEW + 1:
            pending = finish(g - SKEW - 1)
        if reuse:
            mixed = carrier
        else:
            x = carrier
            mixed, mixed_x = _mixer(x, p), units[g][0]
        assert (x.shape[0] * rows) % (SC_WORKERS * SC_TOKB) == 0
        x1, htw, idx, gate = _route_unit(x, mixed, s0, rows, p)
        s = _peer_down(idx, htw, p["down"])
        x1s.append(x1)
        idxs.append(idx)
        acts.append(_act(s, gate, _tile(x1.shape[0], 1024)))
    for g in range(len(peers), len(units)):
        peers.append(_peer_up(idxs[g], acts[g], p["up"]))
    for j in range(max(len(units) - SKEW - 1, 0), len(units)):
        finish(j)
    return outs


def kernel(x_prompt, x_sample, norm1_g, w_in, conv_w, conv_b, conv_ln_g, conv_ln_b, gn_fourier_g,
           gn_conv_g, w_out, norm2_g, w_q, keys_a, keys_b, expert_down, expert_up, final_g):
    assert norm1_g.shape[0] == 1
    p = dict(
        g1=norm1_g[0].reshape(1, D_MODEL), win=w_in[0].astype(BF16), cdft=_channel_dft(),
        conv_w=conv_w[0], conv_b=conv_b[0].reshape(1, D_CONV),
        ln_g=conv_ln_g[0].reshape(1, D_CONV), ln_b=conv_ln_b[0].reshape(1, D_CONV),
        gn_f=gn_fourier_g[0].reshape(1, D_FOURIER), gn_c=gn_conv_g[0].reshape(1, D_CONV),
        wout=w_out[0].astype(BF16), g2=norm2_g[0].reshape(1, D_MODEL), wq=w_q[0].astype(BF16),
        ka=keys_a[0].astype(BF16), kb=keys_b[0].astype(BF16),
        down=_pack_halves(expert_down[0]), up=_pack_halves(expert_up[0]),
        final_g=final_g.reshape(1, D_MODEL),
    )
    units = _plan_units(x_prompt, PROMPT_PLAN, 0) + _plan_units(x_sample, SAMPLE_PLAN, 1)
    outs = _encode(units, (x_prompt.shape, x_sample.shape), x_sample, p)
    return outs[0], outs[1]
```

```python
import functools
import math

import numpy as np
import jax
import jax.numpy as jnp
from jax import lax
from jax.experimental import pallas as pl
from jax.experimental.pallas import tpu as pltpu
from jax.experimental.pallas import tpu_sc as plsc

F32 = jnp.float32
BF16 = jnp.bfloat16

D_MODEL = 1024
D_FOURIER = 512
GROUPS = 4
GDIM = 128
D_CONV = 512
CONV_W = 31
CONV_PAD = 15
HEADS = 8
N_KEYS = 128
TOPK = 16
D_HALF = 128
D_QUERY = 256
N_SEL = HEADS * TOPK
EPS = 1e-6

SUBLANES = 8
LANES = 128
VMEM_LIMIT = 56 * 1024 * 1024

SC_CORES = 2
SC_SUBCORES = 16
SC_LANES = 16
SC_WORKERS = SC_CORES * SC_SUBCORES


def _cparams(sem):
    return pltpu.CompilerParams(dimension_semantics=sem, vmem_limit_bytes=VMEM_LIMIT)


def _rms(x, g):
    return x * lax.rsqrt(jnp.mean(x * x, axis=-1, keepdims=True) + EPS) * g


D_WORDS = D_MODEL // 2
HI_MASK = -65536


def _bf16_bits(x):
    return lax.bitcast_convert_type(x.astype(BF16).astype(F32), jnp.int32)


def _pack_halves(x):
    lo = lax.shift_right_logical(_bf16_bits(x[..., :D_WORDS]), 16)
    hi = lax.bitwise_and(_bf16_bits(x[..., D_WORDS:]), HI_MASK)
    return lax.bitwise_or(lo, hi)


def _inproj_kernel(x_ref, g1_ref, win_ref, cdft_ref, zg_ref, hg_ref):
    h = _rms(x_ref[...], g1_ref[...]).astype(BF16)
    z = jnp.dot(h, win_ref[...], preferred_element_type=F32)
    for g in range(GROUPS):
        zg = z[:, g * GDIM:(g + 1) * GDIM].astype(BF16)
        zg_ref[g] = jnp.dot(zg, cdft_ref[...], preferred_element_type=F32)
    a = z[:, D_FOURIER:D_FOURIER + D_CONV]
    gate = z[:, D_FOURIER + D_CONV:]
    hg_ref[...] = a * jax.nn.sigmoid(gate)


def _inproj(x, g1, win_bf, cdft, tm, b0, nb):
    B, S = nb, x.shape[1]
    return pl.pallas_call(
        _inproj_kernel,
        grid=(B, S // tm),
        in_specs=[
            pl.BlockSpec((None, tm, D_MODEL), lambda b, m: (b + b0, m, 0)),
            pl.BlockSpec((1, D_MODEL), lambda b, m: (0, 0)),
            pl.BlockSpec((D_MODEL, D_FOURIER + 2 * D_CONV), lambda b, m: (0, 0)),
            pl.BlockSpec((GDIM, 2 * GDIM), lambda b, m: (0, 0)),
        ],
        out_specs=[
            pl.BlockSpec((None, GROUPS, tm, 2 * GDIM), lambda b, m: (b, 0, m, 0)),
            pl.BlockSpec((None, tm, D_CONV), lambda b, m: (b, m, 0)),
        ],
        out_shape=[
            jax.ShapeDtypeStruct((B, GROUPS, S, 2 * GDIM), F32),
            jax.ShapeDtypeStruct((B, S, D_CONV), F32),
        ],
        compiler_params=_cparams(("parallel", "parallel")),
        name="inproj",
    )(x, g1, win_bf, cdft)


def _fft1_kernel(z_ref, k1_ref, tc_ref, ts_ref, a_ref):
    n1 = z_ref.shape[0]
    rows = n1 * SUBLANES
    x = z_ref[...].reshape(rows, 2 * GDIM).astype(BF16)
    p = jnp.dot(k1_ref[...], x, preferred_element_type=F32)
    pc, ps = p[:rows], p[rows:]
    ar = pc[:, :GDIM] + ps[:, GDIM:]
    ai = pc[:, GDIM:] - ps[:, :GDIM]
    tc, ts = tc_ref[...], ts_ref[...]
    a_ref[:, :, :GDIM] = (ar * tc + ai * ts).reshape(n1, SUBLANES, GDIM)
    a_ref[:, :, GDIM:] = (ai * tc - ar * ts).reshape(n1, SUBLANES, GDIM)


def _fft2_kernel(a_ref, c2_ref, s2_ref, y_ref):
    for r in range(SUBLANES):
        blk = a_ref[r].astype(BF16)
        y = jnp.dot(c2_ref[...], blk[:, :GDIM], preferred_element_type=F32)
        y = y + jnp.dot(s2_ref[...], blk[:, GDIM:], preferred_element_type=F32)
        y_ref[:, r * GDIM:(r + 1) * GDIM] = y


def _split_seq(S):
    n1 = 1 << (int(math.log2(S)) // 2)
    return n1, S // n1


@functools.lru_cache(maxsize=None)
def _fft_tables(S):
    n1, n2 = _split_seq(S)
    k1 = np.arange(n1)
    ang1 = 2.0 * np.pi * np.outer(k1, np.arange(n1)) / n1
    eye = np.eye(SUBLANES)
    kmat = np.concatenate([np.kron(np.cos(ang1), eye), np.kron(np.sin(ang1), eye)], axis=0)
    nn2 = np.arange(n2).reshape(n2 // SUBLANES, 1, SUBLANES)
    angt = 2.0 * np.pi * k1.reshape(1, n1, 1) * nn2 / S
    angt = angt.reshape(n2 // SUBLANES, n1 * SUBLANES, 1)
    tc = np.broadcast_to(np.cos(angt), (n2 // SUBLANES, n1 * SUBLANES, GDIM))
    ts = np.broadcast_to(np.sin(angt), (n2 // SUBLANES, n1 * SUBLANES, GDIM))
    ang2 = 2.0 * np.pi * np.outer(np.arange(n2), np.arange(n2)) / n2
    scale = 1.0 / math.sqrt(S)
    return (kmat.astype(np.float32), np.ascontiguousarray(tc, np.float32),
            np.ascontiguousarray(ts, np.float32),
            (np.cos(ang2) * scale).astype(np.float32), (np.sin(ang2) * scale).astype(np.float32))


def _seq_dft(zg):
    B, G, S, _ = zg.shape
    n1, n2 = _split_seq(S)
    kmat, tc, ts, c2, s2 = _fft_tables(S)
    kmat = jnp.asarray(kmat, BF16)
    c2 = jnp.asarray(c2, BF16)
    s2 = jnp.asarray(s2, BF16)
    nb = n2 // SUBLANES
    z6 = zg.reshape(B, G, n1, nb, SUBLANES, 2 * GDIM)
    blk6 = (None, None, n1, None, SUBLANES, 2 * GDIM)
    a6 = pl.pallas_call(
        _fft1_kernel,
        grid=(nb, B, G),
        in_specs=[
            pl.BlockSpec(blk6, lambda j, b, g: (b, g, 0, j, 0, 0)),
            pl.BlockSpec((2 * n1 * SUBLANES, n1 * SUBLANES), lambda j, b, g: (0, 0)),
            pl.BlockSpec((None, n1 * SUBLANES, GDIM), lambda j, b, g: (j, 0, 0)),
            pl.BlockSpec((None, n1 * SUBLANES, GDIM), lambda j, b, g: (j, 0, 0)),
        ],
        out_specs=pl.BlockSpec(blk6, lambda j, b, g: (b, g, 0, j, 0, 0)),
        out_shape=jax.ShapeDtypeStruct(z6.shape, F32),
        compiler_params=_cparams(("parallel", "parallel", "parallel")),
        name="fft_stage1",
    )(z6, kmat, jnp.asarray(tc), jnp.asarray(ts))
    a5 = a6.reshape(B, G, n1, n2, 2 * GDIM)
    y = pl.pallas_call(
        _fft2_kernel,
        grid=(B, G, n1 // SUBLANES),
        in_specs=[
            pl.BlockSpec((None, None, SUBLANES, n2, 2 * GDIM), lambda b, g, i: (b, g, i, 0, 0)),
            pl.BlockSpec((n2, n2), lambda b, g, i: (0, 0)),
            pl.BlockSpec((n2, n2), lambda b, g, i: (0, 0)),
        ],
        out_specs=pl.BlockSpec((None, None, n2, SUBLANES * GDIM), lambda b, g, i: (b, g, 0, i)),
        out_shape=jax.ShapeDtypeStruct((B, G, n2, n1 * GDIM), F32),
        compiler_params=_cparams(("parallel", "parallel", "parallel")),
        name="fft_stage2",
    )(a5, c2, s2)
    return y.reshape(B, G, S, GDIM)


CONV_HALO = 16
CONV_ROWS = 16


def _conv_kernel(prev_ref, cur_ref, next_ref, w_ref, cb_ref, lg_ref, lb_ref, gc_ref, o_ref, scr):
    ts = cur_ref.shape[0]
    m = pl.program_id(1)
    last = pl.num_programs(1) - 1
    scr[0:CONV_HALO, :] = jnp.where(m > 0, prev_ref[...], 0.0)
    scr[CONV_HALO:CONV_HALO + ts, :] = cur_ref[...]
    scr[CONV_HALO + ts:, :] = jnp.where(m < last, next_ref[...], 0.0)
    off = CONV_HALO - CONV_PAD

    def step(c, carry):
        base = pl.multiple_of(c * CONV_ROWS, CONV_ROWS)
        win = scr[pl.ds(base, CONV_ROWS + 2 * CONV_HALO), :]
        acc = jnp.broadcast_to(cb_ref[...], (CONV_ROWS, D_CONV))
        for k in range(CONV_W):
            acc = acc + win[off + k:off + k + CONV_ROWS, :] * w_ref[k:k + 1, :]
        mu = jnp.mean(acc, axis=-1, keepdims=True)
        d = acc - mu
        var = jnp.mean(d * d, axis=-1, keepdims=True)
        y = d * lax.rsqrt(var + EPS) * lg_ref[...] + lb_ref[...]
        y = y * jax.nn.sigmoid(y)
        o_ref[pl.ds(base, CONV_ROWS), :] = _rms(y, gc_ref[...])
        return carry

    lax.fori_loop(0, ts // CONV_ROWS, step, 0)


def _conv_module(hg, conv_w, conv_b, ln_g, ln_b, gn_c, ts):
    B, S, _ = hg.shape
    hb = ts // CONV_HALO
    nh = S // CONV_HALO
    vec = pl.BlockSpec((1, D_CONV), lambda b, m: (0, 0))
    return pl.pallas_call(
        _conv_kernel,
        grid=(B, S // ts),
        in_specs=[
            pl.BlockSpec((None, CONV_HALO, D_CONV), lambda b, m: (b, jnp.maximum(m * hb - 1, 0), 0)),
            pl.BlockSpec((None, ts, D_CONV), lambda b, m: (b, m, 0)),
            pl.BlockSpec((None, CONV_HALO, D_CONV), lambda b, m: (b, jnp.minimum((m + 1) * hb, nh - 1), 0)),
            pl.BlockSpec((CONV_W, D_CONV), lambda b, m: (0, 0)),
            vec, vec, vec, vec,
        ],
        out_specs=pl.BlockSpec((None, ts, D_CONV), lambda b, m: (b, m, 0)),
        out_shape=jax.ShapeDtypeStruct((B, S, D_CONV), F32),
        scratch_shapes=[pltpu.VMEM((ts + 2 * CONV_HALO, D_CONV), F32)],
        compiler_params=_cparams(("parallel", "parallel")),
        name="conv_module",
    )(hg, hg, hg, conv_w, conv_b, ln_g, ln_b, gn_c)


def _extract_topk(s, order, k, payload=None):
    big = jnp.int32(1 << 30)
    vals, sel = [], []
    for _ in range(k):
        m = jnp.max(s, axis=0, keepdims=True)
        am = jnp.min(jnp.where(s == m, order, big), axis=0, keepdims=True)
        hit = order == am
        vals.append(m)
        if payload is None:
            sel.append(am)
        else:
            sel.append(jnp.max(jnp.where(hit, payload, -1), axis=0, keepdims=True))
        s = jnp.where(hit, -jnp.inf, s)
    return jnp.concatenate(vals, axis=0), jnp.concatenate(sel, axis=0)


_CAND_A = 4
_CAND_B = 3


def _route_kernel(x_ref, yf_ref, yc_ref, gf_ref, wout_ref, g2_ref, wq_ref, ka_ref, kb_ref,
                  x1_ref, htw_ref, idx_ref, gate_ref):
    tm = x_ref.shape[0]
    yf = jnp.concatenate([yf_ref[g] for g in range(GROUPS)], axis=-1)
    yfn = _rms(yf, gf_ref[...])
    ycat = jnp.concatenate([yfn, yc_ref[...]], axis=-1).astype(BF16)
    x1 = x_ref[...] + jnp.dot(ycat, wout_ref[...], preferred_element_type=F32)
    x1_ref[...] = x1
    ht = _rms(x1, g2_ref[...])
    htw_ref[...] = _pack_halves(ht)
    q = jnp.dot(ht.astype(BF16), wq_ref[...], preferred_element_type=F32)

    row16 = lax.broadcasted_iota(jnp.int32, (TOPK, tm), 0)
    key_iota = lax.broadcasted_iota(jnp.int32, (N_KEYS, tm), 0)
    nt = (((1,), (1,)), ((), ()))
    idx_rows, gate_rows = [], []
    for h in range(HEADS):
        qa = q[:, h * D_QUERY:h * D_QUERY + D_HALF].astype(BF16)
        qb = q[:, h * D_QUERY + D_HALF:(h + 1) * D_QUERY].astype(BF16)
        sa = lax.dot_general(ka_ref[h], qa, nt, preferred_element_type=F32)
        sb = lax.dot_general(kb_ref[h], qb, nt, preferred_element_type=F32)
        va, ia = _extract_topk(sa, key_iota, TOPK)
        vb, ib = _extract_topk(sb, key_iota, TOPK)
        cs, ce, co = [], [], []
        for i in range(_CAND_A):
            ok = (i + 1) * (row16 + 1) <= TOPK
            cs.append(jnp.where(ok, va[i:i + 1] + vb, -jnp.inf))
            ce.append(ia[i:i + 1] * N_KEYS + ib)
            co.append(i * TOPK + row16)
        for j in range(_CAND_B):
            ok = jnp.logical_and(row16 >= _CAND_A, (row16 + 1) * (j + 1) <= TOPK)
            cs.append(jnp.where(ok, va + vb[j:j + 1], -jnp.inf))
            ce.append(ia * N_KEYS + ib[j:j + 1])
            co.append(row16 * TOPK + j)
        top_s, top_e = _extract_topk(jnp.concatenate(cs, axis=0), jnp.concatenate(co, axis=0),
                                     TOPK, payload=jnp.concatenate(ce, axis=0))
        ex = jnp.exp(top_s - top_s[0:1])
        gate_rows.append(ex / jnp.sum(ex, axis=0, keepdims=True))
        idx_rows.append(top_e)
    idx_ref[...] = jnp.concatenate(idx_rows, axis=0).T
    gate_ref[...] = jnp.concatenate(gate_rows, axis=0).T


def _route(x, yf, yc, gf, wout_bf, g2, wq_bf, ka_bf, kb_bf, tm, s0, rows, b0):
    B = yc.shape[0]
    S = rows
    m0 = s0 // tm
    assert s0 % tm == 0 and rows % tm == 0
    const2 = lambda b, m: (0, 0)
    const3 = lambda b, m: (0, 0, 0)
    row = lambda w: pl.BlockSpec((None, tm, w), lambda b, m: (b, m, 0))
    row_in = lambda w: pl.BlockSpec((None, tm, w), lambda b, m: (b, m + m0, 0))
    return pl.pallas_call(
        _route_kernel,
        grid=(B, S // tm),
        in_specs=[
            pl.BlockSpec((None, tm, D_MODEL), lambda b, m: (b + b0, m + m0, 0)),
            pl.BlockSpec((None, GROUPS, tm, GDIM), lambda b, m: (b, 0, m + m0, 0)),
            row_in(D_CONV),
            pl.BlockSpec((1, D_FOURIER), const2),
            pl.BlockSpec((D_MODEL, D_MODEL), const2),
            pl.BlockSpec((1, D_MODEL), const2),
            pl.BlockSpec((D_MODEL, HEADS * D_QUERY), const2),
            pl.BlockSpec((HEADS, N_KEYS, D_HALF), const3),
            pl.BlockSpec((HEADS, N_KEYS, D_HALF), const3),
        ],
        out_specs=[row(D_MODEL), row(D_WORDS), row(N_SEL), row(N_SEL)],
        out_shape=[
            jax.ShapeDtypeStruct((B, S, D_MODEL), F32),
            jax.ShapeDtypeStruct((B, S, D_WORDS), jnp.int32),
            jax.ShapeDtypeStruct((B, S, N_SEL), jnp.int32),
            jax.ShapeDtypeStruct((B, S, N_SEL), F32),
        ],
        compiler_params=_cparams(("parallel", "parallel")),
        name="route",
    )(x, yf, yc, gf, wout_bf, g2, wq_bf, ka_bf, kb_bf)


SC_UROWS = 64
SC_UNITS = N_SEL // SC_UROWS
SC_NBUF = 3
SC_TOKB = 16
SC_WCH = D_WORDS // SC_LANES
SC_QUAD = 4
SC_KG = 8
SC_COST_WEIGHT = 16


def _sc_mesh():
    return plsc.VectorSubcoreMesh(core_axis_name="c", subcore_axis_name="s",
                                  num_cores=SC_CORES, num_subcores=SC_SUBCORES)


def _sc_cost(tokens):
    rows = tokens * N_SEL * SC_COST_WEIGHT
    return pl.CostEstimate(flops=2 * rows * D_MODEL, transcendentals=0, bytes_accessed=rows * D_WORDS * 4)


def _sc_bf16(words):
    return plsc.bitcast(words, BF16)


def _sc_widen(pairs):
    w = plsc.bitcast(pairs, jnp.int32)
    lo = plsc.bitcast(lax.shift_left(w, 16), F32)
    hi = plsc.bitcast(lax.bitwise_and(w, jnp.int32(HI_MASK)), F32)
    return lo, hi


def _sc_quad_sum(row_chunk, scale):
    part = None
    for u in range(SC_QUAD):
        prod = _sc_bf16(row_chunk(u)) * scale(u)
        part = prod if part is None else part + prod
    return part


def _tree_sum(xs):
    while len(xs) > 1:
        xs = [xs[i] + xs[i + 1] for i in range(0, len(xs), 2)]
    return xs[0]


def _sc_token_loop(table_hbm, idx_hbm, side_hbm, out_hbm, idx_v, side_v, rows_v, out_v, gsem, osem, bsem,
                   tokens_per_worker, compute):
    wid = lax.axis_index("s") * SC_CORES + lax.axis_index("c")
    base = wid * tokens_per_worker
    n_batches = tokens_per_worker // SC_TOKB

    def batch_copies(bi, bslot):
        src = pl.ds(base + bi * SC_TOKB, SC_TOKB)
        return (pltpu.make_async_copy(idx_hbm.at[src], idx_v.at[bslot], bsem.at[2 * bslot]),
                pltpu.make_async_copy(side_hbm.at[src], side_v.at[bslot], bsem.at[2 * bslot + 1]))

    def gather(t, unit, buf):
        rows = idx_v.at[lax.rem(t // SC_TOKB, 2), lax.rem(t, SC_TOKB), pl.ds(unit * SC_UROWS, SC_UROWS)]
        return pltpu.make_async_copy(table_hbm.at[rows], rows_v.at[buf], gsem.at[buf])

    def writeback(tok, slot):
        return pltpu.make_async_copy(out_v.at[slot], out_hbm.at[tok], osem.at[slot])

    for c in batch_copies(0, 0):
        c.start()
    for c in batch_copies(0, 0):
        c.wait()
    if n_batches > 1:
        for c in batch_copies(1, 1):
            c.start()
    for unit in range(SC_UNITS):
        gather(0, unit, unit).start()

    @pl.loop(0, tokens_per_worker)
    def _(t):
        slot = lax.rem(t, 2)
        bi = t // SC_TOKB
        bslot = lax.rem(bi, 2)
        tl = lax.rem(t, SC_TOKB)
        nxt = t + 1

        @pl.when(t >= 2)
        def _():
            writeback(base + t - 2, slot).wait()

        for unit in range(SC_UNITS):
            u = t * SC_UNITS + unit
            buf = lax.rem(u, SC_NBUF)
            gather(t, unit, buf).wait()

            if unit == 0:
                @pl.when(jnp.logical_and(nxt < tokens_per_worker, lax.rem(nxt, SC_TOKB) == 0))
                def _():
                    for c in batch_copies(nxt // SC_TOKB, lax.rem(nxt // SC_TOKB, 2)):
                        c.wait()

            @pl.when(nxt < tokens_per_worker)
            def _():
                gather(nxt, unit, lax.rem(u + SC_UNITS, SC_NBUF)).start()

            compute(bslot, tl, unit, buf, slot)
        writeback(base + t, slot).start()

        @pl.when(jnp.logical_and(tl == SC_TOKB - 1, bi + 2 < n_batches))
        def _():
            for c in batch_copies(bi + 2, bslot):
                c.start()

    for t in range(tokens_per_worker - 2, tokens_per_worker):
        writeback(base + t, t % 2).wait()


def _peer_down(idx, htw, down_w):
    T = htw.shape[0]
    tpw = T // SC_WORKERS
    L = SC_LANES

    @functools.partial(
        pl.kernel, mesh=_sc_mesh(),
        out_type=jax.ShapeDtypeStruct((T, N_SEL), F32),
        scratch_types=[
            pltpu.VMEM((2, SC_TOKB, N_SEL), jnp.int32),
            pltpu.VMEM((2, SC_TOKB, D_WORDS), jnp.int32),
            pltpu.VMEM((SC_NBUF, SC_UROWS, D_WORDS), jnp.int32),
            pltpu.VMEM((2, N_SEL), F32),
            pltpu.SemaphoreType.DMA((SC_NBUF,)),
            pltpu.SemaphoreType.DMA((2,)),
            pltpu.SemaphoreType.DMA((4,)),
        ],
        compiler_params=pltpu.CompilerParams(needs_layout_passes=False),
        name="peer_down", cost_estimate=_sc_cost(T),
    )
    def k(idx_hbm, h_hbm, down_hbm, s_hbm, idx_v, h_v, rows_v, s_v, gsem, osem, bsem):
        lane = lax.iota(jnp.int32, L)

        def compute(bslot, tl, unit, buf, slot):
            svec = jnp.full((L,), slot, jnp.int32)
            hs = [_sc_bf16(h_v[bslot, tl, pl.ds(w * L, L)]) for w in range(SC_WCH)]

            @plsc.parallel_loop(0, SC_UROWS, unroll=2)
            def _(r):
                parts = []
                for q in range(SC_WCH // SC_QUAD):
                    part = _sc_quad_sum(lambda u: rows_v[buf, r, pl.ds((q * SC_QUAD + u) * L, L)],
                                        lambda u: hs[q * SC_QUAD + u])
                    parts.extend(_sc_widen(part))
                total = plsc.cumsum(_tree_sum(parts))
                col = jnp.full((L,), unit * SC_UROWS, jnp.int32) + r
                plsc.store_scatter(s_v, [svec, col], total, mask=lane == L - 1)

        _sc_token_loop(down_hbm, idx_hbm, h_hbm, s_hbm, idx_v, h_v, rows_v, s_v, gsem, osem, bsem, tpw, compute)

    return k(idx, htw, down_w)


def _peer_up(idx, act_w, up_w):
    T = act_w.shape[0]
    tpw = T // SC_WORKERS
    L = SC_LANES

    @functools.partial(
        pl.kernel, mesh=_sc_mesh(),
        out_type=jax.ShapeDtypeStruct((T, D_MODEL), F32),
        scratch_types=[
            pltpu.VMEM((2, SC_TOKB, N_SEL), jnp.int32),
            pltpu.VMEM((2, SC_TOKB, N_SEL), jnp.int32),
            pltpu.VMEM((SC_NBUF, SC_UROWS, D_WORDS), jnp.int32),
            pltpu.VMEM((2, D_MODEL), F32),
            pltpu.SemaphoreType.DMA((SC_NBUF,)),
            pltpu.SemaphoreType.DMA((2,)),
            pltpu.SemaphoreType.DMA((4,)),
        ],
        compiler_params=pltpu.CompilerParams(needs_layout_passes=False),
        name="peer_up", cost_estimate=_sc_cost(T),
    )
    def k(idx_hbm, act_hbm, up_hbm, o_hbm, idx_v, act_v, rows_v, o_v, gsem, osem, bsem):
        def compute(bslot, tl, unit, buf, slot):
            for kg in range(SC_WCH // SC_KG):
                lo_cols = [pl.ds((kg * SC_KG + kk) * L, L) for kk in range(SC_KG)]
                hi_cols = [pl.ds(D_WORDS + (kg * SC_KG + kk) * L, L) for kk in range(SC_KG)]
                if unit == 0:
                    init = tuple(jnp.zeros((L,), F32) for _ in range(2 * SC_KG))
                else:
                    init = tuple(o_v[slot, col] for col in lo_cols + hi_cols)

                def rgbody(rg, accs):
                    per_vec = L // SC_QUAD
                    a16 = act_v[bslot, tl, pl.ds(unit * SC_UROWS + (rg // per_vec) * L, L)]
                    lane0 = jnp.full((L,), 0, jnp.int32) + lax.rem(rg, per_vec) * SC_QUAD
                    a = [_sc_bf16(a16.at[lane0 + u].get(mode="promise_in_bounds")) for u in range(SC_QUAD)]
                    out_lo, out_hi = [], []
                    for kk in range(SC_KG):
                        part = _sc_quad_sum(lambda u: rows_v[buf, rg * SC_QUAD + u, lo_cols[kk]], lambda u: a[u])
                        lo, hi = _sc_widen(part)
                        out_lo.append(accs[kk] + lo)
                        out_hi.append(accs[SC_KG + kk] + hi)
                    return tuple(out_lo + out_hi)
                accs = lax.fori_loop(0, SC_UROWS // SC_QUAD, rgbody, init)
                for kk in range(SC_KG):
                    o_v[slot, lo_cols[kk]] = accs[kk]
                    o_v[slot, hi_cols[kk]] = accs[SC_KG + kk]

        _sc_token_loop(up_hbm, idx_hbm, act_hbm, o_hbm, idx_v, act_v, rows_v, o_v, gsem, osem, bsem, tpw, compute)

    return k(idx, act_w, up_w)


def _act_kernel(s_ref, gate_ref, a_ref):
    s = s_ref[...]
    act = 0.5 * s * (1.0 + lax.erf(s * math.sqrt(0.5))) * gate_ref[...]
    bits = _bf16_bits(act)
    a_ref[...] = lax.bitwise_or(lax.shift_right_logical(bits, 16), bits)


def _act(s, gate, tm):
    T = s.shape[0]
    spec = pl.BlockSpec((tm, N_SEL), lambda i: (i, 0))
    return pl.pallas_call(
        _act_kernel, grid=(T // tm,), in_specs=[spec, spec], out_specs=spec,
        out_shape=jax.ShapeDtypeStruct((T, N_SEL), jnp.int32),
        compiler_params=_cparams(("parallel",)), name="expert_act",
    )(s, gate)


def _final_kernel(x1_ref, p_ref, g_ref, *rest):
    rest[-1][...] = _rms(x1_ref[...] + p_ref[...], g_ref[...])


def _final_into(out, out_shape, x1, peer, g, b0, nb, s0, rows):
    tm = _tile(rows, 512)
    assert s0 % tm == 0
    m0 = s0 // tm
    tok = pl.BlockSpec((None, tm, D_MODEL), lambda b, m: (b, m, 0))
    in_specs = [tok, tok, pl.BlockSpec((1, D_MODEL), lambda b, m: (0, 0))]
    args = [x1.reshape(nb, rows, D_MODEL), peer.reshape(nb, rows, D_MODEL), g]
    aliases = {}
    if out is not None:
        in_specs.append(pl.BlockSpec(memory_space=pl.ANY))
        args.append(out)
        aliases = {3: 0}
    return pl.pallas_call(
        _final_kernel, grid=(nb, rows // tm), in_specs=in_specs,
        out_specs=pl.BlockSpec((None, tm, D_MODEL), lambda b, m: (b + b0, m + m0, 0)),
        out_shape=jax.ShapeDtypeStruct(out_shape, F32), input_output_aliases=aliases,
        compiler_params=_cparams(("parallel", "parallel")), name="final_norm",
    )(*args)


PROMPT_PLAN = ((1, 2),) + ((1, 1),) * 7
SAMPLE_PLAN = ((1, 2),)
SKEW = 2
EARLY_FRONT_SLOT = 4
EARLY_CONV_SLOT = 6


def _tile(n, want):
    t = min(n, want)
    assert n % t == 0
    return t


def _channel_dft():
    ang = 2.0 * np.pi * np.outer(np.arange(GDIM), np.arange(GDIM)) / GDIM
    m = np.concatenate([np.cos(ang), -np.sin(ang)], axis=1) / math.sqrt(GDIM)
    return jnp.asarray(m.astype(np.float32), BF16)


def _plan_units(x, plan, out_id):
    units = []
    b = 0
    for nb, parts in plan:
        rows = x.shape[1] // parts
        units += [(x, i * rows, rows, out_id, b, nb) for i in range(parts)]
        b += nb
    assert b == x.shape[0]
    return units


def _mixer_front(x, p, b0, nb):
    zg, hg = _inproj(x, p["g1"], p["win"], p["cdft"], _tile(x.shape[1], 512), b0, nb)
    return _seq_dft(zg), hg


def _mixer_conv(hg, p):
    return _conv_module(hg, p["conv_w"], p["conv_b"], p["ln_g"], p["ln_b"], p["gn_c"], _tile(hg.shape[1], 512))


def _mixer(x, p, b0, nb):
    yf, hg = _mixer_front(x, p, b0, nb)
    return yf, _mixer_conv(hg, p)


def _route_unit(x, mixed, s0, rows, b0, p):
    T = mixed[1].shape[0] * rows
    x1, htw, idx, gate = _route(x, mixed[0], mixed[1], p["gn_f"], p["wout"], p["g2"], p["wq"], p["ka"],
                                p["kb"], _tile(rows, 256), s0, rows, b0)
    return x1.reshape(T, D_MODEL), htw.reshape(T, D_WORDS), idx.reshape(T, N_SEL), gate.reshape(T, N_SEL)


def _encode(units, out_shapes, early_x, p):
    x1s, idxs, acts, peers = [], [], [], []
    early_key = next((out_id, b0) for x, _, _, out_id, b0, _ in units if x is early_x)
    early_front, early_conv = [_mixer_front(early_x, p, early_key[1], early_x.shape[0])], [None]
    mixed, mixed_key = None, None
    for g, (x, s0, rows, out_id, b0, nb) in enumerate(units):
        key = (out_id, b0)
        if key == early_key and mixed_key != key:
            mixed, mixed_key = (early_front[0][0], early_conv[0]), key
        reuse = mixed_key == key
        carrier = mixed if reuse else x
        slots = [(acts, g - SKEW)] if g >= SKEW else []
        if g >= SKEW + 1:
            slots.append((peers, g - SKEW - 1))
        if g == EARLY_FRONT_SLOT:
            slots.append((early_front, 0))
        if g == EARLY_CONV_SLOT:
            slots.append((early_conv, 0))
        if slots:
            carrier, *tied = lax.optimization_barrier((carrier, *[lst[i] for lst, i in slots]))
            for (lst, i), v in zip(slots, tied):
                lst[i] = v
        if g == EARLY_FRONT_SLOT:
            early_conv[0] = _mixer_conv(early_front[0][1], p)
        if g >= SKEW:
            peers.append(_peer_up(idxs[g - SKEW], acts[g - SKEW], p["up"]))
        if reuse:
            mixed = carrier
        else:
            x = carrier
            mixed, mixed_key = _mixer(x, p, b0, nb), key
        assert (nb * rows) % (SC_WORKERS * SC_TOKB) == 0
        x1, htw, idx, gate = _route_unit(x, mixed, s0, rows, b0, p)
        s = _peer_down(idx, htw, p["down"])
        x1s.append(x1)
        idxs.append(idx)
        acts.append(_act(s, gate, _tile(x1.shape[0], 1024)))
    for g in range(len(peers), len(units)):
        peers.append(_peer_up(idxs[g], acts[g], p["up"]))
    outs = {}
    for (_, s0, rows, out_id, b0, nb), x1, peer in zip(units, x1s, peers):
        outs[out_id] = _final_into(outs.get(out_id), out_shapes[out_id], x1, peer, p["final_g"],
                                   b0, nb, s0, rows)
    return outs


def kernel(x_prompt, x_sample, norm1_g, w_in, conv_w, conv_b, conv_ln_g, conv_ln_b, gn_fourier_g,
           gn_conv_g, w_out, norm2_g, w_q, keys_a, keys_b, expert_down, expert_up, final_g):
    assert norm1_g.shape[0] == 1
    p = dict(
        g1=norm1_g[0].reshape(1, D_MODEL), win=w_in[0].astype(BF16), cdft=_channel_dft(),
        conv_w=conv_w[0], conv_b=conv_b[0].reshape(1, D_CONV),
        ln_g=conv_ln_g[0].reshape(1, D_CONV), ln_b=conv_ln_b[0].reshape(1, D_CONV),
        gn_f=gn_fourier_g[0].reshape(1, D_FOURIER), gn_c=gn_conv_g[0].reshape(1, D_CONV),
        wout=w_out[0].astype(BF16), g2=norm2_g[0].reshape(1, D_MODEL), wq=w_q[0].astype(BF16),
        ka=keys_a[0].astype(BF16), kb=keys_b[0].astype(BF16),
        down=_pack_halves(expert_down[0]), up=_pack_halves(expert_up[0]),
        final_g=final_g.reshape(1, D_MODEL),
    )
    units = _plan_units(x_prompt, PROMPT_PLAN, 0) + _plan_units(x_sample, SAMPLE_PLAN, 1)
    outs = _encode(units, (x_prompt.shape, x_sample.shape), x_sample, p)
    return outs[0], outs[1]
```

```python
import functools
import math

import numpy as np
import jax
import jax.numpy as jnp
from jax import lax
from jax.experimental import pallas as pl
from jax.experimental.pallas import tpu as pltpu
from jax.experimental.pallas import tpu_sc as plsc

F32 = jnp.float32
BF16 = jnp.bfloat16

D_MODEL = 1024
D_FOURIER = 512
GROUPS = 4
GDIM = 128
D_CONV = 512
CONV_W = 31
CONV_PAD = 15
HEADS = 8
N_KEYS = 128
TOPK = 16
D_HALF = 128
D_QUERY = 256
N_SEL = HEADS * TOPK
EPS = 1e-6

SUBLANES = 8
LANES = 128
VMEM_LIMIT = 56 * 1024 * 1024

SC_CORES = 2
SC_SUBCORES = 16
SC_LANES = 16
SC_WORKERS = SC_CORES * SC_SUBCORES


def _cparams(sem):
    return pltpu.CompilerParams(dimension_semantics=sem, vmem_limit_bytes=VMEM_LIMIT)


def _rms(x, g):
    return x * lax.rsqrt(jnp.mean(x * x, axis=-1, keepdims=True) + EPS) * g


D_WORDS = D_MODEL // 2
HI_MASK = -65536


def _bf16_bits(x):
    return lax.bitcast_convert_type(x.astype(BF16).astype(F32), jnp.int32)


def _pack_halves(x):
    lo = lax.shift_right_logical(_bf16_bits(x[..., :D_WORDS]), 16)
    hi = lax.bitwise_and(_bf16_bits(x[..., D_WORDS:]), HI_MASK)
    return lax.bitwise_or(lo, hi)


def _inproj_kernel(x_ref, g1_ref, win_ref, cdft_ref, zg_ref, hg_ref):
    h = _rms(x_ref[...], g1_ref[...]).astype(BF16)
    z = jnp.dot(h, win_ref[...], preferred_element_type=F32)
    for g in range(GROUPS):
        zg = z[:, g * GDIM:(g + 1) * GDIM].astype(BF16)
        zg_ref[g] = jnp.dot(zg, cdft_ref[...], preferred_element_type=F32)
    a = z[:, D_FOURIER:D_FOURIER + D_CONV]
    gate = z[:, D_FOURIER + D_CONV:]
    hg_ref[...] = a * jax.nn.sigmoid(gate)


def _inproj(x, g1, win_bf, cdft, tm, b0, nb):
    B, S = nb, x.shape[1]
    return pl.pallas_call(
        _inproj_kernel,
        grid=(B, S // tm),
        in_specs=[
            pl.BlockSpec((None, tm, D_MODEL), lambda b, m: (b + b0, m, 0)),
            pl.BlockSpec((1, D_MODEL), lambda b, m: (0, 0)),
            pl.BlockSpec((D_MODEL, D_FOURIER + 2 * D_CONV), lambda b, m: (0, 0)),
            pl.BlockSpec((GDIM, 2 * GDIM), lambda b, m: (0, 0)),
        ],
        out_specs=[
            pl.BlockSpec((None, GROUPS, tm, 2 * GDIM), lambda b, m: (b, 0, m, 0)),
            pl.BlockSpec((None, tm, D_CONV), lambda b, m: (b, m, 0)),
        ],
        out_shape=[
            jax.ShapeDtypeStruct((B, GROUPS, S, 2 * GDIM), F32),
            jax.ShapeDtypeStruct((B, S, D_CONV), F32),
        ],
        compiler_params=_cparams(("parallel", "parallel")),
        name="inproj",
    )(x, g1, win_bf, cdft)


def _fft1_kernel(z_ref, k1_ref, tc_ref, ts_ref, a_ref):
    n1 = z_ref.shape[0]
    rows = n1 * SUBLANES
    x = z_ref[...].reshape(rows, 2 * GDIM).astype(BF16)
    p = jnp.dot(k1_ref[...], x, preferred_element_type=F32)
    pc, ps = p[:rows], p[rows:]
    ar = pc[:, :GDIM] + ps[:, GDIM:]
    ai = pc[:, GDIM:] - ps[:, :GDIM]
    tc, ts = tc_ref[...], ts_ref[...]
    a_ref[:, :, :GDIM] = (ar * tc + ai * ts).reshape(n1, SUBLANES, GDIM)
    a_ref[:, :, GDIM:] = (ai * tc - ar * ts).reshape(n1, SUBLANES, GDIM)


def _fft2_kernel(a_ref, c2_ref, s2_ref, y_ref):
    for r in range(SUBLANES):
        blk = a_ref[r].astype(BF16)
        y = jnp.dot(c2_ref[...], blk[:, :GDIM], preferred_element_type=F32)
        y = y + jnp.dot(s2_ref[...], blk[:, GDIM:], preferred_element_type=F32)
        y_ref[:, r * GDIM:(r + 1) * GDIM] = y


def _split_seq(S):
    n1 = 1 << (int(math.log2(S)) // 2)
    return n1, S // n1


@functools.lru_cache(maxsize=None)
def _fft_tables(S):
    n1, n2 = _split_seq(S)
    k1 = np.arange(n1)
    ang1 = 2.0 * np.pi * np.outer(k1, np.arange(n1)) / n1
    eye = np.eye(SUBLANES)
    kmat = np.concatenate([np.kron(np.cos(ang1), eye), np.kron(np.sin(ang1), eye)], axis=0)
    nn2 = np.arange(n2).reshape(n2 // SUBLANES, 1, SUBLANES)
    angt = 2.0 * np.pi * k1.reshape(1, n1, 1) * nn2 / S
    angt = angt.reshape(n2 // SUBLANES, n1 * SUBLANES, 1)
    tc = np.broadcast_to(np.cos(angt), (n2 // SUBLANES, n1 * SUBLANES, GDIM))
    ts = np.broadcast_to(np.sin(angt), (n2 // SUBLANES, n1 * SUBLANES, GDIM))
    ang2 = 2.0 * np.pi * np.outer(np.arange(n2), np.arange(n2)) / n2
    scale = 1.0 / math.sqrt(S)
    return (kmat.astype(np.float32), np.ascontiguousarray(tc, np.float32),
            np.ascontiguousarray(ts, np.float32),
            (np.cos(ang2) * scale).astype(np.float32), (np.sin(ang2) * scale).astype(np.float32))


def _seq_dft(zg):
    B, G, S, _ = zg.shape
    n1, n2 = _split_seq(S)
    kmat, tc, ts, c2, s2 = _fft_tables(S)
    kmat = jnp.asarray(kmat, BF16)
    c2 = jnp.asarray(c2, BF16)
    s2 = jnp.asarray(s2, BF16)
    nb = n2 // SUBLANES
    z6 = zg.reshape(B, G, n1, nb, SUBLANES, 2 * GDIM)
    blk6 = (None, None, n1, None, SUBLANES, 2 * GDIM)
    a6 = pl.pallas_call(
        _fft1_kernel,
        grid=(nb, B, G),
        in_specs=[
            pl.BlockSpec(blk6, lambda j, b, g: (b, g, 0, j, 0, 0)),
            pl.BlockSpec((2 * n1 * SUBLANES, n1 * SUBLANES), lambda j, b, g: (0, 0)),
            pl.BlockSpec((None, n1 * SUBLANES, GDIM), lambda j, b, g: (j, 0, 0)),
            pl.BlockSpec((None, n1 * SUBLANES, GDIM), lambda j, b, g: (j, 0, 0)),
        ],
        out_specs=pl.BlockSpec(blk6, lambda j, b, g: (b, g, 0, j, 0, 0)),
        out_shape=jax.ShapeDtypeStruct(z6.shape, F32),
        compiler_params=_cparams(("parallel", "parallel", "parallel")),
        name="fft_stage1",
    )(z6, kmat, jnp.asarray(tc), jnp.asarray(ts))
    a5 = a6.reshape(B, G, n1, n2, 2 * GDIM)
    y = pl.pallas_call(
        _fft2_kernel,
        grid=(B, G, n1 // SUBLANES),
        in_specs=[
            pl.BlockSpec((None, None, SUBLANES, n2, 2 * GDIM), lambda b, g, i: (b, g, i, 0, 0)),
            pl.BlockSpec((n2, n2), lambda b, g, i: (0, 0)),
            pl.BlockSpec((n2, n2), lambda b, g, i: (0, 0)),
        ],
        out_specs=pl.BlockSpec((None, None, n2, SUBLANES * GDIM), lambda b, g, i: (b, g, 0, i)),
        out_shape=jax.ShapeDtypeStruct((B, G, n2, n1 * GDIM), F32),
        compiler_params=_cparams(("parallel", "parallel", "parallel")),
        name="fft_stage2",
    )(a5, c2, s2)
    return y.reshape(B, G, S, GDIM)


CONV_HALO = 16
CONV_ROWS = 16


def _conv_kernel(prev_ref, cur_ref, next_ref, w_ref, cb_ref, lg_ref, lb_ref, gc_ref, o_ref, scr):
    ts = cur_ref.shape[0]
    m = pl.program_id(1)
    last = pl.num_programs(1) - 1
    scr[0:CONV_HALO, :] = jnp.where(m > 0, prev_ref[...], 0.0)
    scr[CONV_HALO:CONV_HALO + ts, :] = cur_ref[...]
    scr[CONV_HALO + ts:, :] = jnp.where(m < last, next_ref[...], 0.0)
    off = CONV_HALO - CONV_PAD

    def step(c, carry):
        base = pl.multiple_of(c * CONV_ROWS, CONV_ROWS)
        win = scr[pl.ds(base, CONV_ROWS + 2 * CONV_HALO), :]
        acc = jnp.broadcast_to(cb_ref[...], (CONV_ROWS, D_CONV))
        for k in range(CONV_W):
            acc = acc + win[off + k:off + k + CONV_ROWS, :] * w_ref[k:k + 1, :]
        mu = jnp.mean(acc, axis=-1, keepdims=True)
        d = acc - mu
        var = jnp.mean(d * d, axis=-1, keepdims=True)
        y = d * lax.rsqrt(var + EPS) * lg_ref[...] + lb_ref[...]
        y = y * jax.nn.sigmoid(y)
        o_ref[pl.ds(base, CONV_ROWS), :] = _rms(y, gc_ref[...])
        return carry

    lax.fori_loop(0, ts // CONV_ROWS, step, 0)


def _conv_module(hg, conv_w, conv_b, ln_g, ln_b, gn_c, ts):
    B, S, _ = hg.shape
    hb = ts // CONV_HALO
    nh = S // CONV_HALO
    vec = pl.BlockSpec((1, D_CONV), lambda b, m: (0, 0))
    return pl.pallas_call(
        _conv_kernel,
        grid=(B, S // ts),
        in_specs=[
            pl.BlockSpec((None, CONV_HALO, D_CONV), lambda b, m: (b, jnp.maximum(m * hb - 1, 0), 0)),
            pl.BlockSpec((None, ts, D_CONV), lambda b, m: (b, m, 0)),
            pl.BlockSpec((None, CONV_HALO, D_CONV), lambda b, m: (b, jnp.minimum((m + 1) * hb, nh - 1), 0)),
            pl.BlockSpec((CONV_W, D_CONV), lambda b, m: (0, 0)),
            vec, vec, vec, vec,
        ],
        out_specs=pl.BlockSpec((None, ts, D_CONV), lambda b, m: (b, m, 0)),
        out_shape=jax.ShapeDtypeStruct((B, S, D_CONV), F32),
        scratch_shapes=[pltpu.VMEM((ts + 2 * CONV_HALO, D_CONV), F32)],
        compiler_params=_cparams(("parallel", "parallel")),
        name="conv_module",
    )(hg, hg, hg, conv_w, conv_b, ln_g, ln_b, gn_c)


def _extract_topk(s, order, k, payload=None):
    big = jnp.float32(1 << 30)
    vals, sel = [], []
    for _ in range(k):
        m = jnp.max(s, axis=0, keepdims=True)
        am = jnp.min(jnp.where(s == m, order, big), axis=0, keepdims=True)
        hit = order == am
        vals.append(m)
        if payload is None:
            sel.append(am)
        else:
            sel.append(jnp.max(jnp.where(hit, payload, -1.0), axis=0, keepdims=True))
        s = jnp.where(hit, -jnp.inf, s)
    return jnp.concatenate(vals, axis=0), jnp.concatenate(sel, axis=0)


_CAND_A = 4
_CAND_B = 3


def _route_kernel(x_ref, yf_ref, yc_ref, gf_ref, wout_ref, g2_ref, wq_ref, ka_ref, kb_ref,
                  x1_ref, htw_ref, idx_ref, gate_ref):
    tm = x_ref.shape[0]
    yf = jnp.concatenate([yf_ref[g] for g in range(GROUPS)], axis=-1)
    yfn = _rms(yf, gf_ref[...])
    ycat = jnp.concatenate([yfn, yc_ref[...]], axis=-1).astype(BF16)
    x1 = x_ref[...] + jnp.dot(ycat, wout_ref[...], preferred_element_type=F32)
    x1_ref[...] = x1
    ht = _rms(x1, g2_ref[...])
    htw_ref[...] = _pack_halves(ht)
    q = jnp.dot(ht.astype(BF16), wq_ref[...], preferred_element_type=F32)

    row16 = lax.broadcasted_iota(jnp.int32, (TOPK, tm), 0)
    key_iota = lax.broadcasted_iota(jnp.int32, (N_KEYS, tm), 0).astype(F32)
    row16f = row16.astype(F32)
    nt = (((1,), (1,)), ((), ()))
    idx_rows, gate_rows = [], []
    for h in range(HEADS):
        qa = q[:, h * D_QUERY:h * D_QUERY + D_HALF].astype(BF16)
        qb = q[:, h * D_QUERY + D_HALF:(h + 1) * D_QUERY].astype(BF16)
        sa = lax.dot_general(ka_ref[h], qa, nt, preferred_element_type=F32)
        sb = lax.dot_general(kb_ref[h], qb, nt, preferred_element_type=F32)
        va, ia = _extract_topk(sa, key_iota, TOPK)
        vb, ib = _extract_topk(sb, key_iota, TOPK)
        cs, ce, co = [], [], []
        for i in range(_CAND_A):
            ok = (i + 1) * (row16 + 1) <= TOPK
            cs.append(jnp.where(ok, va[i:i + 1] + vb, -jnp.inf))
            ce.append(ia[i:i + 1] * N_KEYS + ib)
            co.append(i * TOPK + row16f)
        for j in range(_CAND_B):
            ok = jnp.logical_and(row16 >= _CAND_A, (row16 + 1) * (j + 1) <= TOPK)
            cs.append(jnp.where(ok, va + vb[j:j + 1], -jnp.inf))
            ce.append(ia * N_KEYS + ib[j:j + 1])
            co.append(row16f * TOPK + j)
        top_s, top_e = _extract_topk(jnp.concatenate(cs, axis=0), jnp.concatenate(co, axis=0),
                                     TOPK, payload=jnp.concatenate(ce, axis=0))
        ex = jnp.exp(top_s - top_s[0:1])
        gate_rows.append(ex / jnp.sum(ex, axis=0, keepdims=True))
        idx_rows.append(top_e.astype(jnp.int32))
    idx_ref[...] = jnp.concatenate(idx_rows, axis=0).T
    gate_ref[...] = jnp.concatenate(gate_rows, axis=0).T


def _route(x, yf, yc, gf, wout_bf, g2, wq_bf, ka_bf, kb_bf, tm, s0, rows, b0):
    B = yc.shape[0]
    S = rows
    m0 = s0 // tm
    assert s0 % tm == 0 and rows % tm == 0
    const2 = lambda b, m: (0, 0)
    const3 = lambda b, m: (0, 0, 0)
    row = lambda w: pl.BlockSpec((None, tm, w), lambda b, m: (b, m, 0))
    row_in = lambda w: pl.BlockSpec((None, tm, w), lambda b, m: (b, m + m0, 0))
    return pl.pallas_call(
        _route_kernel,
        grid=(B, S // tm),
        in_specs=[
            pl.BlockSpec((None, tm, D_MODEL), lambda b, m: (b + b0, m + m0, 0)),
            pl.BlockSpec((None, GROUPS, tm, GDIM), lambda b, m: (b, 0, m + m0, 0)),
            row_in(D_CONV),
            pl.BlockSpec((1, D_FOURIER), const2),
            pl.BlockSpec((D_MODEL, D_MODEL), const2),
            pl.BlockSpec((1, D_MODEL), const2),
            pl.BlockSpec((D_MODEL, HEADS * D_QUERY), const2),
            pl.BlockSpec((HEADS, N_KEYS, D_HALF), const3),
            pl.BlockSpec((HEADS, N_KEYS, D_HALF), const3),
        ],
        out_specs=[row(D_MODEL), row(D_WORDS), row(N_SEL), row(N_SEL)],
        out_shape=[
            jax.ShapeDtypeStruct((B, S, D_MODEL), F32),
            jax.ShapeDtypeStruct((B, S, D_WORDS), jnp.int32),
            jax.ShapeDtypeStruct((B, S, N_SEL), jnp.int32),
            jax.ShapeDtypeStruct((B, S, N_SEL), F32),
        ],
        compiler_params=_cparams(("parallel", "parallel")),
        name="route",
    )(x, yf, yc, gf, wout_bf, g2, wq_bf, ka_bf, kb_bf)


SC_UROWS = 64
SC_UNITS = N_SEL // SC_UROWS
SC_NBUF = 3
SC_TOKB = 16
SC_WCH = D_WORDS // SC_LANES
SC_QUAD = 4
SC_KG = 8
SC_COST_WEIGHT = 16


def _sc_mesh():
    return plsc.VectorSubcoreMesh(core_axis_name="c", subcore_axis_name="s",
                                  num_cores=SC_CORES, num_subcores=SC_SUBCORES)


def _sc_cost(tokens):
    rows = tokens * N_SEL * SC_COST_WEIGHT
    return pl.CostEstimate(flops=2 * rows * D_MODEL, transcendentals=0, bytes_accessed=rows * D_WORDS * 4)


def _sc_bf16(words):
    return plsc.bitcast(words, BF16)


def _sc_widen(pairs):
    w = plsc.bitcast(pairs, jnp.int32)
    lo = plsc.bitcast(lax.shift_left(w, 16), F32)
    hi = plsc.bitcast(lax.bitwise_and(w, jnp.int32(HI_MASK)), F32)
    return lo, hi


def _sc_quad_sum(row_chunk, scale):
    part = None
    for u in range(SC_QUAD):
        prod = _sc_bf16(row_chunk(u)) * scale(u)
        part = prod if part is None else part + prod
    return part


def _tree_sum(xs):
    while len(xs) > 1:
        xs = [xs[i] + xs[i + 1] for i in range(0, len(xs), 2)]
    return xs[0]


def _sc_token_loop(table_hbm, idx_hbm, side_hbm, out_hbm, idx_v, side_v, rows_v, out_v, gsem, osem, bsem,
                   tokens_per_worker, compute):
    wid = lax.axis_index("s") * SC_CORES + lax.axis_index("c")
    base = wid * tokens_per_worker
    n_batches = tokens_per_worker // SC_TOKB

    def batch_copies(bi, bslot):
        src = pl.ds(base + bi * SC_TOKB, SC_TOKB)
        return (pltpu.make_async_copy(idx_hbm.at[src], idx_v.at[bslot], bsem.at[2 * bslot]),
                pltpu.make_async_copy(side_hbm.at[src], side_v.at[bslot], bsem.at[2 * bslot + 1]))

    def gather(t, unit, buf):
        rows = idx_v.at[lax.rem(t // SC_TOKB, 2), lax.rem(t, SC_TOKB), pl.ds(unit * SC_UROWS, SC_UROWS)]
        return pltpu.make_async_copy(table_hbm.at[rows], rows_v.at[buf], gsem.at[buf])

    def writeback(tok, slot):
        return pltpu.make_async_copy(out_v.at[slot], out_hbm.at[tok], osem.at[slot])

    for c in batch_copies(0, 0):
        c.start()
    for c in batch_copies(0, 0):
        c.wait()
    if n_batches > 1:
        for c in batch_copies(1, 1):
            c.start()
    for unit in range(SC_UNITS):
        gather(0, unit, unit).start()

    @pl.loop(0, tokens_per_worker)
    def _(t):
        slot = lax.rem(t, 2)
        bi = t // SC_TOKB
        bslot = lax.rem(bi, 2)
        tl = lax.rem(t, SC_TOKB)
        nxt = t + 1

        @pl.when(t >= 2)
        def _():
            writeback(base + t - 2, slot).wait()

        for unit in range(SC_UNITS):
            u = t * SC_UNITS + unit
            buf = lax.rem(u, SC_NBUF)
            gather(t, unit, buf).wait()

            if unit == 0:
                @pl.when(jnp.logical_and(nxt < tokens_per_worker, lax.rem(nxt, SC_TOKB) == 0))
                def _():
                    for c in batch_copies(nxt // SC_TOKB, lax.rem(nxt // SC_TOKB, 2)):
                        c.wait()

            @pl.when(nxt < tokens_per_worker)
            def _():
                gather(nxt, unit, lax.rem(u + SC_UNITS, SC_NBUF)).start()

            compute(bslot, tl, unit, buf, slot)
        writeback(base + t, slot).start()

        @pl.when(jnp.logical_and(tl == SC_TOKB - 1, bi + 2 < n_batches))
        def _():
            for c in batch_copies(bi + 2, bslot):
                c.start()

    for t in range(tokens_per_worker - 2, tokens_per_worker):
        writeback(base + t, t % 2).wait()


def _peer_down(idx, htw, down_w):
    T = htw.shape[0]
    tpw = T // SC_WORKERS
    L = SC_LANES

    @functools.partial(
        pl.kernel, mesh=_sc_mesh(),
        out_type=jax.ShapeDtypeStruct((T, N_SEL), F32),
        scratch_types=[
            pltpu.VMEM((2, SC_TOKB, N_SEL), jnp.int32),
            pltpu.VMEM((2, SC_TOKB, D_WORDS), jnp.int32),
            pltpu.VMEM((SC_NBUF, SC_UROWS, D_WORDS), jnp.int32),
            pltpu.VMEM((2, N_SEL), F32),
            pltpu.SemaphoreType.DMA((SC_NBUF,)),
            pltpu.SemaphoreType.DMA((2,)),
            pltpu.SemaphoreType.DMA((4,)),
        ],
        compiler_params=pltpu.CompilerParams(needs_layout_passes=False),
        name="peer_down", cost_estimate=_sc_cost(T),
    )
    def k(idx_hbm, h_hbm, down_hbm, s_hbm, idx_v, h_v, rows_v, s_v, gsem, osem, bsem):
        lane = lax.iota(jnp.int32, L)

        def compute(bslot, tl, unit, buf, slot):
            svec = jnp.full((L,), slot, jnp.int32)
            hs = [_sc_bf16(h_v[bslot, tl, pl.ds(w * L, L)]) for w in range(SC_WCH)]

            @plsc.parallel_loop(0, SC_UROWS, unroll=2)
            def _(r):
                parts = []
                for q in range(SC_WCH // SC_QUAD):
                    part = _sc_quad_sum(lambda u: rows_v[buf, r, pl.ds((q * SC_QUAD + u) * L, L)],
                                        lambda u: hs[q * SC_QUAD + u])
                    parts.extend(_sc_widen(part))
                total = plsc.cumsum(_tree_sum(parts))
                col = jnp.full((L,), unit * SC_UROWS, jnp.int32) + r
                plsc.store_scatter(s_v, [svec, col], total, mask=lane == L - 1)

        _sc_token_loop(down_hbm, idx_hbm, h_hbm, s_hbm, idx_v, h_v, rows_v, s_v, gsem, osem, bsem, tpw, compute)

    return k(idx, htw, down_w)


def _peer_up(idx, act_w, up_w):
    T = act_w.shape[0]
    tpw = T // SC_WORKERS
    L = SC_LANES

    @functools.partial(
        pl.kernel, mesh=_sc_mesh(),
        out_type=jax.ShapeDtypeStruct((T, D_MODEL), F32),
        scratch_types=[
            pltpu.VMEM((2, SC_TOKB, N_SEL), jnp.int32),
            pltpu.VMEM((2, SC_TOKB, N_SEL), jnp.int32),
            pltpu.VMEM((SC_NBUF, SC_UROWS, D_WORDS), jnp.int32),
            pltpu.VMEM((2, D_MODEL), F32),
            pltpu.SemaphoreType.DMA((SC_NBUF,)),
            pltpu.SemaphoreType.DMA((2,)),
            pltpu.SemaphoreType.DMA((4,)),
        ],
        compiler_params=pltpu.CompilerParams(needs_layout_passes=False),
        name="peer_up", cost_estimate=_sc_cost(T),
    )
    def k(idx_hbm, act_hbm, up_hbm, o_hbm, idx_v, act_v, rows_v, o_v, gsem, osem, bsem):
        def compute(bslot, tl, unit, buf, slot):
            for kg in range(SC_WCH // SC_KG):
                lo_cols = [pl.ds((kg * SC_KG + kk) * L, L) for kk in range(SC_KG)]
                hi_cols = [pl.ds(D_WORDS + (kg * SC_KG + kk) * L, L) for kk in range(SC_KG)]
                if unit == 0:
                    init = tuple(jnp.zeros((L,), F32) for _ in range(2 * SC_KG))
                else:
                    init = tuple(o_v[slot, col] for col in lo_cols + hi_cols)

                def rgbody(rg, accs):
                    per_vec = L // SC_QUAD
                    a16 = act_v[bslot, tl, pl.ds(unit * SC_UROWS + (rg // per_vec) * L, L)]
                    lane0 = jnp.full((L,), 0, jnp.int32) + lax.rem(rg, per_vec) * SC_QUAD
                    a = [_sc_bf16(a16.at[lane0 + u].get(mode="promise_in_bounds")) for u in range(SC_QUAD)]
                    out_lo, out_hi = [], []
                    for kk in range(SC_KG):
                        part = _sc_quad_sum(lambda u: rows_v[buf, rg * SC_QUAD + u, lo_cols[kk]], lambda u: a[u])
                        lo, hi = _sc_widen(part)
                        out_lo.append(accs[kk] + lo)
                        out_hi.append(accs[SC_KG + kk] + hi)
                    return tuple(out_lo + out_hi)
                accs = lax.fori_loop(0, SC_UROWS // SC_QUAD, rgbody, init)
                for kk in range(SC_KG):
                    o_v[slot, lo_cols[kk]] = accs[kk]
                    o_v[slot, hi_cols[kk]] = accs[SC_KG + kk]

        _sc_token_loop(up_hbm, idx_hbm, act_hbm, o_hbm, idx_v, act_v, rows_v, o_v, gsem, osem, bsem, tpw, compute)

    return k(idx, act_w, up_w)


def _act_kernel(s_ref, gate_ref, a_ref):
    s = s_ref[...]
    act = 0.5 * s * (1.0 + lax.erf(s * math.sqrt(0.5))) * gate_ref[...]
    bits = _bf16_bits(act)
    a_ref[...] = lax.bitwise_or(lax.shift_right_logical(bits, 16), bits)


def _act(s, gate, tm):
    T = s.shape[0]
    spec = pl.BlockSpec((tm, N_SEL), lambda i: (i, 0))
    return pl.pallas_call(
        _act_kernel, grid=(T // tm,), in_specs=[spec, spec], out_specs=spec,
        out_shape=jax.ShapeDtypeStruct((T, N_SEL), jnp.int32),
        compiler_params=_cparams(("parallel",)), name="expert_act",
    )(s, gate)


def _final_kernel(x1_ref, p_ref, g_ref, *rest):
    rest[-1][...] = _rms(x1_ref[...] + p_ref[...], g_ref[...])


def _final_into(out, out_shape, x1, peer, g, b0, nb, s0, rows):
    tm = _tile(rows, 512)
    assert s0 % tm == 0
    m0 = s0 // tm
    tok = pl.BlockSpec((None, tm, D_MODEL), lambda b, m: (b, m, 0))
    in_specs = [tok, tok, pl.BlockSpec((1, D_MODEL), lambda b, m: (0, 0))]
    args = [x1.reshape(nb, rows, D_MODEL), peer.reshape(nb, rows, D_MODEL), g]
    aliases = {}
    if out is not None:
        in_specs.append(pl.BlockSpec(memory_space=pl.ANY))
        args.append(out)
        aliases = {3: 0}
    return pl.pallas_call(
        _final_kernel, grid=(nb, rows // tm), in_specs=in_specs,
        out_specs=pl.BlockSpec((None, tm, D_MODEL), lambda b, m: (b + b0, m + m0, 0)),
        out_shape=jax.ShapeDtypeStruct(out_shape, F32), input_output_aliases=aliases,
        compiler_params=_cparams(("parallel", "parallel")), name="final_norm",
    )(*args)


PROMPT_PLAN = ((1, 2),) + ((1, 1),) * 7
SAMPLE_PLAN = ((1, 2),)
SKEW = 2
EARLY_FRONT_SLOT = 4
EARLY_CONV_SLOT = 6


def _tile(n, want):
    t = min(n, want)
    assert n % t == 0
    return t


def _channel_dft():
    ang = 2.0 * np.pi * np.outer(np.arange(GDIM), np.arange(GDIM)) / GDIM
    m = np.concatenate([np.cos(ang), -np.sin(ang)], axis=1) / math.sqrt(GDIM)
    return jnp.asarray(m.astype(np.float32), BF16)


def _plan_units(x, plan, out_id):
    units = []
    b = 0
    for nb, parts in plan:
        rows = x.shape[1] // parts
        units += [(x, i * rows, rows, out_id, b, nb) for i in range(parts)]
        b += nb
    assert b == x.shape[0]
    return units


def _mixer_front(x, p, b0, nb):
    zg, hg = _inproj(x, p["g1"], p["win"], p["cdft"], _tile(x.shape[1], 512), b0, nb)
    return _seq_dft(zg), hg


def _mixer_conv(hg, p):
    return _conv_module(hg, p["conv_w"], p["conv_b"], p["ln_g"], p["ln_b"], p["gn_c"], _tile(hg.shape[1], 512))


def _mixer(x, p, b0, nb):
    yf, hg = _mixer_front(x, p, b0, nb)
    return yf, _mixer_conv(hg, p)


def _route_unit(x, mixed, s0, rows, b0, p):
    T = mixed[1].shape[0] * rows
    x1, htw, idx, gate = _route(x, mixed[0], mixed[1], p["gn_f"], p["wout"], p["g2"], p["wq"], p["ka"],
                                p["kb"], _tile(rows, 256), s0, rows, b0)
    return x1.reshape(T, D_MODEL), htw.reshape(T, D_WORDS), idx.reshape(T, N_SEL), gate.reshape(T, N_SEL)


def _encode(units, out_shapes, early_x, p):
    x1s, idxs, acts, peers = [], [], [], []
    early_key = next((out_id, b0) for x, _, _, out_id, b0, _ in units if x is early_x)
    early_front, early_conv = [_mixer_front(early_x, p, early_key[1], early_x.shape[0])], [None]
    mixed, mixed_key = None, None
    for g, (x, s0, rows, out_id, b0, nb) in enumerate(units):
        key = (out_id, b0)
        if key == early_key and mixed_key != key:
            mixed, mixed_key = (early_front[0][0], early_conv[0]), key
        reuse = mixed_key == key
        carrier = mixed if reuse else x
        slots = [(acts, g - SKEW)] if g >= SKEW else []
        if g >= SKEW + 1:
            slots.append((peers, g - SKEW - 1))
        if g == EARLY_FRONT_SLOT:
            slots.append((early_front, 0))
        if g == EARLY_CONV_SLOT:
            slots.append((early_conv, 0))
        if slots:
            carrier, *tied = lax.optimization_barrier((carrier, *[lst[i] for lst, i in slots]))
            for (lst, i), v in zip(slots, tied):
                lst[i] = v
        if g == EARLY_FRONT_SLOT:
            early_conv[0] = _mixer_conv(early_front[0][1], p)
        if g >= SKEW:
            peers.append(_peer_up(idxs[g - SKEW], acts[g - SKEW], p["up"]))
        if reuse:
            mixed = carrier
        else:
            x = carrier
            mixed, mixed_key = _mixer(x, p, b0, nb), key
        assert (nb * rows) % (SC_WORKERS * SC_TOKB) == 0
        x1, htw, idx, gate = _route_unit(x, mixed, s0, rows, b0, p)
        s = _peer_down(idx, htw, p["down"])
        x1s.append(x1)
        idxs.append(idx)
        acts.append(_act(s, gate, _tile(x1.shape[0], 1024)))
    for g in range(len(peers), len(units)):
        peers.append(_peer_up(idxs[g], acts[g], p["up"]))
    outs = {}
    for (_, s0, rows, out_id, b0, nb), x1, peer in zip(units, x1s, peers):
        outs[out_id] = _final_into(outs.get(out_id), out_shapes[out_id], x1, peer, p["final_g"],
                                   b0, nb, s0, rows)
    return outs


def kernel(x_prompt, x_sample, norm1_g, w_in, conv_w, conv_b, conv_ln_g, conv_ln_b, gn_fourier_g,
           gn_conv_g, w_out, norm2_g, w_q, keys_a, keys_b, expert_down, expert_up, final_g):
    assert norm1_g.shape[0] == 1
    p = dict(
        g1=norm1_g[0].reshape(1, D_MODEL), win=w_in[0].astype(BF16), cdft=_channel_dft(),
        conv_w=conv_w[0], conv_b=conv_b[0].reshape(1, D_CONV),
        ln_g=conv_ln_g[0].reshape(1, D_CONV), ln_b=conv_ln_b[0].reshape(1, D_CONV),
        gn_f=gn_fourier_g[0].reshape(1, D_FOURIER), gn_c=gn_conv_g[0].reshape(1, D_CONV),
        wout=w_out[0].astype(BF16), g2=norm2_g[0].reshape(1, D_MODEL), wq=w_q[0].astype(BF16),
        ka=keys_a[0].astype(BF16), kb=keys_b[0].astype(BF16),
        down=_pack_halves(expert_down[0]), up=_pack_halves(expert_up[0]),
        final_g=final_g.reshape(1, D_MODEL),
    )
    units = _plan_units(x_prompt, PROMPT_PLAN, 0) + _plan_units(x_sample, SAMPLE_PLAN, 1)
    outs = _encode(units, (x_prompt.shape, x_sample.shape), x_sample, p)
    return outs[0], outs[1]
```

```python
import functools
import math

import numpy as np
import jax
import jax.numpy as jnp
from jax import lax
from jax.experimental import pallas as pl
from jax.experimental.pallas import tpu as pltpu
from jax.experimental.pallas import tpu_sc as plsc

F32 = jnp.float32
BF16 = jnp.bfloat16

D_MODEL = 1024
D_FOURIER = 512
GROUPS = 4
GDIM = 128
D_CONV = 512
CONV_W = 31
CONV_PAD = 15
HEADS = 8
N_KEYS = 128
TOPK = 16
D_HALF = 128
D_QUERY = 256
N_SEL = HEADS * TOPK
EPS = 1e-6

SUBLANES = 8
LANES = 128
VMEM_LIMIT = 56 * 1024 * 1024

SC_CORES = 2
SC_SUBCORES = 16
SC_LANES = 16
SC_WORKERS = SC_CORES * SC_SUBCORES


def _cparams(sem):
    return pltpu.CompilerParams(dimension_semantics=sem, vmem_limit_bytes=VMEM_LIMIT)


def _rms(x, g):
    return x * lax.rsqrt(jnp.mean(x * x, axis=-1, keepdims=True) + EPS) * g


D_WORDS = D_MODEL // 2
HI_MASK = -65536


def _bf16_bits(x):
    return lax.bitcast_convert_type(x.astype(BF16).astype(F32), jnp.int32)


def _pack_halves(x):
    lo = lax.shift_right_logical(_bf16_bits(x[..., :D_WORDS]), 16)
    hi = lax.bitwise_and(_bf16_bits(x[..., D_WORDS:]), HI_MASK)
    return lax.bitwise_or(lo, hi)


def _inproj_kernel(x_ref, g1_ref, win_ref, cdft_ref, zg_ref, hg_ref):
    h = _rms(x_ref[...], g1_ref[...]).astype(BF16)
    z = jnp.dot(h, win_ref[...], preferred_element_type=F32)
    for g in range(GROUPS):
        zg = z[:, g * GDIM:(g + 1) * GDIM].astype(BF16)
        zg_ref[g] = jnp.dot(zg, cdft_ref[...], preferred_element_type=F32)
    a = z[:, D_FOURIER:D_FOURIER + D_CONV]
    gate = z[:, D_FOURIER + D_CONV:]
    hg_ref[...] = a * jax.nn.sigmoid(gate)


def _inproj(x, g1, win_bf, cdft, tm, b0, nb):
    B, S = nb, x.shape[1]
    return pl.pallas_call(
        _inproj_kernel,
        grid=(B, S // tm),
        in_specs=[
            pl.BlockSpec((None, tm, D_MODEL), lambda b, m: (b + b0, m, 0)),
            pl.BlockSpec((1, D_MODEL), lambda b, m: (0, 0)),
            pl.BlockSpec((D_MODEL, D_FOURIER + 2 * D_CONV), lambda b, m: (0, 0)),
            pl.BlockSpec((GDIM, 2 * GDIM), lambda b, m: (0, 0)),
        ],
        out_specs=[
            pl.BlockSpec((None, GROUPS, tm, 2 * GDIM), lambda b, m: (b, 0, m, 0)),
            pl.BlockSpec((None, tm, D_CONV), lambda b, m: (b, m, 0)),
        ],
        out_shape=[
            jax.ShapeDtypeStruct((B, GROUPS, S, 2 * GDIM), F32),
            jax.ShapeDtypeStruct((B, S, D_CONV), F32),
        ],
        compiler_params=_cparams(("parallel", "parallel")),
        name="inproj",
    )(x, g1, win_bf, cdft)


def _fft1_kernel(z_ref, k1_ref, tc_ref, ts_ref, a_ref):
    n1 = z_ref.shape[0]
    rows = n1 * SUBLANES
    x = z_ref[...].reshape(rows, 2 * GDIM).astype(BF16)
    p = jnp.dot(k1_ref[...], x, preferred_element_type=F32)
    pc, ps = p[:rows], p[rows:]
    ar = pc[:, :GDIM] + ps[:, GDIM:]
    ai = pc[:, GDIM:] - ps[:, :GDIM]
    tc, ts = tc_ref[...], ts_ref[...]
    a_ref[:, :, :GDIM] = (ar * tc + ai * ts).reshape(n1, SUBLANES, GDIM)
    a_ref[:, :, GDIM:] = (ai * tc - ar * ts).reshape(n1, SUBLANES, GDIM)


def _fft2_kernel(a_ref, c2_ref, s2_ref, y_ref):
    for r in range(SUBLANES):
        blk = a_ref[r].astype(BF16)
        y = jnp.dot(c2_ref[...], blk[:, :GDIM], preferred_element_type=F32)
        y = y + jnp.dot(s2_ref[...], blk[:, GDIM:], preferred_element_type=F32)
        y_ref[:, r * GDIM:(r + 1) * GDIM] = y


def _split_seq(S):
    n1 = 1 << (int(math.log2(S)) // 2)
    return n1, S // n1


@functools.lru_cache(maxsize=None)
def _fft_tables(S):
    n1, n2 = _split_seq(S)
    k1 = np.arange(n1)
    ang1 = 2.0 * np.pi * np.outer(k1, np.arange(n1)) / n1
    eye = np.eye(SUBLANES)
    kmat = np.concatenate([np.kron(np.cos(ang1), eye), np.kron(np.sin(ang1), eye)], axis=0)
    nn2 = np.arange(n2).reshape(n2 // SUBLANES, 1, SUBLANES)
    angt = 2.0 * np.pi * k1.reshape(1, n1, 1) * nn2 / S
    angt = angt.reshape(n2 // SUBLANES, n1 * SUBLANES, 1)
    tc = np.broadcast_to(np.cos(angt), (n2 // SUBLANES, n1 * SUBLANES, GDIM))
    ts = np.broadcast_to(np.sin(angt), (n2 // SUBLANES, n1 * SUBLANES, GDIM))
    ang2 = 2.0 * np.pi * np.outer(np.arange(n2), np.arange(n2)) / n2
    scale = 1.0 / math.sqrt(S)
    return (kmat.astype(np.float32), np.ascontiguousarray(tc, np.float32),
            np.ascontiguousarray(ts, np.float32),
            (np.cos(ang2) * scale).astype(np.float32), (np.sin(ang2) * scale).astype(np.float32))


def _seq_dft(zg):
    B, G, S, _ = zg.shape
    n1, n2 = _split_seq(S)
    kmat, tc, ts, c2, s2 = _fft_tables(S)
    kmat = jnp.asarray(kmat, BF16)
    c2 = jnp.asarray(c2, BF16)
    s2 = jnp.asarray(s2, BF16)
    nb = n2 // SUBLANES
    z6 = zg.reshape(B, G, n1, nb, SUBLANES, 2 * GDIM)
    blk6 = (None, None, n1, None, SUBLANES, 2 * GDIM)
    a6 = pl.pallas_call(
        _fft1_kernel,
        grid=(nb, B, G),
        in_specs=[
            pl.BlockSpec(blk6, lambda j, b, g: (b, g, 0, j, 0, 0)),
            pl.BlockSpec((2 * n1 * SUBLANES, n1 * SUBLANES), lambda j, b, g: (0, 0)),
            pl.BlockSpec((None, n1 * SUBLANES, GDIM), lambda j, b, g: (j, 0, 0)),
            pl.BlockSpec((None, n1 * SUBLANES, GDIM), lambda j, b, g: (j, 0, 0)),
        ],
        out_specs=pl.BlockSpec(blk6, lambda j, b, g: (b, g, 0, j, 0, 0)),
        out_shape=jax.ShapeDtypeStruct(z6.shape, F32),
        compiler_params=_cparams(("parallel", "parallel", "parallel")),
        name="fft_stage1",
    )(z6, kmat, jnp.asarray(tc), jnp.asarray(ts))
    a5 = a6.reshape(B, G, n1, n2, 2 * GDIM)
    y = pl.pallas_call(
        _fft2_kernel,
        grid=(B, G, n1 // SUBLANES),
        in_specs=[
            pl.BlockSpec((None, None, SUBLANES, n2, 2 * GDIM), lambda b, g, i: (b, g, i, 0, 0)),
            pl.BlockSpec((n2, n2), lambda b, g, i: (0, 0)),
            pl.BlockSpec((n2, n2), lambda b, g, i: (0, 0)),
        ],
        out_specs=pl.BlockSpec((None, None, n2, SUBLANES * GDIM), lambda b, g, i: (b, g, 0, i)),
        out_shape=jax.ShapeDtypeStruct((B, G, n2, n1 * GDIM), F32),
        compiler_params=_cparams(("parallel", "parallel", "parallel")),
        name="fft_stage2",
    )(a5, c2, s2)
    return y.reshape(B, G, S, GDIM)


CONV_HALO = 16
CONV_ROWS = 16


def _conv_kernel(prev_ref, cur_ref, next_ref, w_ref, cb_ref, lg_ref, lb_ref, gc_ref, o_ref, scr):
    ts = cur_ref.shape[0]
    m = pl.program_id(1)
    last = pl.num_programs(1) - 1
    scr[0:CONV_HALO, :] = jnp.where(m > 0, prev_ref[...], 0.0)
    scr[CONV_HALO:CONV_HALO + ts, :] = cur_ref[...]
    scr[CONV_HALO + ts:, :] = jnp.where(m < last, next_ref[...], 0.0)
    off = CONV_HALO - CONV_PAD

    def step(c, carry):
        base = pl.multiple_of(c * CONV_ROWS, CONV_ROWS)
        win = scr[pl.ds(base, CONV_ROWS + 2 * CONV_HALO), :]
        acc = jnp.broadcast_to(cb_ref[...], (CONV_ROWS, D_CONV))
        for k in range(CONV_W):
            acc = acc + win[off + k:off + k + CONV_ROWS, :] * w_ref[k:k + 1, :]
        mu = jnp.mean(acc, axis=-1, keepdims=True)
        d = acc - mu
        var = jnp.mean(d * d, axis=-1, keepdims=True)
        y = d * lax.rsqrt(var + EPS) * lg_ref[...] + lb_ref[...]
        y = y * jax.nn.sigmoid(y)
        o_ref[pl.ds(base, CONV_ROWS), :] = _rms(y, gc_ref[...])
        return carry

    lax.fori_loop(0, ts // CONV_ROWS, step, 0)


def _conv_module(hg, conv_w, conv_b, ln_g, ln_b, gn_c, ts):
    B, S, _ = hg.shape
    hb = ts // CONV_HALO
    nh = S // CONV_HALO
    vec = pl.BlockSpec((1, D_CONV), lambda b, m: (0, 0))
    return pl.pallas_call(
        _conv_kernel,
        grid=(B, S // ts),
        in_specs=[
            pl.BlockSpec((None, CONV_HALO, D_CONV), lambda b, m: (b, jnp.maximum(m * hb - 1, 0), 0)),
            pl.BlockSpec((None, ts, D_CONV), lambda b, m: (b, m, 0)),
            pl.BlockSpec((None, CONV_HALO, D_CONV), lambda b, m: (b, jnp.minimum((m + 1) * hb, nh - 1), 0)),
            pl.BlockSpec((CONV_W, D_CONV), lambda b, m: (0, 0)),
            vec, vec, vec, vec,
        ],
        out_specs=pl.BlockSpec((None, ts, D_CONV), lambda b, m: (b, m, 0)),
        out_shape=jax.ShapeDtypeStruct((B, S, D_CONV), F32),
        scratch_shapes=[pltpu.VMEM((ts + 2 * CONV_HALO, D_CONV), F32)],
        compiler_params=_cparams(("parallel", "parallel")),
        name="conv_module",
    )(hg, hg, hg, conv_w, conv_b, ln_g, ln_b, gn_c)


def _extract_topk(s, order, k, payload=None):
    big = jnp.float32(1 << 30)
    vals, sel = [], []
    for _ in range(k):
        m = jnp.max(s, axis=0, keepdims=True)
        am = jnp.min(jnp.where(s == m, order, big), axis=0, keepdims=True)
        hit = order == am
        vals.append(m)
        if payload is None:
            sel.append(am)
        else:
            sel.append(jnp.max(jnp.where(hit, payload, -1.0), axis=0, keepdims=True))
        s = jnp.where(hit, -jnp.inf, s)
    return vals, sel


_CAND_A = 4
_CAND_B = 3


def _route_kernel(x_ref, yf_ref, yc_ref, gf_ref, wout_ref, g2_ref, wq_ref, ka_ref, kb_ref,
                  x1_ref, htw_ref, idx_ref, gate_ref):
    tm = x_ref.shape[0]
    yf = jnp.concatenate([yf_ref[g] for g in range(GROUPS)], axis=-1)
    yfn = _rms(yf, gf_ref[...])
    ycat = jnp.concatenate([yfn, yc_ref[...]], axis=-1).astype(BF16)
    x1 = x_ref[...] + jnp.dot(ycat, wout_ref[...], preferred_element_type=F32)
    x1_ref[...] = x1
    ht = _rms(x1, g2_ref[...])
    htw_ref[...] = _pack_halves(ht)
    q = jnp.dot(ht.astype(BF16), wq_ref[...], preferred_element_type=F32)

    rows_i = {n: lax.broadcasted_iota(jnp.int32, (n, tm), 0) for n in (SUBLANES, TOPK)}
    rows_f = {n: r.astype(F32) for n, r in rows_i.items()}
    key_iota = lax.broadcasted_iota(jnp.int32, (N_KEYS, tm), 0).astype(F32)
    nt = (((1,), (1,)), ((), ()))
    idx_rows, gate_rows = [], []
    for h in range(HEADS):
        qa = q[:, h * D_QUERY:h * D_QUERY + D_HALF].astype(BF16)
        qb = q[:, h * D_QUERY + D_HALF:(h + 1) * D_QUERY].astype(BF16)
        sa = lax.dot_general(ka_ref[h], qa, nt, preferred_element_type=F32)
        sb = lax.dot_general(kb_ref[h], qb, nt, preferred_element_type=F32)
        va, ia = _extract_topk(sa, key_iota, TOPK)
        vb, ib = _extract_topk(sb, key_iota, TOPK)
        stack = lambda rows, n: jnp.concatenate(rows[:n], axis=0)
        cs, ce, co = [], [], []
        for i in range(_CAND_A):
            n = TOPK if i == 0 else SUBLANES
            ok = (i + 1) * (rows_i[n] + 1) <= TOPK
            cs.append(jnp.where(ok, va[i] + stack(vb, n), -jnp.inf))
            ce.append(ia[i] * N_KEYS + stack(ib, n))
            co.append(i * TOPK + rows_f[n])
        for j in range(_CAND_B):
            n = TOPK if j == 0 else SUBLANES
            ok = jnp.logical_and(rows_i[n] >= _CAND_A, (rows_i[n] + 1) * (j + 1) <= TOPK)
            cs.append(jnp.where(ok, stack(va, n) + vb[j], -jnp.inf))
            ce.append(stack(ia, n) * N_KEYS + ib[j])
            co.append(rows_f[n] * TOPK + j)
        top_s, top_e = _extract_topk(jnp.concatenate(cs, axis=0), jnp.concatenate(co, axis=0),
                                     TOPK, payload=jnp.concatenate(ce, axis=0))
        top_s, top_e = jnp.concatenate(top_s, axis=0), jnp.concatenate(top_e, axis=0)
        ex = jnp.exp(top_s - top_s[0:1])
        gate_rows.append(ex / jnp.sum(ex, axis=0, keepdims=True))
        idx_rows.append(top_e.astype(jnp.int32))
    idx_ref[...] = jnp.concatenate(idx_rows, axis=0).T
    gate_ref[...] = jnp.concatenate(gate_rows, axis=0).T


def _route(x, yf, yc, gf, wout_bf, g2, wq_bf, ka_bf, kb_bf, tm, s0, rows, b0):
    B = yc.shape[0]
    S = rows
    m0 = s0 // tm
    assert s0 % tm == 0 and rows % tm == 0
    const2 = lambda b, m: (0, 0)
    const3 = lambda b, m: (0, 0, 0)
    row = lambda w: pl.BlockSpec((None, tm, w), lambda b, m: (b, m, 0))
    row_in = lambda w: pl.BlockSpec((None, tm, w), lambda b, m: (b, m + m0, 0))
    return pl.pallas_call(
        _route_kernel,
        grid=(B, S // tm),
        in_specs=[
            pl.BlockSpec((None, tm, D_MODEL), lambda b, m: (b + b0, m + m0, 0)),
            pl.BlockSpec((None, GROUPS, tm, GDIM), lambda b, m: (b, 0, m + m0, 0)),
            row_in(D_CONV),
            pl.BlockSpec((1, D_FOURIER), const2),
            pl.BlockSpec((D_MODEL, D_MODEL), const2),
            pl.BlockSpec((1, D_MODEL), const2),
            pl.BlockSpec((D_MODEL, HEADS * D_QUERY), const2),
            pl.BlockSpec((HEADS, N_KEYS, D_HALF), const3),
            pl.BlockSpec((HEADS, N_KEYS, D_HALF), const3),
        ],
        out_specs=[row(D_MODEL), row(D_WORDS), row(N_SEL), row(N_SEL)],
        out_shape=[
            jax.ShapeDtypeStruct((B, S, D_MODEL), F32),
            jax.ShapeDtypeStruct((B, S, D_WORDS), jnp.int32),
            jax.ShapeDtypeStruct((B, S, N_SEL), jnp.int32),
            jax.ShapeDtypeStruct((B, S, N_SEL), F32),
        ],
        compiler_params=_cparams(("parallel", "parallel")),
        name="route",
    )(x, yf, yc, gf, wout_bf, g2, wq_bf, ka_bf, kb_bf)


SC_UROWS = 64
SC_UNITS = N_SEL // SC_UROWS
SC_NBUF = 3
SC_TOKB = 16
SC_WCH = D_WORDS // SC_LANES
SC_QUAD = 4
SC_KG = 8
SC_COST_WEIGHT = 16


def _sc_mesh():
    return plsc.VectorSubcoreMesh(core_axis_name="c", subcore_axis_name="s",
                                  num_cores=SC_CORES, num_subcores=SC_SUBCORES)


def _sc_cost(tokens):
    rows = tokens * N_SEL * SC_COST_WEIGHT
    return pl.CostEstimate(flops=2 * rows * D_MODEL, transcendentals=0, bytes_accessed=rows * D_WORDS * 4)


def _sc_bf16(words):
    return plsc.bitcast(words, BF16)


def _sc_widen(pairs):
    w = plsc.bitcast(pairs, jnp.int32)
    lo = plsc.bitcast(lax.shift_left(w, 16), F32)
    hi = plsc.bitcast(lax.bitwise_and(w, jnp.int32(HI_MASK)), F32)
    return lo, hi


def _sc_quad_sum(row_chunk, scale):
    part = None
    for u in range(SC_QUAD):
        prod = _sc_bf16(row_chunk(u)) * scale(u)
        part = prod if part is None else part + prod
    return part


def _tree_sum(xs):
    while len(xs) > 1:
        xs = [xs[i] + xs[i + 1] for i in range(0, len(xs), 2)]
    return xs[0]


def _sc_token_loop(table_hbm, idx_hbm, side_hbm, out_hbm, idx_v, side_v, rows_v, out_v, gsem, osem, bsem,
                   tokens_per_worker, compute):
    wid = lax.axis_index("s") * SC_CORES + lax.axis_index("c")
    base = wid * tokens_per_worker
    n_batches = tokens_per_worker // SC_TOKB

    def batch_copies(bi, bslot):
        src = pl.ds(base + bi * SC_TOKB, SC_TOKB)
        return (pltpu.make_async_copy(idx_hbm.at[src], idx_v.at[bslot], bsem.at[2 * bslot]),
                pltpu.make_async_copy(side_hbm.at[src], side_v.at[bslot], bsem.at[2 * bslot + 1]))

    def gather(t, unit, buf):
        rows = idx_v.at[lax.rem(t // SC_TOKB, 2), lax.rem(t, SC_TOKB), pl.ds(unit * SC_UROWS, SC_UROWS)]
        return pltpu.make_async_copy(table_hbm.at[rows], rows_v.at[buf], gsem.at[buf])

    def writeback(tok, slot):
        return pltpu.make_async_copy(out_v.at[slot], out_hbm.at[tok], osem.at[slot])

    for c in batch_copies(0, 0):
        c.start()
    for c in batch_copies(0, 0):
        c.wait()
    if n_batches > 1:
        for c in batch_copies(1, 1):
            c.start()
    for unit in range(SC_UNITS):
        gather(0, unit, unit).start()

    @pl.loop(0, tokens_per_worker)
    def _(t):
        slot = lax.rem(t, 2)
        bi = t // SC_TOKB
        bslot = lax.rem(bi, 2)
        tl = lax.rem(t, SC_TOKB)
        nxt = t + 1

        @pl.when(t >= 2)
        def _():
            writeback(base + t - 2, slot).wait()

        for unit in range(SC_UNITS):
            u = t * SC_UNITS + unit
            buf = lax.rem(u, SC_NBUF)
            gather(t, unit, buf).wait()

            if unit == 0:
                @pl.when(jnp.logical_and(nxt < tokens_per_worker, lax.rem(nxt, SC_TOKB) == 0))
                def _():
                    for c in batch_copies(nxt // SC_TOKB, lax.rem(nxt // SC_TOKB, 2)):
                        c.wait()

            @pl.when(nxt < tokens_per_worker)
            def _():
                gather(nxt, unit, lax.rem(u + SC_UNITS, SC_NBUF)).start()

            compute(bslot, tl, unit, buf, slot)
        writeback(base + t, slot).start()

        @pl.when(jnp.logical_and(tl == SC_TOKB - 1, bi + 2 < n_batches))
        def _():
            for c in batch_copies(bi + 2, bslot):
                c.start()

    for t in range(tokens_per_worker - 2, tokens_per_worker):
        writeback(base + t, t % 2).wait()


def _peer_down(idx, htw, down_w):
    T = htw.shape[0]
    tpw = T // SC_WORKERS
    L = SC_LANES

    @functools.partial(
        pl.kernel, mesh=_sc_mesh(),
        out_type=jax.ShapeDtypeStruct((T, N_SEL), F32),
        scratch_types=[
            pltpu.VMEM((2, SC_TOKB, N_SEL), jnp.int32),
            pltpu.VMEM((2, SC_TOKB, D_WORDS), jnp.int32),
            pltpu.VMEM((SC_NBUF, SC_UROWS, D_WORDS), jnp.int32),
            pltpu.VMEM((2, N_SEL), F32),
            pltpu.SemaphoreType.DMA((SC_NBUF,)),
            pltpu.SemaphoreType.DMA((2,)),
            pltpu.SemaphoreType.DMA((4,)),
        ],
        compiler_params=pltpu.CompilerParams(needs_layout_passes=False),
        name="peer_down", cost_estimate=_sc_cost(T),
    )
    def k(idx_hbm, h_hbm, down_hbm, s_hbm, idx_v, h_v, rows_v, s_v, gsem, osem, bsem):
        lane = lax.iota(jnp.int32, L)

        def compute(bslot, tl, unit, buf, slot):
            svec = jnp.full((L,), slot, jnp.int32)
            hs = [_sc_bf16(h_v[bslot, tl, pl.ds(w * L, L)]) for w in range(SC_WCH)]

            @plsc.parallel_loop(0, SC_UROWS, unroll=2)
            def _(r):
                parts = []
                for q in range(SC_WCH // SC_QUAD):
                    part = _sc_quad_sum(lambda u: rows_v[buf, r, pl.ds((q * SC_QUAD + u) * L, L)],
                                        lambda u: hs[q * SC_QUAD + u])
                    parts.extend(_sc_widen(part))
                total = plsc.cumsum(_tree_sum(parts))
                col = jnp.full((L,), unit * SC_UROWS, jnp.int32) + r
                plsc.store_scatter(s_v, [svec, col], total, mask=lane == L - 1)

        _sc_token_loop(down_hbm, idx_hbm, h_hbm, s_hbm, idx_v, h_v, rows_v, s_v, gsem, osem, bsem, tpw, compute)

    return k(idx, htw, down_w)


def _peer_up(idx, act_w, up_w):
    T = act_w.shape[0]
    tpw = T // SC_WORKERS
    L = SC_LANES

    @functools.partial(
        pl.kernel, mesh=_sc_mesh(),
        out_type=jax.ShapeDtypeStruct((T, D_MODEL), F32),
        scratch_types=[
            pltpu.VMEM((2, SC_TOKB, N_SEL), jnp.int32),
            pltpu.VMEM((2, SC_TOKB, N_SEL), jnp.int32),
            pltpu.VMEM((SC_NBUF, SC_UROWS, D_WORDS), jnp.int32),
            pltpu.VMEM((2, D_MODEL), F32),
            pltpu.SemaphoreType.DMA((SC_NBUF,)),
            pltpu.SemaphoreType.DMA((2,)),
            pltpu.SemaphoreType.DMA((4,)),
        ],
        compiler_params=pltpu.CompilerParams(needs_layout_passes=False),
        name="peer_up", cost_estimate=_sc_cost(T),
    )
    def k(idx_hbm, act_hbm, up_hbm, o_hbm, idx_v, act_v, rows_v, o_v, gsem, osem, bsem):
        def compute(bslot, tl, unit, buf, slot):
            for kg in range(SC_WCH // SC_KG):
                lo_cols = [pl.ds((kg * SC_KG + kk) * L, L) for kk in range(SC_KG)]
                hi_cols = [pl.ds(D_WORDS + (kg * SC_KG + kk) * L, L) for kk in range(SC_KG)]
                if unit == 0:
                    init = tuple(jnp.zeros((L,), F32) for _ in range(2 * SC_KG))
                else:
                    init = tuple(o_v[slot, col] for col in lo_cols + hi_cols)

                def rgbody(rg, accs):
                    per_vec = L // SC_QUAD
                    a16 = act_v[bslot, tl, pl.ds(unit * SC_UROWS + (rg // per_vec) * L, L)]
                    lane0 = jnp.full((L,), 0, jnp.int32) + lax.rem(rg, per_vec) * SC_QUAD
                    a = [_sc_bf16(a16.at[lane0 + u].get(mode="promise_in_bounds")) for u in range(SC_QUAD)]
                    out_lo, out_hi = [], []
                    for kk in range(SC_KG):
                        part = _sc_quad_sum(lambda u: rows_v[buf, rg * SC_QUAD + u, lo_cols[kk]], lambda u: a[u])
                        lo, hi = _sc_widen(part)
                        out_lo.append(accs[kk] + lo)
                        out_hi.append(accs[SC_KG + kk] + hi)
                    return tuple(out_lo + out_hi)
                accs = lax.fori_loop(0, SC_UROWS // SC_QUAD, rgbody, init)
                for kk in range(SC_KG):
                    o_v[slot, lo_cols[kk]] = accs[kk]
                    o_v[slot, hi_cols[kk]] = accs[SC_KG + kk]

        _sc_token_loop(up_hbm, idx_hbm, act_hbm, o_hbm, idx_v, act_v, rows_v, o_v, gsem, osem, bsem, tpw, compute)

    return k(idx, act_w, up_w)


def _act_kernel(s_ref, gate_ref, a_ref):
    s = s_ref[...]
    act = 0.5 * s * (1.0 + lax.erf(s * math.sqrt(0.5))) * gate_ref[...]
    bits = _bf16_bits(act)
    a_ref[...] = lax.bitwise_or(lax.shift_right_logical(bits, 16), bits)


def _act(s, gate, tm):
    T = s.shape[0]
    spec = pl.BlockSpec((tm, N_SEL), lambda i: (i, 0))
    return pl.pallas_call(
        _act_kernel, grid=(T // tm,), in_specs=[spec, spec], out_specs=spec,
        out_shape=jax.ShapeDtypeStruct((T, N_SEL), jnp.int32),
        compiler_params=_cparams(("parallel",)), name="expert_act",
    )(s, gate)


def _final_kernel(x1_ref, p_ref, g_ref, *rest):
    rest[-1][...] = _rms(x1_ref[...] + p_ref[...], g_ref[...])


def _final_into(out, out_shape, x1, peer, g, b0, nb, s0, rows):
    tm = _tile(rows, 512)
    assert s0 % tm == 0
    m0 = s0 // tm
    tok = pl.BlockSpec((None, tm, D_MODEL), lambda b, m: (b, m, 0))
    in_specs = [tok, tok, pl.BlockSpec((1, D_MODEL), lambda b, m: (0, 0))]
    args = [x1.reshape(nb, rows, D_MODEL), peer.reshape(nb, rows, D_MODEL), g]
    aliases = {}
    if out is not None:
        in_specs.append(pl.BlockSpec(memory_space=pl.ANY))
        args.append(out)
        aliases = {3: 0}
    return pl.pallas_call(
        _final_kernel, grid=(nb, rows // tm), in_specs=in_specs,
        out_specs=pl.BlockSpec((None, tm, D_MODEL), lambda b, m: (b + b0, m + m0, 0)),
        out_shape=jax.ShapeDtypeStruct(out_shape, F32), input_output_aliases=aliases,
        compiler_params=_cparams(("parallel", "parallel")), name="final_norm",
    )(*args)


PROMPT_PLAN = ((1, 2),) + ((1, 1),) * 7
SAMPLE_PLAN = ((1, 2),)
SKEW = 2
EARLY_FRONT_SLOT = 4
EARLY_CONV_SLOT = 6


def _tile(n, want):
    t = min(n, want)
    assert n % t == 0
    return t


def _channel_dft():
    ang = 2.0 * np.pi * np.outer(np.arange(GDIM), np.arange(GDIM)) / GDIM
    m = np.concatenate([np.cos(ang), -np.sin(ang)], axis=1) / math.sqrt(GDIM)
    return jnp.asarray(m.astype(np.float32), BF16)


def _plan_units(x, plan, out_id):
    units = []
    b = 0
    for nb, parts in plan:
        rows = x.shape[1] // parts
        units += [(x, i * rows, rows, out_id, b, nb) for i in range(parts)]
        b += nb
    assert b == x.shape[0]
    return units


def _mixer_front(x, p, b0, nb):
    zg, hg = _inproj(x, p["g1"], p["win"], p["cdft"], _tile(x.shape[1], 512), b0, nb)
    return _seq_dft(zg), hg


def _mixer_conv(hg, p):
    return _conv_module(hg, p["conv_w"], p["conv_b"], p["ln_g"], p["ln_b"], p["gn_c"], _tile(hg.shape[1], 512))


def _mixer(x, p, b0, nb):
    yf, hg = _mixer_front(x, p, b0, nb)
    return yf, _mixer_conv(hg, p)


def _route_unit(x, mixed, s0, rows, b0, p):
    T = mixed[1].shape[0] * rows
    x1, htw, idx, gate = _route(x, mixed[0], mixed[1], p["gn_f"], p["wout"], p["g2"], p["wq"], p["ka"],
                                p["kb"], _tile(rows, 256), s0, rows, b0)
    return x1.reshape(T, D_MODEL), htw.reshape(T, D_WORDS), idx.reshape(T, N_SEL), gate.reshape(T, N_SEL)


def _encode(units, out_shapes, early_x, p):
    x1s, idxs, acts, peers = [], [], [], []
    early_key = next((out_id, b0) for x, _, _, out_id, b0, _ in units if x is early_x)
    early_front, early_conv = [_mixer_front(early_x, p, early_key[1], early_x.shape[0])], [None]
    mixed, mixed_key = None, None
    for g, (x, s0, rows, out_id, b0, nb) in enumerate(units):
        key = (out_id, b0)
        if key == early_key and mixed_key != key:
            mixed, mixed_key = (early_front[0][0], early_conv[0]), key
        reuse = mixed_key == key
        carrier = mixed if reuse else x
        slots = [(acts, g - SKEW)] if g >= SKEW else []
        if g >= SKEW + 1:
            slots.append((peers, g - SKEW - 1))
        if g == EARLY_FRONT_SLOT:
            slots.append((early_front, 0))
        if g == EARLY_CONV_SLOT:
            slots.append((early_conv, 0))
        if slots:
            carrier, *tied = lax.optimization_barrier((carrier, *[lst[i] for lst, i in slots]))
            for (lst, i), v in zip(slots, tied):
                lst[i] = v
        if g == EARLY_FRONT_SLOT:
            early_conv[0] = _mixer_conv(early_front[0][1], p)
        if g >= SKEW:
            peers.append(_peer_up(idxs[g - SKEW], acts[g - SKEW], p["up"]))
        if reuse:
            mixed = carrier
        else:
            x = carrier
            mixed, mixed_key = _mixer(x, p, b0, nb), key
        assert (nb * rows) % (SC_WORKERS * SC_TOKB) == 0
        x1, htw, idx, gate = _route_unit(x, mixed, s0, rows, b0, p)
        s = _peer_down(idx, htw, p["down"])
        x1s.append(x1)
        idxs.append(idx)
        acts.append(_act(s, gate, _tile(x1.shape[0], 1024)))
    for g in range(len(peers), len(units)):
        peers.append(_peer_up(idxs[g], acts[g], p["up"]))
    outs = {}
    for (_, s0, rows, out_id, b0, nb), x1, peer in zip(units, x1s, peers):
        outs[out_id] = _final_into(outs.get(out_id), out_shapes[out_id], x1, peer, p["final_g"],
                                   b0, nb, s0, rows)
    return outs


def kernel(x_prompt, x_sample, norm1_g, w_in, conv_w, conv_b, conv_ln_g, conv_ln_b, gn_fourier_g,
           gn_conv_g, w_out, norm2_g, w_q, keys_a, keys_b, expert_down, expert_up, final_g):
    assert norm1_g.shape[0] == 1
    p = dict(
        g1=norm1_g[0].reshape(1, D_MODEL), win=w_in[0].astype(BF16), cdft=_channel_dft(),
        conv_w=conv_w[0], conv_b=conv_b[0].reshape(1, D_CONV),
        ln_g=conv_ln_g[0].reshape(1, D_CONV), ln_b=conv_ln_b[0].reshape(1, D_CONV),
        gn_f=gn_fourier_g[0].reshape(1, D_FOURIER), gn_c=gn_conv_g[0].reshape(1, D_CONV),
        wout=w_out[0].astype(BF16), g2=norm2_g[0].reshape(1, D_MODEL), wq=w_q[0].astype(BF16),
        ka=keys_a[0].astype(BF16), kb=keys_b[0].astype(BF16),
        down=_pack_halves(expert_down[0]), up=_pack_halves(expert_up[0]),
        final_g=final_g.reshape(1, D_MODEL),
    )
    units = _plan_units(x_prompt, PROMPT_PLAN, 0) + _plan_units(x_sample, SAMPLE_PLAN, 1)
    outs = _encode(units, (x_prompt.shape, x_sample.shape), x_sample, p)
    return outs[0], outs[1]
```
